```python
import math
import jax
import jax.numpy as jnp
from jax import lax
import numpy as np

D_MODEL = 1024
BATCH = 8
SEQ = 2048
DEPTH = 1
DEC_BATCH = 128
DEC_SEQ = 8
PAST_LEN = 16384
PAGE_SIZE = 128

H_A = 4
DK_A = 128
DV_A = 128
W_A = H_A * DV_A
CONV_W = 4
CONV_DIM = 2 * H_A * DK_A + W_A
GDN_CHUNK = 64
H_B = 4
DK_B = 64
DV_B = 128
W_B = H_B * DV_B
GK_RANK = 16
GLA_GATE_NORM = 16.0
GLA_CHUNK = 16
N_EXPERTS = 32
TOP_K = 4
D_FF = 1024
SWIGLU_LIMIT = 7.0
SWIGLU_ALPHA = 1.702
MOE_BLOCK = 128
RMS_EPS = 1e-6
IN_SIZES = (CONV_DIM, W_A, H_A, H_A, H_B * DK_B, H_B * DK_B, W_B, W_B, GK_RANK, D_MODEL, D_MODEL)
IN_DIM = sum(IN_SIZES)

kernel_name = "hybrid_gdn_gla_moe_step"


def _rms(x, w):
    xf = x.astype(jnp.float32)
    y = xf * lax.rsqrt(jnp.mean(xf * xf, axis=-1, keepdims=True) + RMS_EPS) * w.astype(jnp.float32)
    return y.astype(x.dtype)


def _l2norm(x):
    return x * lax.rsqrt(jnp.sum(x * x, axis=-1, keepdims=True) + 1e-6)


def _gated_head_rms(o, z, w):
    r = o * lax.rsqrt(jnp.mean(o * o, axis=-1, keepdims=True) + RMS_EPS) * w.astype(jnp.float32)
    return r * jax.nn.silu(z.astype(jnp.float32))


def _chunk_len(t, c):
    return math.gcd(t, c)


def _to_chunks(a, c):
    b, t = a.shape[:2]
    return jnp.swapaxes(a.reshape((b, t // c, c) + a.shape[2:]), 2, 3)


def _from_chunks(o):
    n, b, h, c, d = o.shape
    return jnp.transpose(o, (1, 0, 3, 2, 4)).reshape(b, n * c, h, d)


def _short_conv(u, buf, w):
    t = u.shape[1]
    ucat = jnp.concatenate([buf.astype(u.dtype), u], axis=1)
    y = ucat[:, 0:t] * w[0]
    for j in range(1, CONV_W):
        y = y + ucat[:, j:j + t] * w[j]
    return jax.nn.silu(y), ucat[:, -(CONV_W - 1):]


def _gated_delta_rule(q, k, v, g, beta, s0):
    t = q.shape[1]
    c = _chunk_len(t, GDN_CHUNK)
    qc, kc, vc = _to_chunks(q, c), _to_chunks(k, c), _to_chunks(v, c)
    gc = jnp.cumsum(_to_chunks(g, c), axis=-1)
    bc = _to_chunks(beta, c)
    tri = jnp.tril(jnp.ones((c, c), bool))
    strict = jnp.tril(jnp.ones((c, c), bool), -1)
    diff = gc[..., :, None] - gc[..., None, :]
    decay = jnp.where(tri, jnp.exp(jnp.where(tri, diff, 0.0)), 0.0)
    kb = kc * bc[..., None]
    vb = vc * bc[..., None]
    lower = jnp.where(strict, jnp.einsum('bnhid,bnhjd->bnhij', kb, kc) * decay, 0.0)
    amat = lower + jnp.eye(c, dtype=lower.dtype)
    rhs = jnp.concatenate([vb, kb * jnp.exp(gc)[..., None]], axis=-1)
    sol = lax.linalg.triangular_solve(amat, rhs, left_side=True, lower=True, unit_diagonal=True)
    u, w = sol[..., :DV_A], sol[..., DV_A:]
    attn = jnp.where(tri, jnp.einsum('bnhid,bnhjd->bnhij', qc, kc) * decay, 0.0)
    qg = qc * jnp.exp(gc)[..., None]
    kd = kc * jnp.exp(gc[..., -1:] - gc)[..., None]
    glast = jnp.exp(gc[..., -1])

    def step(s, xs):
        u_i, w_i, attn_i, qg_i, kd_i, gl_i = xs
        v_new = u_i - jnp.einsum('bhcd,bhde->bhce', w_i, s)
        o = jnp.einsum('bhcd,bhde->bhce', qg_i, s) + jnp.einsum('bhij,bhje->bhie', attn_i, v_new)
        s = s * gl_i[..., None, None] + jnp.einsum('bhcd,bhce->bhde', kd_i, v_new)
        return s, o

    xs = tuple(jnp.moveaxis(a, 1, 0) for a in (u, w, attn, qg, kd, glast))
    s_fin, o = lax.scan(step, s0, xs)
    return _from_chunks(o), s_fin


def _gla(q, k, v, log_a, s0):
    t = q.shape[1]
    c = _chunk_len(t, GLA_CHUNK)
    qc, kc, vc = _to_chunks(q, c), _to_chunks(k, c), _to_chunks(v, c)
    gcum = jnp.cumsum(_to_chunks(log_a, c), axis=3)
    tri = jnp.tril(jnp.ones((c, c), bool))
    qg = qc * jnp.exp(gcum)
    kg = kc * jnp.exp(-gcum)
    attn = jnp.where(tri, jnp.einsum('bnhid,bnhjd->bnhij', qg, kg), 0.0)
    o_intra = jnp.einsum('bnhij,bnhje->bnhie', attn, vc)
    kd = kc * jnp.exp(gcum[..., -1:, :] - gcum)
    glast = jnp.exp(gcum[..., -1, :])

    def step(s, xs):
        qg_i, kd_i, v_i, gl_i = xs
        o = jnp.einsum('bhcd,bhde->bhce', qg_i, s)
        s = s * gl_i[..., None] + jnp.einsum('bhcd,bhce->bhde', kd_i, v_i)
        return s, o

    xs = tuple(jnp.moveaxis(a, 1, 0) for a in (qg, kd, vc, glast))
    s_fin, o_inter = lax.scan(step, s0, xs)
    return _from_chunks(o_inter) + _from_chunks(jnp.moveaxis(o_intra, 1, 0)), s_fin


def _moe(x2d, w_router, b_router, w_gate_up, b_gate_up, w_down, b_down):
    n, d = x2d.shape
    logits = (x2d @ w_router).astype(jnp.float32) + b_router.astype(jnp.float32)
    top_v, top_e = lax.top_k(logits, TOP_K)
    gates = jax.nn.softmax(top_v, axis=-1)
    n_assign = n * TOP_K
    flat_e = top_e.reshape(-1)
    order = jnp.argsort(flat_e)
    e_sorted = flat_e[order]
    tok_sorted = (order // TOP_K).astype(jnp.int32)
    counts = jnp.zeros((N_EXPERTS,), jnp.int32).at[flat_e].add(1)
    padded = (counts + MOE_BLOCK - 1) // MOE_BLOCK * MOE_BLOCK
    pad_end = jnp.cumsum(padded)
    pad_start = pad_end - padded
    start = jnp.cumsum(counts) - counts
    dest = pad_start[e_sorted] + jnp.arange(n_assign, dtype=jnp.int32) - start[e_sorted]
    n_blocks = -(-n_assign // MOE_BLOCK) + N_EXPERTS
    rows = n_blocks * MOE_BLOCK
    row_tok = jnp.full((rows,), n, jnp.int32).at[dest].set(tok_sorted)
    blk_expert = jnp.minimum(
        jnp.searchsorted(pad_end, jnp.arange(n_blocks, dtype=jnp.int32) * MOE_BLOCK, side='right'),
        N_EXPERTS - 1)
    xpad = jnp.concatenate([x2d, jnp.zeros((1, d), x2d.dtype)], axis=0)
    xb = xpad[row_tok].reshape(n_blocks, MOE_BLOCK, d)

    def expert_block(args):
        xblk, e = args
        gu = xblk @ w_gate_up[e] + b_gate_up[e]
        gate, up = gu[..., :D_FF], gu[..., D_FF:]
        gate = jnp.minimum(gate, SWIGLU_LIMIT)
        up = jnp.clip(up, -SWIGLU_LIMIT, SWIGLU_LIMIT)
        hmid = (up + 1.0) * (gate * jax.nn.sigmoid(SWIGLU_ALPHA * gate))
        return hmid @ w_down[e] + b_down[e]

    yb = lax.map(expert_block, (xb, blk_expert)).reshape(rows, d)
    w_sorted = gates.reshape(-1)[order].astype(yb.dtype)
    return jnp.zeros((n, d), yb.dtype).at[tok_sorted].add(yb[dest] * w_sorted[:, None])


def _layer(x, s_gdn, s_conv, s_gla, norm1_w, w_in, conv_w, a_log, dt_bias, gdn_norm_w,
           gla_gk_up, gla_gk_b, gla_norm_w, w_up_a, w_up_b, w_o, norm2_w,
           w_router, b_router, w_gate_up, b_gate_up, w_down, b_down):
    b, t, _ = x.shape
    f32 = jnp.float32
    h = _rms(x, norm1_w)
    p = h @ w_in
    split_idx = np.cumsum(IN_SIZES)[:-1].tolist()
    (qkv_a, z_a, alpha_a, beta_a, q_b, k_b, v_b, g_b, gk_lr, gate_a, gate_b) = jnp.split(p, split_idx, axis=-1)

    qkv_a, new_conv = _short_conv(qkv_a, s_conv, conv_w)
    qkv_a = qkv_a.astype(f32)
    qa = _l2norm(qkv_a[..., :H_A * DK_A].reshape(b, t, H_A, DK_A)) * (DK_A ** -0.5)
    ka = _l2norm(qkv_a[..., H_A * DK_A:2 * H_A * DK_A].reshape(b, t, H_A, DK_A))
    va = qkv_a[..., 2 * H_A * DK_A:].reshape(b, t, H_A, DV_A)
    g = -jnp.exp(a_log.astype(f32)) * jax.nn.softplus(alpha_a.astype(f32) + dt_bias.astype(f32))
    beta = jax.nn.sigmoid(beta_a.astype(f32))
    o_a, new_gdn = _gated_delta_rule(qa, ka, va, g, beta, s_gdn.astype(f32))
    o_a = _gated_head_rms(o_a, z_a.reshape(b, t, H_A, DV_A), gdn_norm_w).reshape(b, t, W_A).astype(x.dtype)

    gk = (gk_lr @ gla_gk_up + gla_gk_b).astype(f32).reshape(b, t, H_B, DK_B)
    log_a = jax.nn.log_sigmoid(gk) / GLA_GATE_NORM
    qb = q_b.astype(f32).reshape(b, t, H_B, DK_B) * (DK_B ** -0.5)
    kb = k_b.astype(f32).reshape(b, t, H_B, DK_B)
    vb = v_b.astype(f32).reshape(b, t, H_B, DV_B)
    o_b, new_gla = _gla(qb, kb, vb, log_a, s_gla.astype(f32))
    o_b = _gated_head_rms(o_b, g_b.reshape(b, t, H_B, DV_B), gla_norm_w).reshape(b, t, W_B).astype(x.dtype)

    merged = jax.nn.sigmoid(gate_a) * (o_a @ w_up_a) + jax.nn.sigmoid(gate_b) * (o_b @ w_up_b)
    x = x + merged @ w_o

    h2 = _rms(x, norm2_w).reshape(b * t, D_MODEL)
    x = x + _moe(h2, w_router, b_router, w_gate_up, b_gate_up, w_down, b_down).reshape(b, t, D_MODEL)
    return x, new_gdn.astype(s_gdn.dtype), new_conv.astype(s_conv.dtype), new_gla.astype(s_gla.dtype)


def _trunk(x, s_gdn, s_conv, s_gla, norm1_w, w_in, conv_w, a_log, dt_bias, gdn_norm_w,
           gla_gk_up, gla_gk_b, gla_norm_w, w_up_a, w_up_b, w_o, norm2_w,
           w_router, b_router, w_gate_up, b_gate_up, w_down, b_down, norm_f_w):
    gdn_out, conv_out, gla_out = [], [], []
    for l in range(DEPTH):
        x, sg, sc, sl = _layer(x, s_gdn[l], s_conv[l], s_gla[l], norm1_w[l], w_in[l], conv_w[l],
                               a_log[l], dt_bias[l], gdn_norm_w[l], gla_gk_up[l], gla_gk_b[l],
                               gla_norm_w[l], w_up_a[l], w_up_b[l], w_o[l], norm2_w[l],
                               w_router[l], b_router[l], w_gate_up[l], b_gate_up[l],
                               w_down[l], b_down[l])
        gdn_out.append(sg)
        conv_out.append(sc)
        gla_out.append(sl)
    return _rms(x, norm_f_w), jnp.stack(gdn_out), jnp.stack(conv_out), jnp.stack(gla_out)


def setup_inputs(seed: int = 0) -> dict:
    key = jax.random.key(seed)
    ks = jax.random.split(key, 32)
    f32 = jnp.float32

    def nrm(k, shape, scale):
        return jax.random.normal(k, shape, f32) * scale

    L = DEPTH
    dt = jnp.exp(jax.random.uniform(ks[9], (L, H_A), f32, math.log(1e-3), math.log(1e-1)))
    return {
        "x_prompt": nrm(ks[0], (BATCH, SEQ, D_MODEL), 1.0),
        "x_sample": nrm(ks[1], (DEC_BATCH, DEC_SEQ, D_MODEL), 1.0),
        "state_gdn": nrm(ks[2], (L, DEC_BATCH, H_A, DK_A, DV_A), 0.5),
        "state_gdn_conv": nrm(ks[3], (L, DEC_BATCH, CONV_W - 1, CONV_DIM), 1.0),
        "state_gla": nrm(ks[4], (L, DEC_BATCH, H_B, DK_B, DV_B), 0.5),
        "norm1_w": 1.0 + nrm(ks[5], (L, D_MODEL), 0.02),
        "w_in": nrm(ks[6], (L, D_MODEL, IN_DIM), D_MODEL ** -0.5),
        "conv_w": nrm(ks[7], (L, CONV_W, CONV_DIM), CONV_W ** -0.5),
        "a_log": jnp.log(jax.random.uniform(ks[8], (L, H_A), f32, 1.0, 16.0)),
        "dt_bias": dt + jnp.log(-jnp.expm1(-dt)),
        "gdn_norm_w": 1.0 + nrm(ks[10], (L, DV_A), 0.02),
        "gla_gk_up": nrm(ks[11], (L, GK_RANK, H_B * DK_B), GK_RANK ** -0.5),
        "gla_gk_b": nrm(ks[12], (L, H_B * DK_B), 0.01),
        "gla_norm_w": 1.0 + nrm(ks[13], (L, DV_B), 0.02),
        "w_up_a": nrm(ks[14], (L, W_A, D_MODEL), W_A ** -0.5),
        "w_up_b": nrm(ks[15], (L, W_B, D_MODEL), W_B ** -0.5),
        "w_o": nrm(ks[16], (L, D_MODEL, D_MODEL), D_MODEL ** -0.5),
        "norm2_w": 1.0 + nrm(ks[17], (L, D_MODEL), 0.02),
        "w_router": nrm(ks[18], (L, D_MODEL, N_EXPERTS), D_MODEL ** -0.5),
        "b_router": nrm(ks[19], (L, N_EXPERTS), 0.01),
        "w_gate_up": nrm(ks[20], (L, N_EXPERTS, D_MODEL, 2 * D_FF), D_MODEL ** -0.5),
        "b_gate_up": nrm(ks[21], (L, N_EXPERTS, 2 * D_FF), 0.01),
        "w_down": nrm(ks[22], (L, N_EXPERTS, D_FF, D_MODEL), D_FF ** -0.5),
        "b_down": nrm(ks[23], (L, N_EXPERTS, D_MODEL), 0.01),
        "norm_f_w": 1.0 + nrm(ks[24], (D_MODEL,), 0.02),
    }


def reference(x_prompt, x_sample, state_gdn, state_gdn_conv, state_gla, norm1_w, w_in, conv_w,
              a_log, dt_bias, gdn_norm_w, gla_gk_up, gla_gk_b, gla_norm_w, w_up_a, w_up_b, w_o,
              norm2_w, w_router, b_router, w_gate_up, b_gate_up, w_down, b_down, norm_f_w):
    dt = x_prompt.dtype
    zg = jnp.zeros((DEPTH, BATCH, H_A, DK_A, DV_A), dt)
    zc = jnp.zeros((DEPTH, BATCH, CONV_W - 1, CONV_DIM), dt)
    zl = jnp.zeros((DEPTH, BATCH, H_B, DK_B, DV_B), dt)
    y_prompt, gdn_p, conv_p, gla_p = _trunk(
        x_prompt, zg, zc, zl, norm1_w, w_in, conv_w, a_log, dt_bias, gdn_norm_w, gla_gk_up,
        gla_gk_b, gla_norm_w, w_up_a, w_up_b, w_o, norm2_w, w_router, b_router, w_gate_up,
        b_gate_up, w_down, b_down, norm_f_w)
    y_sample, gdn_s, conv_s, gla_s = _trunk(
        x_sample, state_gdn, state_gdn_conv, state_gla, norm1_w, w_in, conv_w, a_log, dt_bias,
        gdn_norm_w, gla_gk_up, gla_gk_b, gla_norm_w, w_up_a, w_up_b, w_o, norm2_w, w_router,
        b_router, w_gate_up, b_gate_up, w_down, b_down, norm_f_w)
    return (y_prompt, y_sample, gdn_p, conv_p, gla_p, gdn_s, conv_s, gla_s)
```

```python
import functools
import math

import jax
import jax.numpy as jnp
import numpy as np
from jax import lax
from jax.experimental import pallas as pl
from jax.experimental.pallas import tpu as pltpu

F32 = jnp.float32
BF16 = jnp.bfloat16

D_MODEL = 1024
N_HEADS = 4
DK_A = 128
DV_A = 128
W_A = N_HEADS * DV_A
CONV_W = 4
CONV_DIM = 2 * N_HEADS * DK_A + W_A
GDN_CHUNK = 64
DK_B = 64
DV_B = 128
W_B = N_HEADS * DV_B
GK_RANK = 16
GLA_GATE_NORM = 16.0
GLA_CHUNK = 16
N_EXPERTS = 32
TOP_K = 4
D_FF = 1024
SWIGLU_LIMIT = 7.0
SWIGLU_ALPHA = 1.702
RMS_EPS = 1e-6
IN_SIZES = (CONV_DIM, W_A, N_HEADS, N_HEADS, N_HEADS * DK_B, N_HEADS * DK_B, W_B, W_B, GK_RANK,
            D_MODEL, D_MODEL)

LANES = 128

OFF_QKV = 0
OFF_Z = 1536
OFF_GATE_A = 2048
OFF_GATE_B = 3072
OFF_QK_B = 4096
OFF_V_B = 4608
OFF_G_B = 5120
OFF_SMALL = 5632
P_COLS = 5760

INPROJ_TM = 512
INPROJ_TN = 1920
GLA_TB = 128
MERGE_TM = 512
MOE_BM = 256
COMBINE_TM = 256
VMEM_LIMIT = 56 * 1024 * 1024


def _dot(a, b):
    return jnp.dot(a.astype(BF16), b.astype(BF16), preferred_element_type=F32)


def _dot_nt(a, b):
    return lax.dot_general(a.astype(BF16), b.astype(BF16), (((1,), (1,)), ((), ())),
                           preferred_element_type=F32)


def _dot_tn(a, b):
    return lax.dot_general(a, b, (((0,), (0,)), ((), ())), preferred_element_type=F32)


def _split3(x):
    hi = x.astype(BF16)
    r1 = x - hi.astype(F32)
    mid = r1.astype(BF16)
    lo = (r1 - mid.astype(F32)).astype(BF16)
    return hi, mid, lo


def _dot01(m01, x):
    hi, mid, lo = _split3(x)
    d = lambda p: jnp.dot(m01, p, preferred_element_type=F32)
    return d(hi) + d(mid) + d(lo)


def _silu(x):
    return x * jax.nn.sigmoid(x)


def _inproj_kernel(x_ref, nw_ref, w_ref, o_ref, h_ref):
    @pl.when(pl.program_id(1) == 0)
    def _():
        x = x_ref[...]
        ms = jnp.mean(x * x, axis=-1, keepdims=True)
        h_ref[...] = (x * lax.rsqrt(ms + RMS_EPS) * nw_ref[...]).astype(BF16)

    o_ref[...] = jnp.dot(h_ref[...], w_ref[...], preferred_element_type=F32)


def _inproj(x, norm_w, w_packed):
    n = x.shape[0]
    grid = (n // INPROJ_TM, P_COLS // INPROJ_TN)
    return pl.pallas_call(
        _inproj_kernel,
        out_shape=jax.ShapeDtypeStruct((n, P_COLS), F32),
        grid=grid,
        in_specs=[
            pl.BlockSpec((INPROJ_TM, D_MODEL), lambda i, j: (i, 0)),
            pl.BlockSpec((1, D_MODEL), lambda i, j: (0, 0)),
            pl.BlockSpec((D_MODEL, INPROJ_TN), lambda i, j: (0, j)),
        ],
        out_specs=pl.BlockSpec((INPROJ_TM, INPROJ_TN), lambda i, j: (i, j)),
        scratch_shapes=[pltpu.VMEM((INPROJ_TM, D_MODEL), BF16)],
        compiler_params=pltpu.CompilerParams(
            dimension_semantics=("arbitrary", "arbitrary"), vmem_limit_bytes=VMEM_LIMIT),
        name="inproj",
    )(x, norm_w.reshape(1, D_MODEL), w_packed)


def _gdn_kernel(qkv_ref, z_ref, sm_ref, cw_ref, hp_ref, nw_ref, sconv_ref, s0_ref,
                o_ref, sout_ref, s_ref, ubuf_ref, *, chunk, n_chunks):
    c_len = chunk
    c = pl.program_id(1)
    hist = CONV_W - 1
    base = 8

    @pl.when(c == 0)
    def _():
        s_ref[...] = s0_ref[...]
        ubuf_ref[base - hist:base, :] = sconv_ref[...]

    if n_chunks > 1:
        @pl.when(c > 0)
        def _():
            ubuf_ref[base - hist:base, :] = ubuf_ref[base + c_len - hist:base + c_len, :]

    ubuf_ref[base:base + c_len, :] = qkv_ref[...]
    cw = cw_ref[...]
    y = ubuf_ref[base - hist:base - hist + c_len, :] * cw[0:1, :]
    for j in range(1, CONV_W):
        y = y + ubuf_ref[base - hist + j:base - hist + j + c_len, :] * cw[j:j + 1, :]
    y = _silu(y)

    sm = sm_ref[...]
    hp = hp_ref[...]
    lane = lax.broadcasted_iota(jnp.int32, (c_len, LANES), 1)
    g_all = -jnp.exp(hp[0:1, :]) * jax.nn.softplus(sm + hp[1:2, :])
    g_all = jnp.where(lane < N_HEADS, g_all, 0.0)
    beta_all = jax.nn.sigmoid(sm)

    r_i = lax.broadcasted_iota(jnp.int32, (c_len, c_len), 0)
    c_i = lax.broadcasted_iota(jnp.int32, (c_len, c_len), 1)
    tri = c_i <= r_i
    strict = c_i < r_i
    eye = c_i == r_i
    gc = _dot01(jnp.where(tri, 1.0, 0.0).astype(BF16), g_all)

    nw = nw_ref[...]
    levels = int(round(math.log2(c_len)))
    for h in range(N_HEADS):
        qh = y[:, h * DK_A:(h + 1) * DK_A]
        kh = y[:, W_A + h * DK_A:W_A + (h + 1) * DK_A]
        vh = y[:, 2 * W_A + h * DV_A:2 * W_A + (h + 1) * DV_A]
        qn = qh * lax.rsqrt(jnp.sum(qh * qh, axis=-1, keepdims=True) + 1e-6) * (DK_A ** -0.5)
        kn = kh * lax.rsqrt(jnp.sum(kh * kh, axis=-1, keepdims=True) + 1e-6)
        gcol = gc[:, h:h + 1]
        gcb = jnp.broadcast_to(gcol, (c_len, c_len))
        grow = jnp.sum(jnp.where(eye, gcb, 0.0), axis=0, keepdims=True)
        decay = jnp.where(tri, jnp.exp(jnp.where(tri, gcb - grow, 0.0)), 0.0)
        bcol = beta_all[:, N_HEADS + h:N_HEADS + h + 1]
        kb = kn * bcol
        vb = vh * bcol
        low = jnp.where(strict, _dot_nt(kb, kn) * decay, 0.0)
        ymat = -low
        lp = low
        for _ in range(levels - 1):
            lp = _dot(lp, lp)
            ymat = ymat + lp + _dot(ymat, lp)
        eg = jnp.exp(gcol)
        rhs = jnp.concatenate([vb, kb * eg], axis=1)
        sol = rhs + _dot(ymat, rhs)
        u_ = sol[:, :DV_A]
        w_ = sol[:, DV_A:]
        attn = jnp.where(tri, _dot_nt(qn, kn) * decay, 0.0)
        qg = qn * eg
        glast = gc[c_len - 1:c_len, h:h + 1]
        kd = kn * jnp.exp(glast - gcol)
        s = s_ref[h]
        ws = _dot(jnp.concatenate([w_, qg], axis=0), s)
        v_new = u_ - ws[:c_len]
        o = ws[c_len:] + _dot(attn, v_new)
        s_ref[h] = s * jnp.exp(glast) + _dot_tn(kd, v_new)
        r = o * lax.rsqrt(jnp.mean(o * o, axis=-1, keepdims=True) + RMS_EPS) * nw
        o_ref[:, h * DV_A:(h + 1) * DV_A] = r * _silu(z_ref[:, h * DV_A:(h + 1) * DV_A])

    @pl.when(c == n_chunks - 1)
    def _():
        sout_ref[...] = s_ref[...]


def _gdn(p, row0, batch, seq, conv_w, hp, norm_w, s_conv, s0):
    chunk = math.gcd(seq, GDN_CHUNK)
    n_chunks = seq // chunk
    blk0 = row0 // chunk
    rows = lambda b, c: blk0 + b * n_chunks + c
    kern = functools.partial(_gdn_kernel, chunk=chunk, n_chunks=n_chunks)
    return pl.pallas_call(
        kern,
        out_shape=(jax.ShapeDtypeStruct((batch * seq, W_A), F32),
                   jax.ShapeDtypeStruct((batch, N_HEADS, DK_A, DV_A), F32)),
        grid=(batch, n_chunks),
        in_specs=[
            pl.BlockSpec((chunk, CONV_DIM), lambda b, c: (rows(b, c), OFF_QKV // CONV_DIM)),
            pl.BlockSpec((chunk, W_A), lambda b, c: (rows(b, c), OFF_Z // W_A)),
            pl.BlockSpec((chunk, LANES), lambda b, c: (rows(b, c), OFF_SMALL // LANES)),
            pl.BlockSpec((CONV_W, CONV_DIM), lambda b, c: (0, 0)),
            pl.BlockSpec((8, LANES), lambda b, c: (0, 0)),
            pl.BlockSpec((1, DV_A), lambda b, c: (0, 0)),
            pl.BlockSpec((None, CONV_W - 1, CONV_DIM), lambda b, c: (b, 0, 0)),
            pl.BlockSpec((None, N_HEADS, DK_A, DV_A), lambda b, c: (b, 0, 0, 0)),
        ],
        out_specs=(
            pl.BlockSpec((chunk, W_A), lambda b, c: (b * n_chunks + c, 0)),
            pl.BlockSpec((None, N_HEADS, DK_A, DV_A), lambda b, c: (b, 0, 0, 0)),
        ),
        scratch_shapes=[pltpu.VMEM((N_HEADS, DK_A, DV_A), F32),
                        pltpu.VMEM((chunk + 8, CONV_DIM), F32)],
        compiler_params=pltpu.CompilerParams(
            dimension_semantics=("arbitrary", "arbitrary"), vmem_limit_bytes=VMEM_LIMIT),
        name=f"gdn_c{chunk}",
    )(p, p, p, conv_w, hp, norm_w.reshape(1, DV_A), s_conv, s0)


def _gla_kernel(qk_ref, v_ref, g_ref, sm_ref, wgk_ref, bgk_ref, nw_ref, s0_ref,
                o_ref, sout_ref, st_ref, *, tb, n_blocks):
    t = pl.program_id(1)
    cs = math.gcd(tb, GLA_CHUNK)
    n_sub = tb // cs
    eye_k = jnp.where(lax.broadcasted_iota(jnp.int32, (DK_B, DK_B), 0)
                      == lax.broadcasted_iota(jnp.int32, (DK_B, DK_B), 1), 1.0, 0.0).astype(BF16)

    @pl.when(t == 0)
    def _():
        for h in range(N_HEADS):
            hi, mid, lo = _split3(s0_ref[h])
            tr = lambda p: lax.dot_general(p, eye_k, (((0,), (0,)), ((), ())), preferred_element_type=F32)
            st_ref[h] = tr(hi) + tr(mid) + tr(lo)

    gk = _dot(sm_ref[...], wgk_ref[...]) + bgk_ref[...]
    log_a = jax.nn.log_sigmoid(gk) / GLA_GATE_NORM
    r_i = lax.broadcasted_iota(jnp.int32, (tb, tb), 0)
    c_i = lax.broadcasted_iota(jnp.int32, (tb, tb), 1)
    sh = int(round(math.log2(cs)))
    same = (r_i >> sh) == (c_i >> sh)
    mask = same & (c_i <= r_i)
    gcum = _dot01(jnp.where(mask, 1.0, 0.0).astype(BF16), log_a)
    gtot = _dot01(jnp.where(same, 1.0, 0.0).astype(BF16), log_a)

    qk = qk_ref[...]
    nw = nw_ref[...]
    w_kb = N_HEADS * DK_B
    for h in range(N_HEADS):
        qh = qk[:, h * DK_B:(h + 1) * DK_B] * (DK_B ** -0.5)
        kh = qk[:, w_kb + h * DK_B:w_kb + (h + 1) * DK_B]
        vh = v_ref[:, h * DV_B:(h + 1) * DV_B]
        gh = gcum[:, h * DK_B:(h + 1) * DK_B]
        gt = gtot[:, h * DK_B:(h + 1) * DK_B]
        qg = qh * jnp.exp(gh)
        kg = kh * jnp.exp(-gh)
        attn = jnp.where(mask, _dot_nt(qg, kg), 0.0)
        o_intra = _dot(attn, vh)
        kd = kh * jnp.exp(gt - gh)
        gl = jnp.exp(gt)
        st = st_ref[h]
        o_inter = []
        for j in range(n_sub):
            sl = slice(j * cs, (j + 1) * cs)
            o_inter.append(_dot_nt(qg[sl], st))
            st = st * gl[j * cs:j * cs + 1, :] + _dot_tn(vh[sl], kd[sl])
        st_ref[h] = st
        o = o_intra + (o_inter[0] if n_sub == 1 else jnp.concatenate(o_inter, axis=0))
        r = o * lax.rsqrt(jnp.mean(o * o, axis=-1, keepdims=True) + RMS_EPS) * nw
        o_ref[:, h * DV_B:(h + 1) * DV_B] = r * _silu(g_ref[:, h * DV_B:(h + 1) * DV_B])

    @pl.when(t == n_blocks - 1)
    def _():
        for h in range(N_HEADS):
            hi, mid, lo = _split3(st_ref[h])
            tr = lambda p: lax.dot_general(eye_k, p, (((1,), (1,)), ((), ())), preferred_element_type=F32)
            sout_ref[h] = tr(hi) + tr(mid) + tr(lo)


def _gla(p, row0, batch, seq, wgk, bgk, norm_w, s0):
    tb = math.gcd(seq, GLA_TB)
    n_blocks = seq // tb
    blk0 = row0 // tb
    rows = lambda b, t: blk0 + b * n_blocks + t
    kern = functools.partial(_gla_kernel, tb=tb, n_blocks=n_blocks)
    return pl.pallas_call(
        kern,
        out_shape=(jax.ShapeDtypeStruct((batch * seq, W_B), F32),
                   jax.ShapeDtypeStruct((batch, N_HEADS, DK_B, DV_B), F32)),
        grid=(batch, n_blocks),
        in_specs=[
            pl.BlockSpec((tb, 2 * N_HEADS * DK_B), lambda b, t: (rows(b, t), OFF_QK_B // (2 * N_HEADS * DK_B))),
            pl.BlockSpec((tb, W_B), lambda b, t: (rows(b, t), OFF_V_B // W_B)),
            pl.BlockSpec((tb, W_B), lambda b, t: (rows(b, t), OFF_G_B // W_B)),
            pl.BlockSpec((tb, LANES), lambda b, t: (rows(b, t), OFF_SMALL // LANES)),
            pl.BlockSpec((LANES, N_HEADS * DK_B), lambda b, t: (0, 0)),
            pl.BlockSpec((1, N_HEADS * DK_B), lambda b, t: (0, 0)),
            pl.BlockSpec((1, DV_B), lambda b, t: (0, 0)),
            pl.BlockSpec((None, N_HEADS, DK_B, DV_B), lambda b, t: (b, 0, 0, 0)),
        ],
        out_specs=(
            pl.BlockSpec((tb, W_B), lambda b, t: (b * n_blocks + t, 0)),
            pl.BlockSpec((None, N_HEADS, DK_B, DV_B), lambda b, t: (b, 0, 0, 0)),
        ),
        scratch_shapes=[pltpu.VMEM((N_HEADS, DV_B, DK_B), F32)],
        compiler_params=pltpu.CompilerParams(
            dimension_semantics=("arbitrary", "arbitrary"), vmem_limit_bytes=VMEM_LIMIT),
        name=f"gla_t{tb}",
    )(p, p, p, p, wgk, bgk.reshape(1, N_HEADS * DK_B), norm_w.reshape(1, DV_B), s0)


def _merge_kernel(x_ref, oa_ref, ob_ref, ga_ref, gb_ref, wua_ref, wub_ref, wo_ref, n2_ref,
                  wr_ref, br_ref, x1_ref, h2_ref, te_ref, tg_ref):
    ua = jnp.dot(oa_ref[...].astype(BF16), wua_ref[...], preferred_element_type=F32)
    ub = jnp.dot(ob_ref[...].astype(BF16), wub_ref[...], preferred_element_type=F32)
    merged = jax.nn.sigmoid(ga_ref[...]) * ua + jax.nn.sigmoid(gb_ref[...]) * ub
    x1 = x_ref[...] + jnp.dot(merged.astype(BF16), wo_ref[...], preferred_element_type=F32)
    x1_ref[...] = x1
    h2 = x1 * lax.rsqrt(jnp.mean(x1 * x1, axis=-1, keepdims=True) + RMS_EPS) * n2_ref[...]
    h2_ref[...] = h2
    logits = jnp.dot(h2, wr_ref[...], precision=lax.Precision.HIGHEST,
                     preferred_element_type=F32) + br_ref[...]
    tm = logits.shape[0]
    lane = lax.broadcasted_iota(jnp.int32, (tm, LANES), 1)
    lane_f = lane.astype(F32)
    neg = jnp.float32(-jnp.inf)
    cur = jnp.where(lane < N_EXPERTS, logits, neg)
    vals, idxs = [], []
    for _ in range(TOP_K):
        m = jnp.max(cur, axis=-1, keepdims=True)
        idx = jnp.min(jnp.where(cur == m, lane_f, float(LANES)), axis=-1, keepdims=True)
        vals.append(m)
        idxs.append(idx)
        cur = jnp.where(lane_f == idx, neg, cur)
    exps = [jnp.exp(v - vals[0]) for v in vals]
    den = exps[0] + exps[1] + exps[2] + exps[3]
    te = jnp.zeros((tm, LANES), F32)
    tg = jnp.zeros((tm, LANES), F32)
    for k in range(TOP_K):
        te = jnp.where(lane == k, idxs[k], te)
        tg = jnp.where(lane == k, exps[k] / den, tg)
    te_ref[...] = te.astype(jnp.int32)
    tg_ref[...] = tg


def _merge(x, o_a, o_b, p, wua, wub, wo, norm2_w, wr, br):
    n = x.shape[0]
    tm = MERGE_TM
    row = lambda i: (i, 0)
    const = lambda i: (0, 0)
    return pl.pallas_call(
        _merge_kernel,
        out_shape=(jax.ShapeDtypeStruct((n, D_MODEL), F32), jax.ShapeDtypeStruct((n, D_MODEL), F32),
                   jax.ShapeDtypeStruct((n, LANES), jnp.int32), jax.ShapeDtypeStruct((n, LANES), F32)),
        grid=(n // tm,),
        in_specs=[
            pl.BlockSpec((tm, D_MODEL), row),
            pl.BlockSpec((tm, W_A), row),
            pl.BlockSpec((tm, W_B), row),
            pl.BlockSpec((tm, D_MODEL), lambda i: (i, OFF_GATE_A // D_MODEL)),
            pl.BlockSpec((tm, D_MODEL), lambda i: (i, OFF_GATE_B // D_MODEL)),
            pl.BlockSpec((W_A, D_MODEL), const),
            pl.BlockSpec((W_B, D_MODEL), const),
            pl.BlockSpec((D_MODEL, D_MODEL), const),
            pl.BlockSpec((1, D_MODEL), const),
            pl.BlockSpec((D_MODEL, LANES), const),
            pl.BlockSpec((1, LANES), const),
        ],
        out_specs=(pl.BlockSpec((tm, D_MODEL), row), pl.BlockSpec((tm, D_MODEL), row),
                   pl.BlockSpec((tm, LANES), row), pl.BlockSpec((tm, LANES), row)),
        compiler_params=pltpu.CompilerParams(
            dimension_semantics=("arbitrary",), vmem_limit_bytes=VMEM_LIMIT),
        name="merge_router",
    )(x, o_a, o_b, p, p, wua, wub, wo, norm2_w.reshape(1, D_MODEL), wr, br)


def _moe_kernel(blk_e_ref, n_used_ref, src_ref, src_nxt_ref, dst_ref, h2_hbm, wgu_ref, bgu_ref,
                wdn_ref, bdn_ref, y_hbm, xbuf, ybuf, gsem, ssem):
    del blk_e_ref
    i = pl.program_id(0)
    n_used = n_used_ref[0]
    slot = lax.rem(i, 2)
    bm = MOE_BM

    def gather_copy(tok, s, r):
        return pltpu.make_async_copy(h2_hbm.at[pl.ds(tok, 1), :], xbuf.at[s, pl.ds(r, 1), :], gsem.at[s])

    def scatter_copy(row, s, r):
        return pltpu.make_async_copy(ybuf.at[s, pl.ds(r, 1), :], y_hbm.at[pl.ds(row, 1), :], ssem.at[s])

    @pl.when(i == 0)
    def _():
        ybuf[...] = jnp.zeros_like(ybuf)
        n_real = y_hbm.shape[0] - 2 * bm
        for s in range(2):
            sink = pltpu.make_async_copy(ybuf.at[s], y_hbm.at[pl.ds(n_real + s * bm, bm), :], ssem.at[s])
            sink.start()
            sink.wait()

    @pl.when((i == 0) & (n_used > 0))
    def _():
        for r in range(bm):
            gather_copy(src_ref[0, 0, r], 0, r).start()

    @pl.when(i + 1 < n_used)
    def _():
        for r in range(bm):
            gather_copy(src_nxt_ref[0, 0, r], 1 - slot, r).start()

    @pl.when(i < n_used)
    def _():
        for r in range(bm):
            gather_copy(0, slot, r).wait()

        @pl.when(i >= 2)
        def _():
            for r in range(bm):
                scatter_copy(0, slot, r).wait()

        x = xbuf[slot].astype(BF16)
        gu = jnp.dot(x, wgu_ref[...], preferred_element_type=F32) + bgu_ref[...]
        gate = jnp.minimum(gu[:, :D_FF], SWIGLU_LIMIT)
        up = jnp.clip(gu[:, D_FF:], -SWIGLU_LIMIT, SWIGLU_LIMIT)
        hmid = (up + 1.0) * (gate * jax.nn.sigmoid(SWIGLU_ALPHA * gate))
        ybuf[slot] = jnp.dot(hmid.astype(BF16), wdn_ref[...], preferred_element_type=F32) + bdn_ref[...]
        for r in range(bm):
            scatter_copy(dst_ref[0, 0, r], slot, r).start()

    @pl.when((i == n_used - 1) & (n_used >= 2))
    def _():
        for r in range(bm):
            scatter_copy(0, 1 - slot, r).wait()

    @pl.when(i == n_used - 1)
    def _():
        for r in range(bm):
            scatter_copy(0, slot, r).wait()


def _moe(blk_e, n_used, src_tok, dst_row, h2, wgu, bgu, wdn, bdn, y_rows):
    n_blocks = src_tok.shape[0]
    bm = MOE_BM
    cur = lambda i, be, nu: (i, 0, 0)
    nxt = lambda i, be, nu: (jnp.minimum(i + 1, n_blocks - 1), 0, 0)
    wsel = lambda i, be, nu: (be[i], 0, 0)
    smem_blk = lambda im: pl.BlockSpec((1, 1, bm), im, memory_space=pltpu.SMEM)
    grid_spec = pltpu.PrefetchScalarGridSpec(
        num_scalar_prefetch=2,
        grid=(n_blocks,),
        in_specs=[
            smem_blk(cur), smem_blk(nxt), smem_blk(cur),
            pl.BlockSpec(memory_space=pl.ANY),
            pl.BlockSpec((None, D_MODEL, 2 * D_FF), wsel),
            pl.BlockSpec((None, 1, 2 * D_FF), wsel),
            pl.BlockSpec((None, D_FF, D_MODEL), wsel),
            pl.BlockSpec((None, 1, D_MODEL), wsel),
        ],
        out_specs=pl.BlockSpec(memory_space=pl.ANY),
        scratch_shapes=[pltpu.VMEM((2, bm, D_MODEL), F32), pltpu.VMEM((2, bm, D_MODEL), F32),
                        pltpu.SemaphoreType.DMA((2,)), pltpu.SemaphoreType.DMA((2,))],
    )
    return pl.pallas_call(
        _moe_kernel,
        out_shape=jax.ShapeDtypeStruct((y_rows, D_MODEL), F32),
        grid_spec=grid_spec,
        compiler_params=pltpu.CompilerParams(
            dimension_semantics=("arbitrary",), vmem_limit_bytes=VMEM_LIMIT),
        name="moe_experts",
    )(blk_e, n_used, src_tok, src_tok, dst_row, h2, wgu, bgu, wdn, bdn)


def _route(top_e, n_tok):
    bm = MOE_BM
    n_assign = n_tok * TOP_K
    n_blocks = n_assign // bm + N_EXPERTS
    n_dummy = N_EXPERTS * bm
    flat_e = top_e.reshape(-1)
    counts = jnp.sum((flat_e[:, None] == jnp.arange(N_EXPERTS, dtype=jnp.int32)[None, :]).astype(jnp.int32), axis=0)
    padded = (counts + bm - 1) // bm * bm
    pad_end = jnp.cumsum(padded)
    d_e = jnp.arange(n_dummy, dtype=jnp.int32) // bm
    d_active = (jnp.arange(n_dummy, dtype=jnp.int32) % bm) < (padded - counts)[d_e]
    keys = jnp.concatenate([flat_e * 2, jnp.where(d_active, d_e * 2 + 1, 2 * N_EXPERTS)])
    vals = jnp.concatenate([jnp.arange(n_assign, dtype=jnp.int32), jnp.full((n_dummy,), -1, jnp.int32)])
    _, assign = lax.sort((keys, vals), num_keys=1, is_stable=True)
    assign = assign[:n_blocks * bm]
    rows = jnp.arange(n_blocks * bm, dtype=jnp.int32)
    valid = assign >= 0
    src_tok = jnp.where(valid, assign // TOP_K, 0)
    dst_row = jnp.where(valid, assign, n_assign + ((rows // bm) % 2) * bm + rows % bm)
    blk_e = jnp.minimum(jnp.searchsorted(pad_end, jnp.arange(n_blocks, dtype=jnp.int32) * bm, side="right"),
                        N_EXPERTS - 1).astype(jnp.int32)
    n_used = (pad_end[-1] // bm).astype(jnp.int32).reshape(1)
    return (blk_e, n_used, src_tok.reshape(n_blocks, 1, bm).astype(jnp.int32),
            dst_row.reshape(n_blocks, 1, bm).astype(jnp.int32))


def _combine_kernel(x1_ref, y4_ref, tg_ref, nf_ref, o_ref):
    tg = tg_ref[...]
    acc = y4_ref[:, 0:D_MODEL] * tg[:, 0:1]
    for k in range(1, TOP_K):
        acc = acc + y4_ref[:, k * D_MODEL:(k + 1) * D_MODEL] * tg[:, k:k + 1]
    x = x1_ref[...] + acc
    o_ref[...] = x * lax.rsqrt(jnp.mean(x * x, axis=-1, keepdims=True) + RMS_EPS) * nf_ref[...]


def _combine(x1, y4, tg, norm_f_w):
    n = x1.shape[0]
    tm = COMBINE_TM
    row = lambda i: (i, 0)
    return pl.pallas_call(
        _combine_kernel,
        out_shape=jax.ShapeDtypeStruct((n, D_MODEL), F32),
        grid=(n // tm,),
        in_specs=[pl.BlockSpec((tm, D_MODEL), row), pl.BlockSpec((tm, TOP_K * D_MODEL), row),
                  pl.BlockSpec((tm, LANES), row), pl.BlockSpec((1, D_MODEL), lambda i: (0, 0))],
        out_specs=pl.BlockSpec((tm, D_MODEL), row),
        compiler_params=pltpu.CompilerParams(
            dimension_semantics=("arbitrary",), vmem_limit_bytes=VMEM_LIMIT),
        name="combine_norm",
    )(x1, y4, tg, norm_f_w.reshape(1, D_MODEL))


def _pack_w_in(w_in):
    s = np.cumsum((0,) + IN_SIZES)
    qkv, z, al, be, qb, kb, vb, gb, gk, ga, gbt = [w_in[:, s[i]:s[i + 1]] for i in range(len(IN_SIZES))]
    small = jnp.concatenate(
        [al, be, gk, jnp.zeros((D_MODEL, LANES - 2 * N_HEADS - GK_RANK), w_in.dtype)], axis=1)
    return jnp.concatenate([qkv, z, ga, gbt, qb, kb, vb, gb, small], axis=1).astype(BF16)


def kernel(x_prompt, x_sample, state_gdn, state_gdn_conv, state_gla, norm1_w, w_in, conv_w, a_log,
           dt_bias, gdn_norm_w, gla_gk_up, gla_gk_b, gla_norm_w, w_up_a, w_up_b, w_o, norm2_w,
           w_router, b_router, w_gate_up, b_gate_up, w_down, b_down, norm_f_w):
    bp, tp, _ = x_prompt.shape
    bs, ts, _ = x_sample.shape
    n_p, n_s = bp * tp, bs * ts
    n = n_p + n_s
    l = 0

    x = jnp.concatenate([x_prompt.reshape(n_p, D_MODEL), x_sample.reshape(n_s, D_MODEL)], axis=0)
    p = _inproj(x, norm1_w[l], _pack_w_in(w_in[l]))

    hp = jnp.zeros((8, LANES), F32).at[0, :N_HEADS].set(a_log[l]).at[1, :N_HEADS].set(dt_bias[l])
    wgk = jnp.zeros((LANES, N_HEADS * DK_B), F32).at[2 * N_HEADS:2 * N_HEADS + GK_RANK].set(gla_gk_up[l]).astype(BF16)
    zeros = lambda shape: jnp.zeros(shape, F32)

    oa_p, gdn_p = _gdn(p, 0, bp, tp, conv_w[l], hp, gdn_norm_w[l],
                       zeros((bp, CONV_W - 1, CONV_DIM)), zeros((bp, N_HEADS, DK_A, DV_A)))
    oa_s, gdn_s = _gdn(p, n_p, bs, ts, conv_w[l], hp, gdn_norm_w[l], state_gdn_conv[l], state_gdn[l])
    ob_p, gla_p = _gla(p, 0, bp, tp, wgk, gla_gk_b[l], gla_norm_w[l], zeros((bp, N_HEADS, DK_B, DV_B)))
    ob_s, gla_s = _gla(p, n_p, bs, ts, wgk, gla_gk_b[l], gla_norm_w[l], state_gla[l])
    o_a = jnp.concatenate([oa_p, oa_s], axis=0)
    o_b = jnp.concatenate([ob_p, ob_s], axis=0)

    wr = jnp.zeros((D_MODEL, LANES), F32).at[:, :N_EXPERTS].set(w_router[l])
    br = jnp.zeros((1, LANES), F32).at[0, :N_EXPERTS].set(b_router[l])
    x1, h2, te, tg = _merge(x, o_a, o_b, p, w_up_a[l].astype(BF16), w_up_b[l].astype(BF16),
                            w_o[l].astype(BF16), norm2_w[l], wr, br)

    blk_e, n_used, src_tok, dst_row = _route(te[:, :TOP_K], n)
    pad_tok = 2 * MOE_BM // TOP_K
    y = _moe(blk_e, n_used, src_tok, dst_row, h2, w_gate_up[l].astype(BF16),
             b_gate_up[l].reshape(N_EXPERTS, 1, 2 * D_FF), w_down[l].astype(BF16),
             b_down[l].reshape(N_EXPERTS, 1, D_MODEL), (n + pad_tok) * TOP_K)
    out = _combine(x1, y.reshape(n + pad_tok, TOP_K * D_MODEL), tg, norm_f_w)

    y_prompt = out[:n_p].reshape(bp, tp, D_MODEL)
    y_sample = out[n_p:].reshape(bs, ts, D_MODEL)
    qkv_p = p[:n_p, :CONV_DIM].reshape(bp, tp, CONV_DIM)
    qkv_s = p[n_p:, :CONV_DIM].reshape(bs, ts, CONV_DIM)
    conv_p = qkv_p[:, tp - (CONV_W - 1):, :]
    conv_s = qkv_s[:, ts - (CONV_W - 1):, :]
    return (y_prompt, y_sample, gdn_p[None], conv_p[None], gla_p[None], gdn_s[None], conv_s[None],
            gla_s[None])
```

```python
import functools
import math

import jax
import jax.numpy as jnp
import numpy as np
from jax import lax
from jax.experimental import pallas as pl
from jax.experimental.pallas import tpu as pltpu

F32 = jnp.float32
BF16 = jnp.bfloat16

D_MODEL = 1024
N_HEADS = 4
DK_A = 128
DV_A = 128
W_A = N_HEADS * DV_A
CONV_W = 4
CONV_DIM = 2 * N_HEADS * DK_A + W_A
GDN_CHUNK = 64
DK_B = 64
DV_B = 128
W_B = N_HEADS * DV_B
GK_RANK = 16
GLA_GATE_NORM = 16.0
GLA_CHUNK = 16
N_EXPERTS = 32
TOP_K = 4
D_FF = 1024
SWIGLU_LIMIT = 7.0
SWIGLU_ALPHA = 1.702
RMS_EPS = 1e-6
IN_SIZES = (CONV_DIM, W_A, N_HEADS, N_HEADS, N_HEADS * DK_B, N_HEADS * DK_B, W_B, W_B, GK_RANK,
            D_MODEL, D_MODEL)

LANES = 128

OFF_QKV = 0
OFF_Z = 1536
OFF_GATE_A = 2048
OFF_GATE_B = 3072
OFF_QK_B = 4096
OFF_V_B = 4608
OFF_G_B = 5120
OFF_SMALL = 5632
P_COLS = 5760

INPROJ_TM = 512
INPROJ_TN = 1920
GLA_TB = 128
GDN_SEGS_CARRY = 4
GDN_SEGS_INDEP = 8
MERGE_TM = 512
MOE_BM = 256
COMBINE_TM = 256
VMEM_LIMIT = 56 * 1024 * 1024


def _dot(a, b):
    return jnp.dot(a.astype(BF16), b.astype(BF16), preferred_element_type=F32)


def _dot_nt(a, b):
    return lax.dot_general(a.astype(BF16), b.astype(BF16), (((1,), (1,)), ((), ())),
                           preferred_element_type=F32)


def _dot_tn(a, b):
    return lax.dot_general(a, b, (((0,), (0,)), ((), ())), preferred_element_type=F32)


def _split3(x):
    hi = x.astype(BF16)
    r1 = x - hi.astype(F32)
    mid = r1.astype(BF16)
    lo = (r1 - mid.astype(F32)).astype(BF16)
    return hi, mid, lo


def _dot01(m01, x):
    hi, mid, lo = _split3(x)
    d = lambda p: jnp.dot(m01, p, preferred_element_type=F32)
    return d(hi) + d(mid) + d(lo)


def _silu(x):
    return x * jax.nn.sigmoid(x)


def _inproj_kernel(x_ref, nw_ref, w_ref, o_ref, h_ref):
    @pl.when(pl.program_id(1) == 0)
    def _():
        x = x_ref[...]
        ms = jnp.mean(x * x, axis=-1, keepdims=True)
        h_ref[...] = (x * lax.rsqrt(ms + RMS_EPS) * nw_ref[...]).astype(BF16)

    o_ref[...] = jnp.dot(h_ref[...], w_ref[...], preferred_element_type=F32)


def _inproj(x, norm_w, w_packed):
    n = x.shape[0]
    grid = (n // INPROJ_TM, P_COLS // INPROJ_TN)
    return pl.pallas_call(
        _inproj_kernel,
        out_shape=jax.ShapeDtypeStruct((n, P_COLS), F32),
        grid=grid,
        in_specs=[
            pl.BlockSpec((INPROJ_TM, D_MODEL), lambda i, j: (i, 0)),
            pl.BlockSpec((1, D_MODEL), lambda i, j: (0, 0)),
            pl.BlockSpec((D_MODEL, INPROJ_TN), lambda i, j: (0, j)),
        ],
        out_specs=pl.BlockSpec((INPROJ_TM, INPROJ_TN), lambda i, j: (i, j)),
        scratch_shapes=[pltpu.VMEM((INPROJ_TM, D_MODEL), BF16)],
        compiler_params=pltpu.CompilerParams(
            dimension_semantics=("arbitrary", "arbitrary"), vmem_limit_bytes=VMEM_LIMIT),
        name="inproj",
    )(x, norm_w.reshape(1, D_MODEL), w_packed)


def _gdn_kernel(qkv_ref, z_ref, sm_ref, cw_ref, hp_ref, nw_ref, sconv_ref, s0_ref,
                o_ref, sout_ref, s_ref, ubuf_ref, *, chunk, n_seg, carry, n_steps):
    c_len = chunk
    rows = n_seg * c_len
    t = pl.program_id(1)
    hist = CONV_W - 1
    base = 8
    conv_len = rows if carry else c_len
    n_conv = 1 if carry else n_seg

    if carry:
        @pl.when(t == 0)
        def _():
            s_ref[...] = s0_ref[...]
            ubuf_ref[:, base - hist:base, :] = sconv_ref[...]

        @pl.when(t > 0)
        def _():
            ubuf_ref[:, base - hist:base, :] = ubuf_ref[:, base + conv_len - hist:base + conv_len, :]
    else:
        ubuf_ref[:, base - hist:base, :] = sconv_ref[...]
    ubuf_ref[:, base:base + conv_len, :] = qkv_ref[...].reshape(n_conv, conv_len, CONV_DIM)

    cw = cw_ref[...]
    y = ubuf_ref[:, base - hist:base - hist + conv_len, :] * cw[0:1, :]
    for j in range(1, CONV_W):
        y = y + ubuf_ref[:, base - hist + j:base - hist + j + conv_len, :] * cw[j:j + 1, :]
    y = _silu(y).reshape(rows, CONV_DIM)

    sm = sm_ref[...]
    hp = hp_ref[...]
    lane = lax.broadcasted_iota(jnp.int32, (rows, LANES), 1)
    g_all = -jnp.exp(hp[0:1, :]) * jax.nn.softplus(sm + hp[1:2, :])
    g_all = jnp.where(lane < N_HEADS, g_all, 0.0)
    beta_all = jax.nn.sigmoid(sm)

    sh = int(round(math.log2(c_len)))
    r_b = lax.broadcasted_iota(jnp.int32, (rows, rows), 0)
    c_b = lax.broadcasted_iota(jnp.int32, (rows, rows), 1)
    seg_tril = ((r_b >> sh) == (c_b >> sh)) & (c_b <= r_b)
    gc = _dot01(jnp.where(seg_tril, 1.0, 0.0).astype(BF16), g_all)

    r_i = lax.broadcasted_iota(jnp.int32, (c_len, c_len), 0)
    c_i = lax.broadcasted_iota(jnp.int32, (c_len, c_len), 1)
    tri = c_i <= r_i
    strict = c_i < r_i
    eye = c_i == r_i
    nw = nw_ref[...]
    levels = sh

    qn, kn, kb, rhs, qg, gcol = [], [], [], [], [], []
    for h in range(N_HEADS):
        qh = y[:, h * DK_A:(h + 1) * DK_A]
        kh = y[:, W_A + h * DK_A:W_A + (h + 1) * DK_A]
        vh = y[:, 2 * W_A + h * DV_A:2 * W_A + (h + 1) * DV_A]
        qn_h = qh * lax.rsqrt(jnp.sum(qh * qh, axis=-1, keepdims=True) + 1e-6) * (DK_A ** -0.5)
        kn_h = kh * lax.rsqrt(jnp.sum(kh * kh, axis=-1, keepdims=True) + 1e-6)
        gcol_h = gc[:, h:h + 1]
        eg = jnp.exp(gcol_h)
        bcol = beta_all[:, N_HEADS + h:N_HEADS + h + 1]
        kb_h = kn_h * bcol
        qn.append(qn_h)
        kn.append(kn_h)
        kb.append(kb_h)
        rhs.append(jnp.concatenate([vh * bcol, kb_h * eg], axis=1))
        qg.append(qn_h * eg)
        gcol.append(gcol_h)

    pairs = [(g, h) for g in range(n_seg) for h in range(N_HEADS)]
    sl = lambda g: slice(g * c_len, (g + 1) * c_len)

    decay, kd, gl = {}, {}, {}
    for (g, h) in pairs:
        gcol_p = gcol[h][sl(g)]
        gcb = jnp.broadcast_to(gcol_p, (c_len, c_len))
        grow = jnp.sum(jnp.where(eye, gcb, 0.0), axis=0, keepdims=True)
        decay[g, h] = jnp.where(tri, jnp.exp(jnp.where(tri, gcb - grow, 0.0)), 0.0)
        glast = gcol[h][(g + 1) * c_len - 1:(g + 1) * c_len]
        kd[g, h] = kn[h][sl(g)] * jnp.exp(glast - gcol_p)
        gl[g, h] = jnp.exp(glast)

    kq = {p: _dot_nt(jnp.concatenate([kb[p[1]][sl(p[0])], qn[p[1]][sl(p[0])]], axis=0), kn[p[1]][sl(p[0])])
          for p in pairs}
    low = {p: jnp.where(strict, kq[p][:c_len] * decay[p], 0.0) for p in pairs}
    attn = {p: jnp.where(tri, kq[p][c_len:] * decay[p], 0.0) for p in pairs}
    ymat = {p: -low[p] for p in pairs}
    lp = low
    for _ in range(levels - 1):
        lp = {p: _dot(lp[p], lp[p]) for p in pairs}
        prod = {p: _dot(ymat[p], lp[p]) for p in pairs}
        ymat = {p: ymat[p] + lp[p] + prod[p] for p in pairs}
    sol = {}
    for (g, h) in pairs:
        rhs_p = rhs[h][sl(g)]
        sol[g, h] = rhs_p + _dot(ymat[g, h], rhs_p)

    def finish(g, h, o):
        r = o * lax.rsqrt(jnp.mean(o * o, axis=-1, keepdims=True) + RMS_EPS) * nw
        o_ref[sl(g), h * DV_A:(h + 1) * DV_A] = r * _silu(z_ref[sl(g), h * DV_A:(h + 1) * DV_A])

    def advance(group, state):
        ws = {p: _dot(jnp.concatenate([sol[p][:, DV_A:], qg[p[1]][sl(p[0])]], axis=0), state[p]) for p in group}
        v_new = {p: sol[p][:, :DV_A] - ws[p][:c_len] for p in group}
        o = {p: ws[p][c_len:] + _dot(attn[p], v_new[p]) for p in group}
        new = {p: state[p] * gl[p] + _dot_tn(kd[p], v_new[p]) for p in group}
        for p in group:
            finish(p[0], p[1], o[p])
        return new

    if carry:
        cur = {h: s_ref[0, h] for h in range(N_HEADS)}
        for g in range(n_seg):
            group = [(g, h) for h in range(N_HEADS)]
            new = advance(group, {(g, h): cur[h] for h in range(N_HEADS)})
            cur = {h: new[g, h] for h in range(N_HEADS)}
        for h in range(N_HEADS):
            s_ref[0, h] = cur[h]

        @pl.when(t == n_steps - 1)
        def _():
            for h in range(N_HEADS):
                sout_ref[0, h] = cur[h]
    else:
        new = advance(pairs, {p: s0_ref[p[0], p[1]] for p in pairs})
        for p in pairs:
            sout_ref[p[0], p[1]] = new[p]


def _gdn(p, row0, batch, seq, conv_w, hp, norm_w, s_conv, s0):
    chunk = math.gcd(seq, GDN_CHUNK)
    n_chunks = seq // chunk
    carry = n_chunks > 1
    if carry:
        n_seg = math.gcd(n_chunks, GDN_SEGS_CARRY)
        grid = (batch, n_chunks // n_seg)
        n_conv, conv_len = 1, n_seg * chunk
    else:
        n_seg = math.gcd(batch, GDN_SEGS_INDEP)
        grid = (batch // n_seg, 1)
        n_conv, conv_len = n_seg, chunk
    rows = n_seg * chunk
    n_steps = grid[1]
    blk0 = row0 // rows
    rowblk = lambda b, t: blk0 + b * n_steps + t
    kern = functools.partial(_gdn_kernel, chunk=chunk, n_seg=n_seg, carry=carry, n_steps=n_steps)
    return pl.pallas_call(
        kern,
        out_shape=(jax.ShapeDtypeStruct((batch * seq, W_A), F32),
                   jax.ShapeDtypeStruct((batch, N_HEADS, DK_A, DV_A), F32)),
        grid=grid,
        in_specs=[
            pl.BlockSpec((rows, CONV_DIM), lambda b, t: (rowblk(b, t), OFF_QKV // CONV_DIM)),
            pl.BlockSpec((rows, W_A), lambda b, t: (rowblk(b, t), OFF_Z // W_A)),
            pl.BlockSpec((rows, LANES), lambda b, t: (rowblk(b, t), OFF_SMALL // LANES)),
            pl.BlockSpec((CONV_W, CONV_DIM), lambda b, t: (0, 0)),
            pl.BlockSpec((8, LANES), lambda b, t: (0, 0)),
            pl.BlockSpec((1, DV_A), lambda b, t: (0, 0)),
            pl.BlockSpec((n_conv, CONV_W - 1, CONV_DIM), lambda b, t: (b, 0, 0)),
            pl.BlockSpec((n_conv, N_HEADS, DK_A, DV_A), lambda b, t: (b, 0, 0, 0)),
        ],
        out_specs=(
            pl.BlockSpec((rows, W_A), lambda b, t: (b * n_steps + t, 0)),
            pl.BlockSpec((n_conv, N_HEADS, DK_A, DV_A), lambda b, t: (b, 0, 0, 0)),
        ),
        scratch_shapes=[pltpu.VMEM((1, N_HEADS, DK_A, DV_A), F32),
                        pltpu.VMEM((n_conv, conv_len + 8, CONV_DIM), F32)],
        compiler_params=pltpu.CompilerParams(
            dimension_semantics=("arbitrary", "arbitrary"), vmem_limit_bytes=VMEM_LIMIT),
        name=f"gdn_c{chunk}",
    )(p, p, p, conv_w, hp, norm_w.reshape(1, DV_A), s_conv, s0)


def _gla_kernel(qk_ref, v_ref, g_ref, sm_ref, wgk_ref, bgk_ref, nw_ref, s0_ref,
                o_ref, sout_ref, s_ref, *, tb, cs, carry, n_steps):
    t = pl.program_id(1)
    n_sub = tb // cs
    w_kb = N_HEADS * DK_B

    if carry:
        @pl.when(t == 0)
        def _():
            s_ref[...] = s0_ref[...]

    gk = _dot(sm_ref[...], wgk_ref[...]) + bgk_ref[...]
    log_a = jax.nn.log_sigmoid(gk) / GLA_GATE_NORM
    sh = int(round(math.log2(cs)))
    r_i = lax.broadcasted_iota(jnp.int32, (tb, tb), 0)
    c_i = lax.broadcasted_iota(jnp.int32, (tb, tb), 1)
    same = (r_i >> sh) == (c_i >> sh)
    mask = same & (c_i <= r_i)
    gcum = _dot01(jnp.where(mask, 1.0, 0.0).astype(BF16), log_a)
    gtot = _dot01(jnp.where(same, 1.0, 0.0).astype(BF16), log_a)
    onehot = jnp.where((lax.broadcasted_iota(jnp.int32, (tb, n_sub), 0) >> sh)
                       == lax.broadcasted_iota(jnp.int32, (tb, n_sub), 1), 1.0, 0.0).astype(BF16)
    tn = lambda part: lax.dot_general(part, onehot, (((0,), (0,)), ((), ())), preferred_element_type=F32)
    hi, mid, lo = _split3(log_a)
    glcol = jnp.exp(tn(hi) + tn(mid) + tn(lo))

    qk = qk_ref[...]
    nw = nw_ref[...]
    heads = range(N_HEADS)
    sl = lambda j: slice(j * cs, (j + 1) * cs)
    qg, kd, vh, o_intra = [], [], [], []
    for h in heads:
        qh = qk[:, h * DK_B:(h + 1) * DK_B] * (DK_B ** -0.5)
        kh = qk[:, w_kb + h * DK_B:w_kb + (h + 1) * DK_B]
        gh = gcum[:, h * DK_B:(h + 1) * DK_B]
        gt = gtot[:, h * DK_B:(h + 1) * DK_B]
        qg.append(qh * jnp.exp(gh))
        kd.append(kh * jnp.exp(gt - gh))
        vh.append(v_ref[:, h * DV_B:(h + 1) * DV_B])
        attn = jnp.where(mask, _dot_nt(qg[h], kh * jnp.exp(-gh)), 0.0)
        o_intra.append(_dot(attn, vh[h]))
    upd = {(j, h): _dot_tn(kd[h][sl(j)], vh[h][sl(j)]) for j in range(n_sub) for h in heads}
    gl = lambda j, h: glcol[h * DK_B:(h + 1) * DK_B, j:j + 1]

    o_inter = [[] for _ in heads]
    if carry:
        cur = [s_ref[0, h] for h in heads]
        for j in range(n_sub):
            for h in heads:
                o_inter[h].append(_dot(qg[h][sl(j)], cur[h]))
                cur[h] = cur[h] * gl(j, h) + upd[j, h]
        for h in heads:
            s_ref[0, h] = cur[h]

        @pl.when(t == n_steps - 1)
        def _():
            for h in heads:
                sout_ref[0, h] = cur[h]
    else:
        for j in range(n_sub):
            for h in heads:
                s = s0_ref[j, h]
                o_inter[h].append(_dot(qg[h][sl(j)], s))
                sout_ref[j, h] = s * gl(j, h) + upd[j, h]

    for h in heads:
        o = o_intra[h] + jnp.concatenate(o_inter[h], axis=0)
        r = o * lax.rsqrt(jnp.mean(o * o, axis=-1, keepdims=True) + RMS_EPS) * nw
        o_ref[:, h * DV_B:(h + 1) * DV_B] = r * _silu(g_ref[:, h * DV_B:(h + 1) * DV_B])


def _gla(p, row0, batch, seq, wgk, bgk, norm_w, s0):
    cs = math.gcd(seq, GLA_CHUNK)
    carry = seq > cs
    if carry:
        tb = math.gcd(seq, GLA_TB)
        grid = (batch, seq // tb)
        n_state = 1
    else:
        tb = math.gcd(batch * seq, GLA_TB)
        n_state = tb // cs
        grid = (batch // n_state, 1)
    n_steps = grid[1]
    blk0 = row0 // tb
    rowblk = lambda b, t: blk0 + b * n_steps + t
    kern = functools.partial(_gla_kernel, tb=tb, cs=cs, carry=carry, n_steps=n_steps)
    return pl.pallas_call(
        kern,
        out_shape=(jax.ShapeDtypeStruct((batch * seq, W_B), F32),
                   jax.ShapeDtypeStruct((batch, N_HEADS, DK_B, DV_B), F32)),
        grid=grid,
        in_specs=[
            pl.BlockSpec((tb, 2 * N_HEADS * DK_B), lambda b, t: (rowblk(b, t), OFF_QK_B // (2 * N_HEADS * DK_B))),
            pl.BlockSpec((tb, W_B), lambda b, t: (rowblk(b, t), OFF_V_B // W_B)),
            pl.BlockSpec((tb, W_B), lambda b, t: (rowblk(b, t), OFF_G_B // W_B)),
            pl.BlockSpec((tb, LANES), lambda b, t: (rowblk(b, t), OFF_SMALL // LANES)),
            pl.BlockSpec((LANES, N_HEADS * DK_B), lambda b, t: (0, 0)),
            pl.BlockSpec((1, N_HEADS * DK_B), lambda b, t: (0, 0)),
            pl.BlockSpec((1, DV_B), lambda b, t: (0, 0)),
            pl.BlockSpec((n_state, N_HEADS, DK_B, DV_B), lambda b, t: (b, 0, 0, 0)),
        ],
        out_specs=(
            pl.BlockSpec((tb, W_B), lambda b, t: (b * n_steps + t, 0)),
            pl.BlockSpec((n_state, N_HEADS, DK_B, DV_B), lambda b, t: (b, 0, 0, 0)),
        ),
        scratch_shapes=[pltpu.VMEM((1, N_HEADS, DK_B, DV_B), F32)],
        compiler_params=pltpu.CompilerParams(
            dimension_semantics=("arbitrary", "arbitrary"), vmem_limit_bytes=VMEM_LIMIT),
        name=f"gla_c{cs}",
    )(p, p, p, p, wgk, bgk.reshape(1, N_HEADS * DK_B), norm_w.reshape(1, DV_B), s0)


def _merge_kernel(x_ref, oa_ref, ob_ref, ga_ref, gb_ref, wua_ref, wub_ref, wo_ref, n2_ref,
                  wr_ref, br_ref, x1_ref, h2_ref, te_ref, tg_ref):
    ua = jnp.dot(oa_ref[...].astype(BF16), wua_ref[...], preferred_element_type=F32)
    ub = jnp.dot(ob_ref[...].astype(BF16), wub_ref[...], preferred_element_type=F32)
    merged = jax.nn.sigmoid(ga_ref[...]) * ua + jax.nn.sigmoid(gb_ref[...]) * ub
    x1 = x_ref[...] + jnp.dot(merged.astype(BF16), wo_ref[...], preferred_element_type=F32)
    x1_ref[...] = x1
    h2 = x1 * lax.rsqrt(jnp.mean(x1 * x1, axis=-1, keepdims=True) + RMS_EPS) * n2_ref[...]
    h2_ref[...] = h2
    logits = jnp.dot(h2, wr_ref[...], precision=lax.Precision.HIGHEST,
                     preferred_element_type=F32) + br_ref[...]
    tm = logits.shape[0]
    lane = lax.broadcasted_iota(jnp.int32, (tm, LANES), 1)
    lane_f = lane.astype(F32)
    neg = jnp.float32(-jnp.inf)
    cur = jnp.where(lane < N_EXPERTS, logits, neg)
    vals, idxs = [], []
    for _ in range(TOP_K):
        m = jnp.max(cur, axis=-1, keepdims=True)
        idx = jnp.min(jnp.where(cur == m, lane_f, float(LANES)), axis=-1, keepdims=True)
        vals.append(m)
        idxs.append(idx)
        cur = jnp.where(lane_f == idx, neg, cur)
    exps = [jnp.exp(v - vals[0]) for v in vals]
    den = exps[0] + exps[1] + exps[2] + exps[3]
    te = jnp.zeros((tm, LANES), F32)
    tg = jnp.zeros((tm, LANES), F32)
    for k in range(TOP_K):
        te = jnp.where(lane == k, idxs[k], te)
        tg = jnp.where(lane == k, exps[k] / den, tg)
    te_ref[...] = te.astype(jnp.int32)
    tg_ref[...] = tg


def _merge(x, o_a, o_b, p, wua, wub, wo, norm2_w, wr, br):
    n = x.shape[0]
    tm = MERGE_TM
    row = lambda i: (i, 0)
    const = lambda i: (0, 0)
    return pl.pallas_call(
        _merge_kernel,
        out_shape=(jax.ShapeDtypeStruct((n, D_MODEL), F32), jax.ShapeDtypeStruct((n, D_MODEL), F32),
                   jax.ShapeDtypeStruct((n, LANES), jnp.int32), jax.ShapeDtypeStruct((n, LANES), F32)),
        grid=(n // tm,),
        in_specs=[
            pl.BlockSpec((tm, D_MODEL), row),
            pl.BlockSpec((tm, W_A), row),
            pl.BlockSpec((tm, W_B), row),
            pl.BlockSpec((tm, D_MODEL), lambda i: (i, OFF_GATE_A // D_MODEL)),
            pl.BlockSpec((tm, D_MODEL), lambda i: (i, OFF_GATE_B // D_MODEL)),
            pl.BlockSpec((W_A, D_MODEL), const),
            pl.BlockSpec((W_B, D_MODEL), const),
            pl.BlockSpec((D_MODEL, D_MODEL), const),
            pl.BlockSpec((1, D_MODEL), const),
            pl.BlockSpec((D_MODEL, LANES), const),
            pl.BlockSpec((1, LANES), const),
        ],
        out_specs=(pl.BlockSpec((tm, D_MODEL), row), pl.BlockSpec((tm, D_MODEL), row),
                   pl.BlockSpec((tm, LANES), row), pl.BlockSpec((tm, LANES), row)),
        compiler_params=pltpu.CompilerParams(
            dimension_semantics=("arbitrary",), vmem_limit_bytes=VMEM_LIMIT),
        name="merge_router",
    )(x, o_a, o_b, p, p, wua, wub, wo, norm2_w.reshape(1, D_MODEL), wr, br)


def _moe_kernel(blk_e_ref, n_used_ref, src_ref, src_nxt_ref, dst_ref, h2_hbm, wgu_ref, bgu_ref,
                wdn_ref, bdn_ref, y_hbm, xbuf, ybuf, gsem, ssem):
    del blk_e_ref
    i = pl.program_id(0)
    n_used = n_used_ref[0]
    slot = lax.rem(i, 2)
    bm = MOE_BM

    def gather_copy(tok, s, r):
        return pltpu.make_async_copy(h2_hbm.at[pl.ds(tok, 1), :], xbuf.at[s, pl.ds(r, 1), :], gsem.at[s])

    def scatter_copy(row, s, r):
        return pltpu.make_async_copy(ybuf.at[s, pl.ds(r, 1), :], y_hbm.at[pl.ds(row, 1), :], ssem.at[s])

    @pl.when(i == 0)
    def _():
        ybuf[...] = jnp.zeros_like(ybuf)
        n_real = y_hbm.shape[0] - 2 * bm
        for s in range(2):
            sink = pltpu.make_async_copy(ybuf.at[s], y_hbm.at[pl.ds(n_real + s * bm, bm), :], ssem.at[s])
            sink.start()
            sink.wait()

    @pl.when((i == 0) & (n_used > 0))
    def _():
        for r in range(bm):
            gather_copy(src_ref[0, 0, r], 0, r).start()

    @pl.when(i + 1 < n_used)
    def _():
        for r in range(bm):
            gather_copy(src_nxt_ref[0, 0, r], 1 - slot, r).start()

    @pl.when(i < n_used)
    def _():
        for r in range(bm):
            gather_copy(0, slot, r).wait()

        @pl.when(i >= 2)
        def _():
            for r in range(bm):
                scatter_copy(0, slot, r).wait()

        x = xbuf[slot].astype(BF16)
        gu = jnp.dot(x, wgu_ref[...], preferred_element_type=F32) + bgu_ref[...]
        gate = jnp.minimum(gu[:, :D_FF], SWIGLU_LIMIT)
        up = jnp.clip(gu[:, D_FF:], -SWIGLU_LIMIT, SWIGLU_LIMIT)
        hmid = (up + 1.0) * (gate * jax.nn.sigmoid(SWIGLU_ALPHA * gate))
        ybuf[slot] = jnp.dot(hmid.astype(BF16), wdn_ref[...], preferred_element_type=F32) + bdn_ref[...]
        for r in range(bm):
            scatter_copy(dst_ref[0, 0, r], slot, r).start()

    @pl.when((i == n_used - 1) & (n_used >= 2))
    def _():
        for r in range(bm):
            scatter_copy(0, 1 - slot, r).wait()

    @pl.when(i == n_used - 1)
    def _():
        for r in range(bm):
            scatter_copy(0, slot, r).wait()


def _moe(blk_e, n_used, src_tok, dst_row, h2, wgu, bgu, wdn, bdn, y_rows):
    n_blocks = src_tok.shape[0]
    bm = MOE_BM
    cur = lambda i, be, nu: (i, 0, 0)
    nxt = lambda i, be, nu: (jnp.minimum(i + 1, n_blocks - 1), 0, 0)
    wsel = lambda i, be, nu: (be[i], 0, 0)
    smem_blk = lambda im: pl.BlockSpec((1, 1, bm), im, memory_space=pltpu.SMEM)
    grid_spec = pltpu.PrefetchScalarGridSpec(
        num_scalar_prefetch=2,
        grid=(n_blocks,),
        in_specs=[
            smem_blk(cur), smem_blk(nxt), smem_blk(cur),
            pl.BlockSpec(memory_space=pl.ANY),
            pl.BlockSpec((None, D_MODEL, 2 * D_FF), wsel),
            pl.BlockSpec((None, 1, 2 * D_FF), wsel),
            pl.BlockSpec((None, D_FF, D_MODEL), wsel),
            pl.BlockSpec((None, 1, D_MODEL), wsel),
        ],
        out_specs=pl.BlockSpec(memory_space=pl.ANY),
        scratch_shapes=[pltpu.VMEM((2, bm, D_MODEL), F32), pltpu.VMEM((2, bm, D_MODEL), F32),
                        pltpu.SemaphoreType.DMA((2,)), pltpu.SemaphoreType.DMA((2,))],
    )
    return pl.pallas_call(
        _moe_kernel,
        out_shape=jax.ShapeDtypeStruct((y_rows, D_MODEL), F32),
        grid_spec=grid_spec,
        compiler_params=pltpu.CompilerParams(
            dimension_semantics=("arbitrary",), vmem_limit_bytes=VMEM_LIMIT),
        name="moe_experts",
    )(blk_e, n_used, src_tok, src_tok, dst_row, h2, wgu, bgu, wdn, bdn)


def _route(top_e, n_tok):
    bm = MOE_BM
    n_assign = n_tok * TOP_K
    n_blocks = n_assign // bm + N_EXPERTS
    n_dummy = N_EXPERTS * bm
    flat_e = top_e.reshape(-1)
    counts = jnp.sum((flat_e[:, None] == jnp.arange(N_EXPERTS, dtype=jnp.int32)[None, :]).astype(jnp.int32), axis=0)
    padded = (counts + bm - 1) // bm * bm
    pad_end = jnp.cumsum(padded)
    d_e = jnp.arange(n_dummy, dtype=jnp.int32) // bm
    d_active = (jnp.arange(n_dummy, dtype=jnp.int32) % bm) < (padded - counts)[d_e]
    keys = jnp.concatenate([flat_e * 2, jnp.where(d_active, d_e * 2 + 1, 2 * N_EXPERTS)])
    vals = jnp.concatenate([jnp.arange(n_assign, dtype=jnp.int32), jnp.full((n_dummy,), -1, jnp.int32)])
    _, assign = lax.sort((keys, vals), num_keys=1, is_stable=True)
    assign = assign[:n_blocks * bm]
    rows = jnp.arange(n_blocks * bm, dtype=jnp.int32)
    valid = assign >= 0
    src_tok = jnp.where(valid, assign // TOP_K, 0)
    dst_row = jnp.where(valid, (assign % TOP_K) * n_tok + assign // TOP_K,
                        n_assign + ((rows // bm) % 2) * bm + rows % bm)
    blk_start = jnp.arange(n_blocks, dtype=jnp.int32) * bm
    blk_e = jnp.minimum(jnp.sum((pad_end[None, :] <= blk_start[:, None]).astype(jnp.int32), axis=1),
                        N_EXPERTS - 1).astype(jnp.int32)
    n_used = (pad_end[-1] // bm).astype(jnp.int32).reshape(1)
    return (blk_e, n_used, src_tok.reshape(n_blocks, 1, bm).astype(jnp.int32),
            dst_row.reshape(n_blocks, 1, bm).astype(jnp.int32))


def _combine_kernel(x1_ref, y0_ref, y1_ref, y2_ref, y3_ref, tg_ref, nf_ref, o_ref):
    tg = tg_ref[...]
    acc = y0_ref[...] * tg[:, 0:1]
    for k, y_ref in enumerate((y1_ref, y2_ref, y3_ref), start=1):
        acc = acc + y_ref[...] * tg[:, k:k + 1]
    x = x1_ref[...] + acc
    o_ref[...] = x * lax.rsqrt(jnp.mean(x * x, axis=-1, keepdims=True) + RMS_EPS) * nf_ref[...]


def _combine(x1, y, tg, norm_f_w):
    n = x1.shape[0]
    tm = COMBINE_TM
    row = lambda i: (i, 0)
    y_spec = lambda k: pl.BlockSpec((tm, D_MODEL), lambda i: (k * (n // tm) + i, 0))
    return pl.pallas_call(
        _combine_kernel,
        out_shape=jax.ShapeDtypeStruct((n, D_MODEL), F32),
        grid=(n // tm,),
        in_specs=[pl.BlockSpec((tm, D_MODEL), row), y_spec(0), y_spec(1), y_spec(2), y_spec(3),
                  pl.BlockSpec((tm, LANES), row), pl.BlockSpec((1, D_MODEL), lambda i: (0, 0))],
        out_specs=pl.BlockSpec((tm, D_MODEL), row),
        compiler_params=pltpu.CompilerParams(
            dimension_semantics=("arbitrary",), vmem_limit_bytes=VMEM_LIMIT),
        name="combine_norm",
    )(x1, y, y, y, y, tg, norm_f_w.reshape(1, D_MODEL))


def _pack_w_in(w_in):
    s = np.cumsum((0,) + IN_SIZES)
    qkv, z, al, be, qb, kb, vb, gb, gk, ga, gbt = [w_in[:, s[i]:s[i + 1]] for i in range(len(IN_SIZES))]
    small = jnp.concatenate(
        [al, be, gk, jnp.zeros((D_MODEL, LANES - 2 * N_HEADS - GK_RANK), w_in.dtype)], axis=1)
    return jnp.concatenate([qkv, z, ga, gbt, qb, kb, vb, gb, small], axis=1).astype(BF16)


def kernel(x_prompt, x_sample, state_gdn, state_gdn_conv, state_gla, norm1_w, w_in, conv_w, a_log,
           dt_bias, gdn_norm_w, gla_gk_up, gla_gk_b, gla_norm_w, w_up_a, w_up_b, w_o, norm2_w,
           w_router, b_router, w_gate_up, b_gate_up, w_down, b_down, norm_f_w):
    bp, tp, _ = x_prompt.shape
    bs, ts, _ = x_sample.shape
    n_p, n_s = bp * tp, bs * ts
    n = n_p + n_s
    l = 0

    x = jnp.concatenate([x_prompt.reshape(n_p, D_MODEL), x_sample.reshape(n_s, D_MODEL)], axis=0)
    p = _inproj(x, norm1_w[l], _pack_w_in(w_in[l]))

    hp = jnp.zeros((8, LANES), F32).at[0, :N_HEADS].set(a_log[l]).at[1, :N_HEADS].set(dt_bias[l])
    wgk = jnp.zeros((LANES, N_HEADS * DK_B), F32).at[2 * N_HEADS:2 * N_HEADS + GK_RANK].set(gla_gk_up[l]).astype(BF16)
    zeros = lambda shape: jnp.zeros(shape, F32)

    oa_p, gdn_p = _gdn(p, 0, bp, tp, conv_w[l], hp, gdn_norm_w[l],
                       zeros((bp, CONV_W - 1, CONV_DIM)), zeros((bp, N_HEADS, DK_A, DV_A)))
    oa_s, gdn_s = _gdn(p, n_p, bs, ts, conv_w[l], hp, gdn_norm_w[l], state_gdn_conv[l], state_gdn[l])
    ob_p, gla_p = _gla(p, 0, bp, tp, wgk, gla_gk_b[l], gla_norm_w[l], zeros((bp, N_HEADS, DK_B, DV_B)))
    ob_s, gla_s = _gla(p, n_p, bs, ts, wgk, gla_gk_b[l], gla_norm_w[l], state_gla[l])
    o_a = jnp.concatenate([oa_p, oa_s], axis=0)
    o_b = jnp.concatenate([ob_p, ob_s], axis=0)

    wr = jnp.zeros((D_MODEL, LANES), F32).at[:, :N_EXPERTS].set(w_router[l])
    br = jnp.zeros((1, LANES), F32).at[0, :N_EXPERTS].set(b_router[l])
    x1, h2, te, tg = _merge(x, o_a, o_b, p, w_up_a[l].astype(BF16), w_up_b[l].astype(BF16),
                            w_o[l].astype(BF16), norm2_w[l], wr, br)

    blk_e, n_used, src_tok, dst_row = _route(te[:, :TOP_K], n)
    y = _moe(blk_e, n_used, src_tok, dst_row, h2, w_gate_up[l].astype(BF16),
             b_gate_up[l].reshape(N_EXPERTS, 1, 2 * D_FF), w_down[l].astype(BF16),
             b_down[l].reshape(N_EXPERTS, 1, D_MODEL), n * TOP_K + 2 * MOE_BM)
    out = _combine(x1, y, tg, norm_f_w)

    y_prompt = out[:n_p].reshape(bp, tp, D_MODEL)
    y_sample = out[n_p:].reshape(bs, ts, D_MODEL)
    qkv_p = p[:n_p, :CONV_DIM].reshape(bp, tp, CONV_DIM)
    qkv_s = p[n_p:, :CONV_DIM].reshape(bs, ts, CONV_DIM)
    conv_p = qkv_p[:, tp - (CONV_W - 1):, :]
    conv_s = qkv_s[:, ts - (CONV_W - 1):, :]
    return (y_prompt, y_sample, gdn_p[None], conv_p[None], gla_p[None], gdn_s[None], conv_s[None],
            gla_s[None])
```

```python
import functools
import math

import jax
import jax.numpy as jnp
import numpy as np
from jax import lax
from jax.experimental import pallas as pl
from jax.experimental.pallas import tpu as pltpu

F32 = jnp.float32
BF16 = jnp.bfloat16

D_MODEL = 1024
N_HEADS = 4
DK_A = 128
DV_A = 128
W_A = N_HEADS * DV_A
CONV_W = 4
CONV_DIM = 2 * N_HEADS * DK_A + W_A
GDN_CHUNK = 64
DK_B = 64
DV_B = 128
W_B = N_HEADS * DV_B
GK_RANK = 16
GLA_GATE_NORM = 16.0
GLA_CHUNK = 16
N_EXPERTS = 32
TOP_K = 4
D_FF = 1024
SWIGLU_LIMIT = 7.0
SWIGLU_ALPHA = 1.702
RMS_EPS = 1e-6
IN_SIZES = (CONV_DIM, W_A, N_HEADS, N_HEADS, N_HEADS * DK_B, N_HEADS * DK_B, W_B, W_B, GK_RANK,
            D_MODEL, D_MODEL)

LANES = 128

OFF_QKV = 0
OFF_Z = 1536
OFF_GATE_A = 2048
OFF_GATE_B = 3072
OFF_QK_B = 4096
OFF_V_B = 4608
OFF_G_B = 5120
OFF_SMALL = 5632
P_COLS = 5760

INPROJ_TM = 512
GLA_TB = 128
GDN_SEGS_CARRY = 4
GDN_SEGS_INDEP = 8
MERGE_TM = 512
MOE_BM = 256
MOE_SUB = 2
TOK_TILE = D_MODEL // LANES
COMBINE_TM = 256
VMEM_LIMIT = 56 * 1024 * 1024


def _dot(a, b):
    return jnp.dot(a.astype(BF16), b.astype(BF16), preferred_element_type=F32)


def _dot_nt(a, b):
    return lax.dot_general(a.astype(BF16), b.astype(BF16), (((1,), (1,)), ((), ())),
                           preferred_element_type=F32)


def _dot_tn(a, b):
    return lax.dot_general(a, b, (((0,), (0,)), ((), ())), preferred_element_type=F32)


def _split3(x):
    hi = x.astype(BF16)
    r1 = x - hi.astype(F32)
    mid = r1.astype(BF16)
    lo = (r1 - mid.astype(F32)).astype(BF16)
    return hi, mid, lo


def _dot01(m01, x):
    hi, mid, lo = _split3(x)
    d = lambda p: jnp.dot(m01, p, preferred_element_type=F32)
    return d(hi) + d(mid) + d(lo)


def _silu(x):
    return x * jax.nn.sigmoid(x)


def _pick(i, n_first, first_ref, second_ref):
    return jnp.where(i < n_first, first_ref[...], second_ref[...])


def _split_specs(block, n_first):
    return (pl.BlockSpec(block, lambda i: (jnp.minimum(i, n_first - 1), 0)),
            pl.BlockSpec(block, lambda i: (jnp.maximum(i - n_first, 0), 0)))


def _inproj_kernel(xp_ref, xs_ref, nw_ref, w_ref, o_ref, *, n_first):
    x = _pick(pl.program_id(0), n_first, xp_ref, xs_ref)
    ms = jnp.mean(x * x, axis=-1, keepdims=True)
    h = (x * lax.rsqrt(ms + RMS_EPS) * nw_ref[...]).astype(BF16)
    o_ref[...] = jnp.dot(h, w_ref[...], preferred_element_type=F32)


def _inproj(x_p, x_s, norm_w, w_packed):
    tm = INPROJ_TM
    n_first = x_p.shape[0] // tm
    n = x_p.shape[0] + x_s.shape[0]
    return pl.pallas_call(
        functools.partial(_inproj_kernel, n_first=n_first),
        out_shape=jax.ShapeDtypeStruct((n, P_COLS), F32),
        grid=(n // tm,),
        in_specs=[
            *_split_specs((tm, D_MODEL), n_first),
            pl.BlockSpec((1, D_MODEL), lambda i: (0, 0)),
            pl.BlockSpec((D_MODEL, P_COLS), lambda i: (0, 0), pipeline_mode=pl.Buffered(1)),
        ],
        out_specs=pl.BlockSpec((tm, P_COLS), lambda i: (i, 0)),
        compiler_params=pltpu.CompilerParams(
            dimension_semantics=("arbitrary",), vmem_limit_bytes=VMEM_LIMIT),
        name="inproj",
    )(x_p, x_s, norm_w.reshape(1, D_MODEL), w_packed)


def _gdn_kernel(qkv_ref, z_ref, sm_ref, cw_ref, hp_ref, nw_ref, sconv_ref, s0_ref,
                o_ref, sout_ref, s_ref, ubuf_ref, *, chunk, n_seg, carry, n_steps):
    c_len = chunk
    rows = n_seg * c_len
    t = pl.program_id(1)
    hist = CONV_W - 1
    base = 8
    conv_len = rows if carry else c_len
    n_conv = 1 if carry else n_seg

    if carry:
        @pl.when(t == 0)
        def _():
            s_ref[...] = s0_ref[...]
            ubuf_ref[:, base - hist:base, :] = sconv_ref[...]

        @pl.when(t > 0)
        def _():
            ubuf_ref[:, base - hist:base, :] = ubuf_ref[:, base + conv_len - hist:base + conv_len, :]
    else:
        ubuf_ref[:, base - hist:base, :] = sconv_ref[...]
    ubuf_ref[:, base:base + conv_len, :] = qkv_ref[...].reshape(n_conv, conv_len, CONV_DIM)

    cw = cw_ref[...]
    y = ubuf_ref[:, base - hist:base - hist + conv_len, :] * cw[0:1, :]
    for j in range(1, CONV_W):
        y = y + ubuf_ref[:, base - hist + j:base - hist + j + conv_len, :] * cw[j:j + 1, :]
    y = _silu(y).reshape(rows, CONV_DIM)

    sm = sm_ref[...]
    hp = hp_ref[...]
    lane = lax.broadcasted_iota(jnp.int32, (rows, LANES), 1)
    g_all = -jnp.exp(hp[0:1, :]) * jax.nn.softplus(sm + hp[1:2, :])
    g_all = jnp.where(lane < N_HEADS, g_all, 0.0)
    beta_all = jax.nn.sigmoid(sm)

    sh = int(round(math.log2(c_len)))
    r_b = lax.broadcasted_iota(jnp.int32, (rows, rows), 0)
    c_b = lax.broadcasted_iota(jnp.int32, (rows, rows), 1)
    seg_tril = ((r_b >> sh) == (c_b >> sh)) & (c_b <= r_b)
    gc = _dot01(jnp.where(seg_tril, 1.0, 0.0).astype(BF16), g_all)

    r_i = lax.broadcasted_iota(jnp.int32, (c_len, c_len), 0)
    c_i = lax.broadcasted_iota(jnp.int32, (c_len, c_len), 1)
    tri = c_i <= r_i
    strict = c_i < r_i
    eye = c_i == r_i
    nw = nw_ref[...]
    levels = sh

    qn, kn, kb, rhs, qg, gcol = [], [], [], [], [], []
    for h in range(N_HEADS):
        qh = y[:, h * DK_A:(h + 1) * DK_A]
        kh = y[:, W_A + h * DK_A:W_A + (h + 1) * DK_A]
        vh = y[:, 2 * W_A + h * DV_A:2 * W_A + (h + 1) * DV_A]
        qn_h = qh * lax.rsqrt(jnp.sum(qh * qh, axis=-1, keepdims=True) + 1e-6) * (DK_A ** -0.5)
        kn_h = kh * lax.rsqrt(jnp.sum(kh * kh, axis=-1, keepdims=True) + 1e-6)
        gcol_h = gc[:, h:h + 1]
        eg = jnp.exp(gcol_h)
        bcol = beta_all[:, N_HEADS + h:N_HEADS + h + 1]
        kb_h = kn_h * bcol
        qn.append(qn_h)
        kn.append(kn_h)
        kb.append(kb_h)
        rhs.append(jnp.concatenate([vh * bcol, kb_h * eg], axis=1))
        qg.append(qn_h * eg)
        gcol.append(gcol_h)

    pairs = [(g, h) for g in range(n_seg) for h in range(N_HEADS)]
    sl = lambda g: slice(g * c_len, (g + 1) * c_len)

    decay, kd, gl = {}, {}, {}
    for (g, h) in pairs:
        gcol_p = gcol[h][sl(g)]
        gcb = jnp.broadcast_to(gcol_p, (c_len, c_len))
        grow = jnp.sum(jnp.where(eye, gcb, 0.0), axis=0, keepdims=True)
        decay[g, h] = jnp.where(tri, jnp.exp(jnp.where(tri, gcb - grow, 0.0)), 0.0)
        glast = gcol[h][(g + 1) * c_len - 1:(g + 1) * c_len]
        kd[g, h] = kn[h][sl(g)] * jnp.exp(glast - gcol_p)
        gl[g, h] = jnp.exp(glast)

    kq = {p: _dot_nt(jnp.concatenate([kb[p[1]][sl(p[0])], qn[p[1]][sl(p[0])]], axis=0), kn[p[1]][sl(p[0])])
          for p in pairs}
    low = {p: jnp.where(strict, kq[p][:c_len] * decay[p], 0.0) for p in pairs}
    attn = {p: jnp.where(tri, kq[p][c_len:] * decay[p], 0.0) for p in pairs}
    ymat = {p: -low[p] for p in pairs}
    lp = low
    for _ in range(levels - 1):
        lp = {p: _dot(lp[p], lp[p]) for p in pairs}
        prod = {p: _dot(ymat[p], lp[p]) for p in pairs}
        ymat = {p: ymat[p] + lp[p] + prod[p] for p in pairs}
    sol = {}
    for (g, h) in pairs:
        rhs_p = rhs[h][sl(g)]
        sol[g, h] = rhs_p + _dot(ymat[g, h], rhs_p)

    def finish(g, h, o):
        r = o * lax.rsqrt(jnp.mean(o * o, axis=-1, keepdims=True) + RMS_EPS) * nw
        o_ref[sl(g), h * DV_A:(h + 1) * DV_A] = r * _silu(z_ref[sl(g), h * DV_A:(h + 1) * DV_A])

    def advance(group, state):
        ws = {p: _dot(jnp.concatenate([sol[p][:, DV_A:], qg[p[1]][sl(p[0])]], axis=0), state[p]) for p in group}
        v_new = {p: sol[p][:, :DV_A] - ws[p][:c_len] for p in group}
        o = {p: ws[p][c_len:] + _dot(attn[p], v_new[p]) for p in group}
        new = {p: state[p] * gl[p] + _dot_tn(kd[p], v_new[p]) for p in group}
        for p in group:
            finish(p[0], p[1], o[p])
        return new

    if carry:
        cur = {h: s_ref[0, h] for h in range(N_HEADS)}
        for g in range(n_seg):
            group = [(g, h) for h in range(N_HEADS)]
            new = advance(group, {(g, h): cur[h] for h in range(N_HEADS)})
            cur = {h: new[g, h] for h in range(N_HEADS)}
        for h in range(N_HEADS):
            s_ref[0, h] = cur[h]

        @pl.when(t == n_steps - 1)
        def _():
            for h in range(N_HEADS):
                sout_ref[0, h] = cur[h]
    else:
        new = advance(pairs, {p: s0_ref[p[0], p[1]] for p in pairs})
        for p in pairs:
            sout_ref[p[0], p[1]] = new[p]


def _gdn(p, row0, batch, seq, conv_w, hp, norm_w, s_conv, s0):
    chunk = math.gcd(seq, GDN_CHUNK)
    n_chunks = seq // chunk
    carry = n_chunks > 1
    if carry:
        n_seg = math.gcd(n_chunks, GDN_SEGS_CARRY)
        grid = (batch, n_chunks // n_seg)
        n_conv, conv_len = 1, n_seg * chunk
    else:
        n_seg = math.gcd(batch, GDN_SEGS_INDEP)
        grid = (batch // n_seg, 1)
        n_conv, conv_len = n_seg, chunk
    rows = n_seg * chunk
    n_steps = grid[1]
    blk0 = row0 // rows
    rowblk = lambda b, t: blk0 + b * n_steps + t
    kern = functools.partial(_gdn_kernel, chunk=chunk, n_seg=n_seg, carry=carry, n_steps=n_steps)
    return pl.pallas_call(
        kern,
        out_shape=(jax.ShapeDtypeStruct((batch * seq, W_A), F32),
                   jax.ShapeDtypeStruct((batch, N_HEADS, DK_A, DV_A), F32)),
        grid=grid,
        in_specs=[
            pl.BlockSpec((rows, CONV_DIM), lambda b, t: (rowblk(b, t), OFF_QKV // CONV_DIM)),
            pl.BlockSpec((rows, W_A), lambda b, t: (rowblk(b, t), OFF_Z // W_A)),
            pl.BlockSpec((rows, LANES), lambda b, t: (rowblk(b, t), OFF_SMALL // LANES)),
            pl.BlockSpec((CONV_W, CONV_DIM), lambda b, t: (0, 0)),
            pl.BlockSpec((8, LANES), lambda b, t: (0, 0)),
            pl.BlockSpec((1, DV_A), lambda b, t: (0, 0)),
            pl.BlockSpec((n_conv, CONV_W - 1, CONV_DIM), lambda b, t: (b, 0, 0)),
            pl.BlockSpec((n_conv, N_HEADS, DK_A, DV_A), lambda b, t: (b, 0, 0, 0)),
        ],
        out_specs=(
            pl.BlockSpec((rows, W_A), lambda b, t: (b * n_steps + t, 0)),
            pl.BlockSpec((n_conv, N_HEADS, DK_A, DV_A), lambda b, t: (b, 0, 0, 0)),
        ),
        scratch_shapes=[pltpu.VMEM((1, N_HEADS, DK_A, DV_A), F32),
                        pltpu.VMEM((n_conv, conv_len + 8, CONV_DIM), F32)],
        compiler_params=pltpu.CompilerParams(
            dimension_semantics=("arbitrary", "arbitrary"), vmem_limit_bytes=VMEM_LIMIT),
        name=f"gdn_c{chunk}",
    )(p, p, p, conv_w, hp, norm_w.reshape(1, DV_A), s_conv, s0)


def _gla_kernel(qk_ref, v_ref, g_ref, sm_ref, wgk_ref, bgk_ref, nw_ref, s0_ref,
                o_ref, sout_ref, s_ref, *, tb, cs, carry, n_steps):
    t = pl.program_id(1)
    n_sub = tb // cs
    w_kb = N_HEADS * DK_B

    if carry:
        @pl.when(t == 0)
        def _():
            s_ref[...] = s0_ref[...]

    gk = _dot(sm_ref[...], wgk_ref[...]) + bgk_ref[...]
    log_a = jax.nn.log_sigmoid(gk) / GLA_GATE_NORM
    sh = int(round(math.log2(cs)))
    r_i = lax.broadcasted_iota(jnp.int32, (tb, tb), 0)
    c_i = lax.broadcasted_iota(jnp.int32, (tb, tb), 1)
    same = (r_i >> sh) == (c_i >> sh)
    mask = same & (c_i <= r_i)
    gcum = _dot01(jnp.where(mask, 1.0, 0.0).astype(BF16), log_a)
    gtot = _dot01(jnp.where(same, 1.0, 0.0).astype(BF16), log_a)
    onehot = jnp.where((lax.broadcasted_iota(jnp.int32, (tb, n_sub), 0) >> sh)
                       == lax.broadcasted_iota(jnp.int32, (tb, n_sub), 1), 1.0, 0.0).astype(BF16)
    tn = lambda part: lax.dot_general(part, onehot, (((0,), (0,)), ((), ())), preferred_element_type=F32)
    hi, mid, lo = _split3(log_a)
    glcol = jnp.exp(tn(hi) + tn(mid) + tn(lo))

    qk = qk_ref[...]
    nw = nw_ref[...]
    heads = range(N_HEADS)
    sl = lambda j: slice(j * cs, (j + 1) * cs)
    qg, kd, vh, o_intra = [], [], [], []
    for h in heads:
        qh = qk[:, h * DK_B:(h + 1) * DK_B] * (DK_B ** -0.5)
        kh = qk[:, w_kb + h * DK_B:w_kb + (h + 1) * DK_B]
        gh = gcum[:, h * DK_B:(h + 1) * DK_B]
        gt = gtot[:, h * DK_B:(h + 1) * DK_B]
        qg.append(qh * jnp.exp(gh))
        kd.append(kh * jnp.exp(gt - gh))
        vh.append(v_ref[:, h * DV_B:(h + 1) * DV_B])
        attn = jnp.where(mask, _dot_nt(qg[h], kh * jnp.exp(-gh)), 0.0)
        o_intra.append(_dot(attn, vh[h]))
    upd = {(j, h): _dot_tn(kd[h][sl(j)], vh[h][sl(j)]) for j in range(n_sub) for h in heads}
    gl = lambda j, h: glcol[h * DK_B:(h + 1) * DK_B, j:j + 1]

    o_inter = [[] for _ in heads]
    if carry:
        cur = [s_ref[0, h] for h in heads]
        for j in range(n_sub):
            for h in heads:
                o_inter[h].append(_dot(qg[h][sl(j)], cur[h]))
                cur[h] = cur[h] * gl(j, h) + upd[j, h]
        for h in heads:
            s_ref[0, h] = cur[h]

        @pl.when(t == n_steps - 1)
        def _():
            for h in heads:
                sout_ref[0, h] = cur[h]
    else:
        for j in range(n_sub):
            for h in heads:
                s = s0_ref[j, h]
                o_inter[h].append(_dot(qg[h][sl(j)], s))
                sout_ref[j, h] = s * gl(j, h) + upd[j, h]

    for h in heads:
        o = o_intra[h] + jnp.concatenate(o_inter[h], axis=0)
        r = o * lax.rsqrt(jnp.mean(o * o, axis=-1, keepdims=True) + RMS_EPS) * nw
        o_ref[:, h * DV_B:(h + 1) * DV_B] = r * _silu(g_ref[:, h * DV_B:(h + 1) * DV_B])


def _gla(p, row0, batch, seq, wgk, bgk, norm_w, s0):
    cs = math.gcd(seq, GLA_CHUNK)
    carry = seq > cs
    if carry:
        tb = math.gcd(seq, GLA_TB)
        grid = (batch, seq // tb)
        n_state = 1
    else:
        tb = math.gcd(batch * seq, GLA_TB)
        n_state = tb // cs
        grid = (batch // n_state, 1)
    n_steps = grid[1]
    blk0 = row0 // tb
    rowblk = lambda b, t: blk0 + b * n_steps + t
    kern = functools.partial(_gla_kernel, tb=tb, cs=cs, carry=carry, n_steps=n_steps)
    return pl.pallas_call(
        kern,
        out_shape=(jax.ShapeDtypeStruct((batch * seq, W_B), F32),
                   jax.ShapeDtypeStruct((batch, N_HEADS, DK_B, DV_B), F32)),
        grid=grid,
        in_specs=[
            pl.BlockSpec((tb, 2 * N_HEADS * DK_B), lambda b, t: (rowblk(b, t), OFF_QK_B // (2 * N_HEADS * DK_B))),
            pl.BlockSpec((tb, W_B), lambda b, t: (rowblk(b, t), OFF_V_B // W_B)),
            pl.BlockSpec((tb, W_B), lambda b, t: (rowblk(b, t), OFF_G_B // W_B)),
            pl.BlockSpec((tb, LANES), lambda b, t: (rowblk(b, t), OFF_SMALL // LANES)),
            pl.BlockSpec((LANES, N_HEADS * DK_B), lambda b, t: (0, 0)),
            pl.BlockSpec((1, N_HEADS * DK_B), lambda b, t: (0, 0)),
            pl.BlockSpec((1, DV_B), lambda b, t: (0, 0)),
            pl.BlockSpec((n_state, N_HEADS, DK_B, DV_B), lambda b, t: (b, 0, 0, 0)),
        ],
        out_specs=(
            pl.BlockSpec((tb, W_B), lambda b, t: (b * n_steps + t, 0)),
            pl.BlockSpec((n_state, N_HEADS, DK_B, DV_B), lambda b, t: (b, 0, 0, 0)),
        ),
        scratch_shapes=[pltpu.VMEM((1, N_HEADS, DK_B, DV_B), F32)],
        compiler_params=pltpu.CompilerParams(
            dimension_semantics=("arbitrary", "arbitrary"), vmem_limit_bytes=VMEM_LIMIT),
        name=f"gla_c{cs}",
    )(p, p, p, p, wgk, bgk.reshape(1, N_HEADS * DK_B), norm_w.reshape(1, DV_B), s0)


def _merge_kernel(xp_ref, xs_ref, oap_ref, oas_ref, obp_ref, obs_ref, ga_ref, gb_ref, wua_ref, wub_ref, wo_ref,
                  n2_ref, wr_ref, br_ref, x1_ref, h2_ref, te_ref, tg_ref, *, n_first):
    i = pl.program_id(0)
    ua = jnp.dot(_pick(i, n_first, oap_ref, oas_ref).astype(BF16), wua_ref[...], preferred_element_type=F32)
    ub = jnp.dot(_pick(i, n_first, obp_ref, obs_ref).astype(BF16), wub_ref[...], preferred_element_type=F32)
    merged = jax.nn.sigmoid(ga_ref[...]) * ua + jax.nn.sigmoid(gb_ref[...]) * ub
    x1 = _pick(i, n_first, xp_ref, xs_ref) + jnp.dot(merged.astype(BF16), wo_ref[...], preferred_element_type=F32)
    x1_ref[...] = x1
    h2 = x1 * lax.rsqrt(jnp.mean(x1 * x1, axis=-1, keepdims=True) + RMS_EPS) * n2_ref[...]
    for s in range(TOK_TILE):
        h2_ref[pl.ds(s, x1.shape[0], stride=TOK_TILE), :] = h2[:, s * LANES:(s + 1) * LANES]
    logits = jnp.dot(h2, wr_ref[...], precision=lax.Precision.HIGHEST,
                     preferred_element_type=F32) + br_ref[...]
    tm = logits.shape[0]
    lane = lax.broadcasted_iota(jnp.int32, (tm, LANES), 1)
    lane_f = lane.astype(F32)
    neg = jnp.float32(-jnp.inf)
    cur = jnp.where(lane < N_EXPERTS, logits, neg)
    vals, idxs = [], []
    for _ in range(TOP_K):
        m = jnp.max(cur, axis=-1, keepdims=True)
        idx = jnp.min(jnp.where(cur == m, lane_f, float(LANES)), axis=-1, keepdims=True)
        vals.append(m)
        idxs.append(idx)
        cur = jnp.where(lane_f == idx, neg, cur)
    exps = [jnp.exp(v - vals[0]) for v in vals]
    den = exps[0] + exps[1] + exps[2] + exps[3]
    te = jnp.zeros((tm, LANES), F32)
    tg = jnp.zeros((tm, LANES), F32)
    for k in range(TOP_K):
        te = jnp.where(lane == k, idxs[k], te)
        tg = jnp.where(lane == k, exps[k] / den, tg)
    te_ref[...] = te.astype(jnp.int32)
    tg_ref[...] = tg


def _merge(x_p, x_s, oa_p, oa_s, ob_p, ob_s, p, wua, wub, wo, norm2_w, wr, br):
    n = x_p.shape[0] + x_s.shape[0]
    tm = MERGE_TM
    n_first = x_p.shape[0] // tm
    row = lambda i: (i, 0)
    const = lambda i: (0, 0)
    return pl.pallas_call(
        functools.partial(_merge_kernel, n_first=n_first),
        out_shape=(jax.ShapeDtypeStruct((n, D_MODEL), F32), jax.ShapeDtypeStruct((n * TOK_TILE, LANES), F32),
                   jax.ShapeDtypeStruct((n, LANES), jnp.int32), jax.ShapeDtypeStruct((n, LANES), F32)),
        grid=(n // tm,),
        in_specs=[
            *_split_specs((tm, D_MODEL), n_first),
            *_split_specs((tm, W_A), n_first),
            *_split_specs((tm, W_B), n_first),
            pl.BlockSpec((tm, D_MODEL), lambda i: (i, OFF_GATE_A // D_MODEL)),
            pl.BlockSpec((tm, D_MODEL), lambda i: (i, OFF_GATE_B // D_MODEL)),
            pl.BlockSpec((W_A, D_MODEL), const),
            pl.BlockSpec((W_B, D_MODEL), const),
            pl.BlockSpec((D_MODEL, D_MODEL), const),
            pl.BlockSpec((1, D_MODEL), const),
            pl.BlockSpec((D_MODEL, LANES), const),
            pl.BlockSpec((1, LANES), const),
        ],
        out_specs=(pl.BlockSpec((tm, D_MODEL), row), pl.BlockSpec((tm * TOK_TILE, LANES), row),
                   pl.BlockSpec((tm, LANES), row), pl.BlockSpec((tm, LANES), row)),
        compiler_params=pltpu.CompilerParams(
            dimension_semantics=("arbitrary",), vmem_limit_bytes=VMEM_LIMIT),
        name="merge_router",
    )(x_p, x_s, oa_p, oa_s, ob_p, ob_s, p, p, wua, wub, wo, norm2_w.reshape(1, D_MODEL), wr, br)


def _moe_kernel(blk_e_ref, n_used_ref, src_ref, src_nxt_ref, dst_ref, h2_hbm, wgu_ref, bgu_ref,
                wdn_ref, bdn_ref, y_hbm, xbuf, ybuf, gsem, ssem):
    del blk_e_ref
    i = pl.program_id(0)
    n_used = n_used_ref[0]
    bm = MOE_BM
    tt = TOK_TILE

    def gather_copy(tok, s, r):
        return pltpu.make_async_copy(h2_hbm.at[pl.ds(pl.multiple_of(tok * tt, tt), tt), :],
                                     xbuf.at[s, pl.ds(r * tt, tt), :], gsem.at[s])

    def scatter_copy(row, s, r):
        return pltpu.make_async_copy(ybuf.at[s, pl.ds(r * tt, tt), :],
                                     y_hbm.at[pl.ds(pl.multiple_of(row * tt, tt), tt), :], ssem.at[s])

    @pl.when(i == 0)
    def _():
        ybuf[...] = jnp.zeros_like(ybuf)
        n_real = y_hbm.shape[0] - MOE_SUB * bm * tt
        for s in range(MOE_SUB):
            sink = pltpu.make_async_copy(ybuf.at[s], y_hbm.at[pl.ds(n_real + s * bm * tt, bm * tt), :], ssem.at[s])
            sink.start()
            sink.wait()

    @pl.when((i == 0) & (n_used > 0))
    def _():
        for r in range(bm):
            gather_copy(src_ref[0, 0, r], 0, r).start()

    @pl.when(i < n_used)
    def _():
        for s in range(MOE_SUB):
            nxt = (s + 1) % MOE_SUB
            for r in range(bm):
                gather_copy(0, s, r).wait()

            @pl.when(i >= 1)
            def _():
                for r in range(bm):
                    scatter_copy(0, s, r).wait()

            for r in range(bm):
                tok = src_ref[0, 0, (s + 1) * bm + r] if s + 1 < MOE_SUB else src_nxt_ref[0, 0, r]
                gather_copy(tok, nxt, r).start()

            x = jnp.concatenate([xbuf[s, pl.ds(c, bm, stride=tt), :].astype(BF16) for c in range(tt)], axis=1)
            gu = jnp.dot(x, wgu_ref[...], preferred_element_type=F32) + bgu_ref[...]
            gate = jnp.minimum(gu[:, :D_FF], SWIGLU_LIMIT)
            up = jnp.clip(gu[:, D_FF:], -SWIGLU_LIMIT, SWIGLU_LIMIT)
            hmid = (up + 1.0) * (gate * jax.nn.sigmoid(SWIGLU_ALPHA * gate))
            y = jnp.dot(hmid.astype(BF16), wdn_ref[...], preferred_element_type=F32) + bdn_ref[...]
            for c in range(tt):
                ybuf[s, pl.ds(c, bm, stride=tt), :] = y[:, c * LANES:(c + 1) * LANES]
            for r in range(bm):
                scatter_copy(dst_ref[0, 0, s * bm + r], s, r).start()

    @pl.when(i == n_used - 1)
    def _():
        for r in range(bm):
            gather_copy(0, 0, r).wait()
        for s in range(MOE_SUB):
            for r in range(bm):
                scatter_copy(0, s, r).wait()


def _moe(blk_e, n_used, src_tok, dst_row, h2, wgu, bgu, wdn, bdn, y_rows):
    n_steps = src_tok.shape[0]
    step_rows = MOE_SUB * MOE_BM
    cur = lambda i, be, nu: (i, 0, 0)
    nxt = lambda i, be, nu: (jnp.minimum(i + 1, n_steps - 1), 0, 0)
    wsel = lambda i, be, nu: (be[i], 0, 0)
    smem_blk = lambda im: pl.BlockSpec((1, 1, step_rows), im, memory_space=pltpu.SMEM)
    grid_spec = pltpu.PrefetchScalarGridSpec(
        num_scalar_prefetch=2,
        grid=(n_steps,),
        in_specs=[
            smem_blk(cur), smem_blk(nxt), smem_blk(cur),
            pl.BlockSpec(memory_space=pl.ANY),
            pl.BlockSpec((None, D_MODEL, 2 * D_FF), wsel),
            pl.BlockSpec((None, 1, 2 * D_FF), wsel),
            pl.BlockSpec((None, D_FF, D_MODEL), wsel),
            pl.BlockSpec((None, 1, D_MODEL), wsel),
        ],
        out_specs=pl.BlockSpec(memory_space=pl.ANY),
        scratch_shapes=[pltpu.VMEM((MOE_SUB, MOE_BM * TOK_TILE, LANES), F32),
                        pltpu.VMEM((MOE_SUB, MOE_BM * TOK_TILE, LANES), F32),
                        pltpu.SemaphoreType.DMA((MOE_SUB,)), pltpu.SemaphoreType.DMA((MOE_SUB,))],
    )
    return pl.pallas_call(
        _moe_kernel,
        out_shape=jax.ShapeDtypeStruct((y_rows * TOK_TILE, LANES), F32),
        grid_spec=grid_spec,
        compiler_params=pltpu.CompilerParams(
            dimension_semantics=("arbitrary",), vmem_limit_bytes=VMEM_LIMIT),
        name="moe_experts",
    )(blk_e, n_used, src_tok, src_tok, dst_row, h2, wgu, bgu, wdn, bdn)


def _route(top_e, n_tok):
    bm = MOE_SUB * MOE_BM
    n_assign = n_tok * TOP_K
    n_blocks = n_assign // bm + N_EXPERTS
    n_dummy = N_EXPERTS * bm
    flat_e = top_e.reshape(-1)
    counts = jnp.sum((flat_e[:, None] == jnp.arange(N_EXPERTS, dtype=jnp.int32)[None, :]).astype(jnp.int32), axis=0)
    padded = (counts + bm - 1) // bm * bm
    pad_end = jnp.cumsum(padded)
    d_e = jnp.arange(n_dummy, dtype=jnp.int32) // bm
    d_active = (jnp.arange(n_dummy, dtype=jnp.int32) % bm) < (padded - counts)[d_e]
    keys = jnp.concatenate([flat_e * 2, jnp.where(d_active, d_e * 2 + 1, 2 * N_EXPERTS)])
    vals = jnp.concatenate([jnp.arange(n_assign, dtype=jnp.int32), jnp.full((n_dummy,), -1, jnp.int32)])
    _, assign = lax.sort((keys, vals), num_keys=1, is_stable=True)
    assign = assign[:n_blocks * bm]
    rows = jnp.arange(n_blocks * bm, dtype=jnp.int32)
    valid = assign >= 0
    src_tok = jnp.where(valid, assign // TOP_K, 0)
    dst_row = jnp.where(valid, (assign % TOP_K) * n_tok + assign // TOP_K, n_assign + rows % bm)
    blk_start = jnp.arange(n_blocks, dtype=jnp.int32) * bm
    blk_e = jnp.minimum(jnp.sum((pad_end[None, :] <= blk_start[:, None]).astype(jnp.int32), axis=1),
                        N_EXPERTS - 1).astype(jnp.int32)
    n_used = (pad_end[-1] // bm).astype(jnp.int32).reshape(1)
    return (blk_e, n_used, src_tok.reshape(n_blocks, 1, bm).astype(jnp.int32),
            dst_row.reshape(n_blocks, 1, bm).astype(jnp.int32))


def _combine_kernel(x1_ref, y0_ref, y1_ref, y2_ref, y3_ref, tg_ref, nf_ref, op_ref, os_ref, *, n_first):
    tg = tg_ref[...]
    tm = x1_ref.shape[0]
    rows = lambda y_ref: jnp.concatenate(
        [y_ref[pl.ds(c, tm, stride=TOK_TILE), :] for c in range(TOK_TILE)], axis=1)
    acc = rows(y0_ref) * tg[:, 0:1]
    for k, y_ref in enumerate((y1_ref, y2_ref, y3_ref), start=1):
        acc = acc + rows(y_ref) * tg[:, k:k + 1]
    x = x1_ref[...] + acc
    out = x * lax.rsqrt(jnp.mean(x * x, axis=-1, keepdims=True) + RMS_EPS) * nf_ref[...]
    i = pl.program_id(0)

    @pl.when(i < n_first)
    def _():
        op_ref[...] = out

    @pl.when(i >= n_first)
    def _():
        os_ref[...] = out


def _combine(x1, y, tg, norm_f_w, n_p):
    n = x1.shape[0]
    tm = COMBINE_TM
    n_first = n_p // tm
    row = lambda i: (i, 0)
    y_spec = lambda k: pl.BlockSpec((tm * TOK_TILE, LANES), lambda i: (k * (n // tm) + i, 0))
    return pl.pallas_call(
        functools.partial(_combine_kernel, n_first=n_first),
        out_shape=(jax.ShapeDtypeStruct((n_p, D_MODEL), F32), jax.ShapeDtypeStruct((n - n_p, D_MODEL), F32)),
        grid=(n // tm,),
        in_specs=[pl.BlockSpec((tm, D_MODEL), row), y_spec(0), y_spec(1), y_spec(2), y_spec(3),
                  pl.BlockSpec((tm, LANES), row), pl.BlockSpec((1, D_MODEL), lambda i: (0, 0))],
        out_specs=_split_specs((tm, D_MODEL), n_first),
        compiler_params=pltpu.CompilerParams(
            dimension_semantics=("arbitrary",), vmem_limit_bytes=VMEM_LIMIT),
        name="combine_norm",
    )(x1, y, y, y, y, tg, norm_f_w.reshape(1, D_MODEL))


def _pack_w_in(w_in):
    s = np.cumsum((0,) + IN_SIZES)
    qkv, z, al, be, qb, kb, vb, gb, gk, ga, gbt = [w_in[:, s[i]:s[i + 1]] for i in range(len(IN_SIZES))]
    small = jnp.concatenate(
        [al, be, gk, jnp.zeros((D_MODEL, LANES - 2 * N_HEADS - GK_RANK), w_in.dtype)], axis=1)
    return jnp.concatenate([qkv, z, ga, gbt, qb, kb, vb, gb, small], axis=1).astype(BF16)


def kernel(x_prompt, x_sample, state_gdn, state_gdn_conv, state_gla, norm1_w, w_in, conv_w, a_log,
           dt_bias, gdn_norm_w, gla_gk_up, gla_gk_b, gla_norm_w, w_up_a, w_up_b, w_o, norm2_w,
           w_router, b_router, w_gate_up, b_gate_up, w_down, b_down, norm_f_w):
    bp, tp, _ = x_prompt.shape
    bs, ts, _ = x_sample.shape
    n_p, n_s = bp * tp, bs * ts
    n = n_p + n_s
    l = 0

    x_p = x_prompt.reshape(n_p, D_MODEL)
    x_s = x_sample.reshape(n_s, D_MODEL)
    p = _inproj(x_p, x_s, norm1_w[l], _pack_w_in(w_in[l]))

    hp = jnp.zeros((8, LANES), F32).at[0, :N_HEADS].set(a_log[l]).at[1, :N_HEADS].set(dt_bias[l])
    wgk = jnp.zeros((LANES, N_HEADS * DK_B), F32).at[2 * N_HEADS:2 * N_HEADS + GK_RANK].set(gla_gk_up[l]).astype(BF16)
    zeros = lambda shape: jnp.zeros(shape, F32)

    oa_p, gdn_p = _gdn(p, 0, bp, tp, conv_w[l], hp, gdn_norm_w[l],
                       zeros((bp, CONV_W - 1, CONV_DIM)), zeros((bp, N_HEADS, DK_A, DV_A)))
    oa_s, gdn_s = _gdn(p, n_p, bs, ts, conv_w[l], hp, gdn_norm_w[l], state_gdn_conv[l], state_gdn[l])
    ob_p, gla_p = _gla(p, 0, bp, tp, wgk, gla_gk_b[l], gla_norm_w[l], zeros((bp, N_HEADS, DK_B, DV_B)))
    ob_s, gla_s = _gla(p, n_p, bs, ts, wgk, gla_gk_b[l], gla_norm_w[l], state_gla[l])

    wr = jnp.zeros((D_MODEL, LANES), F32).at[:, :N_EXPERTS].set(w_router[l])
    br = jnp.zeros((1, LANES), F32).at[0, :N_EXPERTS].set(b_router[l])
    x1, h2, te, tg = _merge(x_p, x_s, oa_p, oa_s, ob_p, ob_s, p, w_up_a[l].astype(BF16),
                            w_up_b[l].astype(BF16), w_o[l].astype(BF16), norm2_w[l], wr, br)

    blk_e, n_used, src_tok, dst_row = _route(te[:, :TOP_K], n)
    y = _moe(blk_e, n_used, src_tok, dst_row, h2, w_gate_up[l].astype(BF16),
             b_gate_up[l].reshape(N_EXPERTS, 1, 2 * D_FF), w_down[l].astype(BF16),
             b_down[l].reshape(N_EXPERTS, 1, D_MODEL), n * TOP_K + MOE_SUB * MOE_BM)
    out_p, out_s = _combine(x1, y, tg, norm_f_w, n_p)
    y_prompt = out_p.reshape(bp, tp, D_MODEL)
    y_sample = out_s.reshape(bs, ts, D_MODEL)
    qkv_p = p[:n_p, :CONV_DIM].reshape(bp, tp, CONV_DIM)
    qkv_s = p[n_p:, :CONV_DIM].reshape(bs, ts, CONV_DIM)
    conv_p = qkv_p[:, tp - (CONV_W - 1):, :]
    conv_s = qkv_s[:, ts - (CONV_W - 1):, :]
    return (y_prompt, y_sample, gdn_p[None], conv_p[None], gla_p[None], gdn_s[None], conv_s[None],
            gla_s[None])
```

```python
import functools
import math

import jax
import jax.numpy as jnp
import numpy as np
from jax import lax
from jax.experimental import pallas as pl
from jax.experimental.pallas import tpu as pltpu

F32 = jnp.float32
BF16 = jnp.bfloat16

D_MODEL = 1024
N_HEADS = 4
DK_A = 128
DV_A = 128
W_A = N_HEADS * DV_A
CONV_W = 4
CONV_DIM = 2 * N_HEADS * DK_A + W_A
GDN_CHUNK = 64
DK_B = 64
DV_B = 128
W_B = N_HEADS * DV_B
GK_RANK = 16
GLA_GATE_NORM = 16.0
GLA_CHUNK = 16
N_EXPERTS = 32
TOP_K = 4
D_FF = 1024
SWIGLU_LIMIT = 7.0
SWIGLU_ALPHA = 1.702
RMS_EPS = 1e-6
IN_SIZES = (CONV_DIM, W_A, N_HEADS, N_HEADS, N_HEADS * DK_B, N_HEADS * DK_B, W_B, W_B, GK_RANK,
            D_MODEL, D_MODEL)

LANES = 128

OFF_QKV = 0
OFF_Z = 1536
OFF_GATE_A = 2048
OFF_GATE_B = 3072
OFF_QK_B = 4096
OFF_V_B = 4608
OFF_G_B = 5120
OFF_SMALL = 5632
P_COLS = 5760

INPROJ_TM = 512
GLA_TB = 128
GDN_SEGS_CARRY = 4
GDN_SEGS_INDEP = 8
MERGE_TM = 512
MOE_BM = 256
MOE_SUB = 2
TOK_TILE = D_MODEL // LANES
COMBINE_TM = 256
VMEM_LIMIT = 56 * 1024 * 1024


def _dot(a, b):
    return jnp.dot(a.astype(BF16), b.astype(BF16), preferred_element_type=F32)


def _dot_nt(a, b):
    return lax.dot_general(a.astype(BF16), b.astype(BF16), (((1,), (1,)), ((), ())),
                           preferred_element_type=F32)


def _dot_tn(a, b):
    return lax.dot_general(a, b, (((0,), (0,)), ((), ())), preferred_element_type=F32)


def _split3(x):
    hi = x.astype(BF16)
    r1 = x - hi.astype(F32)
    mid = r1.astype(BF16)
    lo = (r1 - mid.astype(F32)).astype(BF16)
    return hi, mid, lo


def _dot01(m01, x):
    hi, mid, lo = _split3(x)
    d = lambda p: jnp.dot(m01, p, preferred_element_type=F32)
    return d(hi) + d(mid) + d(lo)


def _silu(x):
    return x * jax.nn.sigmoid(x)


def _pick(i, n_first, first_ref, second_ref):
    return jnp.where(i < n_first, first_ref[...], second_ref[...])


def _split_specs(block, n_first):
    return (pl.BlockSpec(block, lambda i: (jnp.minimum(i, n_first - 1), 0)),
            pl.BlockSpec(block, lambda i: (jnp.maximum(i - n_first, 0), 0)))


def _inproj_kernel(xp_ref, xs_ref, nw_ref, w_ref, o_ref, *, n_first):
    x = _pick(pl.program_id(0), n_first, xp_ref, xs_ref)
    ms = jnp.mean(x * x, axis=-1, keepdims=True)
    h = (x * lax.rsqrt(ms + RMS_EPS) * nw_ref[...]).astype(BF16)
    o_ref[...] = jnp.dot(h, w_ref[...], preferred_element_type=F32)


def _inproj(x_p, x_s, norm_w, w_packed):
    tm = INPROJ_TM
    n_first = x_p.shape[0] // tm
    n = x_p.shape[0] + x_s.shape[0]
    return pl.pallas_call(
        functools.partial(_inproj_kernel, n_first=n_first),
        out_shape=jax.ShapeDtypeStruct((n, P_COLS), F32),
        grid=(n // tm,),
        in_specs=[
            *_split_specs((tm, D_MODEL), n_first),
            pl.BlockSpec((1, D_MODEL), lambda i: (0, 0)),
            pl.BlockSpec((D_MODEL, P_COLS), lambda i: (0, 0), pipeline_mode=pl.Buffered(1)),
        ],
        out_specs=pl.BlockSpec((tm, P_COLS), lambda i: (i, 0)),
        compiler_params=pltpu.CompilerParams(
            dimension_semantics=("arbitrary",), vmem_limit_bytes=VMEM_LIMIT),
        name="inproj",
    )(x_p, x_s, norm_w.reshape(1, D_MODEL), w_packed)


def _gdn_kernel(qkv_ref, z_ref, sm_ref, cw_ref, hp_ref, nw_ref, sconv_ref, s0_ref,
                o_ref, sout_ref, s_ref, ubuf_ref, *, chunk, n_seg, carry, n_steps):
    c_len = chunk
    rows = n_seg * c_len
    t = pl.program_id(1)
    hist = CONV_W - 1
    base = 8
    conv_len = rows if carry else c_len
    n_conv = 1 if carry else n_seg

    if carry:
        @pl.when(t == 0)
        def _():
            s_ref[...] = s0_ref[...]
            ubuf_ref[:, base - hist:base, :] = sconv_ref[...]

        @pl.when(t > 0)
        def _():
            ubuf_ref[:, base - hist:base, :] = ubuf_ref[:, base + conv_len - hist:base + conv_len, :]
    else:
        ubuf_ref[:, base - hist:base, :] = sconv_ref[...]
    ubuf_ref[:, base:base + conv_len, :] = qkv_ref[...].reshape(n_conv, conv_len, CONV_DIM)

    cw = cw_ref[...]
    y = ubuf_ref[:, base - hist:base - hist + conv_len, :] * cw[0:1, :]
    for j in range(1, CONV_W):
        y = y + ubuf_ref[:, base - hist + j:base - hist + j + conv_len, :] * cw[j:j + 1, :]
    y = _silu(y).reshape(rows, CONV_DIM)

    sm = sm_ref[...]
    hp = hp_ref[...]
    lane = lax.broadcasted_iota(jnp.int32, (rows, LANES), 1)
    g_all = -jnp.exp(hp[0:1, :]) * jax.nn.softplus(sm + hp[1:2, :])
    g_all = jnp.where(lane < N_HEADS, g_all, 0.0)
    beta_all = jax.nn.sigmoid(sm)

    sh = int(round(math.log2(c_len)))
    r_b = lax.broadcasted_iota(jnp.int32, (rows, rows), 0)
    c_b = lax.broadcasted_iota(jnp.int32, (rows, rows), 1)
    seg_tril = ((r_b >> sh) == (c_b >> sh)) & (c_b <= r_b)
    gc = _dot01(jnp.where(seg_tril, 1.0, 0.0).astype(BF16), g_all)

    r_i = lax.broadcasted_iota(jnp.int32, (c_len, c_len), 0)
    c_i = lax.broadcasted_iota(jnp.int32, (c_len, c_len), 1)
    tri = c_i <= r_i
    strict = c_i < r_i
    eye = c_i == r_i
    nw = nw_ref[...]
    levels = sh

    qn, kn, kb, rhs, qg, gcol = [], [], [], [], [], []
    for h in range(N_HEADS):
        qh = y[:, h * DK_A:(h + 1) * DK_A]
        kh = y[:, W_A + h * DK_A:W_A + (h + 1) * DK_A]
        vh = y[:, 2 * W_A + h * DV_A:2 * W_A + (h + 1) * DV_A]
        qn_h = qh * lax.rsqrt(jnp.sum(qh * qh, axis=-1, keepdims=True) + 1e-6) * (DK_A ** -0.5)
        kn_h = kh * lax.rsqrt(jnp.sum(kh * kh, axis=-1, keepdims=True) + 1e-6)
        gcol_h = gc[:, h:h + 1]
        eg = jnp.exp(gcol_h)
        bcol = beta_all[:, N_HEADS + h:N_HEADS + h + 1]
        kb_h = kn_h * bcol
        qn.append(qn_h)
        kn.append(kn_h)
        kb.append(kb_h)
        rhs.append(jnp.concatenate([vh * bcol, kb_h * eg], axis=1))
        qg.append(qn_h * eg)
        gcol.append(gcol_h)

    pairs = [(g, h) for g in range(n_seg) for h in range(N_HEADS)]
    sl = lambda g: slice(g * c_len, (g + 1) * c_len)

    decay, kd, gl = {}, {}, {}
    for (g, h) in pairs:
        gcol_p = gcol[h][sl(g)]
        gcb = jnp.broadcast_to(gcol_p, (c_len, c_len))
        grow = jnp.sum(jnp.where(eye, gcb, 0.0), axis=0, keepdims=True)
        decay[g, h] = jnp.where(tri, jnp.exp(jnp.where(tri, gcb - grow, 0.0)), 0.0)
        glast = gcol[h][(g + 1) * c_len - 1:(g + 1) * c_len]
        kd[g, h] = kn[h][sl(g)] * jnp.exp(glast - gcol_p)
        gl[g, h] = jnp.exp(glast)

    kq = {p: _dot_nt(jnp.concatenate([kb[p[1]][sl(p[0])], qn[p[1]][sl(p[0])]], axis=0), kn[p[1]][sl(p[0])])
          for p in pairs}
    low = {p: jnp.where(strict, kq[p][:c_len] * decay[p], 0.0) for p in pairs}
    attn = {p: jnp.where(tri, kq[p][c_len:] * decay[p], 0.0) for p in pairs}
    ymat = {p: -low[p] for p in pairs}
    lp = low
    for _ in range(levels - 1):
        lp = {p: _dot(lp[p], lp[p]) for p in pairs}
        prod = {p: _dot(ymat[p], lp[p]) for p in pairs}
        ymat = {p: ymat[p] + lp[p] + prod[p] for p in pairs}
    sol = {}
    for (g, h) in pairs:
        rhs_p = rhs[h][sl(g)]
        sol[g, h] = rhs_p + _dot(ymat[g, h], rhs_p)

    def finish(g, h, o):
        r = o * lax.rsqrt(jnp.mean(o * o, axis=-1, keepdims=True) + RMS_EPS) * nw
        o_ref[sl(g), h * DV_A:(h + 1) * DV_A] = r * _silu(z_ref[sl(g), h * DV_A:(h + 1) * DV_A])

    def advance(group, state):
        ws = {p: _dot(jnp.concatenate([sol[p][:, DV_A:], qg[p[1]][sl(p[0])]], axis=0), state[p]) for p in group}
        v_new = {p: sol[p][:, :DV_A] - ws[p][:c_len] for p in group}
        o = {p: ws[p][c_len:] + _dot(attn[p], v_new[p]) for p in group}
        new = {p: state[p] * gl[p] + _dot_tn(kd[p], v_new[p]) for p in group}
        for p in group:
            finish(p[0], p[1], o[p])
        return new

    if carry:
        cur = {h: s_ref[0, h] for h in range(N_HEADS)}
        for g in range(n_seg):
            group = [(g, h) for h in range(N_HEADS)]
            new = advance(group, {(g, h): cur[h] for h in range(N_HEADS)})
            cur = {h: new[g, h] for h in range(N_HEADS)}
        for h in range(N_HEADS):
            s_ref[0, h] = cur[h]

        @pl.when(t == n_steps - 1)
        def _():
            for h in range(N_HEADS):
                sout_ref[0, h] = cur[h]
    else:
        new = advance(pairs, {p: s0_ref[p[0], p[1]] for p in pairs})
        for p in pairs:
            sout_ref[p[0], p[1]] = new[p]


def _gdn(p, row0, batch, seq, conv_w, hp, norm_w, s_conv, s0):
    chunk = math.gcd(seq, GDN_CHUNK)
    n_chunks = seq // chunk
    carry = n_chunks > 1
    if carry:
        n_seg = math.gcd(n_chunks, GDN_SEGS_CARRY)
        grid = (batch, n_chunks // n_seg)
        n_conv, conv_len = 1, n_seg * chunk
    else:
        n_seg = math.gcd(batch, GDN_SEGS_INDEP)
        grid = (batch // n_seg, 1)
        n_conv, conv_len = n_seg, chunk
    rows = n_seg * chunk
    n_steps = grid[1]
    blk0 = row0 // rows
    rowblk = lambda b, t: blk0 + b * n_steps + t
    kern = functools.partial(_gdn_kernel, chunk=chunk, n_seg=n_seg, carry=carry, n_steps=n_steps)
    return pl.pallas_call(
        kern,
        out_shape=(jax.ShapeDtypeStruct((batch * seq, W_A), F32),
                   jax.ShapeDtypeStruct((batch, N_HEADS, DK_A, DV_A), F32)),
        grid=grid,
        in_specs=[
            pl.BlockSpec((rows, CONV_DIM), lambda b, t: (rowblk(b, t), OFF_QKV // CONV_DIM)),
            pl.BlockSpec((rows, W_A), lambda b, t: (rowblk(b, t), OFF_Z // W_A)),
            pl.BlockSpec((rows, LANES), lambda b, t: (rowblk(b, t), OFF_SMALL // LANES)),
            pl.BlockSpec((CONV_W, CONV_DIM), lambda b, t: (0, 0)),
            pl.BlockSpec((8, LANES), lambda b, t: (0, 0)),
            pl.BlockSpec((1, DV_A), lambda b, t: (0, 0)),
            pl.BlockSpec((n_conv, CONV_W - 1, CONV_DIM), lambda b, t: (b, 0, 0)),
            pl.BlockSpec((n_conv, N_HEADS, DK_A, DV_A), lambda b, t: (b, 0, 0, 0)),
        ],
        out_specs=(
            pl.BlockSpec((rows, W_A), lambda b, t: (b * n_steps + t, 0)),
            pl.BlockSpec((n_conv, N_HEADS, DK_A, DV_A), lambda b, t: (b, 0, 0, 0)),
        ),
        scratch_shapes=[pltpu.VMEM((1, N_HEADS, DK_A, DV_A), F32),
                        pltpu.VMEM((n_conv, conv_len + 8, CONV_DIM), F32)],
        compiler_params=pltpu.CompilerParams(
            dimension_semantics=("arbitrary", "arbitrary"), vmem_limit_bytes=VMEM_LIMIT),
        name=f"gdn_c{chunk}",
    )(p, p, p, conv_w, hp, norm_w.reshape(1, DV_A), s_conv, s0)


def _gla_kernel(qk_ref, v_ref, g_ref, sm_ref, wgk_ref, bgk_ref, nw_ref, s0_ref,
                o_ref, sout_ref, s_ref, *, tb, cs, carry, n_steps):
    t = pl.program_id(1)
    n_sub = tb // cs
    w_kb = N_HEADS * DK_B

    if carry:
        @pl.when(t == 0)
        def _():
            s_ref[...] = s0_ref[...]

    gk = _dot(sm_ref[...], wgk_ref[...]) + bgk_ref[...]
    log_a = jax.nn.log_sigmoid(gk) / GLA_GATE_NORM
    sh = int(round(math.log2(cs)))
    r_i = lax.broadcasted_iota(jnp.int32, (tb, tb), 0)
    c_i = lax.broadcasted_iota(jnp.int32, (tb, tb), 1)
    same = (r_i >> sh) == (c_i >> sh)
    mask = same & (c_i <= r_i)
    gcum = _dot01(jnp.where(mask, 1.0, 0.0).astype(BF16), log_a)
    gtot = _dot01(jnp.where(same, 1.0, 0.0).astype(BF16), log_a)
    onehot = jnp.where((lax.broadcasted_iota(jnp.int32, (tb, n_sub), 0) >> sh)
                       == lax.broadcasted_iota(jnp.int32, (tb, n_sub), 1), 1.0, 0.0).astype(BF16)
    tn = lambda part: lax.dot_general(part, onehot, (((0,), (0,)), ((), ())), preferred_element_type=F32)
    hi, mid, lo = _split3(log_a)
    glcol = jnp.exp(tn(hi) + tn(mid) + tn(lo))

    qk = qk_ref[...]
    nw = nw_ref[...]
    heads = range(N_HEADS)
    sl = lambda j: slice(j * cs, (j + 1) * cs)
    qg, kd, vh, o_intra = [], [], [], []
    for h in heads:
        qh = qk[:, h * DK_B:(h + 1) * DK_B] * (DK_B ** -0.5)
        kh = qk[:, w_kb + h * DK_B:w_kb + (h + 1) * DK_B]
        gh = gcum[:, h * DK_B:(h + 1) * DK_B]
        gt = gtot[:, h * DK_B:(h + 1) * DK_B]
        qg.append(qh * jnp.exp(gh))
        kd.append(kh * jnp.exp(gt - gh))
        vh.append(v_ref[:, h * DV_B:(h + 1) * DV_B])
        attn = jnp.where(mask, _dot_nt(qg[h], kh * jnp.exp(-gh)), 0.0)
        o_intra.append(_dot(attn, vh[h]))
    upd = {(j, h): _dot_tn(kd[h][sl(j)], vh[h][sl(j)]) for j in range(n_sub) for h in heads}
    gl = lambda j, h: glcol[h * DK_B:(h + 1) * DK_B, j:j + 1]

    o_inter = [[] for _ in heads]
    if carry:
        cur = [s_ref[0, h] for h in heads]
        for j in range(n_sub):
            for h in heads:
                o_inter[h].append(_dot(qg[h][sl(j)], cur[h]))
                cur[h] = cur[h] * gl(j, h) + upd[j, h]
        for h in heads:
            s_ref[0, h] = cur[h]

        @pl.when(t == n_steps - 1)
        def _():
            for h in heads:
                sout_ref[0, h] = cur[h]
    else:
        for j in range(n_sub):
            for h in heads:
                s = s0_ref[j, h]
                o_inter[h].append(_dot(qg[h][sl(j)], s))
                sout_ref[j, h] = s * gl(j, h) + upd[j, h]

    for h in heads:
        o = o_intra[h] + jnp.concatenate(o_inter[h], axis=0)
        r = o * lax.rsqrt(jnp.mean(o * o, axis=-1, keepdims=True) + RMS_EPS) * nw
        o_ref[:, h * DV_B:(h + 1) * DV_B] = r * _silu(g_ref[:, h * DV_B:(h + 1) * DV_B])


def _gla(p, row0, batch, seq, wgk, bgk, norm_w, s0):
    cs = math.gcd(seq, GLA_CHUNK)
    carry = seq > cs
    if carry:
        tb = math.gcd(seq, GLA_TB)
        grid = (batch, seq // tb)
        n_state = 1
    else:
        tb = math.gcd(batch * seq, GLA_TB)
        n_state = tb // cs
        grid = (batch // n_state, 1)
    n_steps = grid[1]
    blk0 = row0 // tb
    rowblk = lambda b, t: blk0 + b * n_steps + t
    kern = functools.partial(_gla_kernel, tb=tb, cs=cs, carry=carry, n_steps=n_steps)
    return pl.pallas_call(
        kern,
        out_shape=(jax.ShapeDtypeStruct((batch * seq, W_B), F32),
                   jax.ShapeDtypeStruct((batch, N_HEADS, DK_B, DV_B), F32)),
        grid=grid,
        in_specs=[
            pl.BlockSpec((tb, 2 * N_HEADS * DK_B), lambda b, t: (rowblk(b, t), OFF_QK_B // (2 * N_HEADS * DK_B))),
            pl.BlockSpec((tb, W_B), lambda b, t: (rowblk(b, t), OFF_V_B // W_B)),
            pl.BlockSpec((tb, W_B), lambda b, t: (rowblk(b, t), OFF_G_B // W_B)),
            pl.BlockSpec((tb, LANES), lambda b, t: (rowblk(b, t), OFF_SMALL // LANES)),
            pl.BlockSpec((LANES, N_HEADS * DK_B), lambda b, t: (0, 0)),
            pl.BlockSpec((1, N_HEADS * DK_B), lambda b, t: (0, 0)),
            pl.BlockSpec((1, DV_B), lambda b, t: (0, 0)),
            pl.BlockSpec((n_state, N_HEADS, DK_B, DV_B), lambda b, t: (b, 0, 0, 0)),
        ],
        out_specs=(
            pl.BlockSpec((tb, W_B), lambda b, t: (b * n_steps + t, 0)),
            pl.BlockSpec((n_state, N_HEADS, DK_B, DV_B), lambda b, t: (b, 0, 0, 0)),
        ),
        scratch_shapes=[pltpu.VMEM((1, N_HEADS, DK_B, DV_B), F32)],
        compiler_params=pltpu.CompilerParams(
            dimension_semantics=("arbitrary", "arbitrary"), vmem_limit_bytes=VMEM_LIMIT),
        name=f"gla_c{cs}",
    )(p, p, p, p, wgk, bgk.reshape(1, N_HEADS * DK_B), norm_w.reshape(1, DV_B), s0)


def _merge_kernel(xp_ref, xs_ref, oap_ref, oas_ref, obp_ref, obs_ref, ga_ref, gb_ref, wua_ref, wub_ref, wo_ref,
                  n2_ref, wr_ref, br_ref, x1_ref, h2_ref, te_ref, tg_ref, *, n_first):
    i = pl.program_id(0)
    ua = jnp.dot(_pick(i, n_first, oap_ref, oas_ref).astype(BF16), wua_ref[...], preferred_element_type=F32)
    ub = jnp.dot(_pick(i, n_first, obp_ref, obs_ref).astype(BF16), wub_ref[...], preferred_element_type=F32)
    merged = jax.nn.sigmoid(ga_ref[...]) * ua + jax.nn.sigmoid(gb_ref[...]) * ub
    x1 = _pick(i, n_first, xp_ref, xs_ref) + jnp.dot(merged.astype(BF16), wo_ref[...], preferred_element_type=F32)
    x1_ref[...] = x1
    h2 = x1 * lax.rsqrt(jnp.mean(x1 * x1, axis=-1, keepdims=True) + RMS_EPS) * n2_ref[...]
    for s in range(TOK_TILE):
        h2_ref[pl.ds(s, x1.shape[0], stride=TOK_TILE), :] = h2[:, s * LANES:(s + 1) * LANES]
    logits = jnp.dot(h2, wr_ref[...], precision=lax.Precision.HIGHEST,
                     preferred_element_type=F32) + br_ref[...]
    tm = logits.shape[0]
    lane = lax.broadcasted_iota(jnp.int32, (tm, LANES), 1)
    lane_f = lane.astype(F32)
    neg = jnp.float32(-jnp.inf)
    cur = jnp.where(lane < N_EXPERTS, logits, neg)
    vals, idxs = [], []
    for _ in range(TOP_K):
        m = jnp.max(cur, axis=-1, keepdims=True)
        idx = jnp.min(jnp.where(cur == m, lane_f, float(LANES)), axis=-1, keepdims=True)
        vals.append(m)
        idxs.append(idx)
        cur = jnp.where(lane_f == idx, neg, cur)
    exps = [jnp.exp(v - vals[0]) for v in vals]
    den = exps[0] + exps[1] + exps[2] + exps[3]
    te = jnp.zeros((tm, LANES), F32)
    tg = jnp.zeros((tm, LANES), F32)
    for k in range(TOP_K):
        te = jnp.where(lane == k, idxs[k], te)
        tg = jnp.where(lane == k, exps[k] / den, tg)
    te_ref[...] = te.astype(jnp.int32)
    tg_ref[...] = tg


def _merge(x_p, x_s, oa_p, oa_s, ob_p, ob_s, p, wua, wub, wo, norm2_w, wr, br):
    n = x_p.shape[0] + x_s.shape[0]
    tm = MERGE_TM
    n_first = x_p.shape[0] // tm
    row = lambda i: (i, 0)
    const = lambda i: (0, 0)
    return pl.pallas_call(
        functools.partial(_merge_kernel, n_first=n_first),
        out_shape=(jax.ShapeDtypeStruct((n, D_MODEL), F32), jax.ShapeDtypeStruct((n * TOK_TILE, LANES), F32),
                   jax.ShapeDtypeStruct((n, LANES), jnp.int32), jax.ShapeDtypeStruct((n, LANES), F32)),
        grid=(n // tm,),
        in_specs=[
            *_split_specs((tm, D_MODEL), n_first),
            *_split_specs((tm, W_A), n_first),
            *_split_specs((tm, W_B), n_first),
            pl.BlockSpec((tm, D_MODEL), lambda i: (i, OFF_GATE_A // D_MODEL)),
            pl.BlockSpec((tm, D_MODEL), lambda i: (i, OFF_GATE_B // D_MODEL)),
            pl.BlockSpec((W_A, D_MODEL), const),
            pl.BlockSpec((W_B, D_MODEL), const),
            pl.BlockSpec((D_MODEL, D_MODEL), const),
            pl.BlockSpec((1, D_MODEL), const),
            pl.BlockSpec((D_MODEL, LANES), const),
            pl.BlockSpec((1, LANES), const),
        ],
        out_specs=(pl.BlockSpec((tm, D_MODEL), row), pl.BlockSpec((tm * TOK_TILE, LANES), row),
                   pl.BlockSpec((tm, LANES), row), pl.BlockSpec((tm, LANES), row)),
        compiler_params=pltpu.CompilerParams(
            dimension_semantics=("arbitrary",), vmem_limit_bytes=VMEM_LIMIT),
        name="merge_router",
    )(x_p, x_s, oa_p, oa_s, ob_p, ob_s, p, p, wua, wub, wo, norm2_w.reshape(1, D_MODEL), wr, br)


def _moe_kernel(blk_e_ref, n_used_ref, src_ref, src_nxt_ref, dst_ref, h2_hbm, wgu_ref, bgu_ref,
                wdn_ref, bdn_ref, y_hbm, xbuf, ybuf, gsem, ssem):
    del blk_e_ref
    i = pl.program_id(0)
    n_used = n_used_ref[0]
    bm = MOE_BM
    tt = TOK_TILE

    def gather_copy(tok, s, r):
        return pltpu.make_async_copy(h2_hbm.at[pl.ds(pl.multiple_of(tok * tt, tt), tt), :],
                                     xbuf.at[s, pl.ds(r * tt, tt), :], gsem.at[s])

    def scatter_copy(row, s, r):
        return pltpu.make_async_copy(ybuf.at[s, pl.ds(r * tt, tt), :],
                                     y_hbm.at[pl.ds(pl.multiple_of(row * tt, tt), tt), :], ssem.at[s])

    @pl.when(i == 0)
    def _():
        ybuf[...] = jnp.zeros_like(ybuf)
        n_real = y_hbm.shape[0] - MOE_SUB * bm * tt
        for s in range(MOE_SUB):
            sink = pltpu.make_async_copy(ybuf.at[s], y_hbm.at[pl.ds(n_real + s * bm * tt, bm * tt), :], ssem.at[s])
            sink.start()
            sink.wait()

    @pl.when((i == 0) & (n_used > 0))
    def _():
        for r in range(bm):
            gather_copy(src_ref[0, 0, r], 0, r).start(priority=r % 2)

    @pl.when(i < n_used)
    def _():
        for s in range(MOE_SUB):
            nxt = (s + 1) % MOE_SUB
            for r in range(bm):
                gather_copy(0, s, r).wait()

            @pl.when(i >= 1)
            def _():
                for r in range(bm):
                    scatter_copy(0, s, r).wait()

            for r in range(bm):
                tok = src_ref[0, 0, (s + 1) * bm + r] if s + 1 < MOE_SUB else src_nxt_ref[0, 0, r]
                gather_copy(tok, nxt, r).start(priority=r % 2)

            x = jnp.concatenate([xbuf[s, pl.ds(c, bm, stride=tt), :].astype(BF16) for c in range(tt)], axis=1)
            gu = jnp.dot(x, wgu_ref[...], preferred_element_type=F32) + bgu_ref[...]
            gate = jnp.minimum(gu[:, :D_FF], SWIGLU_LIMIT)
            up = jnp.clip(gu[:, D_FF:], -SWIGLU_LIMIT, SWIGLU_LIMIT)
            hmid = (up + 1.0) * (gate * jax.nn.sigmoid(SWIGLU_ALPHA * gate))
            y = jnp.dot(hmid.astype(BF16), wdn_ref[...], preferred_element_type=F32) + bdn_ref[...]
            for c in range(tt):
                ybuf[s, pl.ds(c, bm, stride=tt), :] = y[:, c * LANES:(c + 1) * LANES]
            for r in range(bm):
                scatter_copy(dst_ref[0, 0, s * bm + r], s, r).start(priority=r % 2)

    @pl.when(i == n_used - 1)
    def _():
        for r in range(bm):
            gather_copy(0, 0, r).wait()
        for s in range(MOE_SUB):
            for r in range(bm):
                scatter_copy(0, s, r).wait()


def _moe(blk_e, n_used, src_tok, dst_row, h2, wgu, bgu, wdn, bdn, y_rows):
    n_steps = src_tok.shape[0]
    step_rows = MOE_SUB * MOE_BM
    cur = lambda i, be, nu: (i, 0, 0)
    nxt = lambda i, be, nu: (jnp.minimum(i + 1, n_steps - 1), 0, 0)
    wsel = lambda i, be, nu: (be[i], 0, 0)
    smem_blk = lambda im: pl.BlockSpec((1, 1, step_rows), im, memory_space=pltpu.SMEM)
    grid_spec = pltpu.PrefetchScalarGridSpec(
        num_scalar_prefetch=2,
        grid=(n_steps,),
        in_specs=[
            smem_blk(cur), smem_blk(nxt), smem_blk(cur),
            pl.BlockSpec(memory_space=pl.ANY),
            pl.BlockSpec((None, D_MODEL, 2 * D_FF), wsel),
            pl.BlockSpec((None, 1, 2 * D_FF), wsel),
            pl.BlockSpec((None, D_FF, D_MODEL), wsel),
            pl.BlockSpec((None, 1, D_MODEL), wsel),
        ],
        out_specs=pl.BlockSpec(memory_space=pl.ANY),
        scratch_shapes=[pltpu.VMEM((MOE_SUB, MOE_BM * TOK_TILE, LANES), F32),
                        pltpu.VMEM((MOE_SUB, MOE_BM * TOK_TILE, LANES), F32),
                        pltpu.SemaphoreType.DMA((MOE_SUB,)), pltpu.SemaphoreType.DMA((MOE_SUB,))],
    )
    return pl.pallas_call(
        _moe_kernel,
        out_shape=jax.ShapeDtypeStruct((y_rows * TOK_TILE, LANES), F32),
        grid_spec=grid_spec,
        compiler_params=pltpu.CompilerParams(
            dimension_semantics=("arbitrary",), vmem_limit_bytes=VMEM_LIMIT),
        name="moe_experts",
    )(blk_e, n_used, src_tok, src_tok, dst_row, h2, wgu, bgu, wdn, bdn)


def _route(top_e, n_tok):
    bm = MOE_SUB * MOE_BM
    n_assign = n_tok * TOP_K
    n_blocks = n_assign // bm + N_EXPERTS
    n_dummy = N_EXPERTS * bm
    flat_e = top_e.reshape(-1)
    counts = jnp.sum((flat_e[:, None] == jnp.arange(N_EXPERTS, dtype=jnp.int32)[None, :]).astype(jnp.int32), axis=0)
    padded = (counts + bm - 1) // bm * bm
    pad_end = jnp.cumsum(padded)
    d_e = jnp.arange(n_dummy, dtype=jnp.int32) // bm
    d_active = (jnp.arange(n_dummy, dtype=jnp.int32) % bm) < (padded - counts)[d_e]
    keys = jnp.concatenate([flat_e * 2, jnp.where(d_active, d_e * 2 + 1, 2 * N_EXPERTS)])
    vals = jnp.concatenate([jnp.arange(n_assign, dtype=jnp.int32), jnp.full((n_dummy,), -1, jnp.int32)])
    _, assign = lax.sort((keys, vals), num_keys=1, is_stable=True)
    assign = assign[:n_blocks * bm]
    rows = jnp.arange(n_blocks * bm, dtype=jnp.int32)
    valid = assign >= 0
    src_tok = jnp.where(valid, assign // TOP_K, 0)
    dst_row = jnp.where(valid, (assign % TOP_K) * n_tok + assign // TOP_K, n_assign + rows % bm)
    blk_start = jnp.arange(n_blocks, dtype=jnp.int32) * bm
    blk_e = jnp.minimum(jnp.sum((pad_end[None, :] <= blk_start[:, None]).astype(jnp.int32), axis=1),
                        N_EXPERTS - 1).astype(jnp.int32)
    n_used = (pad_end[-1] // bm).astype(jnp.int32).reshape(1)
    return (blk_e, n_used, src_tok.reshape(n_blocks, 1, bm).astype(jnp.int32),
            dst_row.reshape(n_blocks, 1, bm).astype(jnp.int32))


def _combine_kernel(x1_ref, y0_ref, y1_ref, y2_ref, y3_ref, tg_ref, nf_ref, op_ref, os_ref, *, n_first):
    tg = tg_ref[...]
    tm = x1_ref.shape[0]
    rows = lambda y_ref: jnp.concatenate(
        [y_ref[pl.ds(c, tm, stride=TOK_TILE), :] for c in range(TOK_TILE)], axis=1)
    acc = rows(y0_ref) * tg[:, 0:1]
    for k, y_ref in enumerate((y1_ref, y2_ref, y3_ref), start=1):
        acc = acc + rows(y_ref) * tg[:, k:k + 1]
    x = x1_ref[...] + acc
    out = x * lax.rsqrt(jnp.mean(x * x, axis=-1, keepdims=True) + RMS_EPS) * nf_ref[...]
    i = pl.program_id(0)

    @pl.when(i < n_first)
    def _():
        op_ref[...] = out

    @pl.when(i >= n_first)
    def _():
        os_ref[...] = out


def _combine(x1, y, tg, norm_f_w, n_p):
    n = x1.shape[0]
    tm = COMBINE_TM
    n_first = n_p // tm
    row = lambda i: (i, 0)
    y_spec = lambda k: pl.BlockSpec((tm * TOK_TILE, LANES), lambda i: (k * (n // tm) + i, 0))
    return pl.pallas_call(
        functools.partial(_combine_kernel, n_first=n_first),
        out_shape=(jax.ShapeDtypeStruct((n_p, D_MODEL), F32), jax.ShapeDtypeStruct((n - n_p, D_MODEL), F32)),
        grid=(n // tm,),
        in_specs=[pl.BlockSpec((tm, D_MODEL), row), y_spec(0), y_spec(1), y_spec(2), y_spec(3),
                  pl.BlockSpec((tm, LANES), row), pl.BlockSpec((1, D_MODEL), lambda i: (0, 0))],
        out_specs=_split_specs((tm, D_MODEL), n_first),
        compiler_params=pltpu.CompilerParams(
            dimension_semantics=("arbitrary",), vmem_limit_bytes=VMEM_LIMIT),
        name="combine_norm",
    )(x1, y, y, y, y, tg, norm_f_w.reshape(1, D_MODEL))


def _pack_w_in(w_in):
    s = np.cumsum((0,) + IN_SIZES)
    qkv, z, al, be, qb, kb, vb, gb, gk, ga, gbt = [w_in[:, s[i]:s[i + 1]] for i in range(len(IN_SIZES))]
    small = jnp.concatenate(
        [al, be, gk, jnp.zeros((D_MODEL, LANES - 2 * N_HEADS - GK_RANK), w_in.dtype)], axis=1)
    return jnp.concatenate([qkv, z, ga, gbt, qb, kb, vb, gb, small], axis=1).astype(BF16)


def kernel(x_prompt, x_sample, state_gdn, state_gdn_conv, state_gla, norm1_w, w_in, conv_w, a_log,
           dt_bias, gdn_norm_w, gla_gk_up, gla_gk_b, gla_norm_w, w_up_a, w_up_b, w_o, norm2_w,
           w_router, b_router, w_gate_up, b_gate_up, w_down, b_down, norm_f_w):
    bp, tp, _ = x_prompt.shape
    bs, ts, _ = x_sample.shape
    n_p, n_s = bp * tp, bs * ts
    n = n_p + n_s
    l = 0

    x_p = x_prompt.reshape(n_p, D_MODEL)
    x_s = x_sample.reshape(n_s, D_MODEL)
    p = _inproj(x_p, x_s, norm1_w[l], _pack_w_in(w_in[l]))

    hp = jnp.zeros((8, LANES), F32).at[0, :N_HEADS].set(a_log[l]).at[1, :N_HEADS].set(dt_bias[l])
    wgk = jnp.zeros((LANES, N_HEADS * DK_B), F32).at[2 * N_HEADS:2 * N_HEADS + GK_RANK].set(gla_gk_up[l]).astype(BF16)
    zeros = lambda shape: jnp.zeros(shape, F32)

    oa_p, gdn_p = _gdn(p, 0, bp, tp, conv_w[l], hp, gdn_norm_w[l],
                       zeros((bp, CONV_W - 1, CONV_DIM)), zeros((bp, N_HEADS, DK_A, DV_A)))
    oa_s, gdn_s = _gdn(p, n_p, bs, ts, conv_w[l], hp, gdn_norm_w[l], state_gdn_conv[l], state_gdn[l])
    ob_p, gla_p = _gla(p, 0, bp, tp, wgk, gla_gk_b[l], gla_norm_w[l], zeros((bp, N_HEADS, DK_B, DV_B)))
    ob_s, gla_s = _gla(p, n_p, bs, ts, wgk, gla_gk_b[l], gla_norm_w[l], state_gla[l])

    wr = jnp.zeros((D_MODEL, LANES), F32).at[:, :N_EXPERTS].set(w_router[l])
    br = jnp.zeros((1, LANES), F32).at[0, :N_EXPERTS].set(b_router[l])
    x1, h2, te, tg = _merge(x_p, x_s, oa_p, oa_s, ob_p, ob_s, p, w_up_a[l].astype(BF16),
                            w_up_b[l].astype(BF16), w_o[l].astype(BF16), norm2_w[l], wr, br)

    blk_e, n_used, src_tok, dst_row = _route(te[:, :TOP_K], n)
    y = _moe(blk_e, n_used, src_tok, dst_row, h2, w_gate_up[l].astype(BF16),
             b_gate_up[l].reshape(N_EXPERTS, 1, 2 * D_FF), w_down[l].astype(BF16),
             b_down[l].reshape(N_EXPERTS, 1, D_MODEL), n * TOP_K + MOE_SUB * MOE_BM)
    out_p, out_s = _combine(x1, y, tg, norm_f_w, n_p)
    y_prompt = out_p.reshape(bp, tp, D_MODEL)
    y_sample = out_s.reshape(bs, ts, D_MODEL)
    qkv_p = p[:n_p, :CONV_DIM].reshape(bp, tp, CONV_DIM)
    qkv_s = p[n_p:, :CONV_DIM].reshape(bs, ts, CONV_DIM)
    conv_p = qkv_p[:, tp - (CONV_W - 1):, :]
    conv_s = qkv_s[:, ts - (CONV_W - 1):, :]
    return (y_prompt, y_sample, gdn_p[None], conv_p[None], gla_p[None], gdn_s[None], conv_s[None],
            gla_s[None])
```

```python
import functools
import math

import jax
import jax.numpy as jnp
import numpy as np
from jax import lax
from jax.experimental import pallas as pl
from jax.experimental.pallas import tpu as pltpu

F32 = jnp.float32
BF16 = jnp.bfloat16

D_MODEL = 1024
N_HEADS = 4
DK_A = 128
DV_A = 128
W_A = N_HEADS * DV_A
CONV_W = 4
CONV_DIM = 2 * N_HEADS * DK_A + W_A
GDN_CHUNK = 64
DK_B = 64
DV_B = 128
W_B = N_HEADS * DV_B
GK_RANK = 16
GLA_GATE_NORM = 16.0
GLA_CHUNK = 16
N_EXPERTS = 32
TOP_K = 4
D_FF = 1024
SWIGLU_LIMIT = 7.0
SWIGLU_ALPHA = 1.702
RMS_EPS = 1e-6
IN_SIZES = (CONV_DIM, W_A, N_HEADS, N_HEADS, N_HEADS * DK_B, N_HEADS * DK_B, W_B, W_B, GK_RANK,
            D_MODEL, D_MODEL)

LANES = 128

OFF_QKV = 0
OFF_Z = 1536
OFF_GATE_A = 2048
OFF_GATE_B = 3072
OFF_QK_B = 4096
OFF_V_B = 4608
OFF_G_B = 5120
OFF_SMALL = 5632
P_COLS = 5760

INPROJ_TM = 512
GLA_TB = 128
GDN_SEGS_CARRY = 4
GDN_SEGS_INDEP = 8
MERGE_TM = 512
MOE_BM = 256
MOE_SUB = 2
TOK_TILE = D_MODEL // LANES
PACK_ROWS = D_MODEL // 2 // LANES
MOE_VMEM_LIMIT = 60 * 1024 * 1024
COMBINE_TM = 256
VMEM_LIMIT = 56 * 1024 * 1024


def _dot(a, b):
    return jnp.dot(a.astype(BF16), b.astype(BF16), preferred_element_type=F32)


def _dot_nt(a, b):
    return lax.dot_general(a.astype(BF16), b.astype(BF16), (((1,), (1,)), ((), ())),
                           preferred_element_type=F32)


def _dot_tn(a, b):
    return lax.dot_general(a, b, (((0,), (0,)), ((), ())), preferred_element_type=F32)


def _split3(x):
    hi = x.astype(BF16)
    r1 = x - hi.astype(F32)
    mid = r1.astype(BF16)
    lo = (r1 - mid.astype(F32)).astype(BF16)
    return hi, mid, lo


def _dot01(m01, x):
    hi, mid, lo = _split3(x)
    d = lambda p: jnp.dot(m01, p, preferred_element_type=F32)
    return d(hi) + d(mid) + d(lo)


def _silu(x):
    return x * jax.nn.sigmoid(x)


def _pick(i, n_first, first_ref, second_ref):
    return jnp.where(i < n_first, first_ref[...], second_ref[...])


def _split_specs(block, n_first):
    return (pl.BlockSpec(block, lambda i: (jnp.minimum(i, n_first - 1), 0)),
            pl.BlockSpec(block, lambda i: (jnp.maximum(i - n_first, 0), 0)))


def _inproj_kernel(xp_ref, xs_ref, nw_ref, w_ref, o_ref, *, n_first):
    x = _pick(pl.program_id(0), n_first, xp_ref, xs_ref)
    ms = jnp.mean(x * x, axis=-1, keepdims=True)
    h = (x * lax.rsqrt(ms + RMS_EPS) * nw_ref[...]).astype(BF16)
    o_ref[...] = jnp.dot(h, w_ref[...], preferred_element_type=F32)


def _inproj(x_p, x_s, norm_w, w_packed):
    tm = INPROJ_TM
    n_first = x_p.shape[0] // tm
    n = x_p.shape[0] + x_s.shape[0]
    return pl.pallas_call(
        functools.partial(_inproj_kernel, n_first=n_first),
        out_shape=jax.ShapeDtypeStruct((n, P_COLS), F32),
        grid=(n // tm,),
        in_specs=[
            *_split_specs((tm, D_MODEL), n_first),
            pl.BlockSpec((1, D_MODEL), lambda i: (0, 0)),
            pl.BlockSpec((D_MODEL, P_COLS), lambda i: (0, 0), pipeline_mode=pl.Buffered(1)),
        ],
        out_specs=pl.BlockSpec((tm, P_COLS), lambda i: (i, 0)),
        compiler_params=pltpu.CompilerParams(
            dimension_semantics=("arbitrary",), vmem_limit_bytes=VMEM_LIMIT),
        name="inproj",
    )(x_p, x_s, norm_w.reshape(1, D_MODEL), w_packed)


def _gdn_kernel(qkv_ref, z_ref, sm_ref, cw_ref, hp_ref, nw_ref, sconv_ref, s0_ref,
                o_ref, sout_ref, cout_ref, s_ref, ubuf_ref, *, chunk, n_seg, carry, n_steps):
    c_len = chunk
    rows = n_seg * c_len
    t = pl.program_id(1)
    hist = CONV_W - 1
    base = 8
    conv_len = rows if carry else c_len
    n_conv = 1 if carry else n_seg

    if carry:
        @pl.when(t == 0)
        def _():
            s_ref[...] = s0_ref[...]
            ubuf_ref[:, base - hist:base, :] = sconv_ref[...]

        @pl.when(t > 0)
        def _():
            ubuf_ref[:, base - hist:base, :] = ubuf_ref[:, base + conv_len - hist:base + conv_len, :]
    else:
        ubuf_ref[:, base - hist:base, :] = sconv_ref[...]
    ubuf_ref[:, base:base + conv_len, :] = qkv_ref[...].reshape(n_conv, conv_len, CONV_DIM)

    cw = cw_ref[...]
    y = ubuf_ref[:, base - hist:base - hist + conv_len, :] * cw[0:1, :]
    for j in range(1, CONV_W):
        y = y + ubuf_ref[:, base - hist + j:base - hist + j + conv_len, :] * cw[j:j + 1, :]
    y = _silu(y).reshape(rows, CONV_DIM)

    sm = sm_ref[...]
    hp = hp_ref[...]
    lane = lax.broadcasted_iota(jnp.int32, (rows, LANES), 1)
    g_all = -jnp.exp(hp[0:1, :]) * jax.nn.softplus(sm + hp[1:2, :])
    g_all = jnp.where(lane < N_HEADS, g_all, 0.0)
    beta_all = jax.nn.sigmoid(sm)

    sh = int(round(math.log2(c_len)))
    r_b = lax.broadcasted_iota(jnp.int32, (rows, rows), 0)
    c_b = lax.broadcasted_iota(jnp.int32, (rows, rows), 1)
    seg_tril = ((r_b >> sh) == (c_b >> sh)) & (c_b <= r_b)
    gc = _dot01(jnp.where(seg_tril, 1.0, 0.0).astype(BF16), g_all)

    r_i = lax.broadcasted_iota(jnp.int32, (c_len, c_len), 0)
    c_i = lax.broadcasted_iota(jnp.int32, (c_len, c_len), 1)
    tri = c_i <= r_i
    strict = c_i < r_i
    eye = c_i == r_i
    nw = nw_ref[...]
    levels = sh

    qn, kn, kb, rhs, qg, gcol = [], [], [], [], [], []
    for h in range(N_HEADS):
        qh = y[:, h * DK_A:(h + 1) * DK_A]
        kh = y[:, W_A + h * DK_A:W_A + (h + 1) * DK_A]
        vh = y[:, 2 * W_A + h * DV_A:2 * W_A + (h + 1) * DV_A]
        qn_h = qh * lax.rsqrt(jnp.sum(qh * qh, axis=-1, keepdims=True) + 1e-6) * (DK_A ** -0.5)
        kn_h = kh * lax.rsqrt(jnp.sum(kh * kh, axis=-1, keepdims=True) + 1e-6)
        gcol_h = gc[:, h:h + 1]
        eg = jnp.exp(gcol_h)
        bcol = beta_all[:, N_HEADS + h:N_HEADS + h + 1]
        kb_h = kn_h * bcol
        qn.append(qn_h)
        kn.append(kn_h)
        kb.append(kb_h)
        rhs.append(jnp.concatenate([vh * bcol, kb_h * eg], axis=1))
        qg.append(qn_h * eg)
        gcol.append(gcol_h)

    pairs = [(g, h) for g in range(n_seg) for h in range(N_HEADS)]
    sl = lambda g: slice(g * c_len, (g + 1) * c_len)

    decay, kd, gl = {}, {}, {}
    for (g, h) in pairs:
        gcol_p = gcol[h][sl(g)]
        gcb = jnp.broadcast_to(gcol_p, (c_len, c_len))
        grow = jnp.sum(jnp.where(eye, gcb, 0.0), axis=0, keepdims=True)
        decay[g, h] = jnp.where(tri, jnp.exp(jnp.where(tri, gcb - grow, 0.0)), 0.0)
        glast = gcol[h][(g + 1) * c_len - 1:(g + 1) * c_len]
        kd[g, h] = kn[h][sl(g)] * jnp.exp(glast - gcol_p)
        gl[g, h] = jnp.exp(glast)

    kq = {p: _dot_nt(jnp.concatenate([kb[p[1]][sl(p[0])], qn[p[1]][sl(p[0])]], axis=0), kn[p[1]][sl(p[0])])
          for p in pairs}
    low = {p: jnp.where(strict, kq[p][:c_len] * decay[p], 0.0) for p in pairs}
    attn = {p: jnp.where(tri, kq[p][c_len:] * decay[p], 0.0) for p in pairs}
    ymat = {p: -low[p] for p in pairs}
    lp = low
    for _ in range(levels - 1):
        lp = {p: _dot(lp[p], lp[p]) for p in pairs}
        prod = {p: _dot(ymat[p], lp[p]) for p in pairs}
        ymat = {p: ymat[p] + lp[p] + prod[p] for p in pairs}
    sol = {}
    for (g, h) in pairs:
        rhs_p = rhs[h][sl(g)]
        sol[g, h] = rhs_p + _dot(ymat[g, h], rhs_p)

    def finish(g, h, o):
        r = o * lax.rsqrt(jnp.mean(o * o, axis=-1, keepdims=True) + RMS_EPS) * nw
        o_ref[sl(g), h * DV_A:(h + 1) * DV_A] = r * _silu(z_ref[sl(g), h * DV_A:(h + 1) * DV_A])

    def advance(group, state):
        ws = {p: _dot(jnp.concatenate([sol[p][:, DV_A:], qg[p[1]][sl(p[0])]], axis=0), state[p]) for p in group}
        v_new = {p: sol[p][:, :DV_A] - ws[p][:c_len] for p in group}
        o = {p: ws[p][c_len:] + _dot(attn[p], v_new[p]) for p in group}
        new = {p: state[p] * gl[p] + _dot_tn(kd[p], v_new[p]) for p in group}
        for p in group:
            finish(p[0], p[1], o[p])
        return new

    if carry:
        cur = {h: s_ref[0, h] for h in range(N_HEADS)}
        for g in range(n_seg):
            group = [(g, h) for h in range(N_HEADS)]
            new = advance(group, {(g, h): cur[h] for h in range(N_HEADS)})
            cur = {h: new[g, h] for h in range(N_HEADS)}
        for h in range(N_HEADS):
            s_ref[0, h] = cur[h]

        @pl.when(t == n_steps - 1)
        def _():
            for h in range(N_HEADS):
                sout_ref[0, h] = cur[h]
            cout_ref[...] = ubuf_ref[:, base + conv_len - hist:base + conv_len, :]
    else:
        new = advance(pairs, {p: s0_ref[p[0], p[1]] for p in pairs})
        for p in pairs:
            sout_ref[p[0], p[1]] = new[p]
        cout_ref[...] = ubuf_ref[:, base + conv_len - hist:base + conv_len, :]


def _gdn(p, row0, batch, seq, conv_w, hp, norm_w, s_conv, s0):
    chunk = math.gcd(seq, GDN_CHUNK)
    n_chunks = seq // chunk
    carry = n_chunks > 1
    if carry:
        n_seg = math.gcd(n_chunks, GDN_SEGS_CARRY)
        grid = (batch, n_chunks // n_seg)
        n_conv, conv_len = 1, n_seg * chunk
    else:
        n_seg = math.gcd(batch, GDN_SEGS_INDEP)
        grid = (batch // n_seg, 1)
        n_conv, conv_len = n_seg, chunk
    rows = n_seg * chunk
    n_steps = grid[1]
    blk0 = row0 // rows
    rowblk = lambda b, t: blk0 + b * n_steps + t
    kern = functools.partial(_gdn_kernel, chunk=chunk, n_seg=n_seg, carry=carry, n_steps=n_steps)
    return pl.pallas_call(
        kern,
        out_shape=(jax.ShapeDtypeStruct((batch * seq, W_A), F32),
                   jax.ShapeDtypeStruct((batch, N_HEADS, DK_A, DV_A), F32),
                   jax.ShapeDtypeStruct((batch, CONV_W - 1, CONV_DIM), F32)),
        grid=grid,
        in_specs=[
            pl.BlockSpec((rows, CONV_DIM), lambda b, t: (rowblk(b, t), OFF_QKV // CONV_DIM)),
            pl.BlockSpec((rows, W_A), lambda b, t: (rowblk(b, t), OFF_Z // W_A)),
            pl.BlockSpec((rows, LANES), lambda b, t: (rowblk(b, t), OFF_SMALL // LANES)),
            pl.BlockSpec((CONV_W, CONV_DIM), lambda b, t: (0, 0)),
            pl.BlockSpec((8, LANES), lambda b, t: (0, 0)),
            pl.BlockSpec((1, DV_A), lambda b, t: (0, 0)),
            pl.BlockSpec((n_conv, CONV_W - 1, CONV_DIM), lambda b, t: (b, 0, 0)),
            pl.BlockSpec((n_conv, N_HEADS, DK_A, DV_A), lambda b, t: (b, 0, 0, 0)),
        ],
        out_specs=(
            pl.BlockSpec((rows, W_A), lambda b, t: (b * n_steps + t, 0)),
            pl.BlockSpec((n_conv, N_HEADS, DK_A, DV_A), lambda b, t: (b, 0, 0, 0)),
            pl.BlockSpec((n_conv, CONV_W - 1, CONV_DIM), lambda b, t: (b, 0, 0)),
        ),
        scratch_shapes=[pltpu.VMEM((1, N_HEADS, DK_A, DV_A), F32),
                        pltpu.VMEM((n_conv, conv_len + 8, CONV_DIM), F32)],
        compiler_params=pltpu.CompilerParams(
            dimension_semantics=("arbitrary", "arbitrary"), vmem_limit_bytes=VMEM_LIMIT),
        name=f"gdn_c{chunk}",
    )(p, p, p, conv_w, hp, norm_w.reshape(1, DV_A), s_conv, s0)


def _gla_kernel(qk_ref, v_ref, g_ref, sm_ref, wgk_ref, bgk_ref, nw_ref, s0_ref,
                o_ref, sout_ref, s_ref, *, tb, cs, carry, n_steps):
    t = pl.program_id(1)
    n_sub = tb // cs
    w_kb = N_HEADS * DK_B

    if carry:
        @pl.when(t == 0)
        def _():
            s_ref[...] = s0_ref[...]

    gk = _dot(sm_ref[...], wgk_ref[...]) + bgk_ref[...]
    log_a = jax.nn.log_sigmoid(gk) / GLA_GATE_NORM
    sh = int(round(math.log2(cs)))
    r_i = lax.broadcasted_iota(jnp.int32, (tb, tb), 0)
    c_i = lax.broadcasted_iota(jnp.int32, (tb, tb), 1)
    same = (r_i >> sh) == (c_i >> sh)
    mask = same & (c_i <= r_i)
    gcum = _dot01(jnp.where(mask, 1.0, 0.0).astype(BF16), log_a)
    gtot = _dot01(jnp.where(same, 1.0, 0.0).astype(BF16), log_a)
    onehot = jnp.where((lax.broadcasted_iota(jnp.int32, (tb, n_sub), 0) >> sh)
                       == lax.broadcasted_iota(jnp.int32, (tb, n_sub), 1), 1.0, 0.0).astype(BF16)
    tn = lambda part: lax.dot_general(part, onehot, (((0,), (0,)), ((), ())), preferred_element_type=F32)
    hi, mid, lo = _split3(log_a)
    glcol = jnp.exp(tn(hi) + tn(mid) + tn(lo))

    qk = qk_ref[...]
    nw = nw_ref[...]
    heads = range(N_HEADS)
    sl = lambda j: slice(j * cs, (j + 1) * cs)
    qg, kd, vh, o_intra = [], [], [], []
    for h in heads:
        qh = qk[:, h * DK_B:(h + 1) * DK_B] * (DK_B ** -0.5)
        kh = qk[:, w_kb + h * DK_B:w_kb + (h + 1) * DK_B]
        gh = gcum[:, h * DK_B:(h + 1) * DK_B]
        gt = gtot[:, h * DK_B:(h + 1) * DK_B]
        qg.append(qh * jnp.exp(gh))
        kd.append(kh * jnp.exp(gt - gh))
        vh.append(v_ref[:, h * DV_B:(h + 1) * DV_B])
        attn = jnp.where(mask, _dot_nt(qg[h], kh * jnp.exp(-gh)), 0.0)
        o_intra.append(_dot(attn, vh[h]))
    upd = {(j, h): _dot_tn(kd[h][sl(j)], vh[h][sl(j)]) for j in range(n_sub) for h in heads}
    gl = lambda j, h: glcol[h * DK_B:(h + 1) * DK_B, j:j + 1]

    o_inter = [[] for _ in heads]
    if carry:
        cur = [s_ref[0, h] for h in heads]
        for j in range(n_sub):
            for h in heads:
                o_inter[h].append(_dot(qg[h][sl(j)], cur[h]))
                cur[h] = cur[h] * gl(j, h) + upd[j, h]
        for h in heads:
            s_ref[0, h] = cur[h]

        @pl.when(t == n_steps - 1)
        def _():
            for h in heads:
                sout_ref[0, h] = cur[h]
    else:
        for j in range(n_sub):
            for h in heads:
                s = s0_ref[j, h]
                o_inter[h].append(_dot(qg[h][sl(j)], s))
                sout_ref[j, h] = s * gl(j, h) + upd[j, h]

    for h in heads:
        o = o_intra[h] + jnp.concatenate(o_inter[h], axis=0)
        r = o * lax.rsqrt(jnp.mean(o * o, axis=-1, keepdims=True) + RMS_EPS) * nw
        o_ref[:, h * DV_B:(h + 1) * DV_B] = r * _silu(g_ref[:, h * DV_B:(h + 1) * DV_B])


def _gla(p, row0, batch, seq, wgk, bgk, norm_w, s0):
    cs = math.gcd(seq, GLA_CHUNK)
    carry = seq > cs
    if carry:
        tb = math.gcd(seq, GLA_TB)
        grid = (batch, seq // tb)
        n_state = 1
    else:
        tb = math.gcd(batch * seq, GLA_TB)
        n_state = tb // cs
        grid = (batch // n_state, 1)
    n_steps = grid[1]
    blk0 = row0 // tb
    rowblk = lambda b, t: blk0 + b * n_steps + t
    kern = functools.partial(_gla_kernel, tb=tb, cs=cs, carry=carry, n_steps=n_steps)
    return pl.pallas_call(
        kern,
        out_shape=(jax.ShapeDtypeStruct((batch * seq, W_B), F32),
                   jax.ShapeDtypeStruct((batch, N_HEADS, DK_B, DV_B), F32)),
        grid=grid,
        in_specs=[
            pl.BlockSpec((tb, 2 * N_HEADS * DK_B), lambda b, t: (rowblk(b, t), OFF_QK_B // (2 * N_HEADS * DK_B))),
            pl.BlockSpec((tb, W_B), lambda b, t: (rowblk(b, t), OFF_V_B // W_B)),
            pl.BlockSpec((tb, W_B), lambda b, t: (rowblk(b, t), OFF_G_B // W_B)),
            pl.BlockSpec((tb, LANES), lambda b, t: (rowblk(b, t), OFF_SMALL // LANES)),
            pl.BlockSpec((LANES, N_HEADS * DK_B), lambda b, t: (0, 0)),
            pl.BlockSpec((1, N_HEADS * DK_B), lambda b, t: (0, 0)),
            pl.BlockSpec((1, DV_B), lambda b, t: (0, 0)),
            pl.BlockSpec((n_state, N_HEADS, DK_B, DV_B), lambda b, t: (b, 0, 0, 0)),
        ],
        out_specs=(
            pl.BlockSpec((tb, W_B), lambda b, t: (b * n_steps + t, 0)),
            pl.BlockSpec((n_state, N_HEADS, DK_B, DV_B), lambda b, t: (b, 0, 0, 0)),
        ),
        scratch_shapes=[pltpu.VMEM((1, N_HEADS, DK_B, DV_B), F32)],
        compiler_params=pltpu.CompilerParams(
            dimension_semantics=("arbitrary", "arbitrary"), vmem_limit_bytes=VMEM_LIMIT),
        name=f"gla_c{cs}",
    )(p, p, p, p, wgk, bgk.reshape(1, N_HEADS * DK_B), norm_w.reshape(1, DV_B), s0)


def _merge_kernel(xp_ref, xs_ref, oap_ref, oas_ref, obp_ref, obs_ref, ga_ref, gb_ref, wua_ref, wub_ref, wo_ref,
                  n2_ref, wr_ref, br_ref, x1_ref, h2_ref, te_ref, tg_ref, *, n_first):
    i = pl.program_id(0)
    ua = jnp.dot(_pick(i, n_first, oap_ref, oas_ref).astype(BF16), wua_ref[...], preferred_element_type=F32)
    ub = jnp.dot(_pick(i, n_first, obp_ref, obs_ref).astype(BF16), wub_ref[...], preferred_element_type=F32)
    merged = jax.nn.sigmoid(ga_ref[...]) * ua + jax.nn.sigmoid(gb_ref[...]) * ub
    x1 = _pick(i, n_first, xp_ref, xs_ref) + jnp.dot(merged.astype(BF16), wo_ref[...], preferred_element_type=F32)
    x1_ref[...] = x1
    h2 = x1 * lax.rsqrt(jnp.mean(x1 * x1, axis=-1, keepdims=True) + RMS_EPS) * n2_ref[...]
    bits = pltpu.bitcast(h2.astype(BF16).astype(F32), jnp.uint32)
    half = D_MODEL // 2
    words = (bits[:, :half] >> 16) | (bits[:, half:] & jnp.uint32(0xFFFF0000))
    for s in range(PACK_ROWS):
        h2_ref[pl.ds(s, x1.shape[0], stride=PACK_ROWS), :] = words[:, s * LANES:(s + 1) * LANES]
    logits = jnp.dot(h2, wr_ref[...], precision=lax.Precision.HIGHEST,
                     preferred_element_type=F32) + br_ref[...]
    tm = logits.shape[0]
    lane = lax.broadcasted_iota(jnp.int32, (tm, LANES), 1)
    lane_f = lane.astype(F32)
    neg = jnp.float32(-jnp.inf)
    cur = jnp.where(lane < N_EXPERTS, logits, neg)
    vals, idxs = [], []
    for _ in range(TOP_K):
        m = jnp.max(cur, axis=-1, keepdims=True)
        idx = jnp.min(jnp.where(cur == m, lane_f, float(LANES)), axis=-1, keepdims=True)
        vals.append(m)
        idxs.append(idx)
        cur = jnp.where(lane_f == idx, neg, cur)
    exps = [jnp.exp(v - vals[0]) for v in vals]
    den = exps[0] + exps[1] + exps[2] + exps[3]
    te = jnp.zeros((tm, LANES), F32)
    tg = jnp.zeros((tm, LANES), F32)
    for k in range(TOP_K):
        te = jnp.where(lane == k, idxs[k], te)
        tg = jnp.where(lane == k, exps[k] / den, tg)
    te_ref[...] = te.astype(jnp.int32)
    tg_ref[...] = tg


def _merge(x_p, x_s, oa_p, oa_s, ob_p, ob_s, p, wua, wub, wo, norm2_w, wr, br):
    n = x_p.shape[0] + x_s.shape[0]
    tm = MERGE_TM
    n_first = x_p.shape[0] // tm
    row = lambda i: (i, 0)
    const = lambda i: (0, 0)
    return pl.pallas_call(
        functools.partial(_merge_kernel, n_first=n_first),
        out_shape=(jax.ShapeDtypeStruct((n, D_MODEL), F32), jax.ShapeDtypeStruct((n * PACK_ROWS, LANES), jnp.uint32),
                   jax.ShapeDtypeStruct((n, LANES), jnp.int32), jax.ShapeDtypeStruct((n, LANES), F32)),
        grid=(n // tm,),
        in_specs=[
            *_split_specs((tm, D_MODEL), n_first),
            *_split_specs((tm, W_A), n_first),
            *_split_specs((tm, W_B), n_first),
            pl.BlockSpec((tm, D_MODEL), lambda i: (i, OFF_GATE_A // D_MODEL)),
            pl.BlockSpec((tm, D_MODEL), lambda i: (i, OFF_GATE_B // D_MODEL)),
            pl.BlockSpec((W_A, D_MODEL), const),
            pl.BlockSpec((W_B, D_MODEL), const),
            pl.BlockSpec((D_MODEL, D_MODEL), const),
            pl.BlockSpec((1, D_MODEL), const),
            pl.BlockSpec((D_MODEL, LANES), const),
            pl.BlockSpec((1, LANES), const),
        ],
        out_specs=(pl.BlockSpec((tm, D_MODEL), row), pl.BlockSpec((tm * PACK_ROWS, LANES), row),
                   pl.BlockSpec((tm, LANES), row), pl.BlockSpec((tm, LANES), row)),
        compiler_params=pltpu.CompilerParams(
            dimension_semantics=("arbitrary",), vmem_limit_bytes=VMEM_LIMIT),
        name="merge_router",
    )(x_p, x_s, oa_p, oa_s, ob_p, ob_s, p, p, wua, wub, wo, norm2_w.reshape(1, D_MODEL), wr, br)


def _moe_kernel(blk_e_ref, n_used_ref, src_ref, dst_ref, h2_ref, wgu_ref, bgu_ref,
                wdn_ref, bdn_ref, y_hbm, xstage, ybuf, ssem):
    del blk_e_ref
    i = pl.program_id(0)
    n_used = n_used_ref[0]
    bm = MOE_BM
    tt = TOK_TILE
    pr = PACK_ROWS

    def scatter_copy(row, s, r):
        return pltpu.make_async_copy(ybuf.at[s, pl.ds(r * tt, tt), :],
                                     y_hbm.at[pl.ds(pl.multiple_of(row * tt, tt), tt), :], ssem.at[s])

    @pl.when(i == 0)
    def _():
        ybuf[...] = jnp.zeros_like(ybuf)
        n_real = y_hbm.shape[0] - MOE_SUB * bm * tt
        for s in range(MOE_SUB):
            sink = pltpu.make_async_copy(ybuf.at[s], y_hbm.at[pl.ds(n_real + s * bm * tt, bm * tt), :], ssem.at[s])
            sink.start()
            sink.wait()

    @pl.when(i < n_used)
    def _():
        for s in range(MOE_SUB):
            @pl.when(i >= 1)
            def _():
                for r in range(bm):
                    scatter_copy(0, s, r).wait()

            for r in range(bm):
                tok = pl.multiple_of(src_ref[0, 0, s * bm + r] * pr, pr)
                xstage[s, pl.ds(r * pr, pr), :] = h2_ref[pl.ds(tok, pr), :]
            words = [xstage[s, pl.ds(c, bm, stride=pr), :] for c in range(pr)]
            lo = [pltpu.bitcast(w << 16, F32).astype(BF16) for w in words]
            hi = [pltpu.bitcast(w & jnp.uint32(0xFFFF0000), F32).astype(BF16) for w in words]
            x = jnp.concatenate(lo + hi, axis=1)
            gu = jnp.dot(x, wgu_ref[...], preferred_element_type=F32) + bgu_ref[...]
            gate = jnp.minimum(gu[:, :D_FF], SWIGLU_LIMIT)
            up = jnp.clip(gu[:, D_FF:], -SWIGLU_LIMIT, SWIGLU_LIMIT)
            hmid = (up + 1.0) * (gate * jax.nn.sigmoid(SWIGLU_ALPHA * gate))
            y = jnp.dot(hmid.astype(BF16), wdn_ref[...], preferred_element_type=F32) + bdn_ref[...]
            for c in range(tt):
                ybuf[s, pl.ds(c, bm, stride=tt), :] = y[:, c * LANES:(c + 1) * LANES]
            for r in range(bm):
                scatter_copy(dst_ref[0, 0, s * bm + r], s, r).start(priority=r % 2)

    @pl.when(i == n_used - 1)
    def _():
        for s in range(MOE_SUB):
            for r in range(bm):
                scatter_copy(0, s, r).wait()


def _moe(blk_e, n_used, src_tok, dst_row, h2, wgu, bgu, wdn, bdn, y_rows):
    n_steps = src_tok.shape[0]
    step_rows = MOE_SUB * MOE_BM
    cur = lambda i, be, nu: (i, 0, 0)
    wsel = lambda i, be, nu: (be[i], 0, 0)
    smem_blk = lambda im: pl.BlockSpec((1, 1, step_rows), im, memory_space=pltpu.SMEM)
    grid_spec = pltpu.PrefetchScalarGridSpec(
        num_scalar_prefetch=2,
        grid=(n_steps,),
        in_specs=[
            smem_blk(cur), smem_blk(cur),
            pl.BlockSpec(h2.shape, lambda i, be, nu: (0, 0), pipeline_mode=pl.Buffered(1)),
            pl.BlockSpec((None, D_MODEL, 2 * D_FF), wsel),
            pl.BlockSpec((None, 1, 2 * D_FF), wsel),
            pl.BlockSpec((None, D_FF, D_MODEL), wsel),
            pl.BlockSpec((None, 1, D_MODEL), wsel),
        ],
        out_specs=pl.BlockSpec(memory_space=pl.ANY),
        scratch_shapes=[pltpu.VMEM((MOE_SUB, MOE_BM * PACK_ROWS, LANES), jnp.uint32),
                        pltpu.VMEM((MOE_SUB, MOE_BM * TOK_TILE, LANES), F32),
                        pltpu.SemaphoreType.DMA((MOE_SUB,))],
    )
    return pl.pallas_call(
        _moe_kernel,
        out_shape=jax.ShapeDtypeStruct((y_rows * TOK_TILE, LANES), F32),
        grid_spec=grid_spec,
        compiler_params=pltpu.CompilerParams(
            dimension_semantics=("arbitrary",), vmem_limit_bytes=MOE_VMEM_LIMIT),
        name="moe_experts",
    )(blk_e, n_used, src_tok, dst_row, h2, wgu, bgu, wdn, bdn)


def _route(top_e, n_tok):
    bm = MOE_SUB * MOE_BM
    n_assign = n_tok * TOP_K
    n_blocks = n_assign // bm + N_EXPERTS
    n_dummy = N_EXPERTS * bm
    flat_e = top_e.reshape(-1)
    counts = jnp.sum((flat_e[:, None] == jnp.arange(N_EXPERTS, dtype=jnp.int32)[None, :]).astype(jnp.int32), axis=0)
    padded = (counts + bm - 1) // bm * bm
    pad_end = jnp.cumsum(padded)
    d_e = jnp.arange(n_dummy, dtype=jnp.int32) // bm
    d_active = (jnp.arange(n_dummy, dtype=jnp.int32) % bm) < (padded - counts)[d_e]
    keys = jnp.concatenate([flat_e * 2, jnp.where(d_active, d_e * 2 + 1, 2 * N_EXPERTS)])
    vals = jnp.concatenate([jnp.arange(n_assign, dtype=jnp.int32), jnp.full((n_dummy,), -1, jnp.int32)])
    _, assign = lax.sort((keys, vals), num_keys=1, is_stable=True)
    assign = assign[:n_blocks * bm]
    rows = jnp.arange(n_blocks * bm, dtype=jnp.int32)
    valid = assign >= 0
    src_tok = jnp.where(valid, assign // TOP_K, 0)
    dst_row = jnp.where(valid, (assign % TOP_K) * n_tok + assign // TOP_K, n_assign + rows % bm)
    blk_start = jnp.arange(n_blocks, dtype=jnp.int32) * bm
    blk_e = jnp.minimum(jnp.sum((pad_end[None, :] <= blk_start[:, None]).astype(jnp.int32), axis=1),
                        N_EXPERTS - 1).astype(jnp.int32)
    n_used = (pad_end[-1] // bm).astype(jnp.int32).reshape(1)
    return (blk_e, n_used, src_tok.reshape(n_blocks, 1, bm).astype(jnp.int32),
            dst_row.reshape(n_blocks, 1, bm).astype(jnp.int32))


def _combine_kernel(x1_ref, y0_ref, y1_ref, y2_ref, y3_ref, tg_ref, nf_ref, op_ref, os_ref, *, n_first):
    tg = tg_ref[...]
    tm = x1_ref.shape[0]
    rows = lambda y_ref: jnp.concatenate(
        [y_ref[pl.ds(c, tm, stride=TOK_TILE), :] for c in range(TOK_TILE)], axis=1)
    acc = rows(y0_ref) * tg[:, 0:1]
    for k, y_ref in enumerate((y1_ref, y2_ref, y3_ref), start=1):
        acc = acc + rows(y_ref) * tg[:, k:k + 1]
    x = x1_ref[...] + acc
    out = x * lax.rsqrt(jnp.mean(x * x, axis=-1, keepdims=True) + RMS_EPS) * nf_ref[...]
    i = pl.program_id(0)

    @pl.when(i < n_first)
    def _():
        op_ref[...] = out

    @pl.when(i >= n_first)
    def _():
        os_ref[...] = out


def _combine(x1, y, tg, norm_f_w, n_p):
    n = x1.shape[0]
    tm = COMBINE_TM
    n_first = n_p // tm
    row = lambda i: (i, 0)
    y_spec = lambda k: pl.BlockSpec((tm * TOK_TILE, LANES), lambda i: (k * (n // tm) + i, 0))
    return pl.pallas_call(
        functools.partial(_combine_kernel, n_first=n_first),
        out_shape=(jax.ShapeDtypeStruct((n_p, D_MODEL), F32), jax.ShapeDtypeStruct((n - n_p, D_MODEL), F32)),
        grid=(n // tm,),
        in_specs=[pl.BlockSpec((tm, D_MODEL), row), y_spec(0), y_spec(1), y_spec(2), y_spec(3),
                  pl.BlockSpec((tm, LANES), row), pl.BlockSpec((1, D_MODEL), lambda i: (0, 0))],
        out_specs=_split_specs((tm, D_MODEL), n_first),
        compiler_params=pltpu.CompilerParams(
            dimension_semantics=("arbitrary",), vmem_limit_bytes=VMEM_LIMIT),
        name="combine_norm",
    )(x1, y, y, y, y, tg, norm_f_w.reshape(1, D_MODEL))


def _pack_w_in(w_in):
    s = np.cumsum((0,) + IN_SIZES)
    qkv, z, al, be, qb, kb, vb, gb, gk, ga, gbt = [w_in[:, s[i]:s[i + 1]] for i in range(len(IN_SIZES))]
    small = jnp.concatenate(
        [al, be, gk, jnp.zeros((D_MODEL, LANES - 2 * N_HEADS - GK_RANK), w_in.dtype)], axis=1)
    return jnp.concatenate([qkv, z, ga, gbt, qb, kb, vb, gb, small], axis=1).astype(BF16)


def kernel(x_prompt, x_sample, state_gdn, state_gdn_conv, state_gla, norm1_w, w_in, conv_w, a_log,
           dt_bias, gdn_norm_w, gla_gk_up, gla_gk_b, gla_norm_w, w_up_a, w_up_b, w_o, norm2_w,
           w_router, b_router, w_gate_up, b_gate_up, w_down, b_down, norm_f_w):
    bp, tp, _ = x_prompt.shape
    bs, ts, _ = x_sample.shape
    n_p, n_s = bp * tp, bs * ts
    n = n_p + n_s
    l = 0

    x_p = x_prompt.reshape(n_p, D_MODEL)
    x_s = x_sample.reshape(n_s, D_MODEL)
    p = _inproj(x_p, x_s, norm1_w[l], _pack_w_in(w_in[l]))

    hp = jnp.zeros((8, LANES), F32).at[0, :N_HEADS].set(a_log[l]).at[1, :N_HEADS].set(dt_bias[l])
    wgk = jnp.zeros((LANES, N_HEADS * DK_B), F32).at[2 * N_HEADS:2 * N_HEADS + GK_RANK].set(gla_gk_up[l]).astype(BF16)
    zeros = lambda shape: jnp.zeros(shape, F32)

    oa_p, gdn_p, conv_p = _gdn(p, 0, bp, tp, conv_w[l], hp, gdn_norm_w[l],
                               zeros((bp, CONV_W - 1, CONV_DIM)), zeros((bp, N_HEADS, DK_A, DV_A)))
    oa_s, gdn_s, conv_s = _gdn(p, n_p, bs, ts, conv_w[l], hp, gdn_norm_w[l], state_gdn_conv[l], state_gdn[l])
    ob_p, gla_p = _gla(p, 0, bp, tp, wgk, gla_gk_b[l], gla_norm_w[l], zeros((bp, N_HEADS, DK_B, DV_B)))
    ob_s, gla_s = _gla(p, n_p, bs, ts, wgk, gla_gk_b[l], gla_norm_w[l], state_gla[l])

    wr = jnp.zeros((D_MODEL, LANES), F32).at[:, :N_EXPERTS].set(w_router[l])
    br = jnp.zeros((1, LANES), F32).at[0, :N_EXPERTS].set(b_router[l])
    x1, h2, te, tg = _merge(x_p, x_s, oa_p, oa_s, ob_p, ob_s, p, w_up_a[l].astype(BF16),
                            w_up_b[l].astype(BF16), w_o[l].astype(BF16), norm2_w[l], wr, br)

    blk_e, n_used, src_tok, dst_row = _route(te[:, :TOP_K], n)
    y = _moe(blk_e, n_used, src_tok, dst_row, h2, w_gate_up[l].astype(BF16),
             b_gate_up[l].reshape(N_EXPERTS, 1, 2 * D_FF), w_down[l].astype(BF16),
             b_down[l].reshape(N_EXPERTS, 1, D_MODEL), n * TOP_K + MOE_SUB * MOE_BM)
    out_p, out_s = _combine(x1, y, tg, norm_f_w, n_p)
    y_prompt = out_p.reshape(bp, tp, D_MODEL)
    y_sample = out_s.reshape(bs, ts, D_MODEL)
    return (y_prompt, y_sample, gdn_p[None], conv_p[None], gla_p[None], gdn_s[None], conv_s[None],
            gla_s[None])
```

```python
import functools
import math

import jax
import jax.numpy as jnp
import numpy as np
from jax import lax
from jax.experimental import pallas as pl
from jax.experimental.pallas import tpu as pltpu

F32 = jnp.float32
BF16 = jnp.bfloat16

D_MODEL = 1024
N_HEADS = 4
DK_A = 128
DV_A = 128
W_A = N_HEADS * DV_A
CONV_W = 4
CONV_DIM = 2 * N_HEADS * DK_A + W_A
GDN_CHUNK = 64
DK_B = 64
DV_B = 128
W_B = N_HEADS * DV_B
GK_RANK = 16
GLA_GATE_NORM = 16.0
GLA_CHUNK = 16
N_EXPERTS = 32
TOP_K = 4
D_FF = 1024
SWIGLU_LIMIT = 7.0
SWIGLU_ALPHA = 1.702
RMS_EPS = 1e-6
IN_SIZES = (CONV_DIM, W_A, N_HEADS, N_HEADS, N_HEADS * DK_B, N_HEADS * DK_B, W_B, W_B, GK_RANK,
            D_MODEL, D_MODEL)

LANES = 128

OFF_QKV = 0
OFF_Z = 1536
OFF_GATE_A = 2048
OFF_GATE_B = 3072
OFF_QK_B = 4096
OFF_V_B = 4608
OFF_G_B = 5120
OFF_SMALL = 5632
P_COLS = 5760

INPROJ_TM = 512
GLA_TB = 256
GDN_SEGS_CARRY = 4
GDN_SEGS_INDEP = 8
MERGE_TM = 512
MERGE_SPLIT = 2
MOE_BM = 256
MOE_SUB = 2
TOK_TILE = D_MODEL // LANES
PACK_ROWS = D_MODEL // 2 // LANES
MOE_VMEM_LIMIT = 60 * 1024 * 1024
COMBINE_TM = 256
VMEM_LIMIT = 56 * 1024 * 1024


def _dot(a, b):
    return jnp.dot(a.astype(BF16), b.astype(BF16), preferred_element_type=F32)


def _dot_nt(a, b):
    return lax.dot_general(a.astype(BF16), b.astype(BF16), (((1,), (1,)), ((), ())),
                           preferred_element_type=F32)


def _dot_tn(a, b):
    return lax.dot_general(a, b, (((0,), (0,)), ((), ())), preferred_element_type=F32)


def _split3(x):
    hi = x.astype(BF16)
    r1 = x - hi.astype(F32)
    mid = r1.astype(BF16)
    lo = (r1 - mid.astype(F32)).astype(BF16)
    return hi, mid, lo


def _dot01(m01, x):
    hi, mid, lo = _split3(x)
    d = lambda p: jnp.dot(m01, p, preferred_element_type=F32)
    return d(hi) + d(mid) + d(lo)


def _silu(x):
    return x * jax.nn.sigmoid(x)


def _pick(i, n_first, first_ref, second_ref):
    return jnp.where(i < n_first, first_ref[...], second_ref[...])


def _split_specs(block, n_first):
    return (pl.BlockSpec(block, lambda i: (jnp.minimum(i, n_first - 1), 0)),
            pl.BlockSpec(block, lambda i: (jnp.maximum(i - n_first, 0), 0)))


def _inproj_kernel(xp_ref, xs_ref, nw_ref, w_ref, o_ref, *, n_first):
    x = _pick(pl.program_id(0), n_first, xp_ref, xs_ref)
    ms = jnp.mean(x * x, axis=-1, keepdims=True)
    h = (x * lax.rsqrt(ms + RMS_EPS) * nw_ref[...]).astype(BF16)
    o_ref[...] = jnp.dot(h, w_ref[...], preferred_element_type=F32)


def _inproj(x_p, x_s, norm_w, w_packed):
    tm = INPROJ_TM
    n_first = x_p.shape[0] // tm
    n = x_p.shape[0] + x_s.shape[0]
    return pl.pallas_call(
        functools.partial(_inproj_kernel, n_first=n_first),
        out_shape=jax.ShapeDtypeStruct((n, P_COLS), F32),
        grid=(n // tm,),
        in_specs=[
            *_split_specs((tm, D_MODEL), n_first),
            pl.BlockSpec((1, D_MODEL), lambda i: (0, 0)),
            pl.BlockSpec((D_MODEL, P_COLS), lambda i: (0, 0), pipeline_mode=pl.Buffered(1)),
        ],
        out_specs=pl.BlockSpec((tm, P_COLS), lambda i: (i, 0)),
        compiler_params=pltpu.CompilerParams(
            dimension_semantics=("arbitrary",), vmem_limit_bytes=VMEM_LIMIT),
        name="inproj",
    )(x_p, x_s, norm_w.reshape(1, D_MODEL), w_packed)


def _gdn_kernel(qkv_ref, z_ref, sm_ref, cw_ref, hp_ref, nw_ref, sconv_ref, s0_ref,
                o_ref, sout_ref, cout_ref, s_ref, ubuf_ref, *, chunk, n_seg, carry, n_steps):
    c_len = chunk
    rows = n_seg * c_len
    t = pl.program_id(1)
    hist = CONV_W - 1
    base = 8
    conv_len = rows if carry else c_len
    n_conv = 1 if carry else n_seg

    def init_history():
        ubuf_ref[:, 0:base, :] = jnp.zeros((n_conv, base, CONV_DIM), F32)
        ubuf_ref[:, base - hist:base, :] = sconv_ref[...]

    if carry:
        @pl.when(t == 0)
        def _():
            s_ref[...] = s0_ref[...]
            init_history()

        @pl.when(t > 0)
        def _():
            ubuf_ref[:, base - hist:base, :] = ubuf_ref[:, base + conv_len - hist:base + conv_len, :]
    else:
        init_history()
    ubuf_ref[:, base:base + conv_len, :] = qkv_ref[...].reshape(n_conv, conv_len, CONV_DIM)

    cw = cw_ref[...]
    u = ubuf_ref[...]
    u1 = pltpu.roll(u, 1, axis=1)
    v = u1 * cw[0:1, :] + u * cw[1:2, :]
    y = (pltpu.roll(v, 2, axis=1) + u1 * cw[2:3, :]) + u * cw[3:4, :]
    y = _silu(y[:, base:, :]).reshape(rows, CONV_DIM)

    sm = sm_ref[...]
    hp = hp_ref[...]
    lane = lax.broadcasted_iota(jnp.int32, (rows, LANES), 1)
    g_all = -jnp.exp(hp[0:1, :]) * jax.nn.softplus(sm + hp[1:2, :])
    g_all = jnp.where(lane < N_HEADS, g_all, 0.0)
    beta_all = jax.nn.sigmoid(sm)

    sh = int(round(math.log2(c_len)))
    r_b = lax.broadcasted_iota(jnp.int32, (rows, rows), 0)
    c_b = lax.broadcasted_iota(jnp.int32, (rows, rows), 1)
    seg_tril = ((r_b >> sh) == (c_b >> sh)) & (c_b <= r_b)
    gc = _dot01(jnp.where(seg_tril, 1.0, 0.0).astype(BF16), g_all)

    r_i = lax.broadcasted_iota(jnp.int32, (c_len, c_len), 0)
    c_i = lax.broadcasted_iota(jnp.int32, (c_len, c_len), 1)
    tri = c_i <= r_i
    strict = c_i < r_i
    eye = c_i == r_i
    nw = nw_ref[...]
    levels = sh

    qn, kn, kb, rhs, qg, gcol = [], [], [], [], [], []
    for h in range(N_HEADS):
        qh = y[:, h * DK_A:(h + 1) * DK_A]
        kh = y[:, W_A + h * DK_A:W_A + (h + 1) * DK_A]
        vh = y[:, 2 * W_A + h * DV_A:2 * W_A + (h + 1) * DV_A]
        qn_h = qh * lax.rsqrt(jnp.sum(qh * qh, axis=-1, keepdims=True) + 1e-6) * (DK_A ** -0.5)
        kn_h = kh * lax.rsqrt(jnp.sum(kh * kh, axis=-1, keepdims=True) + 1e-6)
        gcol_h = gc[:, h:h + 1]
        eg = jnp.exp(gcol_h)
        bcol = beta_all[:, N_HEADS + h:N_HEADS + h + 1]
        kb_h = kn_h * bcol
        qn.append(qn_h)
        kn.append(kn_h)
        kb.append(kb_h)
        rhs.append(jnp.concatenate([vh * bcol, kb_h * eg], axis=1))
        qg.append(qn_h * eg)
        gcol.append(gcol_h)

    pairs = [(g, h) for g in range(n_seg) for h in range(N_HEADS)]
    sl = lambda g: slice(g * c_len, (g + 1) * c_len)

    decay, kd, gl = {}, {}, {}
    for (g, h) in pairs:
        gcol_p = gcol[h][sl(g)]
        gcb = jnp.broadcast_to(gcol_p, (c_len, c_len))
        grow = jnp.sum(jnp.where(eye, gcb, 0.0), axis=0, keepdims=True)
        decay[g, h] = jnp.where(tri, jnp.exp(jnp.where(tri, gcb - grow, 0.0)), 0.0)
        glast = gcol[h][(g + 1) * c_len - 1:(g + 1) * c_len]
        kd[g, h] = kn[h][sl(g)] * jnp.exp(glast - gcol_p)
        gl[g, h] = jnp.exp(glast)

    kq = {p: _dot_nt(jnp.concatenate([kb[p[1]][sl(p[0])], qn[p[1]][sl(p[0])]], axis=0), kn[p[1]][sl(p[0])])
          for p in pairs}
    low = {p: jnp.where(strict, kq[p][:c_len] * decay[p], 0.0) for p in pairs}
    attn = {p: jnp.where(tri, kq[p][c_len:] * decay[p], 0.0) for p in pairs}
    ymat = {p: -low[p] for p in pairs}
    lp = low
    for _ in range(levels - 1):
        lp = {p: _dot(lp[p], lp[p]) for p in pairs}
        prod = {p: _dot(ymat[p], lp[p]) for p in pairs}
        ymat = {p: ymat[p] + lp[p] + prod[p] for p in pairs}
    sol = {}
    for (g, h) in pairs:
        rhs_p = rhs[h][sl(g)]
        sol[g, h] = rhs_p + _dot(ymat[g, h], rhs_p)

    def finish(g, h, o):
        r = o * lax.rsqrt(jnp.mean(o * o, axis=-1, keepdims=True) + RMS_EPS) * nw
        o_ref[sl(g), h * DV_A:(h + 1) * DV_A] = r * _silu(z_ref[sl(g), h * DV_A:(h + 1) * DV_A])

    def advance(group, state):
        ws = {p: _dot(jnp.concatenate([sol[p][:, DV_A:], qg[p[1]][sl(p[0])]], axis=0), state[p]) for p in group}
        v_new = {p: sol[p][:, :DV_A] - ws[p][:c_len] for p in group}
        o = {p: ws[p][c_len:] + _dot(attn[p], v_new[p]) for p in group}
        new = {p: state[p] * gl[p] + _dot_tn(kd[p], v_new[p]) for p in group}
        for p in group:
            finish(p[0], p[1], o[p])
        return new

    if carry:
        cur = {h: s_ref[0, h] for h in range(N_HEADS)}
        for g in range(n_seg):
            group = [(g, h) for h in range(N_HEADS)]
            new = advance(group, {(g, h): cur[h] for h in range(N_HEADS)})
            cur = {h: new[g, h] for h in range(N_HEADS)}
        for h in range(N_HEADS):
            s_ref[0, h] = cur[h]

        @pl.when(t == n_steps - 1)
        def _():
            for h in range(N_HEADS):
                sout_ref[0, h] = cur[h]
            cout_ref[...] = ubuf_ref[:, base + conv_len - hist:base + conv_len, :]
    else:
        new = advance(pairs, {p: s0_ref[p[0], p[1]] for p in pairs})
        for p in pairs:
            sout_ref[p[0], p[1]] = new[p]
        cout_ref[...] = ubuf_ref[:, base + conv_len - hist:base + conv_len, :]


def _gdn(p, row0, batch, seq, conv_w, hp, norm_w, s_conv, s0):
    chunk = math.gcd(seq, GDN_CHUNK)
    n_chunks = seq // chunk
    carry = n_chunks > 1
    if carry:
        n_seg = math.gcd(n_chunks, GDN_SEGS_CARRY)
        grid = (batch, n_chunks // n_seg)
        n_conv, conv_len = 1, n_seg * chunk
    else:
        n_seg = math.gcd(batch, GDN_SEGS_INDEP)
        grid = (batch // n_seg, 1)
        n_conv, conv_len = n_seg, chunk
    rows = n_seg * chunk
    n_steps = grid[1]
    blk0 = row0 // rows
    rowblk = lambda b, t: blk0 + b * n_steps + t
    kern = functools.partial(_gdn_kernel, chunk=chunk, n_seg=n_seg, carry=carry, n_steps=n_steps)
    return pl.pallas_call(
        kern,
        out_shape=(jax.ShapeDtypeStruct((batch * seq, W_A), F32),
                   jax.ShapeDtypeStruct((batch, N_HEADS, DK_A, DV_A), F32),
                   jax.ShapeDtypeStruct((batch, CONV_W - 1, CONV_DIM), F32)),
        grid=grid,
        in_specs=[
            pl.BlockSpec((rows, CONV_DIM), lambda b, t: (rowblk(b, t), OFF_QKV // CONV_DIM)),
            pl.BlockSpec((rows, W_A), lambda b, t: (rowblk(b, t), OFF_Z // W_A)),
            pl.BlockSpec((rows, LANES), lambda b, t: (rowblk(b, t), OFF_SMALL // LANES)),
            pl.BlockSpec((CONV_W, CONV_DIM), lambda b, t: (0, 0)),
            pl.BlockSpec((8, LANES), lambda b, t: (0, 0)),
            pl.BlockSpec((1, DV_A), lambda b, t: (0, 0)),
            pl.BlockSpec((n_conv, CONV_W - 1, CONV_DIM), lambda b, t: (b, 0, 0)),
            pl.BlockSpec((n_conv, N_HEADS, DK_A, DV_A), lambda b, t: (b, 0, 0, 0)),
        ],
        out_specs=(
            pl.BlockSpec((rows, W_A), lambda b, t: (b * n_steps + t, 0)),
            pl.BlockSpec((n_conv, N_HEADS, DK_A, DV_A), lambda b, t: (b, 0, 0, 0)),
            pl.BlockSpec((n_conv, CONV_W - 1, CONV_DIM), lambda b, t: (b, 0, 0)),
        ),
        scratch_shapes=[pltpu.VMEM((1, N_HEADS, DK_A, DV_A), F32),
                        pltpu.VMEM((n_conv, conv_len + 8, CONV_DIM), F32)],
        compiler_params=pltpu.CompilerParams(
            dimension_semantics=("arbitrary", "arbitrary"), vmem_limit_bytes=VMEM_LIMIT),
        name=f"gdn_c{chunk}",
    )(p, p, p, conv_w, hp, norm_w.reshape(1, DV_A), s_conv, s0)


def _gla_kernel(qk_ref, v_ref, g_ref, sm_ref, wgk_ref, bgk_ref, nw_ref, s0_ref,
                o_ref, sout_ref, s_ref, *, tb, cs, carry, n_steps):
    t = pl.program_id(1)
    n_sub = tb // cs
    w_kb = N_HEADS * DK_B

    if carry:
        @pl.when(t == 0)
        def _():
            s_ref[...] = s0_ref[...]

    gk = _dot(sm_ref[...], wgk_ref[...]) + bgk_ref[...]
    log_a = jax.nn.log_sigmoid(gk) / GLA_GATE_NORM
    sh = int(round(math.log2(cs)))
    r_i = lax.broadcasted_iota(jnp.int32, (tb, tb), 0)
    c_i = lax.broadcasted_iota(jnp.int32, (tb, tb), 1)
    same = (r_i >> sh) == (c_i >> sh)
    mask = same & (c_i <= r_i)
    gcum = _dot01(jnp.where(mask, 1.0, 0.0).astype(BF16), log_a)
    gtot = _dot01(jnp.where(same, 1.0, 0.0).astype(BF16), log_a)
    onehot = jnp.where((lax.broadcasted_iota(jnp.int32, (tb, n_sub), 0) >> sh)
                       == lax.broadcasted_iota(jnp.int32, (tb, n_sub), 1), 1.0, 0.0).astype(BF16)
    tn = lambda part: lax.dot_general(part, onehot, (((0,), (0,)), ((), ())), preferred_element_type=F32)
    hi, mid, lo = _split3(log_a)
    glcol = jnp.exp(tn(hi) + tn(mid) + tn(lo))

    qk = qk_ref[...]
    nw = nw_ref[...]
    heads = range(N_HEADS)
    sl = lambda j: slice(j * cs, (j + 1) * cs)
    qg, kd, vh, o_intra = [], [], [], []
    for h in heads:
        qh = qk[:, h * DK_B:(h + 1) * DK_B] * (DK_B ** -0.5)
        kh = qk[:, w_kb + h * DK_B:w_kb + (h + 1) * DK_B]
        gh = gcum[:, h * DK_B:(h + 1) * DK_B]
        gt = gtot[:, h * DK_B:(h + 1) * DK_B]
        qg.append(qh * jnp.exp(gh))
        kd.append(kh * jnp.exp(gt - gh))
        vh.append(v_ref[:, h * DV_B:(h + 1) * DV_B])
        attn = jnp.where(mask, _dot_nt(qg[h], kh * jnp.exp(-gh)), 0.0)
        o_intra.append(_dot(attn, vh[h]))
    upd = {(j, h): _dot_tn(kd[h][sl(j)], vh[h][sl(j)]) for j in range(n_sub) for h in heads}
    gl = lambda j, h: glcol[h * DK_B:(h + 1) * DK_B, j:j + 1]

    o_inter = [[] for _ in heads]
    if carry:
        cur = [s_ref[0, h] for h in heads]
        for j in range(n_sub):
            for h in heads:
                o_inter[h].append(_dot(qg[h][sl(j)], cur[h]))
                cur[h] = cur[h] * gl(j, h) + upd[j, h]
        for h in heads:
            s_ref[0, h] = cur[h]

        @pl.when(t == n_steps - 1)
        def _():
            for h in heads:
                sout_ref[0, h] = cur[h]
    else:
        for j in range(n_sub):
            for h in heads:
                s = s0_ref[j, h]
                o_inter[h].append(_dot(qg[h][sl(j)], s))
                sout_ref[j, h] = s * gl(j, h) + upd[j, h]

    for h in heads:
        o = o_intra[h] + jnp.concatenate(o_inter[h], axis=0)
        r = o * lax.rsqrt(jnp.mean(o * o, axis=-1, keepdims=True) + RMS_EPS) * nw
        o_ref[:, h * DV_B:(h + 1) * DV_B] = r * _silu(g_ref[:, h * DV_B:(h + 1) * DV_B])


def _gla(p, row0, batch, seq, wgk, bgk, norm_w, s0):
    cs = math.gcd(seq, GLA_CHUNK)
    carry = seq > cs
    if carry:
        tb = math.gcd(seq, GLA_TB)
        grid = (batch, seq // tb)
        n_state = 1
    else:
        tb = math.gcd(batch * seq, GLA_TB)
        n_state = tb // cs
        grid = (batch // n_state, 1)
    n_steps = grid[1]
    blk0 = row0 // tb
    rowblk = lambda b, t: blk0 + b * n_steps + t
    kern = functools.partial(_gla_kernel, tb=tb, cs=cs, carry=carry, n_steps=n_steps)
    return pl.pallas_call(
        kern,
        out_shape=(jax.ShapeDtypeStruct((batch * seq, W_B), F32),
                   jax.ShapeDtypeStruct((batch, N_HEADS, DK_B, DV_B), F32)),
        grid=grid,
        in_specs=[
            pl.BlockSpec((tb, 2 * N_HEADS * DK_B), lambda b, t: (rowblk(b, t), OFF_QK_B // (2 * N_HEADS * DK_B))),
            pl.BlockSpec((tb, W_B), lambda b, t: (rowblk(b, t), OFF_V_B // W_B)),
            pl.BlockSpec((tb, W_B), lambda b, t: (rowblk(b, t), OFF_G_B // W_B)),
            pl.BlockSpec((tb, LANES), lambda b, t: (rowblk(b, t), OFF_SMALL // LANES)),
            pl.BlockSpec((LANES, N_HEADS * DK_B), lambda b, t: (0, 0)),
            pl.BlockSpec((1, N_HEADS * DK_B), lambda b, t: (0, 0)),
            pl.BlockSpec((1, DV_B), lambda b, t: (0, 0)),
            pl.BlockSpec((n_state, N_HEADS, DK_B, DV_B), lambda b, t: (b, 0, 0, 0)),
        ],
        out_specs=(
            pl.BlockSpec((tb, W_B), lambda b, t: (b * n_steps + t, 0)),
            pl.BlockSpec((n_state, N_HEADS, DK_B, DV_B), lambda b, t: (b, 0, 0, 0)),
        ),
        scratch_shapes=[pltpu.VMEM((1, N_HEADS, DK_B, DV_B), F32)],
        compiler_params=pltpu.CompilerParams(
            dimension_semantics=("arbitrary", "arbitrary"), vmem_limit_bytes=VMEM_LIMIT),
        name=f"gla_c{cs}",
    )(p, p, p, p, wgk, bgk.reshape(1, N_HEADS * DK_B), norm_w.reshape(1, DV_B), s0)


def _topk_softmax(logits):
    tm = logits.shape[0]
    lane = lax.broadcasted_iota(jnp.int32, (tm, LANES), 1)
    lane_f = lane.astype(F32)
    neg = jnp.float32(-jnp.inf)
    cur = jnp.where(lane < N_EXPERTS, logits, neg)
    vals, idxs = [], []
    for _ in range(TOP_K):
        m = jnp.max(cur, axis=-1, keepdims=True)
        idx = jnp.min(jnp.where(cur == m, lane_f, float(LANES)), axis=-1, keepdims=True)
        vals.append(m)
        idxs.append(idx)
        cur = jnp.where(lane_f == idx, neg, cur)
    exps = [jnp.exp(v - vals[0]) for v in vals]
    den = exps[0] + exps[1] + exps[2] + exps[3]
    te = jnp.zeros((tm, LANES), F32)
    tg = jnp.zeros((tm, LANES), F32)
    for k in range(TOP_K):
        te = jnp.where(lane == k, idxs[k], te)
        tg = jnp.where(lane == k, exps[k] / den, tg)
    return te.astype(jnp.int32), tg


def _merge_kernel(xp_ref, xs_ref, oap_ref, oas_ref, obp_ref, obs_ref, ga_ref, gb_ref, wua_ref, wub_ref, wo_ref,
                  n2_ref, wr_ref, br_ref, x1_ref, h2_ref, te_ref, tg_ref, *, n_first):
    i = pl.program_id(0)
    tm = x1_ref.shape[0]
    hm = tm // MERGE_SPLIT
    groups = [slice(a * hm, (a + 1) * hm) for a in range(MERGE_SPLIT)]
    oa = _pick(i, n_first, oap_ref, oas_ref).astype(BF16)
    ob = _pick(i, n_first, obp_ref, obs_ref).astype(BF16)
    x = _pick(i, n_first, xp_ref, xs_ref)
    ua = [jnp.dot(oa[g], wua_ref[...], preferred_element_type=F32) for g in groups]
    ub = [jnp.dot(ob[g], wub_ref[...], preferred_element_type=F32) for g in groups]
    merged = [jax.nn.sigmoid(ga_ref[g, :]) * ua[a] + jax.nn.sigmoid(gb_ref[g, :]) * ub[a]
              for a, g in enumerate(groups)]
    x1 = [x[g] + jnp.dot(merged[a].astype(BF16), wo_ref[...], preferred_element_type=F32)
          for a, g in enumerate(groups)]
    h2 = [v * lax.rsqrt(jnp.mean(v * v, axis=-1, keepdims=True) + RMS_EPS) * n2_ref[...] for v in x1]
    h_hi = [v.astype(BF16) for v in h2]
    h_lo = [(v - hi.astype(F32)).astype(BF16) for v, hi in zip(h2, h_hi)]
    l_hi = [jnp.dot(hi, wr_ref[...], preferred_element_type=F32) for hi in h_hi]
    l_lo = [jnp.dot(lo, wr_ref[:, :LANES], preferred_element_type=F32) for lo in h_lo]
    half = D_MODEL // 2
    for a, g in enumerate(groups):
        x1_ref[g, :] = x1[a]
        bits = pltpu.bitcast(h_hi[a].astype(F32), jnp.uint32)
        words = (bits[:, :half] >> 16) | (bits[:, half:] & jnp.uint32(0xFFFF0000))
        for s in range(PACK_ROWS):
            h2_ref[pl.ds(a * hm * PACK_ROWS + s, hm, stride=PACK_ROWS), :] = words[:, s * LANES:(s + 1) * LANES]
        logits = l_hi[a][:, :LANES] + l_hi[a][:, LANES:] + l_lo[a] + br_ref[...]
        te, tg = _topk_softmax(logits)
        te_ref[g, :] = te
        tg_ref[g, :] = tg


def _merge(x_p, x_s, oa_p, oa_s, ob_p, ob_s, p, wua, wub, wo, norm2_w, wr, br):
    n = x_p.shape[0] + x_s.shape[0]
    tm = MERGE_TM
    n_first = x_p.shape[0] // tm
    row = lambda i: (i, 0)
    const = lambda i: (0, 0)
    return pl.pallas_call(
        functools.partial(_merge_kernel, n_first=n_first),
        out_shape=(jax.ShapeDtypeStruct((n, D_MODEL), F32), jax.ShapeDtypeStruct((n * PACK_ROWS, LANES), jnp.uint32),
                   jax.ShapeDtypeStruct((n, LANES), jnp.int32), jax.ShapeDtypeStruct((n, LANES), F32)),
        grid=(n // tm,),
        in_specs=[
            *_split_specs((tm, D_MODEL), n_first),
            *_split_specs((tm, W_A), n_first),
            *_split_specs((tm, W_B), n_first),
            pl.BlockSpec((tm, D_MODEL), lambda i: (i, OFF_GATE_A // D_MODEL)),
            pl.BlockSpec((tm, D_MODEL), lambda i: (i, OFF_GATE_B // D_MODEL)),
            pl.BlockSpec((W_A, D_MODEL), const),
            pl.BlockSpec((W_B, D_MODEL), const),
            pl.BlockSpec((D_MODEL, D_MODEL), const),
            pl.BlockSpec((1, D_MODEL), const),
            pl.BlockSpec((D_MODEL, 2 * LANES), const),
            pl.BlockSpec((1, LANES), const),
        ],
        out_specs=(pl.BlockSpec((tm, D_MODEL), row), pl.BlockSpec((tm * PACK_ROWS, LANES), row),
                   pl.BlockSpec((tm, LANES), row), pl.BlockSpec((tm, LANES), row)),
        compiler_params=pltpu.CompilerParams(
            dimension_semantics=("arbitrary",), vmem_limit_bytes=VMEM_LIMIT),
        name="merge_router",
    )(x_p, x_s, oa_p, oa_s, ob_p, ob_s, p, p, wua, wub, wo, norm2_w.reshape(1, D_MODEL), wr, br)


def _moe_kernel(blk_e_ref, n_used_ref, src_ref, dst_ref, h2_ref, wgu_ref, bgu_ref,
                wdn_ref, bdn_ref, y_hbm, xstage, ybuf, ssem):
    del blk_e_ref
    i = pl.program_id(0)
    n_used = n_used_ref[0]
    bm = MOE_BM
    tt = TOK_TILE
    pr = PACK_ROWS

    def scatter_copy(row, s, r):
        return pltpu.make_async_copy(ybuf.at[s, pl.ds(r * tt, tt), :],
                                     y_hbm.at[pl.ds(pl.multiple_of(row * tt, tt), tt), :], ssem.at[s])

    @pl.when(i == 0)
    def _():
        ybuf[...] = jnp.zeros_like(ybuf)
        n_real = y_hbm.shape[0] - MOE_SUB * bm * tt
        for s in range(MOE_SUB):
            sink = pltpu.make_async_copy(ybuf.at[s], y_hbm.at[pl.ds(n_real + s * bm * tt, bm * tt), :], ssem.at[s])
            sink.start()
            sink.wait()

    @pl.when(i < n_used)
    def _():
        for s in range(MOE_SUB):
            @pl.when(i >= 1)
            def _():
                for r in range(bm):
                    scatter_copy(0, s, r).wait()

            for r in range(bm):
                tok = pl.multiple_of(src_ref[0, 0, s * bm + r] * pr, pr)
                xstage[s, pl.ds(r * pr, pr), :] = h2_ref[pl.ds(tok, pr), :]
            words = [xstage[s, pl.ds(c, bm, stride=pr), :] for c in range(pr)]
            lo = [pltpu.bitcast(w << 16, F32).astype(BF16) for w in words]
            hi = [pltpu.bitcast(w & jnp.uint32(0xFFFF0000), F32).astype(BF16) for w in words]
            x = jnp.concatenate(lo + hi, axis=1)
            gu = jnp.dot(x, wgu_ref[...], preferred_element_type=F32) + bgu_ref[...]
            gate = jnp.minimum(gu[:, :D_FF], SWIGLU_LIMIT)
            up = jnp.clip(gu[:, D_FF:], -SWIGLU_LIMIT, SWIGLU_LIMIT)
            hmid = (up + 1.0) * (gate * jax.nn.sigmoid(SWIGLU_ALPHA * gate))
            y = jnp.dot(hmid.astype(BF16), wdn_ref[...], preferred_element_type=F32) + bdn_ref[...]
            for c in range(tt):
                ybuf[s, pl.ds(c, bm, stride=tt), :] = y[:, c * LANES:(c + 1) * LANES]
            for r in range(bm):
                scatter_copy(dst_ref[0, 0, s * bm + r], s, r).start(priority=r % 2)

    @pl.when(i == n_used - 1)
    def _():
        for s in range(MOE_SUB):
            for r in range(bm):
                scatter_copy(0, s, r).wait()


def _moe(blk_e, n_used, src_tok, dst_row, h2, wgu, bgu, wdn, bdn, y_rows):
    n_steps = src_tok.shape[0]
    step_rows = MOE_SUB * MOE_BM
    cur = lambda i, be, nu: (i, 0, 0)
    wsel = lambda i, be, nu: (be[i], 0, 0)
    smem_blk = lambda im: pl.BlockSpec((1, 1, step_rows), im, memory_space=pltpu.SMEM)
    grid_spec = pltpu.PrefetchScalarGridSpec(
        num_scalar_prefetch=2,
        grid=(n_steps,),
        in_specs=[
            smem_blk(cur), smem_blk(cur),
            pl.BlockSpec(h2.shape, lambda i, be, nu: (0, 0), pipeline_mode=pl.Buffered(1)),
            pl.BlockSpec((None, D_MODEL, 2 * D_FF), wsel),
            pl.BlockSpec((None, 1, 2 * D_FF), wsel),
            pl.BlockSpec((None, D_FF, D_MODEL), wsel),
            pl.BlockSpec((None, 1, D_MODEL), wsel),
        ],
        out_specs=pl.BlockSpec(memory_space=pl.ANY),
        scratch_shapes=[pltpu.VMEM((MOE_SUB, MOE_BM * PACK_ROWS, LANES), jnp.uint32),
                        pltpu.VMEM((MOE_SUB, MOE_BM * TOK_TILE, LANES), F32),
                        pltpu.SemaphoreType.DMA((MOE_SUB,))],
    )
    return pl.pallas_call(
        _moe_kernel,
        out_shape=jax.ShapeDtypeStruct((y_rows * TOK_TILE, LANES), F32),
        grid_spec=grid_spec,
        compiler_params=pltpu.CompilerParams(
            dimension_semantics=("arbitrary",), vmem_limit_bytes=MOE_VMEM_LIMIT),
        name="moe_experts",
    )(blk_e, n_used, src_tok, dst_row, h2, wgu, bgu, wdn, bdn)


def _route(top_e, n_tok):
    bm = MOE_SUB * MOE_BM
    n_assign = n_tok * TOP_K
    n_blocks = n_assign // bm + N_EXPERTS
    n_dummy = N_EXPERTS * bm
    flat_e = top_e.reshape(-1)
    counts = jnp.sum((flat_e[:, None] == jnp.arange(N_EXPERTS, dtype=jnp.int32)[None, :]).astype(jnp.int32), axis=0)
    padded = (counts + bm - 1) // bm * bm
    pad_end = jnp.cumsum(padded)
    d_e = jnp.arange(n_dummy, dtype=jnp.int32) // bm
    d_active = (jnp.arange(n_dummy, dtype=jnp.int32) % bm) < (padded - counts)[d_e]
    keys = jnp.concatenate([flat_e * 2, jnp.where(d_active, d_e * 2 + 1, 2 * N_EXPERTS)])
    vals = jnp.concatenate([jnp.arange(n_assign, dtype=jnp.int32), jnp.full((n_dummy,), -1, jnp.int32)])
    _, assign = lax.sort((keys, vals), num_keys=1, is_stable=True)
    assign = assign[:n_blocks * bm]
    rows = jnp.arange(n_blocks * bm, dtype=jnp.int32)
    valid = assign >= 0
    src_tok = jnp.where(valid, assign // TOP_K, 0)
    dst_row = jnp.where(valid, (assign % TOP_K) * n_tok + assign // TOP_K, n_assign + rows % bm)
    blk_start = jnp.arange(n_blocks, dtype=jnp.int32) * bm
    blk_e = jnp.minimum(jnp.sum((pad_end[None, :] <= blk_start[:, None]).astype(jnp.int32), axis=1),
                        N_EXPERTS - 1).astype(jnp.int32)
    n_used = (pad_end[-1] // bm).astype(jnp.int32).reshape(1)
    return (blk_e, n_used, src_tok.reshape(n_blocks, 1, bm).astype(jnp.int32),
            dst_row.reshape(n_blocks, 1, bm).astype(jnp.int32))


def _combine_kernel(x1_ref, y0_ref, y1_ref, y2_ref, y3_ref, tg_ref, nf_ref, op_ref, os_ref, *, n_first):
    tg = tg_ref[...]
    tm = x1_ref.shape[0]
    rows = lambda y_ref: jnp.concatenate(
        [y_ref[pl.ds(c, tm, stride=TOK_TILE), :] for c in range(TOK_TILE)], axis=1)
    acc = rows(y0_ref) * tg[:, 0:1]
    for k, y_ref in enumerate((y1_ref, y2_ref, y3_ref), start=1):
        acc = acc + rows(y_ref) * tg[:, k:k + 1]
    x = x1_ref[...] + acc
    out = x * lax.rsqrt(jnp.mean(x * x, axis=-1, keepdims=True) + RMS_EPS) * nf_ref[...]
    i = pl.program_id(0)

    @pl.when(i < n_first)
    def _():
        op_ref[...] = out

    @pl.when(i >= n_first)
    def _():
        os_ref[...] = out


def _combine(x1, y, tg, norm_f_w, n_p):
    n = x1.shape[0]
    tm = COMBINE_TM
    n_first = n_p // tm
    row = lambda i: (i, 0)
    y_spec = lambda k: pl.BlockSpec((tm * TOK_TILE, LANES), lambda i: (k * (n // tm) + i, 0))
    return pl.pallas_call(
        functools.partial(_combine_kernel, n_first=n_first),
        out_shape=(jax.ShapeDtypeStruct((n_p, D_MODEL), F32), jax.ShapeDtypeStruct((n - n_p, D_MODEL), F32)),
        grid=(n // tm,),
        in_specs=[pl.BlockSpec((tm, D_MODEL), row), y_spec(0), y_spec(1), y_spec(2), y_spec(3),
                  pl.BlockSpec((tm, LANES), row), pl.BlockSpec((1, D_MODEL), lambda i: (0, 0))],
        out_specs=_split_specs((tm, D_MODEL), n_first),
        compiler_params=pltpu.CompilerParams(
            dimension_semantics=("arbitrary",), vmem_limit_bytes=VMEM_LIMIT),
        name="combine_norm",
    )(x1, y, y, y, y, tg, norm_f_w.reshape(1, D_MODEL))


def _pack_w_in(w_in):
    s = np.cumsum((0,) + IN_SIZES)
    qkv, z, al, be, qb, kb, vb, gb, gk, ga, gbt = [w_in[:, s[i]:s[i + 1]] for i in range(len(IN_SIZES))]
    small = jnp.concatenate(
        [al, be, gk, jnp.zeros((D_MODEL, LANES - 2 * N_HEADS - GK_RANK), w_in.dtype)], axis=1)
    return jnp.concatenate([qkv, z, ga, gbt, qb, kb, vb, gb, small], axis=1).astype(BF16)


def kernel(x_prompt, x_sample, state_gdn, state_gdn_conv, state_gla, norm1_w, w_in, conv_w, a_log,
           dt_bias, gdn_norm_w, gla_gk_up, gla_gk_b, gla_norm_w, w_up_a, w_up_b, w_o, norm2_w,
           w_router, b_router, w_gate_up, b_gate_up, w_down, b_down, norm_f_w):
    bp, tp, _ = x_prompt.shape
    bs, ts, _ = x_sample.shape
    n_p, n_s = bp * tp, bs * ts
    n = n_p + n_s
    l = 0

    x_p = x_prompt.reshape(n_p, D_MODEL)
    x_s = x_sample.reshape(n_s, D_MODEL)
    p = _inproj(x_p, x_s, norm1_w[l], _pack_w_in(w_in[l]))

    hp = jnp.zeros((8, LANES), F32).at[0, :N_HEADS].set(a_log[l]).at[1, :N_HEADS].set(dt_bias[l])
    wgk = jnp.zeros((LANES, N_HEADS * DK_B), F32).at[2 * N_HEADS:2 * N_HEADS + GK_RANK].set(gla_gk_up[l]).astype(BF16)
    zeros = lambda shape: jnp.zeros(shape, F32)

    oa_p, gdn_p, conv_p = _gdn(p, 0, bp, tp, conv_w[l], hp, gdn_norm_w[l],
                               zeros((bp, CONV_W - 1, CONV_DIM)), zeros((bp, N_HEADS, DK_A, DV_A)))
    oa_s, gdn_s, conv_s = _gdn(p, n_p, bs, ts, conv_w[l], hp, gdn_norm_w[l], state_gdn_conv[l], state_gdn[l])
    ob_p, gla_p = _gla(p, 0, bp, tp, wgk, gla_gk_b[l], gla_norm_w[l], zeros((bp, N_HEADS, DK_B, DV_B)))
    ob_s, gla_s = _gla(p, n_p, bs, ts, wgk, gla_gk_b[l], gla_norm_w[l], state_gla[l])

    wr = jnp.zeros((D_MODEL, LANES), F32).at[:, :N_EXPERTS].set(w_router[l])
    wr_hi = wr.astype(BF16)
    wr = jnp.concatenate([wr_hi, (wr - wr_hi.astype(F32)).astype(BF16)], axis=1)
    br = jnp.zeros((1, LANES), F32).at[0, :N_EXPERTS].set(b_router[l])
    x1, h2, te, tg = _merge(x_p, x_s, oa_p, oa_s, ob_p, ob_s, p, w_up_a[l].astype(BF16),
                            w_up_b[l].astype(BF16), w_o[l].astype(BF16), norm2_w[l], wr, br)

    blk_e, n_used, src_tok, dst_row = _route(te[:, :TOP_K], n)
    y = _moe(blk_e, n_used, src_tok, dst_row, h2, w_gate_up[l].astype(BF16),
             b_gate_up[l].reshape(N_EXPERTS, 1, 2 * D_FF), w_down[l].astype(BF16),
             b_down[l].reshape(N_EXPERTS, 1, D_MODEL), n * TOP_K + MOE_SUB * MOE_BM)
    out_p, out_s = _combine(x1, y, tg, norm_f_w, n_p)
    y_prompt = out_p.reshape(bp, tp, D_MODEL)
    y_sample = out_s.reshape(bs, ts, D_MODEL)
    return (y_prompt, y_sample, gdn_p[None], conv_p[None], gla_p[None], gdn_s[None], conv_s[None],
            gla_s[None])
```

```python
import functools
import math

import jax
import jax.numpy as jnp
import numpy as np
from jax import lax
from jax.experimental import pallas as pl
from jax.experimental.pallas import tpu as pltpu

F32 = jnp.float32
BF16 = jnp.bfloat16

D_MODEL = 1024
N_HEADS = 4
DK_A = 128
DV_A = 128
W_A = N_HEADS * DV_A
CONV_W = 4
CONV_DIM = 2 * N_HEADS * DK_A + W_A
GDN_CHUNK = 64
DK_B = 64
DV_B = 128
W_B = N_HEADS * DV_B
GK_RANK = 16
GLA_GATE_NORM = 16.0
GLA_CHUNK = 16
N_EXPERTS = 32
TOP_K = 4
D_FF = 1024
SWIGLU_LIMIT = 7.0
SWIGLU_ALPHA = 1.702
RMS_EPS = 1e-6
IN_SIZES = (CONV_DIM, W_A, N_HEADS, N_HEADS, N_HEADS * DK_B, N_HEADS * DK_B, W_B, W_B, GK_RANK,
            D_MODEL, D_MODEL)

LANES = 128

OFF_QKV = 0
OFF_Z = 1536
OFF_GATE_A = 2048
OFF_GATE_B = 3072
OFF_QK_B = 4096
OFF_V_B = 4608
OFF_G_B = 5120
OFF_SMALL = 5632
P_COLS = 5760

INPROJ_TM = 512
GLA_TB = 256
GDN_SEGS_CARRY = 4
GDN_SEGS_INDEP = 8
MERGE_TM = 512
MERGE_SPLIT = 2
MOE_BM = 256
MOE_SUB = 2
PACK_ROWS = D_MODEL // 2 // LANES
MOE_VMEM_LIMIT = 60 * 1024 * 1024
COMBINE_TM = 256
VMEM_LIMIT = 56 * 1024 * 1024


def _dot(a, b):
    return jnp.dot(a.astype(BF16), b.astype(BF16), preferred_element_type=F32)


def _dot_nt(a, b):
    return lax.dot_general(a.astype(BF16), b.astype(BF16), (((1,), (1,)), ((), ())),
                           preferred_element_type=F32)


def _dot_tn(a, b):
    return lax.dot_general(a, b, (((0,), (0,)), ((), ())), preferred_element_type=F32)


def _split3(x):
    hi = x.astype(BF16)
    r1 = x - hi.astype(F32)
    mid = r1.astype(BF16)
    lo = (r1 - mid.astype(F32)).astype(BF16)
    return hi, mid, lo


def _dot01(m01, x):
    hi, mid, lo = _split3(x)
    d = lambda p: jnp.dot(m01, p, preferred_element_type=F32)
    return d(hi) + d(mid) + d(lo)


def _silu(x):
    return x * jax.nn.sigmoid(x)


def _pack_bf16_pairs(x):
    bits = pltpu.bitcast(x.astype(BF16).astype(F32), jnp.uint32)
    half = D_MODEL // 2
    return (bits[:, :half] >> 16) | (bits[:, half:] & jnp.uint32(0xFFFF0000))


def _store_packed_rows(ref, lead, row0, words):
    n = words.shape[0]
    for c in range(PACK_ROWS):
        ref[(*lead, pl.ds(row0 * PACK_ROWS + c, n, stride=PACK_ROWS), slice(None))] = words[:, c * LANES:(c + 1) * LANES]


def _load_packed_rows(ref, lead, n, dtype):
    words = [ref[(*lead, pl.ds(c, n, stride=PACK_ROWS), slice(None))] for c in range(PACK_ROWS)]
    lo = [pltpu.bitcast(w << 16, F32).astype(dtype) for w in words]
    hi = [pltpu.bitcast(w & jnp.uint32(0xFFFF0000), F32).astype(dtype) for w in words]
    return jnp.concatenate(lo + hi, axis=1)


def _pick(i, n_first, first_ref, second_ref):
    return jnp.where(i < n_first, first_ref[...], second_ref[...])


def _split_specs(block, n_first):
    return (pl.BlockSpec(block, lambda i: (jnp.minimum(i, n_first - 1), 0)),
            pl.BlockSpec(block, lambda i: (jnp.maximum(i - n_first, 0), 0)))


def _inproj_kernel(xp_ref, xs_ref, nw_ref, w_ref, o_ref, *, n_first):
    x = _pick(pl.program_id(0), n_first, xp_ref, xs_ref)
    ms = jnp.mean(x * x, axis=-1, keepdims=True)
    h = (x * lax.rsqrt(ms + RMS_EPS) * nw_ref[...]).astype(BF16)
    o_ref[...] = jnp.dot(h, w_ref[...], preferred_element_type=F32)


def _inproj(x_p, x_s, norm_w, w_packed):
    tm = INPROJ_TM
    n_first = x_p.shape[0] // tm
    n = x_p.shape[0] + x_s.shape[0]
    return pl.pallas_call(
        functools.partial(_inproj_kernel, n_first=n_first),
        out_shape=jax.ShapeDtypeStruct((n, P_COLS), F32),
        grid=(n // tm,),
        in_specs=[
            *_split_specs((tm, D_MODEL), n_first),
            pl.BlockSpec((1, D_MODEL), lambda i: (0, 0)),
            pl.BlockSpec((D_MODEL, P_COLS), lambda i: (0, 0), pipeline_mode=pl.Buffered(1)),
        ],
        out_specs=pl.BlockSpec((tm, P_COLS), lambda i: (i, 0)),
        compiler_params=pltpu.CompilerParams(
            dimension_semantics=("arbitrary",), vmem_limit_bytes=VMEM_LIMIT),
        name="inproj",
    )(x_p, x_s, norm_w.reshape(1, D_MODEL), w_packed)


def _gdn_kernel(qkv_ref, z_ref, sm_ref, cw_ref, hp_ref, nw_ref, sconv_ref, s0_ref,
                o_ref, sout_ref, cout_ref, s_ref, ubuf_ref, *, chunk, n_seg, carry, n_steps):
    c_len = chunk
    rows = n_seg * c_len
    t = pl.program_id(1)
    hist = CONV_W - 1
    base = 8
    conv_len = rows if carry else c_len
    n_conv = 1 if carry else n_seg

    def init_history():
        ubuf_ref[:, 0:base, :] = jnp.zeros((n_conv, base, CONV_DIM), F32)
        ubuf_ref[:, base - hist:base, :] = sconv_ref[...]

    if carry:
        @pl.when(t == 0)
        def _():
            s_ref[...] = s0_ref[...]
            init_history()

        @pl.when(t > 0)
        def _():
            ubuf_ref[:, base - hist:base, :] = ubuf_ref[:, base + conv_len - hist:base + conv_len, :]
    else:
        init_history()
    ubuf_ref[:, base:base + conv_len, :] = qkv_ref[...].reshape(n_conv, conv_len, CONV_DIM)

    cw = cw_ref[...]
    u = ubuf_ref[...]
    u1 = pltpu.roll(u, 1, axis=1)
    v = u1 * cw[0:1, :] + u * cw[1:2, :]
    y = (pltpu.roll(v, 2, axis=1) + u1 * cw[2:3, :]) + u * cw[3:4, :]
    y = _silu(y[:, base:, :]).reshape(rows, CONV_DIM)

    sm = sm_ref[...]
    hp = hp_ref[...]
    lane = lax.broadcasted_iota(jnp.int32, (rows, LANES), 1)
    g_all = -jnp.exp(hp[0:1, :]) * jax.nn.softplus(sm + hp[1:2, :])
    g_all = jnp.where(lane < N_HEADS, g_all, 0.0)
    beta_all = jax.nn.sigmoid(sm)

    sh = int(round(math.log2(c_len)))
    r_b = lax.broadcasted_iota(jnp.int32, (rows, rows), 0)
    c_b = lax.broadcasted_iota(jnp.int32, (rows, rows), 1)
    seg_tril = ((r_b >> sh) == (c_b >> sh)) & (c_b <= r_b)
    gc = _dot01(jnp.where(seg_tril, 1.0, 0.0).astype(BF16), g_all)

    r_i = lax.broadcasted_iota(jnp.int32, (c_len, c_len), 0)
    c_i = lax.broadcasted_iota(jnp.int32, (c_len, c_len), 1)
    tri = c_i <= r_i
    strict = c_i < r_i
    eye = c_i == r_i
    nw = nw_ref[...]
    levels = sh

    qn, kn, kb, rhs, qg, gcol = [], [], [], [], [], []
    for h in range(N_HEADS):
        qh = y[:, h * DK_A:(h + 1) * DK_A]
        kh = y[:, W_A + h * DK_A:W_A + (h + 1) * DK_A]
        vh = y[:, 2 * W_A + h * DV_A:2 * W_A + (h + 1) * DV_A]
        qn_h = qh * lax.rsqrt(jnp.sum(qh * qh, axis=-1, keepdims=True) + 1e-6) * (DK_A ** -0.5)
        kn_h = kh * lax.rsqrt(jnp.sum(kh * kh, axis=-1, keepdims=True) + 1e-6)
        gcol_h = gc[:, h:h + 1]
        eg = jnp.exp(gcol_h)
        bcol = beta_all[:, N_HEADS + h:N_HEADS + h + 1]
        kb_h = kn_h * bcol
        qn.append(qn_h)
        kn.append(kn_h)
        kb.append(kb_h)
        rhs.append(jnp.concatenate([vh * bcol, kb_h * eg], axis=1))
        qg.append(qn_h * eg)
        gcol.append(gcol_h)

    pairs = [(g, h) for g in range(n_seg) for h in range(N_HEADS)]
    sl = lambda g: slice(g * c_len, (g + 1) * c_len)

    decay, kd, gl = {}, {}, {}
    for (g, h) in pairs:
        gcol_p = gcol[h][sl(g)]
        gcb = jnp.broadcast_to(gcol_p, (c_len, c_len))
        grow = jnp.sum(jnp.where(eye, gcb, 0.0), axis=0, keepdims=True)
        decay[g, h] = jnp.where(tri, jnp.exp(jnp.where(tri, gcb - grow, 0.0)), 0.0)
        glast = gcol[h][(g + 1) * c_len - 1:(g + 1) * c_len]
        kd[g, h] = kn[h][sl(g)] * jnp.exp(glast - gcol_p)
        gl[g, h] = jnp.exp(glast)

    kq = {p: _dot_nt(jnp.concatenate([kb[p[1]][sl(p[0])], qn[p[1]][sl(p[0])]], axis=0), kn[p[1]][sl(p[0])])
          for p in pairs}
    low = {p: jnp.where(strict, kq[p][:c_len] * decay[p], 0.0) for p in pairs}
    attn = {p: jnp.where(tri, kq[p][c_len:] * decay[p], 0.0) for p in pairs}
    ymat = {p: -low[p] for p in pairs}
    lp = low
    for _ in range(levels - 1):
        lp = {p: _dot(lp[p], lp[p]) for p in pairs}
        prod = {p: _dot(ymat[p], lp[p]) for p in pairs}
        ymat = {p: ymat[p] + lp[p] + prod[p] for p in pairs}
    sol = {}
    for (g, h) in pairs:
        rhs_p = rhs[h][sl(g)]
        sol[g, h] = rhs_p + _dot(ymat[g, h], rhs_p)

    def finish(g, h, o):
        r = o * lax.rsqrt(jnp.mean(o * o, axis=-1, keepdims=True) + RMS_EPS) * nw
        o_ref[sl(g), h * DV_A:(h + 1) * DV_A] = r * _silu(z_ref[sl(g), h * DV_A:(h + 1) * DV_A])

    def advance(group, state):
        ws = {p: _dot(jnp.concatenate([sol[p][:, DV_A:], qg[p[1]][sl(p[0])]], axis=0), state[p]) for p in group}
        v_new = {p: sol[p][:, :DV_A] - ws[p][:c_len] for p in group}
        o = {p: ws[p][c_len:] + _dot(attn[p], v_new[p]) for p in group}
        new = {p: state[p] * gl[p] + _dot_tn(kd[p], v_new[p]) for p in group}
        for p in group:
            finish(p[0], p[1], o[p])
        return new

    if carry:
        cur = {h: s_ref[0, h] for h in range(N_HEADS)}
        for g in range(n_seg):
            group = [(g, h) for h in range(N_HEADS)]
            new = advance(group, {(g, h): cur[h] for h in range(N_HEADS)})
            cur = {h: new[g, h] for h in range(N_HEADS)}
        for h in range(N_HEADS):
            s_ref[0, h] = cur[h]

        @pl.when(t == n_steps - 1)
        def _():
            for h in range(N_HEADS):
                sout_ref[0, h] = cur[h]
            cout_ref[...] = ubuf_ref[:, base + conv_len - hist:base + conv_len, :]
    else:
        new = advance(pairs, {p: s0_ref[p[0], p[1]] for p in pairs})
        for p in pairs:
            sout_ref[p[0], p[1]] = new[p]
        cout_ref[...] = ubuf_ref[:, base + conv_len - hist:base + conv_len, :]


def _gdn(p, row0, batch, seq, conv_w, hp, norm_w, s_conv, s0):
    chunk = math.gcd(seq, GDN_CHUNK)
    n_chunks = seq // chunk
    carry = n_chunks > 1
    if carry:
        n_seg = math.gcd(n_chunks, GDN_SEGS_CARRY)
        grid = (batch, n_chunks // n_seg)
        n_conv, conv_len = 1, n_seg * chunk
    else:
        n_seg = math.gcd(batch, GDN_SEGS_INDEP)
        grid = (batch // n_seg, 1)
        n_conv, conv_len = n_seg, chunk
    rows = n_seg * chunk
    n_steps = grid[1]
    blk0 = row0 // rows
    rowblk = lambda b, t: blk0 + b * n_steps + t
    kern = functools.partial(_gdn_kernel, chunk=chunk, n_seg=n_seg, carry=carry, n_steps=n_steps)
    return pl.pallas_call(
        kern,
        out_shape=(jax.ShapeDtypeStruct((batch * seq, W_A), F32),
                   jax.ShapeDtypeStruct((batch, N_HEADS, DK_A, DV_A), F32),
                   jax.ShapeDtypeStruct((batch, CONV_W - 1, CONV_DIM), F32)),
        grid=grid,
        in_specs=[
            pl.BlockSpec((rows, CONV_DIM), lambda b, t: (rowblk(b, t), OFF_QKV // CONV_DIM)),
            pl.BlockSpec((rows, W_A), lambda b, t: (rowblk(b, t), OFF_Z // W_A)),
            pl.BlockSpec((rows, LANES), lambda b, t: (rowblk(b, t), OFF_SMALL // LANES)),
            pl.BlockSpec((CONV_W, CONV_DIM), lambda b, t: (0, 0)),
            pl.BlockSpec((8, LANES), lambda b, t: (0, 0)),
            pl.BlockSpec((1, DV_A), lambda b, t: (0, 0)),
            pl.BlockSpec((n_conv, CONV_W - 1, CONV_DIM), lambda b, t: (b, 0, 0)),
            pl.BlockSpec((n_conv, N_HEADS, DK_A, DV_A), lambda b, t: (b, 0, 0, 0)),
        ],
        out_specs=(
            pl.BlockSpec((rows, W_A), lambda b, t: (b * n_steps + t, 0)),
            pl.BlockSpec((n_conv, N_HEADS, DK_A, DV_A), lambda b, t: (b, 0, 0, 0)),
            pl.BlockSpec((n_conv, CONV_W - 1, CONV_DIM), lambda b, t: (b, 0, 0)),
        ),
        scratch_shapes=[pltpu.VMEM((1, N_HEADS, DK_A, DV_A), F32),
                        pltpu.VMEM((n_conv, conv_len + 8, CONV_DIM), F32)],
        compiler_params=pltpu.CompilerParams(
            dimension_semantics=("arbitrary", "arbitrary"), vmem_limit_bytes=VMEM_LIMIT),
        name=f"gdn_c{chunk}",
    )(p, p, p, conv_w, hp, norm_w.reshape(1, DV_A), s_conv, s0)


def _gla_kernel(qk_ref, v_ref, g_ref, sm_ref, wgk_ref, bgk_ref, nw_ref, s0_ref,
                o_ref, sout_ref, s_ref, *, tb, cs, carry, n_steps):
    t = pl.program_id(1)
    n_sub = tb // cs
    w_kb = N_HEADS * DK_B

    if carry:
        @pl.when(t == 0)
        def _():
            s_ref[...] = s0_ref[...]

    gk = _dot(sm_ref[...], wgk_ref[...]) + bgk_ref[...]
    log_a = jax.nn.log_sigmoid(gk) / GLA_GATE_NORM
    sh = int(round(math.log2(cs)))
    r_i = lax.broadcasted_iota(jnp.int32, (tb, tb), 0)
    c_i = lax.broadcasted_iota(jnp.int32, (tb, tb), 1)
    same = (r_i >> sh) == (c_i >> sh)
    mask = same & (c_i <= r_i)
    gcum = _dot01(jnp.where(mask, 1.0, 0.0).astype(BF16), log_a)
    gtot = _dot01(jnp.where(same, 1.0, 0.0).astype(BF16), log_a)
    onehot = jnp.where((lax.broadcasted_iota(jnp.int32, (tb, n_sub), 0) >> sh)
                       == lax.broadcasted_iota(jnp.int32, (tb, n_sub), 1), 1.0, 0.0).astype(BF16)
    tn = lambda part: lax.dot_general(part, onehot, (((0,), (0,)), ((), ())), preferred_element_type=F32)
    hi, mid, lo = _split3(log_a)
    glcol = jnp.exp(tn(hi) + tn(mid) + tn(lo))

    qk = qk_ref[...]
    nw = nw_ref[...]
    heads = range(N_HEADS)
    sl = lambda j: slice(j * cs, (j + 1) * cs)
    qg, kd, vh, o_intra = [], [], [], []
    for h in heads:
        qh = qk[:, h * DK_B:(h + 1) * DK_B] * (DK_B ** -0.5)
        kh = qk[:, w_kb + h * DK_B:w_kb + (h + 1) * DK_B]
        gh = gcum[:, h * DK_B:(h + 1) * DK_B]
        gt = gtot[:, h * DK_B:(h + 1) * DK_B]
        qg.append(qh * jnp.exp(gh))
        kd.append(kh * jnp.exp(gt - gh))
        vh.append(v_ref[:, h * DV_B:(h + 1) * DV_B])
        attn = jnp.where(mask, _dot_nt(qg[h], kh * jnp.exp(-gh)), 0.0)
        o_intra.append(_dot(attn, vh[h]))
    upd = {(j, h): _dot_tn(kd[h][sl(j)], vh[h][sl(j)]) for j in range(n_sub) for h in heads}
    gl = lambda j, h: glcol[h * DK_B:(h + 1) * DK_B, j:j + 1]

    o_inter = [[] for _ in heads]
    if carry:
        cur = [s_ref[0, h] for h in heads]
        for j in range(n_sub):
            for h in heads:
                o_inter[h].append(_dot(qg[h][sl(j)], cur[h]))
                cur[h] = cur[h] * gl(j, h) + upd[j, h]
        for h in heads:
            s_ref[0, h] = cur[h]

        @pl.when(t == n_steps - 1)
        def _():
            for h in heads:
                sout_ref[0, h] = cur[h]
    else:
        for j in range(n_sub):
            for h in heads:
                s = s0_ref[j, h]
                o_inter[h].append(_dot(qg[h][sl(j)], s))
                sout_ref[j, h] = s * gl(j, h) + upd[j, h]

    for h in heads:
        o = o_intra[h] + jnp.concatenate(o_inter[h], axis=0)
        r = o * lax.rsqrt(jnp.mean(o * o, axis=-1, keepdims=True) + RMS_EPS) * nw
        o_ref[:, h * DV_B:(h + 1) * DV_B] = r * _silu(g_ref[:, h * DV_B:(h + 1) * DV_B])


def _gla(p, row0, batch, seq, wgk, bgk, norm_w, s0):
    cs = math.gcd(seq, GLA_CHUNK)
    carry = seq > cs
    if carry:
        tb = math.gcd(seq, GLA_TB)
        grid = (batch, seq // tb)
        n_state = 1
    else:
        tb = math.gcd(batch * seq, GLA_TB)
        n_state = tb // cs
        grid = (batch // n_state, 1)
    n_steps = grid[1]
    blk0 = row0 // tb
    rowblk = lambda b, t: blk0 + b * n_steps + t
    kern = functools.partial(_gla_kernel, tb=tb, cs=cs, carry=carry, n_steps=n_steps)
    return pl.pallas_call(
        kern,
        out_shape=(jax.ShapeDtypeStruct((batch * seq, W_B), F32),
                   jax.ShapeDtypeStruct((batch, N_HEADS, DK_B, DV_B), F32)),
        grid=grid,
        in_specs=[
            pl.BlockSpec((tb, 2 * N_HEADS * DK_B), lambda b, t: (rowblk(b, t), OFF_QK_B // (2 * N_HEADS * DK_B))),
            pl.BlockSpec((tb, W_B), lambda b, t: (rowblk(b, t), OFF_V_B // W_B)),
            pl.BlockSpec((tb, W_B), lambda b, t: (rowblk(b, t), OFF_G_B // W_B)),
            pl.BlockSpec((tb, LANES), lambda b, t: (rowblk(b, t), OFF_SMALL // LANES)),
            pl.BlockSpec((LANES, N_HEADS * DK_B), lambda b, t: (0, 0)),
            pl.BlockSpec((1, N_HEADS * DK_B), lambda b, t: (0, 0)),
            pl.BlockSpec((1, DV_B), lambda b, t: (0, 0)),
            pl.BlockSpec((n_state, N_HEADS, DK_B, DV_B), lambda b, t: (b, 0, 0, 0)),
        ],
        out_specs=(
            pl.BlockSpec((tb, W_B), lambda b, t: (b * n_steps + t, 0)),
            pl.BlockSpec((n_state, N_HEADS, DK_B, DV_B), lambda b, t: (b, 0, 0, 0)),
        ),
        scratch_shapes=[pltpu.VMEM((1, N_HEADS, DK_B, DV_B), F32)],
        compiler_params=pltpu.CompilerParams(
            dimension_semantics=("arbitrary", "arbitrary"), vmem_limit_bytes=VMEM_LIMIT),
        name=f"gla_c{cs}",
    )(p, p, p, p, wgk, bgk.reshape(1, N_HEADS * DK_B), norm_w.reshape(1, DV_B), s0)


def _topk_softmax(logits):
    tm = logits.shape[0]
    lane = lax.broadcasted_iota(jnp.int32, (tm, LANES), 1)
    lane_f = lane.astype(F32)
    neg = jnp.float32(-jnp.inf)
    cur = jnp.where(lane < N_EXPERTS, logits, neg)
    vals, idxs = [], []
    for _ in range(TOP_K):
        m = jnp.max(cur, axis=-1, keepdims=True)
        idx = jnp.min(jnp.where(cur == m, lane_f, float(LANES)), axis=-1, keepdims=True)
        vals.append(m)
        idxs.append(idx)
        cur = jnp.where(lane_f == idx, neg, cur)
    exps = [jnp.exp(v - vals[0]) for v in vals]
    den = exps[0] + exps[1] + exps[2] + exps[3]
    te = jnp.zeros((tm, LANES), F32)
    tg = jnp.zeros((tm, LANES), F32)
    for k in range(TOP_K):
        te = jnp.where(lane == k, idxs[k], te)
        tg = jnp.where(lane == k, exps[k] / den, tg)
    return te.astype(jnp.int32), tg


def _merge_kernel(xp_ref, xs_ref, oap_ref, oas_ref, obp_ref, obs_ref, ga_ref, gb_ref, wua_ref, wub_ref, wo_ref,
                  n2_ref, wr_ref, br_ref, x1_ref, h2_ref, te_ref, tg_ref, *, n_first):
    i = pl.program_id(0)
    tm = x1_ref.shape[0]
    hm = tm // MERGE_SPLIT
    groups = [slice(a * hm, (a + 1) * hm) for a in range(MERGE_SPLIT)]
    oa = _pick(i, n_first, oap_ref, oas_ref).astype(BF16)
    ob = _pick(i, n_first, obp_ref, obs_ref).astype(BF16)
    x = _pick(i, n_first, xp_ref, xs_ref)
    ua = [jnp.dot(oa[g], wua_ref[...], preferred_element_type=F32) for g in groups]
    ub = [jnp.dot(ob[g], wub_ref[...], preferred_element_type=F32) for g in groups]
    merged = [jax.nn.sigmoid(ga_ref[g, :]) * ua[a] + jax.nn.sigmoid(gb_ref[g, :]) * ub[a]
              for a, g in enumerate(groups)]
    x1 = [x[g] + jnp.dot(merged[a].astype(BF16), wo_ref[...], preferred_element_type=F32)
          for a, g in enumerate(groups)]
    h2 = [v * lax.rsqrt(jnp.mean(v * v, axis=-1, keepdims=True) + RMS_EPS) * n2_ref[...] for v in x1]
    h_hi = [v.astype(BF16) for v in h2]
    h_lo = [(v - hi.astype(F32)).astype(BF16) for v, hi in zip(h2, h_hi)]
    l_hi = [jnp.dot(hi, wr_ref[...], preferred_element_type=F32) for hi in h_hi]
    l_lo = [jnp.dot(lo, wr_ref[:, :LANES], preferred_element_type=F32) for lo in h_lo]
    for a, g in enumerate(groups):
        x1_ref[g, :] = x1[a]
        _store_packed_rows(h2_ref, (), a * hm, _pack_bf16_pairs(h2[a]))
        logits = l_hi[a][:, :LANES] + l_hi[a][:, LANES:] + l_lo[a] + br_ref[...]
        te, tg = _topk_softmax(logits)
        te_ref[g, :] = te
        tg_ref[g, :] = tg


def _merge(x_p, x_s, oa_p, oa_s, ob_p, ob_s, p, wua, wub, wo, norm2_w, wr, br):
    n = x_p.shape[0] + x_s.shape[0]
    tm = MERGE_TM
    n_first = x_p.shape[0] // tm
    row = lambda i: (i, 0)
    const = lambda i: (0, 0)
    return pl.pallas_call(
        functools.partial(_merge_kernel, n_first=n_first),
        out_shape=(jax.ShapeDtypeStruct((n, D_MODEL), F32), jax.ShapeDtypeStruct((n * PACK_ROWS, LANES), jnp.uint32),
                   jax.ShapeDtypeStruct((n, LANES), jnp.int32), jax.ShapeDtypeStruct((n, LANES), F32)),
        grid=(n // tm,),
        in_specs=[
            *_split_specs((tm, D_MODEL), n_first),
            *_split_specs((tm, W_A), n_first),
            *_split_specs((tm, W_B), n_first),
            pl.BlockSpec((tm, D_MODEL), lambda i: (i, OFF_GATE_A // D_MODEL)),
            pl.BlockSpec((tm, D_MODEL), lambda i: (i, OFF_GATE_B // D_MODEL)),
            pl.BlockSpec((W_A, D_MODEL), const),
            pl.BlockSpec((W_B, D_MODEL), const),
            pl.BlockSpec((D_MODEL, D_MODEL), const),
            pl.BlockSpec((1, D_MODEL), const),
            pl.BlockSpec((D_MODEL, 2 * LANES), const),
            pl.BlockSpec((1, LANES), const),
        ],
        out_specs=(pl.BlockSpec((tm, D_MODEL), row), pl.BlockSpec((tm * PACK_ROWS, LANES), row),
                   pl.BlockSpec((tm, LANES), row), pl.BlockSpec((tm, LANES), row)),
        compiler_params=pltpu.CompilerParams(
            dimension_semantics=("arbitrary",), vmem_limit_bytes=VMEM_LIMIT),
        name="merge_router",
    )(x_p, x_s, oa_p, oa_s, ob_p, ob_s, p, p, wua, wub, wo, norm2_w.reshape(1, D_MODEL), wr, br)


def _moe_kernel(blk_e_ref, n_used_ref, src_ref, dst_ref, h2_ref, wgu_ref, bgu_ref,
                wdn_ref, bdn_ref, y_hbm, xstage, ybuf, ssem):
    del blk_e_ref
    i = pl.program_id(0)
    n_used = n_used_ref[0]
    bm = MOE_BM
    pr = PACK_ROWS

    def scatter_copy(row, s, r):
        return pltpu.make_async_copy(ybuf.at[s, pl.ds(r * pr, pr), :],
                                     y_hbm.at[pl.ds(pl.multiple_of(row * pr, pr), pr), :], ssem.at[s])

    @pl.when(i == 0)
    def _():
        ybuf[...] = jnp.zeros_like(ybuf)
        n_real = y_hbm.shape[0] - MOE_SUB * bm * pr
        for s in range(MOE_SUB):
            sink = pltpu.make_async_copy(ybuf.at[s], y_hbm.at[pl.ds(n_real + s * bm * pr, bm * pr), :], ssem.at[s])
            sink.start()
            sink.wait()

    @pl.when(i < n_used)
    def _():
        for s in range(MOE_SUB):
            @pl.when(i >= 1)
            def _():
                for r in range(bm):
                    scatter_copy(0, s, r).wait()

            for r in range(bm):
                tok = pl.multiple_of(src_ref[0, 0, s * bm + r] * pr, pr)
                xstage[s, pl.ds(r * pr, pr), :] = h2_ref[pl.ds(tok, pr), :]
            x = _load_packed_rows(xstage, (s,), bm, BF16)
            gu = jnp.dot(x, wgu_ref[...], preferred_element_type=F32) + bgu_ref[...]
            gate = jnp.minimum(gu[:, :D_FF], SWIGLU_LIMIT)
            up = jnp.clip(gu[:, D_FF:], -SWIGLU_LIMIT, SWIGLU_LIMIT)
            hmid = (up + 1.0) * (gate * jax.nn.sigmoid(SWIGLU_ALPHA * gate))
            y = jnp.dot(hmid.astype(BF16), wdn_ref[...], preferred_element_type=F32) + bdn_ref[...]
            _store_packed_rows(ybuf, (s,), 0, _pack_bf16_pairs(y))
            for r in range(bm):
                scatter_copy(dst_ref[0, 0, s * bm + r], s, r).start(priority=r % 2)

    @pl.when(i == n_used - 1)
    def _():
        for s in range(MOE_SUB):
            for r in range(bm):
                scatter_copy(0, s, r).wait()


def _moe(blk_e, n_used, src_tok, dst_row, h2, wgu, bgu, wdn, bdn, y_rows):
    n_steps = src_tok.shape[0]
    step_rows = MOE_SUB * MOE_BM
    cur = lambda i, be, nu: (i, 0, 0)
    wsel = lambda i, be, nu: (be[i], 0, 0)
    smem_blk = lambda im: pl.BlockSpec((1, 1, step_rows), im, memory_space=pltpu.SMEM)
    grid_spec = pltpu.PrefetchScalarGridSpec(
        num_scalar_prefetch=2,
        grid=(n_steps,),
        in_specs=[
            smem_blk(cur), smem_blk(cur),
            pl.BlockSpec(h2.shape, lambda i, be, nu: (0, 0), pipeline_mode=pl.Buffered(1)),
            pl.BlockSpec((None, D_MODEL, 2 * D_FF), wsel),
            pl.BlockSpec((None, 1, 2 * D_FF), wsel),
            pl.BlockSpec((None, D_FF, D_MODEL), wsel),
            pl.BlockSpec((None, 1, D_MODEL), wsel),
        ],
        out_specs=pl.BlockSpec(memory_space=pl.ANY),
        scratch_shapes=[pltpu.VMEM((MOE_SUB, MOE_BM * PACK_ROWS, LANES), jnp.uint32),
                        pltpu.VMEM((MOE_SUB, MOE_BM * PACK_ROWS, LANES), jnp.uint32),
                        pltpu.SemaphoreType.DMA((MOE_SUB,))],
    )
    return pl.pallas_call(
        _moe_kernel,
        out_shape=jax.ShapeDtypeStruct((y_rows * PACK_ROWS, LANES), jnp.uint32),
        grid_spec=grid_spec,
        compiler_params=pltpu.CompilerParams(
            dimension_semantics=("arbitrary",), vmem_limit_bytes=MOE_VMEM_LIMIT),
        name="moe_experts",
    )(blk_e, n_used, src_tok, dst_row, h2, wgu, bgu, wdn, bdn)


def _route(top_e, n_tok):
    bm = MOE_SUB * MOE_BM
    n_assign = n_tok * TOP_K
    n_blocks = n_assign // bm + N_EXPERTS
    n_dummy = N_EXPERTS * bm
    flat_e = top_e.reshape(-1)
    counts = jnp.sum((flat_e[:, None] == jnp.arange(N_EXPERTS, dtype=jnp.int32)[None, :]).astype(jnp.int32), axis=0)
    padded = (counts + bm - 1) // bm * bm
    pad_end = jnp.cumsum(padded)
    d_e = jnp.arange(n_dummy, dtype=jnp.int32) // bm
    d_active = (jnp.arange(n_dummy, dtype=jnp.int32) % bm) < (padded - counts)[d_e]
    keys = jnp.concatenate([flat_e * 2, jnp.where(d_active, d_e * 2 + 1, 2 * N_EXPERTS)])
    vals = jnp.concatenate([jnp.arange(n_assign, dtype=jnp.int32), jnp.full((n_dummy,), -1, jnp.int32)])
    _, assign = lax.sort((keys, vals), num_keys=1, is_stable=True)
    assign = assign[:n_blocks * bm]
    rows = jnp.arange(n_blocks * bm, dtype=jnp.int32)
    valid = assign >= 0
    src_tok = jnp.where(valid, assign // TOP_K, 0)
    dst_row = jnp.where(valid, (assign % TOP_K) * n_tok + assign // TOP_K, n_assign + rows % bm)
    blk_start = jnp.arange(n_blocks, dtype=jnp.int32) * bm
    blk_e = jnp.minimum(jnp.sum((pad_end[None, :] <= blk_start[:, None]).astype(jnp.int32), axis=1),
                        N_EXPERTS - 1).astype(jnp.int32)
    n_used = (pad_end[-1] // bm).astype(jnp.int32).reshape(1)
    return (blk_e, n_used, src_tok.reshape(n_blocks, 1, bm).astype(jnp.int32),
            dst_row.reshape(n_blocks, 1, bm).astype(jnp.int32))


def _combine_kernel(x1_ref, y0_ref, y1_ref, y2_ref, y3_ref, tg_ref, nf_ref, op_ref, os_ref, *, n_first):
    tg = tg_ref[...]
    tm = x1_ref.shape[0]
    rows = lambda y_ref: _load_packed_rows(y_ref, (), tm, F32)
    acc = rows(y0_ref) * tg[:, 0:1]
    for k, y_ref in enumerate((y1_ref, y2_ref, y3_ref), start=1):
        acc = acc + rows(y_ref) * tg[:, k:k + 1]
    x = x1_ref[...] + acc
    out = x * lax.rsqrt(jnp.mean(x * x, axis=-1, keepdims=True) + RMS_EPS) * nf_ref[...]
    i = pl.program_id(0)

    @pl.when(i < n_first)
    def _():
        op_ref[...] = out

    @pl.when(i >= n_first)
    def _():
        os_ref[...] = out


def _combine(x1, y, tg, norm_f_w, n_p):
    n = x1.shape[0]
    tm = COMBINE_TM
    n_first = n_p // tm
    row = lambda i: (i, 0)
    y_spec = lambda k: pl.BlockSpec((tm * PACK_ROWS, LANES), lambda i: (k * (n // tm) + i, 0))
    return pl.pallas_call(
        functools.partial(_combine_kernel, n_first=n_first),
        out_shape=(jax.ShapeDtypeStruct((n_p, D_MODEL), F32), jax.ShapeDtypeStruct((n - n_p, D_MODEL), F32)),
        grid=(n // tm,),
        in_specs=[pl.BlockSpec((tm, D_MODEL), row), y_spec(0), y_spec(1), y_spec(2), y_spec(3),
                  pl.BlockSpec((tm, LANES), row), pl.BlockSpec((1, D_MODEL), lambda i: (0, 0))],
        out_specs=_split_specs((tm, D_MODEL), n_first),
        compiler_params=pltpu.CompilerParams(
            dimension_semantics=("arbitrary",), vmem_limit_bytes=VMEM_LIMIT),
        name="combine_norm",
    )(x1, y, y, y, y, tg, norm_f_w.reshape(1, D_MODEL))


def _pack_w_in(w_in):
    s = np.cumsum((0,) + IN_SIZES)
    qkv, z, al, be, qb, kb, vb, gb, gk, ga, gbt = [w_in[:, s[i]:s[i + 1]] for i in range(len(IN_SIZES))]
    small = jnp.concatenate(
        [al, be, gk, jnp.zeros((D_MODEL, LANES - 2 * N_HEADS - GK_RANK), w_in.dtype)], axis=1)
    return jnp.concatenate([qkv, z, ga, gbt, qb, kb, vb, gb, small], axis=1).astype(BF16)


def kernel(x_prompt, x_sample, state_gdn, state_gdn_conv, state_gla, norm1_w, w_in, conv_w, a_log,
           dt_bias, gdn_norm_w, gla_gk_up, gla_gk_b, gla_norm_w, w_up_a, w_up_b, w_o, norm2_w,
           w_router, b_router, w_gate_up, b_gate_up, w_down, b_down, norm_f_w):
    bp, tp, _ = x_prompt.shape
    bs, ts, _ = x_sample.shape
    n_p, n_s = bp * tp, bs * ts
    n = n_p + n_s
    l = 0

    x_p = x_prompt.reshape(n_p, D_MODEL)
    x_s = x_sample.reshape(n_s, D_MODEL)
    p = _inproj(x_p, x_s, norm1_w[l], _pack_w_in(w_in[l]))

    hp = jnp.zeros((8, LANES), F32).at[0, :N_HEADS].set(a_log[l]).at[1, :N_HEADS].set(dt_bias[l])
    wgk = jnp.zeros((LANES, N_HEADS * DK_B), F32).at[2 * N_HEADS:2 * N_HEADS + GK_RANK].set(gla_gk_up[l]).astype(BF16)
    zeros = lambda shape: jnp.zeros(shape, F32)

    oa_p, gdn_p, conv_p = _gdn(p, 0, bp, tp, conv_w[l], hp, gdn_norm_w[l],
                               zeros((bp, CONV_W - 1, CONV_DIM)), zeros((bp, N_HEADS, DK_A, DV_A)))
    oa_s, gdn_s, conv_s = _gdn(p, n_p, bs, ts, conv_w[l], hp, gdn_norm_w[l], state_gdn_conv[l], state_gdn[l])
    ob_p, gla_p = _gla(p, 0, bp, tp, wgk, gla_gk_b[l], gla_norm_w[l], zeros((bp, N_HEADS, DK_B, DV_B)))
    ob_s, gla_s = _gla(p, n_p, bs, ts, wgk, gla_gk_b[l], gla_norm_w[l], state_gla[l])

    wr = jnp.zeros((D_MODEL, LANES), F32).at[:, :N_EXPERTS].set(w_router[l])
    wr_hi = wr.astype(BF16)
    wr = jnp.concatenate([wr_hi, (wr - wr_hi.astype(F32)).astype(BF16)], axis=1)
    br = jnp.zeros((1, LANES), F32).at[0, :N_EXPERTS].set(b_router[l])
    x1, h2, te, tg = _merge(x_p, x_s, oa_p, oa_s, ob_p, ob_s, p, w_up_a[l].astype(BF16),
                            w_up_b[l].astype(BF16), w_o[l].astype(BF16), norm2_w[l], wr, br)

    blk_e, n_used, src_tok, dst_row = _route(te[:, :TOP_K], n)
    y = _moe(blk_e, n_used, src_tok, dst_row, h2, w_gate_up[l].astype(BF16),
             b_gate_up[l].reshape(N_EXPERTS, 1, 2 * D_FF), w_down[l].astype(BF16),
             b_down[l].reshape(N_EXPERTS, 1, D_MODEL), n * TOP_K + MOE_SUB * MOE_BM)
    out_p, out_s = _combine(x1, y, tg, norm_f_w, n_p)
    y_prompt = out_p.reshape(bp, tp, D_MODEL)
    y_sample = out_s.reshape(bs, ts, D_MODEL)
    return (y_prompt, y_sample, gdn_p[None], conv_p[None], gla_p[None], gdn_s[None], conv_s[None],
            gla_s[None])
```

```python
import functools
import math

import jax
import jax.numpy as jnp
import numpy as np
from jax import lax
from jax.experimental import pallas as pl
from jax.experimental.pallas import tpu as pltpu

F32 = jnp.float32
BF16 = jnp.bfloat16

D_MODEL = 1024
N_HEADS = 4
DK_A = 128
DV_A = 128
W_A = N_HEADS * DV_A
CONV_W = 4
CONV_DIM = 2 * N_HEADS * DK_A + W_A
GDN_CHUNK = 64
DK_B = 64
DV_B = 128
W_B = N_HEADS * DV_B
GK_RANK = 16
GLA_GATE_NORM = 16.0
GLA_CHUNK = 16
N_EXPERTS = 32
TOP_K = 4
D_FF = 1024
SWIGLU_LIMIT = 7.0
SWIGLU_ALPHA = 1.702
RMS_EPS = 1e-6
IN_SIZES = (CONV_DIM, W_A, N_HEADS, N_HEADS, N_HEADS * DK_B, N_HEADS * DK_B, W_B, W_B, GK_RANK,
            D_MODEL, D_MODEL)

LANES = 128

OFF_QKV = 0
OFF_Z = 1536
OFF_GATE_A = 2048
OFF_GATE_B = 3072
OFF_QK_B = 4096
OFF_V_B = 4608
OFF_G_B = 5120
OFF_SMALL = 5632
P_COLS = 5760

INPROJ_TM = 512
GLA_TB = 256
GDN_SEGS_CARRY = 4
GDN_SEGS_INDEP = 8
MERGE_TM = 512
MERGE_SPLIT = 2
MOE_BM = 256
MOE_SUB = 2
PACK_ROWS = D_MODEL // 2 // LANES
MOE_VMEM_LIMIT = 60 * 1024 * 1024
COMBINE_TM = 256
VMEM_LIMIT = 56 * 1024 * 1024


def _dot(a, b):
    return jnp.dot(a.astype(BF16), b.astype(BF16), preferred_element_type=F32)


def _dot_nt(a, b):
    return lax.dot_general(a.astype(BF16), b.astype(BF16), (((1,), (1,)), ((), ())),
                           preferred_element_type=F32)


def _dot_tn(a, b):
    return lax.dot_general(a, b, (((0,), (0,)), ((), ())), preferred_element_type=F32)


def _split3(x):
    hi = x.astype(BF16)
    r1 = x - hi.astype(F32)
    mid = r1.astype(BF16)
    lo = (r1 - mid.astype(F32)).astype(BF16)
    return hi, mid, lo


def _dot01(m01, x):
    hi, mid, lo = _split3(x)
    d = lambda p: jnp.dot(m01, p, preferred_element_type=F32)
    return d(hi) + d(mid) + d(lo)


def _silu(x):
    return x * jax.nn.sigmoid(x)


def _pack_bf16_pairs(x):
    bits = pltpu.bitcast(x.astype(BF16).astype(F32), jnp.uint32)
    half = D_MODEL // 2
    return (bits[:, :half] >> 16) | (bits[:, half:] & jnp.uint32(0xFFFF0000))


def _store_packed_rows(ref, lead, row0, words):
    n = words.shape[0]
    for c in range(PACK_ROWS):
        ref[(*lead, pl.ds(row0 * PACK_ROWS + c, n, stride=PACK_ROWS), slice(None))] = words[:, c * LANES:(c + 1) * LANES]


def _load_packed_rows(ref, lead, n, dtype):
    words = [ref[(*lead, pl.ds(c, n, stride=PACK_ROWS), slice(None))] for c in range(PACK_ROWS)]
    lo = [pltpu.bitcast(w << 16, F32).astype(dtype) for w in words]
    hi = [pltpu.bitcast(w & jnp.uint32(0xFFFF0000), F32).astype(dtype) for w in words]
    return jnp.concatenate(lo + hi, axis=1)


def _pick(i, n_first, first_ref, second_ref):
    return jnp.where(i < n_first, first_ref[...], second_ref[...])


def _split_specs(block, n_first):
    return (pl.BlockSpec(block, lambda i: (jnp.minimum(i, n_first - 1), 0)),
            pl.BlockSpec(block, lambda i: (jnp.maximum(i - n_first, 0), 0)))


def _inproj_kernel(xp_ref, xs_ref, nw_ref, w_ref, o_ref, *, n_first):
    x = _pick(pl.program_id(0), n_first, xp_ref, xs_ref)
    ms = jnp.mean(x * x, axis=-1, keepdims=True)
    h = (x * lax.rsqrt(ms + RMS_EPS) * nw_ref[...]).astype(BF16)
    o_ref[...] = jnp.dot(h, w_ref[...], preferred_element_type=F32)


def _inproj(x_p, x_s, norm_w, w_packed):
    tm = INPROJ_TM
    n_first = x_p.shape[0] // tm
    n = x_p.shape[0] + x_s.shape[0]
    return pl.pallas_call(
        functools.partial(_inproj_kernel, n_first=n_first),
        out_shape=jax.ShapeDtypeStruct((n, P_COLS), F32),
        grid=(n // tm,),
        in_specs=[
            *_split_specs((tm, D_MODEL), n_first),
            pl.BlockSpec((1, D_MODEL), lambda i: (0, 0)),
            pl.BlockSpec((D_MODEL, P_COLS), lambda i: (0, 0), pipeline_mode=pl.Buffered(1)),
        ],
        out_specs=pl.BlockSpec((tm, P_COLS), lambda i: (i, 0)),
        compiler_params=pltpu.CompilerParams(
            dimension_semantics=("arbitrary",), vmem_limit_bytes=VMEM_LIMIT),
        name="inproj",
    )(x_p, x_s, norm_w.reshape(1, D_MODEL), w_packed)


def _gdn_kernel(qkv_ref, z_ref, sm_ref, cw_ref, hp_ref, nw_ref, sconv_ref, s0_ref,
                o_ref, sout_ref, cout_ref, s_ref, ubuf_ref, *, chunk, n_seg, carry, n_steps):
    c_len = chunk
    rows = n_seg * c_len
    t = pl.program_id(1)
    hist = CONV_W - 1
    base = 8
    conv_len = rows if carry else c_len
    n_conv = 1 if carry else n_seg

    def init_history():
        ubuf_ref[:, 0:base, :] = jnp.zeros((n_conv, base, CONV_DIM), F32)
        ubuf_ref[:, base - hist:base, :] = sconv_ref[...]

    if carry:
        @pl.when(t == 0)
        def _():
            s_ref[...] = s0_ref[...]
            init_history()

        @pl.when(t > 0)
        def _():
            ubuf_ref[:, base - hist:base, :] = ubuf_ref[:, base + conv_len - hist:base + conv_len, :]
    else:
        init_history()
    ubuf_ref[:, base:base + conv_len, :] = qkv_ref[...].reshape(n_conv, conv_len, CONV_DIM)

    cw = cw_ref[...]
    u = ubuf_ref[...]
    u1 = pltpu.roll(u, 1, axis=1)
    v = u1 * cw[0:1, :] + u * cw[1:2, :]
    y = (pltpu.roll(v, 2, axis=1) + u1 * cw[2:3, :]) + u * cw[3:4, :]
    y = _silu(y[:, base:, :]).reshape(rows, CONV_DIM)

    sm = sm_ref[...]
    hp = hp_ref[...]
    lane = lax.broadcasted_iota(jnp.int32, (rows, LANES), 1)
    g_all = -jnp.exp(hp[0:1, :]) * jax.nn.softplus(sm + hp[1:2, :])
    g_all = jnp.where(lane < N_HEADS, g_all, 0.0)
    beta_all = jax.nn.sigmoid(sm)

    sh = int(round(math.log2(c_len)))
    r_b = lax.broadcasted_iota(jnp.int32, (rows, rows), 0)
    c_b = lax.broadcasted_iota(jnp.int32, (rows, rows), 1)
    seg_tril = ((r_b >> sh) == (c_b >> sh)) & (c_b <= r_b)
    gc = _dot01(jnp.where(seg_tril, 1.0, 0.0).astype(BF16), g_all)

    r_i = lax.broadcasted_iota(jnp.int32, (c_len, c_len), 0)
    c_i = lax.broadcasted_iota(jnp.int32, (c_len, c_len), 1)
    tri = c_i <= r_i
    strict = c_i < r_i
    eye = c_i == r_i
    nw = nw_ref[...]
    levels = sh

    qn, kn, kb, rhs, qg, gcol = [], [], [], [], [], []
    for h in range(N_HEADS):
        qh = y[:, h * DK_A:(h + 1) * DK_A]
        kh = y[:, W_A + h * DK_A:W_A + (h + 1) * DK_A]
        vh = y[:, 2 * W_A + h * DV_A:2 * W_A + (h + 1) * DV_A]
        qn_h = qh * lax.rsqrt(jnp.sum(qh * qh, axis=-1, keepdims=True) + 1e-6) * (DK_A ** -0.5)
        kn_h = kh * lax.rsqrt(jnp.sum(kh * kh, axis=-1, keepdims=True) + 1e-6)
        gcol_h = gc[:, h:h + 1]
        eg = jnp.exp(gcol_h)
        bcol = beta_all[:, N_HEADS + h:N_HEADS + h + 1]
        kb_h = kn_h * bcol
        qn.append(qn_h)
        kn.append(kn_h)
        kb.append(kb_h)
        rhs.append(jnp.concatenate([vh * bcol, kb_h * eg], axis=1))
        qg.append(qn_h * eg)
        gcol.append(gcol_h)

    pairs = [(g, h) for g in range(n_seg) for h in range(N_HEADS)]
    sl = lambda g: slice(g * c_len, (g + 1) * c_len)

    decay, kd, gl = {}, {}, {}
    for (g, h) in pairs:
        gcol_p = gcol[h][sl(g)]
        gcb = jnp.broadcast_to(gcol_p, (c_len, c_len))
        grow = jnp.sum(jnp.where(eye, gcb, 0.0), axis=0, keepdims=True)
        decay[g, h] = jnp.where(tri, jnp.exp(jnp.where(tri, gcb - grow, 0.0)), 0.0)
        glast = gcol[h][(g + 1) * c_len - 1:(g + 1) * c_len]
        kd[g, h] = kn[h][sl(g)] * jnp.exp(glast - gcol_p)
        gl[g, h] = jnp.exp(glast)

    kq = {p: _dot_nt(jnp.concatenate([kb[p[1]][sl(p[0])], qn[p[1]][sl(p[0])]], axis=0), kn[p[1]][sl(p[0])])
          for p in pairs}
    low = {p: jnp.where(strict, kq[p][:c_len] * decay[p], 0.0) for p in pairs}
    attn = {p: jnp.where(tri, kq[p][c_len:] * decay[p], 0.0) for p in pairs}
    ymat = {p: -low[p] for p in pairs}
    lp = low
    for _ in range(levels - 1):
        lp = {p: _dot(lp[p], lp[p]) for p in pairs}
        prod = {p: _dot(ymat[p], lp[p]) for p in pairs}
        ymat = {p: ymat[p] + lp[p] + prod[p] for p in pairs}
    sol = {}
    for (g, h) in pairs:
        rhs_p = rhs[h][sl(g)]
        sol[g, h] = rhs_p + _dot(ymat[g, h], rhs_p)

    def finish(g, h, o):
        r = o * lax.rsqrt(jnp.mean(o * o, axis=-1, keepdims=True) + RMS_EPS) * nw
        o_ref[sl(g), h * DV_A:(h + 1) * DV_A] = r * _silu(z_ref[sl(g), h * DV_A:(h + 1) * DV_A])

    def advance(group, state):
        ws = {p: _dot(jnp.concatenate([sol[p][:, DV_A:], qg[p[1]][sl(p[0])]], axis=0), state[p]) for p in group}
        v_new = {p: sol[p][:, :DV_A] - ws[p][:c_len] for p in group}
        o = {p: ws[p][c_len:] + _dot(attn[p], v_new[p]) for p in group}
        new = {p: state[p] * gl[p] + _dot_tn(kd[p], v_new[p]) for p in group}
        for p in group:
            finish(p[0], p[1], o[p])
        return new

    if carry:
        cur = {h: s_ref[0, h] for h in range(N_HEADS)}
        for g in range(n_seg):
            group = [(g, h) for h in range(N_HEADS)]
            new = advance(group, {(g, h): cur[h] for h in range(N_HEADS)})
            cur = {h: new[g, h] for h in range(N_HEADS)}
        for h in range(N_HEADS):
            s_ref[0, h] = cur[h]

        @pl.when(t == n_steps - 1)
        def _():
            for h in range(N_HEADS):
                sout_ref[0, h] = cur[h]
            cout_ref[...] = ubuf_ref[:, base + conv_len - hist:base + conv_len, :]
    else:
        new = advance(pairs, {p: s0_ref[p[0], p[1]] for p in pairs})
        for p in pairs:
            sout_ref[p[0], p[1]] = new[p]
        cout_ref[...] = ubuf_ref[:, base + conv_len - hist:base + conv_len, :]


def _gdn(p, row0, batch, seq, conv_w, hp, norm_w, s_conv, s0):
    chunk = math.gcd(seq, GDN_CHUNK)
    n_chunks = seq // chunk
    carry = n_chunks > 1
    if carry:
        n_seg = math.gcd(n_chunks, GDN_SEGS_CARRY)
        grid = (batch, n_chunks // n_seg)
        n_conv, conv_len = 1, n_seg * chunk
    else:
        n_seg = math.gcd(batch, GDN_SEGS_INDEP)
        grid = (batch // n_seg, 1)
        n_conv, conv_len = n_seg, chunk
    rows = n_seg * chunk
    n_steps = grid[1]
    blk0 = row0 // rows
    rowblk = lambda b, t: blk0 + b * n_steps + t
    kern = functools.partial(_gdn_kernel, chunk=chunk, n_seg=n_seg, carry=carry, n_steps=n_steps)
    return pl.pallas_call(
        kern,
        out_shape=(jax.ShapeDtypeStruct((batch * seq, W_A), F32),
                   jax.ShapeDtypeStruct((batch, N_HEADS, DK_A, DV_A), F32),
                   jax.ShapeDtypeStruct((batch, CONV_W - 1, CONV_DIM), F32)),
        grid=grid,
        in_specs=[
            pl.BlockSpec((rows, CONV_DIM), lambda b, t: (rowblk(b, t), OFF_QKV // CONV_DIM)),
            pl.BlockSpec((rows, W_A), lambda b, t: (rowblk(b, t), OFF_Z // W_A)),
            pl.BlockSpec((rows, LANES), lambda b, t: (rowblk(b, t), OFF_SMALL // LANES)),
            pl.BlockSpec((CONV_W, CONV_DIM), lambda b, t: (0, 0)),
            pl.BlockSpec((8, LANES), lambda b, t: (0, 0)),
            pl.BlockSpec((1, DV_A), lambda b, t: (0, 0)),
            pl.BlockSpec((n_conv, CONV_W - 1, CONV_DIM), lambda b, t: (b, 0, 0)),
            pl.BlockSpec((n_conv, N_HEADS, DK_A, DV_A), lambda b, t: (b, 0, 0, 0)),
        ],
        out_specs=(
            pl.BlockSpec((rows, W_A), lambda b, t: (b * n_steps + t, 0)),
            pl.BlockSpec((n_conv, N_HEADS, DK_A, DV_A), lambda b, t: (b, 0, 0, 0)),
            pl.BlockSpec((n_conv, CONV_W - 1, CONV_DIM), lambda b, t: (b, 0, 0)),
        ),
        scratch_shapes=[pltpu.VMEM((1, N_HEADS, DK_A, DV_A), F32),
                        pltpu.VMEM((n_conv, conv_len + 8, CONV_DIM), F32)],
        compiler_params=pltpu.CompilerParams(
            dimension_semantics=("arbitrary", "arbitrary"), vmem_limit_bytes=VMEM_LIMIT),
        name=f"gdn_c{chunk}",
    )(p, p, p, conv_w, hp, norm_w.reshape(1, DV_A), s_conv, s0)


def _gla_kernel(qk_ref, v_ref, g_ref, sm_ref, wgk_ref, bgk_ref, nw_ref, s0_ref,
                o_ref, sout_ref, s_ref, *, tb, cs, carry, n_steps):
    t = pl.program_id(1)
    n_sub = tb // cs
    w_kb = N_HEADS * DK_B

    if carry:
        @pl.when(t == 0)
        def _():
            s_ref[...] = s0_ref[...]

    gk = _dot(sm_ref[...], wgk_ref[...]) + bgk_ref[...]
    log_a = jax.nn.log_sigmoid(gk) / GLA_GATE_NORM
    sh = int(round(math.log2(cs)))
    r_i = lax.broadcasted_iota(jnp.int32, (tb, tb), 0)
    c_i = lax.broadcasted_iota(jnp.int32, (tb, tb), 1)
    same = (r_i >> sh) == (c_i >> sh)
    mask = same & (c_i <= r_i)
    gcum = _dot01(jnp.where(mask, 1.0, 0.0).astype(BF16), log_a)
    gtot = _dot01(jnp.where(same, 1.0, 0.0).astype(BF16), log_a)
    onehot = jnp.where((lax.broadcasted_iota(jnp.int32, (tb, n_sub), 0) >> sh)
                       == lax.broadcasted_iota(jnp.int32, (tb, n_sub), 1), 1.0, 0.0).astype(BF16)
    tn = lambda part: lax.dot_general(part, onehot, (((0,), (0,)), ((), ())), preferred_element_type=F32)
    hi, mid, lo = _split3(log_a)
    glcol = jnp.exp(tn(hi) + tn(mid) + tn(lo))

    qk = qk_ref[...]
    nw = nw_ref[...]
    heads = range(N_HEADS)
    sl = lambda j: slice(j * cs, (j + 1) * cs)
    qg, kd, vh, o_intra = [], [], [], []
    for h in heads:
        qh = qk[:, h * DK_B:(h + 1) * DK_B] * (DK_B ** -0.5)
        kh = qk[:, w_kb + h * DK_B:w_kb + (h + 1) * DK_B]
        gh = gcum[:, h * DK_B:(h + 1) * DK_B]
        gt = gtot[:, h * DK_B:(h + 1) * DK_B]
        qg.append(qh * jnp.exp(gh))
        kd.append(kh * jnp.exp(gt - gh))
        vh.append(v_ref[:, h * DV_B:(h + 1) * DV_B])
        attn = jnp.where(mask, _dot_nt(qg[h], kh * jnp.exp(-gh)), 0.0)
        o_intra.append(_dot(attn, vh[h]))
    upd = {(j, h): _dot_tn(kd[h][sl(j)], vh[h][sl(j)]) for j in range(n_sub) for h in heads}
    gl = lambda j, h: glcol[h * DK_B:(h + 1) * DK_B, j:j + 1]

    o_inter = [[] for _ in heads]
    if carry:
        cur = [s_ref[0, h] for h in heads]
        for j in range(n_sub):
            for h in heads:
                o_inter[h].append(_dot(qg[h][sl(j)], cur[h]))
                cur[h] = cur[h] * gl(j, h) + upd[j, h]
        for h in heads:
            s_ref[0, h] = cur[h]

        @pl.when(t == n_steps - 1)
        def _():
            for h in heads:
                sout_ref[0, h] = cur[h]
    else:
        for j in range(n_sub):
            for h in heads:
                s = s0_ref[j, h]
                o_inter[h].append(_dot(qg[h][sl(j)], s))
                sout_ref[j, h] = s * gl(j, h) + upd[j, h]

    for h in heads:
        o = o_intra[h] + jnp.concatenate(o_inter[h], axis=0)
        r = o * lax.rsqrt(jnp.mean(o * o, axis=-1, keepdims=True) + RMS_EPS) * nw
        o_ref[:, h * DV_B:(h + 1) * DV_B] = r * _silu(g_ref[:, h * DV_B:(h + 1) * DV_B])


def _gla(p, row0, batch, seq, wgk, bgk, norm_w, s0):
    cs = math.gcd(seq, GLA_CHUNK)
    carry = seq > cs
    if carry:
        tb = math.gcd(seq, GLA_TB)
        grid = (batch, seq // tb)
        n_state = 1
    else:
        tb = math.gcd(batch * seq, GLA_TB)
        n_state = tb // cs
        grid = (batch // n_state, 1)
    n_steps = grid[1]
    blk0 = row0 // tb
    rowblk = lambda b, t: blk0 + b * n_steps + t
    kern = functools.partial(_gla_kernel, tb=tb, cs=cs, carry=carry, n_steps=n_steps)
    return pl.pallas_call(
        kern,
        out_shape=(jax.ShapeDtypeStruct((batch * seq, W_B), F32),
                   jax.ShapeDtypeStruct((batch, N_HEADS, DK_B, DV_B), F32)),
        grid=grid,
        in_specs=[
            pl.BlockSpec((tb, 2 * N_HEADS * DK_B), lambda b, t: (rowblk(b, t), OFF_QK_B // (2 * N_HEADS * DK_B))),
            pl.BlockSpec((tb, W_B), lambda b, t: (rowblk(b, t), OFF_V_B // W_B)),
            pl.BlockSpec((tb, W_B), lambda b, t: (rowblk(b, t), OFF_G_B // W_B)),
            pl.BlockSpec((tb, LANES), lambda b, t: (rowblk(b, t), OFF_SMALL // LANES)),
            pl.BlockSpec((LANES, N_HEADS * DK_B), lambda b, t: (0, 0)),
            pl.BlockSpec((1, N_HEADS * DK_B), lambda b, t: (0, 0)),
            pl.BlockSpec((1, DV_B), lambda b, t: (0, 0)),
            pl.BlockSpec((n_state, N_HEADS, DK_B, DV_B), lambda b, t: (b, 0, 0, 0)),
        ],
        out_specs=(
            pl.BlockSpec((tb, W_B), lambda b, t: (b * n_steps + t, 0)),
            pl.BlockSpec((n_state, N_HEADS, DK_B, DV_B), lambda b, t: (b, 0, 0, 0)),
        ),
        scratch_shapes=[pltpu.VMEM((1, N_HEADS, DK_B, DV_B), F32)],
        compiler_params=pltpu.CompilerParams(
            dimension_semantics=("arbitrary", "arbitrary"), vmem_limit_bytes=VMEM_LIMIT),
        name=f"gla_c{cs}",
    )(p, p, p, p, wgk, bgk.reshape(1, N_HEADS * DK_B), norm_w.reshape(1, DV_B), s0)


def _topk_softmax(logits):
    tm = logits.shape[0]
    lane = lax.broadcasted_iota(jnp.int32, (tm, LANES), 1)
    lane_f = lane.astype(F32)
    neg = jnp.float32(-jnp.inf)
    cur = jnp.where(lane < N_EXPERTS, logits, neg)
    vals, idxs = [], []
    for _ in range(TOP_K):
        m = jnp.max(cur, axis=-1, keepdims=True)
        idx = jnp.min(jnp.where(cur == m, lane_f, float(LANES)), axis=-1, keepdims=True)
        vals.append(m)
        idxs.append(idx)
        cur = jnp.where(lane_f == idx, neg, cur)
    exps = [jnp.exp(v - vals[0]) for v in vals]
    den = exps[0] + exps[1] + exps[2] + exps[3]
    te = jnp.zeros((tm, LANES), F32)
    tg = jnp.zeros((tm, LANES), F32)
    for k in range(TOP_K):
        te = jnp.where(lane == k, idxs[k], te)
        tg = jnp.where(lane == k, exps[k] / den, tg)
    return te.astype(jnp.int32), tg


def _merge_kernel(xp_ref, xs_ref, oap_ref, oas_ref, obp_ref, obs_ref, ga_ref, gb_ref, wua_ref, wub_ref, wo_ref,
                  n2_ref, wr_ref, br_ref, x1_ref, h2_ref, te_ref, tg_ref, *, n_first):
    i = pl.program_id(0)
    tm = x1_ref.shape[0]
    hm = tm // MERGE_SPLIT
    groups = [slice(a * hm, (a + 1) * hm) for a in range(MERGE_SPLIT)]
    oa = _pick(i, n_first, oap_ref, oas_ref).astype(BF16)
    ob = _pick(i, n_first, obp_ref, obs_ref).astype(BF16)
    x = _pick(i, n_first, xp_ref, xs_ref)
    ua = [jnp.dot(oa[g], wua_ref[...], preferred_element_type=F32) for g in groups]
    ub = [jnp.dot(ob[g], wub_ref[...], preferred_element_type=F32) for g in groups]
    merged = [jax.nn.sigmoid(ga_ref[g, :]) * ua[a] + jax.nn.sigmoid(gb_ref[g, :]) * ub[a]
              for a, g in enumerate(groups)]
    x1 = [x[g] + jnp.dot(merged[a].astype(BF16), wo_ref[...], preferred_element_type=F32)
          for a, g in enumerate(groups)]
    h2 = [v * lax.rsqrt(jnp.mean(v * v, axis=-1, keepdims=True) + RMS_EPS) * n2_ref[...] for v in x1]
    h_hi = [v.astype(BF16) for v in h2]
    h_lo = [(v - hi.astype(F32)).astype(BF16) for v, hi in zip(h2, h_hi)]
    l_hi = [jnp.dot(hi, wr_ref[...], preferred_element_type=F32) for hi in h_hi]
    l_lo = [jnp.dot(lo, wr_ref[:, :LANES], preferred_element_type=F32) for lo in h_lo]
    for a, g in enumerate(groups):
        x1_ref[g, :] = x1[a]
        _store_packed_rows(h2_ref, (), a * hm, _pack_bf16_pairs(h2[a]))
        logits = l_hi[a][:, :LANES] + l_hi[a][:, LANES:] + l_lo[a] + br_ref[...]
        te, tg = _topk_softmax(logits)
        te_ref[g, :] = te
        tg_ref[g, :] = tg


def _merge(x_p, x_s, oa_p, oa_s, ob_p, ob_s, p, wua, wub, wo, norm2_w, wr, br):
    n = x_p.shape[0] + x_s.shape[0]
    tm = MERGE_TM
    n_first = x_p.shape[0] // tm
    row = lambda i: (i, 0)
    const = lambda i: (0, 0)
    return pl.pallas_call(
        functools.partial(_merge_kernel, n_first=n_first),
        out_shape=(jax.ShapeDtypeStruct((n, D_MODEL), F32), jax.ShapeDtypeStruct((n * PACK_ROWS, LANES), jnp.uint32),
                   jax.ShapeDtypeStruct((n, LANES), jnp.int32), jax.ShapeDtypeStruct((n, LANES), F32)),
        grid=(n // tm,),
        in_specs=[
            *_split_specs((tm, D_MODEL), n_first),
            *_split_specs((tm, W_A), n_first),
            *_split_specs((tm, W_B), n_first),
            pl.BlockSpec((tm, D_MODEL), lambda i: (i, OFF_GATE_A // D_MODEL)),
            pl.BlockSpec((tm, D_MODEL), lambda i: (i, OFF_GATE_B // D_MODEL)),
            pl.BlockSpec((W_A, D_MODEL), const),
            pl.BlockSpec((W_B, D_MODEL), const),
            pl.BlockSpec((D_MODEL, D_MODEL), const),
            pl.BlockSpec((1, D_MODEL), const),
            pl.BlockSpec((D_MODEL, 2 * LANES), const),
            pl.BlockSpec((1, LANES), const),
        ],
        out_specs=(pl.BlockSpec((tm, D_MODEL), row), pl.BlockSpec((tm * PACK_ROWS, LANES), row),
                   pl.BlockSpec((tm, LANES), row), pl.BlockSpec((tm, LANES), row)),
        compiler_params=pltpu.CompilerParams(
            dimension_semantics=("arbitrary",), vmem_limit_bytes=VMEM_LIMIT),
        name="merge_router",
    )(x_p, x_s, oa_p, oa_s, ob_p, ob_s, p, p, wua, wub, wo, norm2_w.reshape(1, D_MODEL), wr, br)


def _moe_kernel(blk_e_ref, n_used_ref, src_ref, dst_ref, h2_ref, wgu_ref, bgu_ref,
                wdn_ref, bdn_ref, y_hbm, xstage, ybuf, ssem):
    del blk_e_ref
    i = pl.program_id(0)
    n_used = n_used_ref[0]
    bm = MOE_BM
    pr = PACK_ROWS

    def scatter_copy(row, s, r):
        return pltpu.make_async_copy(ybuf.at[s, pl.ds(r * pr, pr), :],
                                     y_hbm.at[pl.ds(pl.multiple_of(row * pr, pr), pr), :], ssem.at[s])

    @pl.when(i == 0)
    def _():
        ybuf[...] = jnp.zeros_like(ybuf)
        n_real = y_hbm.shape[0] - MOE_SUB * bm * pr
        for s in range(MOE_SUB):
            sink = pltpu.make_async_copy(ybuf.at[s], y_hbm.at[pl.ds(n_real + s * bm * pr, bm * pr), :], ssem.at[s])
            sink.start()
            sink.wait()

    @pl.when(i < n_used)
    def _():
        for s in range(MOE_SUB):
            @pl.when(i >= 1)
            def _():
                for r in range(bm):
                    scatter_copy(0, s, r).wait()

            for r in range(bm):
                tok = pl.multiple_of(src_ref[0, 0, s * bm + r] * pr, pr)
                xstage[s, pl.ds(r * pr, pr), :] = h2_ref[pl.ds(tok, pr), :]
            x = _load_packed_rows(xstage, (s,), bm, BF16)
            gu = jnp.dot(x, wgu_ref[...].astype(BF16), preferred_element_type=F32) + bgu_ref[...]
            gate = jnp.minimum(gu[:, :D_FF], SWIGLU_LIMIT)
            up = jnp.clip(gu[:, D_FF:], -SWIGLU_LIMIT, SWIGLU_LIMIT)
            hmid = (up + 1.0) * (gate * jax.nn.sigmoid(SWIGLU_ALPHA * gate))
            y = jnp.dot(hmid.astype(BF16), wdn_ref[...].astype(BF16), preferred_element_type=F32) + bdn_ref[...]
            _store_packed_rows(ybuf, (s,), 0, _pack_bf16_pairs(y))
            for r in range(bm):
                scatter_copy(dst_ref[0, 0, s * bm + r], s, r).start(priority=r % 2)

    @pl.when(i == n_used - 1)
    def _():
        for s in range(MOE_SUB):
            for r in range(bm):
                scatter_copy(0, s, r).wait()


def _moe(blk_e, n_used, src_tok, dst_row, h2, wgu, bgu, wdn, bdn, y_rows):
    n_steps = src_tok.shape[0]
    step_rows = MOE_SUB * MOE_BM
    cur = lambda i, be, nu: (i, 0, 0)
    wsel = lambda i, be, nu: (be[i], 0, 0)
    smem_blk = lambda im: pl.BlockSpec((1, 1, step_rows), im, memory_space=pltpu.SMEM)
    grid_spec = pltpu.PrefetchScalarGridSpec(
        num_scalar_prefetch=2,
        grid=(n_steps,),
        in_specs=[
            smem_blk(cur), smem_blk(cur),
            pl.BlockSpec(h2.shape, lambda i, be, nu: (0, 0), pipeline_mode=pl.Buffered(1)),
            pl.BlockSpec((None, D_MODEL, 2 * D_FF), wsel, pipeline_mode=pl.Buffered(1)),
            pl.BlockSpec((None, 1, 2 * D_FF), wsel),
            pl.BlockSpec((None, D_FF, D_MODEL), wsel, pipeline_mode=pl.Buffered(1)),
            pl.BlockSpec((None, 1, D_MODEL), wsel),
        ],
        out_specs=pl.BlockSpec(memory_space=pl.ANY),
        scratch_shapes=[pltpu.VMEM((MOE_SUB, MOE_BM * PACK_ROWS, LANES), jnp.uint32),
                        pltpu.VMEM((MOE_SUB, MOE_BM * PACK_ROWS, LANES), jnp.uint32),
                        pltpu.SemaphoreType.DMA((MOE_SUB,))],
    )
    return pl.pallas_call(
        _moe_kernel,
        out_shape=jax.ShapeDtypeStruct((y_rows * PACK_ROWS, LANES), jnp.uint32),
        grid_spec=grid_spec,
        compiler_params=pltpu.CompilerParams(
            dimension_semantics=("arbitrary",), vmem_limit_bytes=MOE_VMEM_LIMIT),
        name="moe_experts",
    )(blk_e, n_used, src_tok, dst_row, h2, wgu, bgu, wdn, bdn)


def _route(top_e, n_tok):
    bm = MOE_SUB * MOE_BM
    n_assign = n_tok * TOP_K
    n_blocks = n_assign // bm + N_EXPERTS
    n_dummy = N_EXPERTS * bm
    flat_e = top_e.reshape(-1)
    counts = jnp.sum((flat_e[:, None] == jnp.arange(N_EXPERTS, dtype=jnp.int32)[None, :]).astype(jnp.int32), axis=0)
    padded = (counts + bm - 1) // bm * bm
    pad_end = jnp.cumsum(padded)
    d_e = jnp.arange(n_dummy, dtype=jnp.int32) // bm
    d_active = (jnp.arange(n_dummy, dtype=jnp.int32) % bm) < (padded - counts)[d_e]
    keys = jnp.concatenate([flat_e * 2, jnp.where(d_active, d_e * 2 + 1, 2 * N_EXPERTS)])
    vals = jnp.concatenate([jnp.arange(n_assign, dtype=jnp.int32), jnp.full((n_dummy,), -1, jnp.int32)])
    _, assign = lax.sort((keys, vals), num_keys=1, is_stable=True)
    assign = assign[:n_blocks * bm]
    rows = jnp.arange(n_blocks * bm, dtype=jnp.int32)
    valid = assign >= 0
    src_tok = jnp.where(valid, assign // TOP_K, 0)
    dst_row = jnp.where(valid, (assign % TOP_K) * n_tok + assign // TOP_K, n_assign + rows % bm)
    blk_start = jnp.arange(n_blocks, dtype=jnp.int32) * bm
    blk_e = jnp.minimum(jnp.sum((pad_end[None, :] <= blk_start[:, None]).astype(jnp.int32), axis=1),
                        N_EXPERTS - 1).astype(jnp.int32)
    n_used = (pad_end[-1] // bm).astype(jnp.int32).reshape(1)
    return (blk_e, n_used, src_tok.reshape(n_blocks, 1, bm).astype(jnp.int32),
            dst_row.reshape(n_blocks, 1, bm).astype(jnp.int32))


def _combine_kernel(x1_ref, y0_ref, y1_ref, y2_ref, y3_ref, tg_ref, nf_ref, op_ref, os_ref, *, n_first):
    tg = tg_ref[...]
    tm = x1_ref.shape[0]
    rows = lambda y_ref: _load_packed_rows(y_ref, (), tm, F32)
    acc = rows(y0_ref) * tg[:, 0:1]
    for k, y_ref in enumerate((y1_ref, y2_ref, y3_ref), start=1):
        acc = acc + rows(y_ref) * tg[:, k:k + 1]
    x = x1_ref[...] + acc
    out = x * lax.rsqrt(jnp.mean(x * x, axis=-1, keepdims=True) + RMS_EPS) * nf_ref[...]
    i = pl.program_id(0)

    @pl.when(i < n_first)
    def _():
        op_ref[...] = out

    @pl.when(i >= n_first)
    def _():
        os_ref[...] = out


def _combine(x1, y, tg, norm_f_w, n_p):
    n = x1.shape[0]
    tm = COMBINE_TM
    n_first = n_p // tm
    row = lambda i: (i, 0)
    y_spec = lambda k: pl.BlockSpec((tm * PACK_ROWS, LANES), lambda i: (k * (n // tm) + i, 0))
    return pl.pallas_call(
        functools.partial(_combine_kernel, n_first=n_first),
        out_shape=(jax.ShapeDtypeStruct((n_p, D_MODEL), F32), jax.ShapeDtypeStruct((n - n_p, D_MODEL), F32)),
        grid=(n // tm,),
        in_specs=[pl.BlockSpec((tm, D_MODEL), row), y_spec(0), y_spec(1), y_spec(2), y_spec(3),
                  pl.BlockSpec((tm, LANES), row), pl.BlockSpec((1, D_MODEL), lambda i: (0, 0))],
        out_specs=_split_specs((tm, D_MODEL), n_first),
        compiler_params=pltpu.CompilerParams(
            dimension_semantics=("arbitrary",), vmem_limit_bytes=VMEM_LIMIT),
        name="combine_norm",
    )(x1, y, y, y, y, tg, norm_f_w.reshape(1, D_MODEL))


def _pack_w_in(w_in):
    s = np.cumsum((0,) + IN_SIZES)
    qkv, z, al, be, qb, kb, vb, gb, gk, ga, gbt = [w_in[:, s[i]:s[i + 1]] for i in range(len(IN_SIZES))]
    small = jnp.concatenate(
        [al, be, gk, jnp.zeros((D_MODEL, LANES - 2 * N_HEADS - GK_RANK), w_in.dtype)], axis=1)
    return jnp.concatenate([qkv, z, ga, gbt, qb, kb, vb, gb, small], axis=1).astype(BF16)


def kernel(x_prompt, x_sample, state_gdn, state_gdn_conv, state_gla, norm1_w, w_in, conv_w, a_log,
           dt_bias, gdn_norm_w, gla_gk_up, gla_gk_b, gla_norm_w, w_up_a, w_up_b, w_o, norm2_w,
           w_router, b_router, w_gate_up, b_gate_up, w_down, b_down, norm_f_w):
    bp, tp, _ = x_prompt.shape
    bs, ts, _ = x_sample.shape
    n_p, n_s = bp * tp, bs * ts
    n = n_p + n_s
    l = 0

    x_p = x_prompt.reshape(n_p, D_MODEL)
    x_s = x_sample.reshape(n_s, D_MODEL)
    p = _inproj(x_p, x_s, norm1_w[l], _pack_w_in(w_in[l]))

    hp = jnp.zeros((8, LANES), F32).at[0, :N_HEADS].set(a_log[l]).at[1, :N_HEADS].set(dt_bias[l])
    wgk = jnp.zeros((LANES, N_HEADS * DK_B), F32).at[2 * N_HEADS:2 * N_HEADS + GK_RANK].set(gla_gk_up[l]).astype(BF16)
    zeros = lambda shape: jnp.zeros(shape, F32)

    oa_p, gdn_p, conv_p = _gdn(p, 0, bp, tp, conv_w[l], hp, gdn_norm_w[l],
                               zeros((bp, CONV_W - 1, CONV_DIM)), zeros((bp, N_HEADS, DK_A, DV_A)))
    oa_s, gdn_s, conv_s = _gdn(p, n_p, bs, ts, conv_w[l], hp, gdn_norm_w[l], state_gdn_conv[l], state_gdn[l])
    ob_p, gla_p = _gla(p, 0, bp, tp, wgk, gla_gk_b[l], gla_norm_w[l], zeros((bp, N_HEADS, DK_B, DV_B)))
    ob_s, gla_s = _gla(p, n_p, bs, ts, wgk, gla_gk_b[l], gla_norm_w[l], state_gla[l])

    wr = jnp.zeros((D_MODEL, LANES), F32).at[:, :N_EXPERTS].set(w_router[l])
    wr_hi = wr.astype(BF16)
    wr = jnp.concatenate([wr_hi, (wr - wr_hi.astype(F32)).astype(BF16)], axis=1)
    br = jnp.zeros((1, LANES), F32).at[0, :N_EXPERTS].set(b_router[l])
    x1, h2, te, tg = _merge(x_p, x_s, oa_p, oa_s, ob_p, ob_s, p, w_up_a[l].astype(BF16),
                            w_up_b[l].astype(BF16), w_o[l].astype(BF16), norm2_w[l], wr, br)

    blk_e, n_used, src_tok, dst_row = _route(te[:, :TOP_K], n)
    y = _moe(blk_e, n_used, src_tok, dst_row, h2, w_gate_up[l],
             b_gate_up[l].reshape(N_EXPERTS, 1, 2 * D_FF), w_down[l],
             b_down[l].reshape(N_EXPERTS, 1, D_MODEL), n * TOP_K + MOE_SUB * MOE_BM)
    out_p, out_s = _combine(x1, y, tg, norm_f_w, n_p)
    y_prompt = out_p.reshape(bp, tp, D_MODEL)
    y_sample = out_s.reshape(bs, ts, D_MODEL)
    return (y_prompt, y_sample, gdn_p[None], conv_p[None], gla_p[None], gdn_s[None], conv_s[None],
            gla_s[None])
```

```python
import functools
import math

import jax
import jax.numpy as jnp
import numpy as np
from jax import lax
from jax.experimental import pallas as pl
from jax.experimental.pallas import tpu as pltpu

F32 = jnp.float32
BF16 = jnp.bfloat16

D_MODEL = 1024
N_HEADS = 4
DK_A = 128
DV_A = 128
W_A = N_HEADS * DV_A
CONV_W = 4
CONV_DIM = 2 * N_HEADS * DK_A + W_A
GDN_CHUNK = 64
DK_B = 64
DV_B = 128
W_B = N_HEADS * DV_B
GK_RANK = 16
GLA_GATE_NORM = 16.0
GLA_CHUNK = 16
N_EXPERTS = 32
TOP_K = 4
D_FF = 1024
SWIGLU_LIMIT = 7.0
SWIGLU_ALPHA = 1.702
RMS_EPS = 1e-6
IN_SIZES = (CONV_DIM, W_A, N_HEADS, N_HEADS, N_HEADS * DK_B, N_HEADS * DK_B, W_B, W_B, GK_RANK,
            D_MODEL, D_MODEL)

LANES = 128

OFF_QKV = 0
OFF_Z = 1536
OFF_GATE_A = 2048
OFF_GATE_B = 3072
OFF_QK_B = 4096
OFF_V_B = 4608
OFF_G_B = 5120
OFF_SMALL = 5632
P_COLS = 5760

INPROJ_TM = 512
GLA_TB = 256
GDN_SEGS_CARRY = 4
GDN_SEGS_INDEP = 8
MERGE_TM = 512
MERGE_SPLIT = 2
MOE_BM = 256
MOE_SUB = 2
PACK_ROWS = D_MODEL // 2 // LANES
MOE_VMEM_LIMIT = 60 * 1024 * 1024
COMBINE_TM = 256
VMEM_LIMIT = 56 * 1024 * 1024


def _dot(a, b):
    return jnp.dot(a.astype(BF16), b.astype(BF16), preferred_element_type=F32)


def _dot_nt(a, b):
    return lax.dot_general(a.astype(BF16), b.astype(BF16), (((1,), (1,)), ((), ())),
                           preferred_element_type=F32)


def _dot_tn(a, b):
    return lax.dot_general(a, b, (((0,), (0,)), ((), ())), preferred_element_type=F32)


def _split3(x):
    hi = x.astype(BF16)
    r1 = x - hi.astype(F32)
    mid = r1.astype(BF16)
    lo = (r1 - mid.astype(F32)).astype(BF16)
    return hi, mid, lo


def _dot01(m01, x):
    hi, mid, lo = _split3(x)
    d = lambda p: jnp.dot(m01, p, preferred_element_type=F32)
    return d(hi) + d(mid) + d(lo)


def _silu(x):
    return x * jax.nn.sigmoid(x)


def _pack_bf16_pairs(x):
    bits = pltpu.bitcast(x.astype(BF16).astype(F32), jnp.uint32)
    half = D_MODEL // 2
    return (bits[:, :half] >> 16) | (bits[:, half:] & jnp.uint32(0xFFFF0000))


def _store_packed_rows(ref, lead, row0, words):
    n = words.shape[0]
    for c in range(PACK_ROWS):
        ref[(*lead, pl.ds(row0 * PACK_ROWS + c, n, stride=PACK_ROWS), slice(None))] = words[:, c * LANES:(c + 1) * LANES]


def _load_packed_rows(ref, lead, n, dtype):
    words = [ref[(*lead, pl.ds(c, n, stride=PACK_ROWS), slice(None))] for c in range(PACK_ROWS)]
    lo = [pltpu.bitcast(w << 16, F32).astype(dtype) for w in words]
    hi = [pltpu.bitcast(w & jnp.uint32(0xFFFF0000), F32).astype(dtype) for w in words]
    return jnp.concatenate(lo + hi, axis=1)


def _pick(i, n_first, first_ref, second_ref):
    return jnp.where(i < n_first, first_ref[...], second_ref[...])


def _split_specs(block, n_first):
    return (pl.BlockSpec(block, lambda i: (jnp.minimum(i, n_first - 1), 0)),
            pl.BlockSpec(block, lambda i: (jnp.maximum(i - n_first, 0), 0)))


def _inproj_kernel(xp_ref, xs_ref, nw_ref, w_ref, o_ref, *, n_first):
    x = _pick(pl.program_id(0), n_first, xp_ref, xs_ref)
    ms = jnp.mean(x * x, axis=-1, keepdims=True)
    h = (x * lax.rsqrt(ms + RMS_EPS) * nw_ref[...]).astype(BF16)
    o_ref[...] = jnp.dot(h, w_ref[...], preferred_element_type=F32)


def _inproj(x_p, x_s, norm_w, w_packed):
    tm = INPROJ_TM
    n_first = x_p.shape[0] // tm
    n = x_p.shape[0] + x_s.shape[0]
    return pl.pallas_call(
        functools.partial(_inproj_kernel, n_first=n_first),
        out_shape=jax.ShapeDtypeStruct((n, P_COLS), F32),
        grid=(n // tm,),
        in_specs=[
            *_split_specs((tm, D_MODEL), n_first),
            pl.BlockSpec((1, D_MODEL), lambda i: (0, 0)),
            pl.BlockSpec((D_MODEL, P_COLS), lambda i: (0, 0), pipeline_mode=pl.Buffered(1)),
        ],
        out_specs=pl.BlockSpec((tm, P_COLS), lambda i: (i, 0)),
        compiler_params=pltpu.CompilerParams(
            dimension_semantics=("arbitrary",), vmem_limit_bytes=VMEM_LIMIT),
        name="inproj",
    )(x_p, x_s, norm_w.reshape(1, D_MODEL), w_packed)


def _gdn_kernel(qkv_ref, z_ref, sm_ref, cw_ref, hp_ref, nw_ref, sconv_ref, s0_ref,
                o_ref, sout_ref, cout_ref, s_ref, ubuf_ref, *, chunk, n_seg, carry, n_steps):
    c_len = chunk
    rows = n_seg * c_len
    t = pl.program_id(1)
    hist = CONV_W - 1
    base = 8
    conv_len = rows if carry else c_len
    n_conv = 1 if carry else n_seg

    def init_history():
        ubuf_ref[:, 0:base, :] = jnp.zeros((n_conv, base, CONV_DIM), F32)
        ubuf_ref[:, base - hist:base, :] = sconv_ref[...]

    if carry:
        @pl.when(t == 0)
        def _():
            s_ref[...] = s0_ref[...]
            init_history()

        @pl.when(t > 0)
        def _():
            ubuf_ref[:, base - hist:base, :] = ubuf_ref[:, base + conv_len - hist:base + conv_len, :]
    else:
        init_history()
    ubuf_ref[:, base:base + conv_len, :] = qkv_ref[...].reshape(n_conv, conv_len, CONV_DIM)

    cw = cw_ref[...]
    u = ubuf_ref[...]
    u1 = pltpu.roll(u, 1, axis=1)
    v = u1 * cw[0:1, :] + u * cw[1:2, :]
    y = (pltpu.roll(v, 2, axis=1) + u1 * cw[2:3, :]) + u * cw[3:4, :]
    y = _silu(y[:, base:, :]).reshape(rows, CONV_DIM)

    sm = sm_ref[...]
    hp = hp_ref[...]
    lane = lax.broadcasted_iota(jnp.int32, (rows, LANES), 1)
    g_all = -jnp.exp(hp[0:1, :]) * jax.nn.softplus(sm + hp[1:2, :])
    g_all = jnp.where(lane < N_HEADS, g_all, 0.0)
    beta_all = jax.nn.sigmoid(sm)

    sh = int(round(math.log2(c_len)))
    r_b = lax.broadcasted_iota(jnp.int32, (rows, rows), 0)
    c_b = lax.broadcasted_iota(jnp.int32, (rows, rows), 1)
    seg_tril = ((r_b >> sh) == (c_b >> sh)) & (c_b <= r_b)
    gc = _dot01(jnp.where(seg_tril, 1.0, 0.0).astype(BF16), g_all)

    r_i = lax.broadcasted_iota(jnp.int32, (c_len, c_len), 0)
    c_i = lax.broadcasted_iota(jnp.int32, (c_len, c_len), 1)
    tri = c_i <= r_i
    strict = c_i < r_i
    eye = c_i == r_i
    nw = nw_ref[...]
    levels = sh

    qn, kn, kb, rhs, qg, gcol = [], [], [], [], [], []
    for h in range(N_HEADS):
        qh = y[:, h * DK_A:(h + 1) * DK_A]
        kh = y[:, W_A + h * DK_A:W_A + (h + 1) * DK_A]
        vh = y[:, 2 * W_A + h * DV_A:2 * W_A + (h + 1) * DV_A]
        qn_h = qh * lax.rsqrt(jnp.sum(qh * qh, axis=-1, keepdims=True) + 1e-6) * (DK_A ** -0.5)
        kn_h = kh * lax.rsqrt(jnp.sum(kh * kh, axis=-1, keepdims=True) + 1e-6)
        gcol_h = gc[:, h:h + 1]
        eg = jnp.exp(gcol_h)
        bcol = beta_all[:, N_HEADS + h:N_HEADS + h + 1]
        kb_h = kn_h * bcol
        qn.append(qn_h)
        kn.append(kn_h)
        kb.append(kb_h)
        rhs.append(jnp.concatenate([vh * bcol, kb_h * eg], axis=1))
        qg.append(qn_h * eg)
        gcol.append(gcol_h)

    pairs = [(g, h) for g in range(n_seg) for h in range(N_HEADS)]
    sl = lambda g: slice(g * c_len, (g + 1) * c_len)

    decay, kd, gl = {}, {}, {}
    for (g, h) in pairs:
        gcol_p = gcol[h][sl(g)]
        gcb = jnp.broadcast_to(gcol_p, (c_len, c_len))
        grow = jnp.sum(jnp.where(eye, gcb, 0.0), axis=0, keepdims=True)
        decay[g, h] = jnp.where(tri, jnp.exp(jnp.where(tri, gcb - grow, 0.0)), 0.0)
        glast = gcol[h][(g + 1) * c_len - 1:(g + 1) * c_len]
        kd[g, h] = kn[h][sl(g)] * jnp.exp(glast - gcol_p)
        gl[g, h] = jnp.exp(glast)

    kq = {p: _dot_nt(jnp.concatenate([kb[p[1]][sl(p[0])], qn[p[1]][sl(p[0])]], axis=0), kn[p[1]][sl(p[0])])
          for p in pairs}
    low = {p: jnp.where(strict, kq[p][:c_len] * decay[p], 0.0) for p in pairs}
    attn = {p: jnp.where(tri, kq[p][c_len:] * decay[p], 0.0) for p in pairs}
    ymat = {p: -low[p] for p in pairs}
    lp = low
    for _ in range(levels - 1):
        lp = {p: _dot(lp[p], lp[p]) for p in pairs}
        prod = {p: _dot(ymat[p], lp[p]) for p in pairs}
        ymat = {p: ymat[p] + lp[p] + prod[p] for p in pairs}
    sol = {}
    for (g, h) in pairs:
        rhs_p = rhs[h][sl(g)]
        sol[g, h] = rhs_p + _dot(ymat[g, h], rhs_p)

    def finish(g, h, o):
        r = o * lax.rsqrt(jnp.mean(o * o, axis=-1, keepdims=True) + RMS_EPS) * nw
        o_ref[sl(g), h * DV_A:(h + 1) * DV_A] = r * _silu(z_ref[sl(g), h * DV_A:(h + 1) * DV_A])

    def advance(group, state):
        ws = {p: _dot(jnp.concatenate([sol[p][:, DV_A:], qg[p[1]][sl(p[0])]], axis=0), state[p]) for p in group}
        v_new = {p: sol[p][:, :DV_A] - ws[p][:c_len] for p in group}
        o = {p: ws[p][c_len:] + _dot(attn[p], v_new[p]) for p in group}
        new = {p: state[p] * gl[p] + _dot_tn(kd[p], v_new[p]) for p in group}
        for p in group:
            finish(p[0], p[1], o[p])
        return new

    if carry:
        cur = {h: s_ref[0, h] for h in range(N_HEADS)}
        for g in range(n_seg):
            group = [(g, h) for h in range(N_HEADS)]
            new = advance(group, {(g, h): cur[h] for h in range(N_HEADS)})
            cur = {h: new[g, h] for h in range(N_HEADS)}
        for h in range(N_HEADS):
            s_ref[0, h] = cur[h]

        @pl.when(t == n_steps - 1)
        def _():
            for h in range(N_HEADS):
                sout_ref[0, h] = cur[h]
            cout_ref[...] = ubuf_ref[:, base + conv_len - hist:base + conv_len, :]
    else:
        new = advance(pairs, {p: s0_ref[p[0], p[1]] for p in pairs})
        for p in pairs:
            sout_ref[p[0], p[1]] = new[p]
        cout_ref[...] = ubuf_ref[:, base + conv_len - hist:base + conv_len, :]


def _gdn(p, row0, batch, seq, conv_w, hp, norm_w, s_conv, s0):
    chunk = math.gcd(seq, GDN_CHUNK)
    n_chunks = seq // chunk
    carry = n_chunks > 1
    if carry:
        n_seg = math.gcd(n_chunks, GDN_SEGS_CARRY)
        grid = (batch, n_chunks // n_seg)
        n_conv, conv_len = 1, n_seg * chunk
    else:
        n_seg = math.gcd(batch, GDN_SEGS_INDEP)
        grid = (batch // n_seg, 1)
        n_conv, conv_len = n_seg, chunk
    rows = n_seg * chunk
    n_steps = grid[1]
    blk0 = row0 // rows
    rowblk = lambda b, t: blk0 + b * n_steps + t
    kern = functools.partial(_gdn_kernel, chunk=chunk, n_seg=n_seg, carry=carry, n_steps=n_steps)
    return pl.pallas_call(
        kern,
        out_shape=(jax.ShapeDtypeStruct((batch * seq, W_A), F32),
                   jax.ShapeDtypeStruct((batch, N_HEADS, DK_A, DV_A), F32),
                   jax.ShapeDtypeStruct((batch, CONV_W - 1, CONV_DIM), F32)),
        grid=grid,
        in_specs=[
            pl.BlockSpec((rows, CONV_DIM), lambda b, t: (rowblk(b, t), OFF_QKV // CONV_DIM)),
            pl.BlockSpec((rows, W_A), lambda b, t: (rowblk(b, t), OFF_Z // W_A)),
            pl.BlockSpec((rows, LANES), lambda b, t: (rowblk(b, t), OFF_SMALL // LANES)),
            pl.BlockSpec((CONV_W, CONV_DIM), lambda b, t: (0, 0)),
            pl.BlockSpec((8, LANES), lambda b, t: (0, 0)),
            pl.BlockSpec((1, DV_A), lambda b, t: (0, 0)),
            pl.BlockSpec((n_conv, CONV_W - 1, CONV_DIM), lambda b, t: (b, 0, 0)),
            pl.BlockSpec((n_conv, N_HEADS, DK_A, DV_A), lambda b, t: (b, 0, 0, 0)),
        ],
        out_specs=(
            pl.BlockSpec((rows, W_A), lambda b, t: (b * n_steps + t, 0)),
            pl.BlockSpec((n_conv, N_HEADS, DK_A, DV_A), lambda b, t: (b, 0, 0, 0)),
            pl.BlockSpec((n_conv, CONV_W - 1, CONV_DIM), lambda b, t: (b, 0, 0)),
        ),
        scratch_shapes=[pltpu.VMEM((1, N_HEADS, DK_A, DV_A), F32),
                        pltpu.VMEM((n_conv, conv_len + 8, CONV_DIM), F32)],
        compiler_params=pltpu.CompilerParams(
            dimension_semantics=("arbitrary", "arbitrary"), vmem_limit_bytes=VMEM_LIMIT),
        name=f"gdn_c{chunk}",
    )(p, p, p, conv_w, hp, norm_w.reshape(1, DV_A), s_conv, s0)


def _gla_kernel(qk_ref, v_ref, g_ref, sm_ref, wgk_ref, bgk_ref, nw_ref, s0_ref,
                o_ref, sout_ref, s_ref, *, tb, cs, carry, n_steps):
    t = pl.program_id(1)
    n_sub = tb // cs
    w_kb = N_HEADS * DK_B

    if carry:
        @pl.when(t == 0)
        def _():
            s_ref[...] = s0_ref[...]

    gk = _dot(sm_ref[...], wgk_ref[...]) + bgk_ref[...]
    log_a = jax.nn.log_sigmoid(gk) / GLA_GATE_NORM
    sh = int(round(math.log2(cs)))
    r_i = lax.broadcasted_iota(jnp.int32, (tb, tb), 0)
    c_i = lax.broadcasted_iota(jnp.int32, (tb, tb), 1)
    same = (r_i >> sh) == (c_i >> sh)
    mask = same & (c_i <= r_i)
    gcum = _dot01(jnp.where(mask, 1.0, 0.0).astype(BF16), log_a)
    gtot = _dot01(jnp.where(same, 1.0, 0.0).astype(BF16), log_a)
    onehot = jnp.where((lax.broadcasted_iota(jnp.int32, (tb, n_sub), 0) >> sh)
                       == lax.broadcasted_iota(jnp.int32, (tb, n_sub), 1), 1.0, 0.0).astype(BF16)
    tn = lambda part: lax.dot_general(part, onehot, (((0,), (0,)), ((), ())), preferred_element_type=F32)
    hi, mid, lo = _split3(log_a)
    glcol = jnp.exp(tn(hi) + tn(mid) + tn(lo))

    qk = qk_ref[...]
    nw = nw_ref[...]
    heads = range(N_HEADS)
    sl = lambda j: slice(j * cs, (j + 1) * cs)
    qg, kd, vh, o_intra = [], [], [], []
    for h in heads:
        qh = qk[:, h * DK_B:(h + 1) * DK_B] * (DK_B ** -0.5)
        kh = qk[:, w_kb + h * DK_B:w_kb + (h + 1) * DK_B]
        gh = gcum[:, h * DK_B:(h + 1) * DK_B]
        gt = gtot[:, h * DK_B:(h + 1) * DK_B]
        qg.append(qh * jnp.exp(gh))
        kd.append(kh * jnp.exp(gt - gh))
        vh.append(v_ref[:, h * DV_B:(h + 1) * DV_B])
        attn = jnp.where(mask, _dot_nt(qg[h], kh * jnp.exp(-gh)), 0.0)
        o_intra.append(_dot(attn, vh[h]))
    upd = {(j, h): _dot_tn(kd[h][sl(j)], vh[h][sl(j)]) for j in range(n_sub) for h in heads}
    gl = lambda j, h: glcol[h * DK_B:(h + 1) * DK_B, j:j + 1]

    o_inter = [[] for _ in heads]
    if carry:
        cur = [s_ref[0, h] for h in heads]
        for j in range(n_sub):
            for h in heads:
                o_inter[h].append(_dot(qg[h][sl(j)], cur[h]))
                cur[h] = cur[h] * gl(j, h) + upd[j, h]
        for h in heads:
            s_ref[0, h] = cur[h]

        @pl.when(t == n_steps - 1)
        def _():
            for h in heads:
                sout_ref[0, h] = cur[h]
    else:
        for j in range(n_sub):
            for h in heads:
                s = s0_ref[j, h]
                o_inter[h].append(_dot(qg[h][sl(j)], s))
                sout_ref[j, h] = s * gl(j, h) + upd[j, h]

    for h in heads:
        o = o_intra[h] + jnp.concatenate(o_inter[h], axis=0)
        r = o * lax.rsqrt(jnp.mean(o * o, axis=-1, keepdims=True) + RMS_EPS) * nw
        o_ref[:, h * DV_B:(h + 1) * DV_B] = r * _silu(g_ref[:, h * DV_B:(h + 1) * DV_B])


def _gla(p, row0, batch, seq, wgk, bgk, norm_w, s0):
    cs = math.gcd(seq, GLA_CHUNK)
    carry = seq > cs
    if carry:
        tb = math.gcd(seq, GLA_TB)
        grid = (batch, seq // tb)
        n_state = 1
    else:
        tb = math.gcd(batch * seq, GLA_TB)
        n_state = tb // cs
        grid = (batch // n_state, 1)
    n_steps = grid[1]
    blk0 = row0 // tb
    rowblk = lambda b, t: blk0 + b * n_steps + t
    kern = functools.partial(_gla_kernel, tb=tb, cs=cs, carry=carry, n_steps=n_steps)
    return pl.pallas_call(
        kern,
        out_shape=(jax.ShapeDtypeStruct((batch * seq, W_B), F32),
                   jax.ShapeDtypeStruct((batch, N_HEADS, DK_B, DV_B), F32)),
        grid=grid,
        in_specs=[
            pl.BlockSpec((tb, 2 * N_HEADS * DK_B), lambda b, t: (rowblk(b, t), OFF_QK_B // (2 * N_HEADS * DK_B))),
            pl.BlockSpec((tb, W_B), lambda b, t: (rowblk(b, t), OFF_V_B // W_B)),
            pl.BlockSpec((tb, W_B), lambda b, t: (rowblk(b, t), OFF_G_B // W_B)),
            pl.BlockSpec((tb, LANES), lambda b, t: (rowblk(b, t), OFF_SMALL // LANES)),
            pl.BlockSpec((LANES, N_HEADS * DK_B), lambda b, t: (0, 0)),
            pl.BlockSpec((1, N_HEADS * DK_B), lambda b, t: (0, 0)),
            pl.BlockSpec((1, DV_B), lambda b, t: (0, 0)),
            pl.BlockSpec((n_state, N_HEADS, DK_B, DV_B), lambda b, t: (b, 0, 0, 0)),
        ],
        out_specs=(
            pl.BlockSpec((tb, W_B), lambda b, t: (b * n_steps + t, 0)),
            pl.BlockSpec((n_state, N_HEADS, DK_B, DV_B), lambda b, t: (b, 0, 0, 0)),
        ),
        scratch_shapes=[pltpu.VMEM((1, N_HEADS, DK_B, DV_B), F32)],
        compiler_params=pltpu.CompilerParams(
            dimension_semantics=("arbitrary", "arbitrary"), vmem_limit_bytes=VMEM_LIMIT),
        name=f"gla_c{cs}",
    )(p, p, p, p, wgk, bgk.reshape(1, N_HEADS * DK_B), norm_w.reshape(1, DV_B), s0)


def _topk_softmax(logits):
    tm = logits.shape[0]
    lane = lax.broadcasted_iota(jnp.int32, (tm, LANES), 1)
    lane_f = lane.astype(F32)
    neg = jnp.float32(-jnp.inf)
    cur = jnp.where(lane < N_EXPERTS, logits, neg)
    vals, idxs = [], []
    for _ in range(TOP_K):
        m = jnp.max(cur, axis=-1, keepdims=True)
        idx = jnp.min(jnp.where(cur == m, lane_f, float(LANES)), axis=-1, keepdims=True)
        vals.append(m)
        idxs.append(idx)
        cur = jnp.where(lane_f == idx, neg, cur)
    exps = [jnp.exp(v - vals[0]) for v in vals]
    den = exps[0] + exps[1] + exps[2] + exps[3]
    te = jnp.zeros((tm, LANES), F32)
    tg = jnp.zeros((tm, LANES), F32)
    for k in range(TOP_K):
        te = jnp.where(lane == k, idxs[k], te)
        tg = jnp.where(lane == k, exps[k] / den, tg)
    return te.astype(jnp.int32), tg


def _merge_kernel(xp_ref, xs_ref, oap_ref, oas_ref, obp_ref, obs_ref, ga_ref, gb_ref, wua_ref, wub_ref, wo_ref,
                  n2_ref, wr_ref, br_ref, x1_ref, h2_ref, te_ref, tg_ref, *, n_first):
    i = pl.program_id(0)
    tm = x1_ref.shape[0]
    hm = tm // MERGE_SPLIT
    groups = [slice(a * hm, (a + 1) * hm) for a in range(MERGE_SPLIT)]
    oa = _pick(i, n_first, oap_ref, oas_ref).astype(BF16)
    ob = _pick(i, n_first, obp_ref, obs_ref).astype(BF16)
    x = _pick(i, n_first, xp_ref, xs_ref)
    ua = [jnp.dot(oa[g], wua_ref[...], preferred_element_type=F32) for g in groups]
    ub = [jnp.dot(ob[g], wub_ref[...], preferred_element_type=F32) for g in groups]
    merged = [jax.nn.sigmoid(ga_ref[g, :]) * ua[a] + jax.nn.sigmoid(gb_ref[g, :]) * ub[a]
              for a, g in enumerate(groups)]
    x1 = [x[g] + jnp.dot(merged[a].astype(BF16), wo_ref[...], preferred_element_type=F32)
          for a, g in enumerate(groups)]
    h2 = [v * lax.rsqrt(jnp.mean(v * v, axis=-1, keepdims=True) + RMS_EPS) * n2_ref[...] for v in x1]
    h_hi = [v.astype(BF16) for v in h2]
    h_lo = [(v - hi.astype(F32)).astype(BF16) for v, hi in zip(h2, h_hi)]
    l_hi = [jnp.dot(hi, wr_ref[...], preferred_element_type=F32) for hi in h_hi]
    l_lo = [jnp.dot(lo, wr_ref[:, :LANES], preferred_element_type=F32) for lo in h_lo]
    for a, g in enumerate(groups):
        x1_ref[g, :] = x1[a]
        _store_packed_rows(h2_ref, (), a * hm, _pack_bf16_pairs(h2[a]))
        logits = l_hi[a][:, :LANES] + l_hi[a][:, LANES:] + l_lo[a] + br_ref[...]
        te, tg = _topk_softmax(logits)
        te_ref[g, :] = te
        tg_ref[g, :] = tg


def _merge(x_p, x_s, oa_p, oa_s, ob_p, ob_s, p, wua, wub, wo, norm2_w, wr, br):
    n = x_p.shape[0] + x_s.shape[0]
    tm = MERGE_TM
    n_first = x_p.shape[0] // tm
    row = lambda i: (i, 0)
    const = lambda i: (0, 0)
    return pl.pallas_call(
        functools.partial(_merge_kernel, n_first=n_first),
        out_shape=(jax.ShapeDtypeStruct((n, D_MODEL), F32), jax.ShapeDtypeStruct((n * PACK_ROWS, LANES), jnp.uint32),
                   jax.ShapeDtypeStruct((n, LANES), jnp.int32), jax.ShapeDtypeStruct((n, LANES), F32)),
        grid=(n // tm,),
        in_specs=[
            *_split_specs((tm, D_MODEL), n_first),
            *_split_specs((tm, W_A), n_first),
            *_split_specs((tm, W_B), n_first),
            pl.BlockSpec((tm, D_MODEL), lambda i: (i, OFF_GATE_A // D_MODEL)),
            pl.BlockSpec((tm, D_MODEL), lambda i: (i, OFF_GATE_B // D_MODEL)),
            pl.BlockSpec((W_A, D_MODEL), const),
            pl.BlockSpec((W_B, D_MODEL), const),
            pl.BlockSpec((D_MODEL, D_MODEL), const),
            pl.BlockSpec((1, D_MODEL), const),
            pl.BlockSpec((D_MODEL, 2 * LANES), const),
            pl.BlockSpec((1, LANES), const),
        ],
        out_specs=(pl.BlockSpec((tm, D_MODEL), row), pl.BlockSpec((tm * PACK_ROWS, LANES), row),
                   pl.BlockSpec((tm, LANES), row), pl.BlockSpec((tm, LANES), row)),
        compiler_params=pltpu.CompilerParams(
            dimension_semantics=("arbitrary",), vmem_limit_bytes=VMEM_LIMIT),
        name="merge_router",
    )(x_p, x_s, oa_p, oa_s, ob_p, ob_s, p, p, wua, wub, wo, norm2_w.reshape(1, D_MODEL), wr, br)


def _moe_kernel(blk_e_ref, n_used_ref, src_ref, dst_ref, h2_ref, wgu_ref, bgu_ref,
                wdn_ref, bdn_ref, y_hbm, xstage, ybuf, ssem):
    del blk_e_ref
    i = pl.program_id(0)
    n_used = n_used_ref[0]
    bm = MOE_BM
    pr = PACK_ROWS

    def scatter_copy(row, s, r):
        return pltpu.make_async_copy(ybuf.at[s, pl.ds(r * pr, pr), :],
                                     y_hbm.at[pl.ds(pl.multiple_of(row * pr, pr), pr), :], ssem.at[s])

    @pl.when(i == 0)
    def _():
        ybuf[...] = jnp.zeros_like(ybuf)
        n_real = y_hbm.shape[0] - MOE_SUB * bm * pr
        for s in range(MOE_SUB):
            sink = pltpu.make_async_copy(ybuf.at[s], y_hbm.at[pl.ds(n_real + s * bm * pr, bm * pr), :], ssem.at[s])
            sink.start()
            sink.wait()

    @pl.when(i < n_used)
    def _():
        for s in range(MOE_SUB):
            @pl.when(i >= 1)
            def _():
                for r in range(bm):
                    scatter_copy(0, s, r).wait()

            for r in range(bm):
                tok = pl.multiple_of(src_ref[0, 0, s * bm + r] * pr, pr)
                xstage[s, pl.ds(r * pr, pr), :] = h2_ref[pl.ds(tok, pr), :]
            x = _load_packed_rows(xstage, (s,), bm, BF16)
            gu = jnp.dot(x, wgu_ref[...].astype(BF16), preferred_element_type=F32) + bgu_ref[...]
            gate = jnp.minimum(gu[:, :D_FF], SWIGLU_LIMIT)
            up = jnp.clip(gu[:, D_FF:], -SWIGLU_LIMIT, SWIGLU_LIMIT)
            hmid = (up + 1.0) * (gate * jax.nn.sigmoid(SWIGLU_ALPHA * gate))
            y = jnp.dot(hmid.astype(BF16), wdn_ref[...].astype(BF16), preferred_element_type=F32) + bdn_ref[...]
            _store_packed_rows(ybuf, (s,), 0, _pack_bf16_pairs(y))
            for r in range(bm):
                scatter_copy(dst_ref[0, 0, s * bm + r], s, r).start(priority=r % 2)

    @pl.when(i == n_used - 1)
    def _():
        for s in range(MOE_SUB):
            for r in range(bm):
                scatter_copy(0, s, r).wait()


def _moe(blk_e, n_used, src_tok, dst_row, h2, wgu, bgu, wdn, bdn, y_rows):
    n_steps = src_tok.shape[0]
    step_rows = MOE_SUB * MOE_BM
    cur = lambda i, be, nu: (i, 0, 0)
    wsel = lambda i, be, nu: (be[i], 0, 0)
    smem_blk = lambda im: pl.BlockSpec((1, 1, step_rows), im, memory_space=pltpu.SMEM)
    grid_spec = pltpu.PrefetchScalarGridSpec(
        num_scalar_prefetch=2,
        grid=(n_steps,),
        in_specs=[
            smem_blk(cur), smem_blk(cur),
            pl.BlockSpec(h2.shape, lambda i, be, nu: (0, 0), pipeline_mode=pl.Buffered(1)),
            pl.BlockSpec((None, D_MODEL, 2 * D_FF), wsel, pipeline_mode=pl.Buffered(1)),
            pl.BlockSpec((None, 1, 2 * D_FF), wsel),
            pl.BlockSpec((None, D_FF, D_MODEL), wsel, pipeline_mode=pl.Buffered(1)),
            pl.BlockSpec((None, 1, D_MODEL), wsel),
        ],
        out_specs=pl.BlockSpec(memory_space=pl.ANY),
        scratch_shapes=[pltpu.VMEM((MOE_SUB, MOE_BM * PACK_ROWS, LANES), jnp.uint32),
                        pltpu.VMEM((MOE_SUB, MOE_BM * PACK_ROWS, LANES), jnp.uint32),
                        pltpu.SemaphoreType.DMA((MOE_SUB,))],
    )
    return pl.pallas_call(
        _moe_kernel,
        out_shape=jax.ShapeDtypeStruct((y_rows * PACK_ROWS, LANES), jnp.uint32),
        grid_spec=grid_spec,
        compiler_params=pltpu.CompilerParams(
            dimension_semantics=("arbitrary",), vmem_limit_bytes=MOE_VMEM_LIMIT),
        name="moe_experts",
    )(blk_e, n_used, src_tok, dst_row, h2, wgu, bgu, wdn, bdn)


def _route(top_e, n_tok):
    bm = MOE_SUB * MOE_BM
    n_assign = n_tok * TOP_K
    n_blocks = n_assign // bm + N_EXPERTS
    n_dummy = N_EXPERTS * bm
    flat_e = top_e.reshape(-1)
    counts = jnp.sum((flat_e[:, None] == jnp.arange(N_EXPERTS, dtype=jnp.int32)[None, :]).astype(jnp.int32), axis=0)
    padded = (counts + bm - 1) // bm * bm
    pad_end = jnp.cumsum(padded)
    d_e = jnp.arange(n_dummy, dtype=jnp.int32) // bm
    d_active = (jnp.arange(n_dummy, dtype=jnp.int32) % bm) < (padded - counts)[d_e]
    keys = jnp.concatenate([flat_e * 2, jnp.where(d_active, d_e * 2 + 1, 2 * N_EXPERTS)])
    vbits = n_assign.bit_length()
    marker = (1 << vbits) - 1
    vals = jnp.concatenate([jnp.arange(n_assign, dtype=jnp.int32), jnp.full((n_dummy,), marker, jnp.int32)])
    assign = jnp.sort(keys * (1 << vbits) + vals)[:n_blocks * bm] & marker
    rows = jnp.arange(n_blocks * bm, dtype=jnp.int32)
    valid = assign != marker
    src_tok = jnp.where(valid, assign // TOP_K, 0)
    dst_row = jnp.where(valid, (assign % TOP_K) * n_tok + assign // TOP_K, n_assign + rows % bm)
    blk_start = jnp.arange(n_blocks, dtype=jnp.int32) * bm
    blk_e = jnp.minimum(jnp.sum((pad_end[None, :] <= blk_start[:, None]).astype(jnp.int32), axis=1),
                        N_EXPERTS - 1).astype(jnp.int32)
    n_used = (pad_end[-1] // bm).astype(jnp.int32).reshape(1)
    return (blk_e, n_used, src_tok.reshape(n_blocks, 1, bm).astype(jnp.int32),
            dst_row.reshape(n_blocks, 1, bm).astype(jnp.int32))


def _combine_kernel(x1_ref, y0_ref, y1_ref, y2_ref, y3_ref, tg_ref, nf_ref, op_ref, os_ref, *, n_first):
    tg = tg_ref[...]
    tm = x1_ref.shape[0]
    rows = lambda y_ref: _load_packed_rows(y_ref, (), tm, F32)
    acc = rows(y0_ref) * tg[:, 0:1]
    for k, y_ref in enumerate((y1_ref, y2_ref, y3_ref), start=1):
        acc = acc + rows(y_ref) * tg[:, k:k + 1]
    x = x1_ref[...] + acc
    out = x * lax.rsqrt(jnp.mean(x * x, axis=-1, keepdims=True) + RMS_EPS) * nf_ref[...]
    i = pl.program_id(0)

    @pl.when(i < n_first)
    def _():
        op_ref[...] = out

    @pl.when(i >= n_first)
    def _():
        os_ref[...] = out


def _combine(x1, y, tg, norm_f_w, n_p):
    n = x1.shape[0]
    tm = COMBINE_TM
    n_first = n_p // tm
    row = lambda i: (i, 0)
    y_spec = lambda k: pl.BlockSpec((tm * PACK_ROWS, LANES), lambda i: (k * (n // tm) + i, 0))
    return pl.pallas_call(
        functools.partial(_combine_kernel, n_first=n_first),
        out_shape=(jax.ShapeDtypeStruct((n_p, D_MODEL), F32), jax.ShapeDtypeStruct((n - n_p, D_MODEL), F32)),
        grid=(n // tm,),
        in_specs=[pl.BlockSpec((tm, D_MODEL), row), y_spec(0), y_spec(1), y_spec(2), y_spec(3),
                  pl.BlockSpec((tm, LANES), row), pl.BlockSpec((1, D_MODEL), lambda i: (0, 0))],
        out_specs=_split_specs((tm, D_MODEL), n_first),
        compiler_params=pltpu.CompilerParams(
            dimension_semantics=("arbitrary",), vmem_limit_bytes=VMEM_LIMIT),
        name="combine_norm",
    )(x1, y, y, y, y, tg, norm_f_w.reshape(1, D_MODEL))


def _pack_w_in(w_in):
    s = np.cumsum((0,) + IN_SIZES)
    w_in = w_in.astype(BF16)
    qkv, z, al, be, qb, kb, vb, gb, gk, ga, gbt = [w_in[:, s[i]:s[i + 1]] for i in range(len(IN_SIZES))]
    small = jnp.concatenate(
        [al, be, gk, jnp.zeros((D_MODEL, LANES - 2 * N_HEADS - GK_RANK), w_in.dtype)], axis=1)
    return jnp.concatenate([qkv, z, ga, gbt, qb, kb, vb, gb, small], axis=1)


def kernel(x_prompt, x_sample, state_gdn, state_gdn_conv, state_gla, norm1_w, w_in, conv_w, a_log,
           dt_bias, gdn_norm_w, gla_gk_up, gla_gk_b, gla_norm_w, w_up_a, w_up_b, w_o, norm2_w,
           w_router, b_router, w_gate_up, b_gate_up, w_down, b_down, norm_f_w):
    bp, tp, _ = x_prompt.shape
    bs, ts, _ = x_sample.shape
    n_p, n_s = bp * tp, bs * ts
    n = n_p + n_s
    l = 0

    x_p = x_prompt.reshape(n_p, D_MODEL)
    x_s = x_sample.reshape(n_s, D_MODEL)
    p = _inproj(x_p, x_s, norm1_w[l], _pack_w_in(w_in[l]))

    hp = jnp.zeros((8, LANES), F32).at[0, :N_HEADS].set(a_log[l]).at[1, :N_HEADS].set(dt_bias[l])
    wgk = jnp.zeros((LANES, N_HEADS * DK_B), F32).at[2 * N_HEADS:2 * N_HEADS + GK_RANK].set(gla_gk_up[l]).astype(BF16)
    zeros = lambda shape: jnp.zeros(shape, F32)

    oa_p, gdn_p, conv_p = _gdn(p, 0, bp, tp, conv_w[l], hp, gdn_norm_w[l],
                               zeros((bp, CONV_W - 1, CONV_DIM)), zeros((bp, N_HEADS, DK_A, DV_A)))
    oa_s, gdn_s, conv_s = _gdn(p, n_p, bs, ts, conv_w[l], hp, gdn_norm_w[l], state_gdn_conv[l], state_gdn[l])
    ob_p, gla_p = _gla(p, 0, bp, tp, wgk, gla_gk_b[l], gla_norm_w[l], zeros((bp, N_HEADS, DK_B, DV_B)))
    ob_s, gla_s = _gla(p, n_p, bs, ts, wgk, gla_gk_b[l], gla_norm_w[l], state_gla[l])

    wr = jnp.zeros((D_MODEL, LANES), F32).at[:, :N_EXPERTS].set(w_router[l])
    wr_hi = wr.astype(BF16)
    wr = jnp.concatenate([wr_hi, (wr - wr_hi.astype(F32)).astype(BF16)], axis=1)
    br = jnp.zeros((1, LANES), F32).at[0, :N_EXPERTS].set(b_router[l])
    x1, h2, te, tg = _merge(x_p, x_s, oa_p, oa_s, ob_p, ob_s, p, w_up_a[l].astype(BF16),
                            w_up_b[l].astype(BF16), w_o[l].astype(BF16), norm2_w[l], wr, br)

    blk_e, n_used, src_tok, dst_row = _route(te[:, :TOP_K], n)
    y = _moe(blk_e, n_used, src_tok, dst_row, h2, w_gate_up[l],
             b_gate_up[l].reshape(N_EXPERTS, 1, 2 * D_FF), w_down[l],
             b_down[l].reshape(N_EXPERTS, 1, D_MODEL), n * TOP_K + MOE_SUB * MOE_BM)
    out_p, out_s = _combine(x1, y, tg, norm_f_w, n_p)
    y_prompt = out_p.reshape(bp, tp, D_MODEL)
    y_sample = out_s.reshape(bs, ts, D_MODEL)
    return (y_prompt, y_sample, gdn_p[None], conv_p[None], gla_p[None], gdn_s[None], conv_s[None],
            gla_s[None])
```

```python
import functools
import math

import jax
import jax.numpy as jnp
import numpy as np
from jax import lax
from jax.experimental import pallas as pl
from jax.experimental.pallas import tpu as pltpu

F32 = jnp.float32
BF16 = jnp.bfloat16

D_MODEL = 1024
N_HEADS = 4
DK_A = 128
DV_A = 128
W_A = N_HEADS * DV_A
CONV_W = 4
CONV_DIM = 2 * N_HEADS * DK_A + W_A
GDN_CHUNK = 64
DK_B = 64
DV_B = 128
W_B = N_HEADS * DV_B
GK_RANK = 16
GLA_GATE_NORM = 16.0
GLA_CHUNK = 16
N_EXPERTS = 32
TOP_K = 4
D_FF = 1024
SWIGLU_LIMIT = 7.0
SWIGLU_ALPHA = 1.702
RMS_EPS = 1e-6
IN_SIZES = (CONV_DIM, W_A, N_HEADS, N_HEADS, N_HEADS * DK_B, N_HEADS * DK_B, W_B, W_B, GK_RANK,
            D_MODEL, D_MODEL)

LANES = 128

OFF_QKV = 0
OFF_Z = 1536
OFF_GATE_A = 2048
OFF_GATE_B = 3072
OFF_QK_B = 4096
OFF_V_B = 4608
OFF_G_B = 5120
OFF_SMALL = 5632
P_COLS = 5760

INPROJ_TM = 512
GLA_TB = 256
GLA_SEQS_CARRY = 2
GDN_SEGS_CARRY = 4
GDN_SEQS_CARRY = 2
GDN_SEGS_INDEP = 8
MERGE_TM = 512
MERGE_SPLIT = 2
MOE_BM = 256
MOE_SUB = 2
PACK_ROWS = D_MODEL // 2 // LANES
MOE_VMEM_LIMIT = 60 * 1024 * 1024
COMBINE_TM = 256
VMEM_LIMIT = 56 * 1024 * 1024


def _dot(a, b):
    return jnp.dot(a.astype(BF16), b.astype(BF16), preferred_element_type=F32)


def _dot_nt(a, b):
    return lax.dot_general(a.astype(BF16), b.astype(BF16), (((1,), (1,)), ((), ())),
                           preferred_element_type=F32)


def _dot_tn(a, b):
    return lax.dot_general(a, b, (((0,), (0,)), ((), ())), preferred_element_type=F32)


def _split3(x):
    hi = x.astype(BF16)
    r1 = x - hi.astype(F32)
    mid = r1.astype(BF16)
    lo = (r1 - mid.astype(F32)).astype(BF16)
    return hi, mid, lo


def _dot01(m01, x):
    hi, mid, lo = _split3(x)
    d = lambda p: jnp.dot(m01, p, preferred_element_type=F32)
    return d(hi) + d(mid) + d(lo)


def _silu(x):
    return x * jax.nn.sigmoid(x)


def _pack_bf16_pairs(x):
    bits = pltpu.bitcast(x.astype(BF16).astype(F32), jnp.uint32)
    half = D_MODEL // 2
    return (bits[:, :half] >> 16) | (bits[:, half:] & jnp.uint32(0xFFFF0000))


def _store_packed_rows(ref, lead, row0, words):
    n = words.shape[0]
    for c in range(PACK_ROWS):
        ref[(*lead, pl.ds(row0 * PACK_ROWS + c, n, stride=PACK_ROWS), slice(None))] = words[:, c * LANES:(c + 1) * LANES]


def _load_packed_rows(ref, lead, n, dtype):
    words = [ref[(*lead, pl.ds(c, n, stride=PACK_ROWS), slice(None))] for c in range(PACK_ROWS)]
    lo = [pltpu.bitcast(w << 16, F32).astype(dtype) for w in words]
    hi = [pltpu.bitcast(w & jnp.uint32(0xFFFF0000), F32).astype(dtype) for w in words]
    return jnp.concatenate(lo + hi, axis=1)


def _pick(i, n_first, first_ref, second_ref):
    return jnp.where(i < n_first, first_ref[...], second_ref[...])


def _split_specs(block, n_first):
    return (pl.BlockSpec(block, lambda i: (jnp.minimum(i, n_first - 1), 0)),
            pl.BlockSpec(block, lambda i: (jnp.maximum(i - n_first, 0), 0)))


def _inproj_kernel(xp_ref, xs_ref, nw_ref, w_ref, o_ref, *, n_first):
    x = _pick(pl.program_id(0), n_first, xp_ref, xs_ref)
    ms = jnp.mean(x * x, axis=-1, keepdims=True)
    h = (x * lax.rsqrt(ms + RMS_EPS) * nw_ref[...]).astype(BF16)
    o_ref[...] = jnp.dot(h, w_ref[...], preferred_element_type=F32)


def _inproj(x_p, x_s, norm_w, w_packed):
    tm = INPROJ_TM
    n_first = x_p.shape[0] // tm
    n = x_p.shape[0] + x_s.shape[0]
    return pl.pallas_call(
        functools.partial(_inproj_kernel, n_first=n_first),
        out_shape=jax.ShapeDtypeStruct((n, P_COLS), F32),
        grid=(n // tm,),
        in_specs=[
            *_split_specs((tm, D_MODEL), n_first),
            pl.BlockSpec((1, D_MODEL), lambda i: (0, 0)),
            pl.BlockSpec((D_MODEL, P_COLS), lambda i: (0, 0), pipeline_mode=pl.Buffered(1)),
        ],
        out_specs=pl.BlockSpec((tm, P_COLS), lambda i: (i, 0)),
        compiler_params=pltpu.CompilerParams(
            dimension_semantics=("arbitrary",), vmem_limit_bytes=VMEM_LIMIT),
        name="inproj",
    )(x_p, x_s, norm_w.reshape(1, D_MODEL), w_packed)


def _gdn_kernel(*refs, chunk, n_seg, n_seq, carry, n_steps):
    qkv_refs, z_refs, sm_refs = refs[:n_seq], refs[n_seq:2 * n_seq], refs[2 * n_seq:3 * n_seq]
    cw_ref, hp_ref, nw_ref, sconv_ref, s0_ref, o_ref, sout_ref, cout_ref, s_ref, ubuf_ref = refs[3 * n_seq:]
    c_len = chunk
    n_seg_all = n_seq * n_seg
    rows = n_seg_all * c_len
    t = pl.program_id(1)
    hist = CONV_W - 1
    base = 8
    conv_len = n_seg * c_len if carry else c_len
    n_conv = n_seq if carry else n_seg

    def init_history():
        ubuf_ref[:, 0:base, :] = jnp.zeros((n_conv, base, CONV_DIM), F32)
        ubuf_ref[:, base - hist:base, :] = sconv_ref[...]

    if carry:
        @pl.when(t == 0)
        def _():
            s_ref[...] = s0_ref[...]
            init_history()

        @pl.when(t > 0)
        def _():
            ubuf_ref[:, base - hist:base, :] = ubuf_ref[:, base + conv_len - hist:base + conv_len, :]
        for q in range(n_seq):
            ubuf_ref[q, base:base + conv_len, :] = qkv_refs[q][...]
    else:
        init_history()
        ubuf_ref[:, base:base + conv_len, :] = qkv_refs[0][...].reshape(n_conv, conv_len, CONV_DIM)

    cw = cw_ref[...]
    u = ubuf_ref[...]
    u1 = pltpu.roll(u, 1, axis=1)
    v = u1 * cw[0:1, :] + u * cw[1:2, :]
    y = (pltpu.roll(v, 2, axis=1) + u1 * cw[2:3, :]) + u * cw[3:4, :]
    y = _silu(y[:, base:, :]).reshape(rows, CONV_DIM)

    sm = sm_refs[0][...] if n_seq == 1 else jnp.concatenate([r[...] for r in sm_refs], axis=0)
    hp = hp_ref[...]
    lane = lax.broadcasted_iota(jnp.int32, (rows, LANES), 1)
    g_all = -jnp.exp(hp[0:1, :]) * jax.nn.softplus(sm + hp[1:2, :])
    g_all = jnp.where(lane < N_HEADS, g_all, 0.0)
    beta_all = jax.nn.sigmoid(sm)

    sh = int(round(math.log2(c_len)))
    r_b = lax.broadcasted_iota(jnp.int32, (rows, rows), 0)
    c_b = lax.broadcasted_iota(jnp.int32, (rows, rows), 1)
    seg_tril = ((r_b >> sh) == (c_b >> sh)) & (c_b <= r_b)
    gc = _dot01(jnp.where(seg_tril, 1.0, 0.0).astype(BF16), g_all)

    r_i = lax.broadcasted_iota(jnp.int32, (c_len, c_len), 0)
    c_i = lax.broadcasted_iota(jnp.int32, (c_len, c_len), 1)
    tri = c_i <= r_i
    strict = c_i < r_i
    eye = c_i == r_i
    nw = nw_ref[...]
    levels = sh

    qn, kn, kb, rhs, qg, gcol = [], [], [], [], [], []
    for h in range(N_HEADS):
        qh = y[:, h * DK_A:(h + 1) * DK_A]
        kh = y[:, W_A + h * DK_A:W_A + (h + 1) * DK_A]
        vh = y[:, 2 * W_A + h * DV_A:2 * W_A + (h + 1) * DV_A]
        qn_h = qh * lax.rsqrt(jnp.sum(qh * qh, axis=-1, keepdims=True) + 1e-6) * (DK_A ** -0.5)
        kn_h = kh * lax.rsqrt(jnp.sum(kh * kh, axis=-1, keepdims=True) + 1e-6)
        gcol_h = gc[:, h:h + 1]
        eg = jnp.exp(gcol_h)
        bcol = beta_all[:, N_HEADS + h:N_HEADS + h + 1]
        kb_h = kn_h * bcol
        qn.append(qn_h)
        kn.append(kn_h)
        kb.append(kb_h)
        rhs.append(jnp.concatenate([vh * bcol, kb_h * eg], axis=1))
        qg.append(qn_h * eg)
        gcol.append(gcol_h)

    pairs = [(g, h) for g in range(n_seg_all) for h in range(N_HEADS)]
    sl = lambda g: slice(g * c_len, (g + 1) * c_len)

    decay, kd, gl = {}, {}, {}
    for (g, h) in pairs:
        gcol_p = gcol[h][sl(g)]
        gcb = jnp.broadcast_to(gcol_p, (c_len, c_len))
        grow = jnp.sum(jnp.where(eye, gcb, 0.0), axis=0, keepdims=True)
        decay[g, h] = jnp.where(tri, jnp.exp(jnp.where(tri, gcb - grow, 0.0)), 0.0)
        glast = gcol[h][(g + 1) * c_len - 1:(g + 1) * c_len]
        kd[g, h] = kn[h][sl(g)] * jnp.exp(glast - gcol_p)
        gl[g, h] = jnp.exp(glast)

    kq = {p: _dot_nt(jnp.concatenate([kb[p[1]][sl(p[0])], qn[p[1]][sl(p[0])]], axis=0), kn[p[1]][sl(p[0])])
          for p in pairs}
    low = {p: jnp.where(strict, kq[p][:c_len] * decay[p], 0.0) for p in pairs}
    attn = {p: jnp.where(tri, kq[p][c_len:] * decay[p], 0.0) for p in pairs}
    ymat = {p: -low[p] for p in pairs}
    lp = low
    for _ in range(levels - 1):
        lp = {p: _dot(lp[p], lp[p]) for p in pairs}
        prod = {p: _dot(ymat[p], lp[p]) for p in pairs}
        ymat = {p: ymat[p] + lp[p] + prod[p] for p in pairs}
    sol = {}
    for (g, h) in pairs:
        rhs_p = rhs[h][sl(g)]
        sol[g, h] = rhs_p + _dot(ymat[g, h], rhs_p)

    def finish(g, h, o):
        r = o * lax.rsqrt(jnp.mean(o * o, axis=-1, keepdims=True) + RMS_EPS) * nw
        q, loc = (g // n_seg, sl(g % n_seg)) if carry else (0, sl(g))
        gated = r * _silu(z_refs[q][loc, h * DV_A:(h + 1) * DV_A])
        if carry:
            o_ref[q, loc, h * DV_A:(h + 1) * DV_A] = gated
        else:
            o_ref[loc, h * DV_A:(h + 1) * DV_A] = gated

    def advance(group, state):
        ws = {p: _dot(jnp.concatenate([sol[p][:, DV_A:], qg[p[1]][sl(p[0])]], axis=0), state[p]) for p in group}
        v_new = {p: sol[p][:, :DV_A] - ws[p][:c_len] for p in group}
        o = {p: ws[p][c_len:] + _dot(attn[p], v_new[p]) for p in group}
        new = {p: state[p] * gl[p] + _dot_tn(kd[p], v_new[p]) for p in group}
        for p in group:
            finish(p[0], p[1], o[p])
        return new

    if carry:
        chains = [(q, h) for q in range(n_seq) for h in range(N_HEADS)]
        cur = {c: s_ref[c[0], c[1]] for c in chains}
        for g in range(n_seg):
            group = [(q * n_seg + g, h) for (q, h) in chains]
            new = advance(group, {(q * n_seg + g, h): cur[q, h] for (q, h) in chains})
            cur = {(q, h): new[q * n_seg + g, h] for (q, h) in chains}
        for (q, h) in chains:
            s_ref[q, h] = cur[q, h]

        @pl.when(t == n_steps - 1)
        def _():
            for (q, h) in chains:
                sout_ref[q, h] = cur[q, h]
            cout_ref[...] = ubuf_ref[:, base + conv_len - hist:base + conv_len, :]
    else:
        new = advance(pairs, {p: s0_ref[p[0], p[1]] for p in pairs})
        for p in pairs:
            sout_ref[p[0], p[1]] = new[p]
        cout_ref[...] = ubuf_ref[:, base + conv_len - hist:base + conv_len, :]


def _gdn(p, row0, batch, seq, conv_w, hp, norm_w, s_conv, s0):
    chunk = math.gcd(seq, GDN_CHUNK)
    n_chunks = seq // chunk
    carry = n_chunks > 1
    if carry:
        n_seg = math.gcd(n_chunks, GDN_SEGS_CARRY)
        n_seq = math.gcd(batch, GDN_SEQS_CARRY)
        grid = (batch // n_seq, n_chunks // n_seg)
        n_conv, conv_len = n_seq, n_seg * chunk
    else:
        n_seg = math.gcd(batch, GDN_SEGS_INDEP)
        n_seq = 1
        grid = (batch // n_seg, 1)
        n_conv, conv_len = n_seg, chunk
    rows = n_seg * chunk
    n_steps = grid[1]
    blk0 = row0 // rows
    rowblk = lambda q: (lambda b, t: blk0 + (b * n_seq + q) * n_steps + t)
    pcol = lambda width, off: [pl.BlockSpec((rows, width), lambda b, t, rb=rowblk(q): (rb(b, t), off // width))
                               for q in range(n_seq)]
    kern = functools.partial(_gdn_kernel, chunk=chunk, n_seg=n_seg, n_seq=n_seq, carry=carry, n_steps=n_steps)
    if carry:
        o_shape, o_spec = (batch, seq, W_A), pl.BlockSpec((n_seq, rows, W_A), lambda b, t: (b, t, 0))
    else:
        o_shape, o_spec = (batch * seq, W_A), pl.BlockSpec((rows, W_A), lambda b, t: (b, 0))
    o, s_new, c_new = pl.pallas_call(
        kern,
        out_shape=(jax.ShapeDtypeStruct(o_shape, F32),
                   jax.ShapeDtypeStruct((batch, N_HEADS, DK_A, DV_A), F32),
                   jax.ShapeDtypeStruct((batch, CONV_W - 1, CONV_DIM), F32)),
        grid=grid,
        in_specs=[
            *pcol(CONV_DIM, OFF_QKV), *pcol(W_A, OFF_Z), *pcol(LANES, OFF_SMALL),
            pl.BlockSpec((CONV_W, CONV_DIM), lambda b, t: (0, 0)),
            pl.BlockSpec((8, LANES), lambda b, t: (0, 0)),
            pl.BlockSpec((1, DV_A), lambda b, t: (0, 0)),
            pl.BlockSpec((n_conv, CONV_W - 1, CONV_DIM), lambda b, t: (b, 0, 0)),
            pl.BlockSpec((n_conv, N_HEADS, DK_A, DV_A), lambda b, t: (b, 0, 0, 0)),
        ],
        out_specs=(
            o_spec,
            pl.BlockSpec((n_conv, N_HEADS, DK_A, DV_A), lambda b, t: (b, 0, 0, 0)),
            pl.BlockSpec((n_conv, CONV_W - 1, CONV_DIM), lambda b, t: (b, 0, 0)),
        ),
        scratch_shapes=[pltpu.VMEM((n_seq, N_HEADS, DK_A, DV_A), F32),
                        pltpu.VMEM((n_conv, conv_len + 8, CONV_DIM), F32)],
        compiler_params=pltpu.CompilerParams(
            dimension_semantics=("arbitrary", "arbitrary"), vmem_limit_bytes=VMEM_LIMIT),
        name=f"gdn_c{chunk}",
    )(*([p] * (3 * n_seq)), conv_w, hp, norm_w.reshape(1, DV_A), s_conv, s0)
    return o.reshape(batch * seq, W_A), s_new, c_new


def _gla_kernel(*refs, tb, cs, n_seq, carry, n_steps):
    qk_refs, v_refs, g_refs, sm_refs = (refs[k * n_seq:(k + 1) * n_seq] for k in range(4))
    wgk_ref, bgk_ref, nw_ref, s0_ref, o_ref, sout_ref, s_ref = refs[4 * n_seq:]
    t = pl.program_id(1)
    n_sub = tb // cs
    w_kb = N_HEADS * DK_B
    seqs = range(n_seq)
    heads = range(N_HEADS)

    if carry:
        @pl.when(t == 0)
        def _():
            s_ref[...] = s0_ref[...]

    sh = int(round(math.log2(cs)))
    r_i = lax.broadcasted_iota(jnp.int32, (tb, tb), 0)
    c_i = lax.broadcasted_iota(jnp.int32, (tb, tb), 1)
    same = (r_i >> sh) == (c_i >> sh)
    mask = same & (c_i <= r_i)
    mask_bf = jnp.where(mask, 1.0, 0.0).astype(BF16)
    same_bf = jnp.where(same, 1.0, 0.0).astype(BF16)
    onehot = jnp.where((lax.broadcasted_iota(jnp.int32, (tb, n_sub), 0) >> sh)
                       == lax.broadcasted_iota(jnp.int32, (tb, n_sub), 1), 1.0, 0.0).astype(BF16)
    tn = lambda part: lax.dot_general(part, onehot, (((0,), (0,)), ((), ())), preferred_element_type=F32)

    log_a = [jax.nn.log_sigmoid(_dot(sm_refs[q][...], wgk_ref[...]) + bgk_ref[...]) / GLA_GATE_NORM for q in seqs]
    gcum = [_dot01(mask_bf, la) for la in log_a]
    gtot = [_dot01(same_bf, la) for la in log_a]
    glcol = []
    for la in log_a:
        hi, mid, lo = _split3(la)
        glcol.append(jnp.exp(tn(hi) + tn(mid) + tn(lo)))

    nw = nw_ref[...]
    sl = lambda j: slice(j * cs, (j + 1) * cs)
    qg, kg, kd, vh = {}, {}, {}, {}
    for q in seqs:
        qk = qk_refs[q][...]
        for h in heads:
            qh = qk[:, h * DK_B:(h + 1) * DK_B] * (DK_B ** -0.5)
            kh = qk[:, w_kb + h * DK_B:w_kb + (h + 1) * DK_B]
            gh = gcum[q][:, h * DK_B:(h + 1) * DK_B]
            gt = gtot[q][:, h * DK_B:(h + 1) * DK_B]
            qg[q, h] = qh * jnp.exp(gh)
            kg[q, h] = kh * jnp.exp(-gh)
            kd[q, h] = kh * jnp.exp(gt - gh)
            vh[q, h] = v_refs[q][:, h * DV_B:(h + 1) * DV_B]
    chains = [(q, h) for q in seqs for h in heads]
    attn = {c: jnp.where(mask, _dot_nt(qg[c], kg[c]), 0.0) for c in chains}
    o_intra = {c: _dot(attn[c], vh[c]) for c in chains}
    upd = {(j, c): _dot_tn(kd[c][sl(j)], vh[c][sl(j)]) for j in range(n_sub) for c in chains}
    gl = lambda j, c: glcol[c[0]][c[1] * DK_B:(c[1] + 1) * DK_B, j:j + 1]

    o_inter = {c: [] for c in chains}
    if carry:
        cur = {c: s_ref[c[0], c[1]] for c in chains}
        for j in range(n_sub):
            for c in chains:
                o_inter[c].append(_dot(qg[c][sl(j)], cur[c]))
                cur[c] = cur[c] * gl(j, c) + upd[j, c]
        for c in chains:
            s_ref[c[0], c[1]] = cur[c]

        @pl.when(t == n_steps - 1)
        def _():
            for c in chains:
                sout_ref[c[0], c[1]] = cur[c]
    else:
        for j in range(n_sub):
            for c in chains:
                s = s0_ref[j, c[1]]
                o_inter[c].append(_dot(qg[c][sl(j)], s))
                sout_ref[j, c[1]] = s * gl(j, c) + upd[j, c]

    for (q, h) in chains:
        o = o_intra[q, h] + jnp.concatenate(o_inter[q, h], axis=0)
        r = o * lax.rsqrt(jnp.mean(o * o, axis=-1, keepdims=True) + RMS_EPS) * nw
        gated = r * _silu(g_refs[q][:, h * DV_B:(h + 1) * DV_B])
        if carry:
            o_ref[q, :, h * DV_B:(h + 1) * DV_B] = gated
        else:
            o_ref[:, h * DV_B:(h + 1) * DV_B] = gated


def _gla(p, row0, batch, seq, wgk, bgk, norm_w, s0):
    cs = math.gcd(seq, GLA_CHUNK)
    carry = seq > cs
    if carry:
        tb = math.gcd(seq, GLA_TB)
        n_seq = math.gcd(batch, GLA_SEQS_CARRY)
        grid = (batch // n_seq, seq // tb)
        n_state = n_seq
    else:
        tb = math.gcd(batch * seq, GLA_TB)
        n_seq = 1
        n_state = tb // cs
        grid = (batch // n_state, 1)
    n_steps = grid[1]
    blk0 = row0 // tb
    rowblk = lambda q: (lambda b, t: blk0 + (b * n_seq + q) * n_steps + t)
    pcol = lambda width, off: [pl.BlockSpec((tb, width), lambda b, t, rb=rowblk(q): (rb(b, t), off // width))
                               for q in range(n_seq)]
    kern = functools.partial(_gla_kernel, tb=tb, cs=cs, n_seq=n_seq, carry=carry, n_steps=n_steps)
    if carry:
        o_shape, o_spec = (batch, seq, W_B), pl.BlockSpec((n_seq, tb, W_B), lambda b, t: (b, t, 0))
    else:
        o_shape, o_spec = (batch * seq, W_B), pl.BlockSpec((tb, W_B), lambda b, t: (b, 0))
    o, s_new = pl.pallas_call(
        kern,
        out_shape=(jax.ShapeDtypeStruct(o_shape, F32),
                   jax.ShapeDtypeStruct((batch, N_HEADS, DK_B, DV_B), F32)),
        grid=grid,
        in_specs=[
            *pcol(2 * N_HEADS * DK_B, OFF_QK_B), *pcol(W_B, OFF_V_B), *pcol(W_B, OFF_G_B), *pcol(LANES, OFF_SMALL),
            pl.BlockSpec((LANES, N_HEADS * DK_B), lambda b, t: (0, 0)),
            pl.BlockSpec((1, N_HEADS * DK_B), lambda b, t: (0, 0)),
            pl.BlockSpec((1, DV_B), lambda b, t: (0, 0)),
            pl.BlockSpec((n_state, N_HEADS, DK_B, DV_B), lambda b, t: (b, 0, 0, 0)),
        ],
        out_specs=(
            o_spec,
            pl.BlockSpec((n_state, N_HEADS, DK_B, DV_B), lambda b, t: (b, 0, 0, 0)),
        ),
        scratch_shapes=[pltpu.VMEM((n_seq, N_HEADS, DK_B, DV_B), F32)],
        compiler_params=pltpu.CompilerParams(
            dimension_semantics=("arbitrary", "arbitrary"), vmem_limit_bytes=VMEM_LIMIT),
        name=f"gla_c{cs}",
    )(*([p] * (4 * n_seq)), wgk, bgk.reshape(1, N_HEADS * DK_B), norm_w.reshape(1, DV_B), s0)
    return o.reshape(batch * seq, W_B), s_new


def _topk_softmax(logits):
    tm = logits.shape[0]
    lane = lax.broadcasted_iota(jnp.int32, (tm, LANES), 1)
    lane_f = lane.astype(F32)
    neg = jnp.float32(-jnp.inf)
    cur = jnp.where(lane < N_EXPERTS, logits, neg)
    vals, idxs = [], []
    for _ in range(TOP_K):
        m = jnp.max(cur, axis=-1, keepdims=True)
        idx = jnp.min(jnp.where(cur == m, lane_f, float(LANES)), axis=-1, keepdims=True)
        vals.append(m)
        idxs.append(idx)
        cur = jnp.where(lane_f == idx, neg, cur)
    exps = [jnp.exp(v - vals[0]) for v in vals]
    den = exps[0] + exps[1] + exps[2] + exps[3]
    te = jnp.zeros((tm, LANES), F32)
    tg = jnp.zeros((tm, LANES), F32)
    for k in range(TOP_K):
        te = jnp.where(lane == k, idxs[k], te)
        tg = jnp.where(lane == k, exps[k] / den, tg)
    return te.astype(jnp.int32), tg


def _merge_kernel(xp_ref, xs_ref, oap_ref, oas_ref, obp_ref, obs_ref, ga_ref, gb_ref, wua_ref, wub_ref, wo_ref,
                  n2_ref, wr_ref, br_ref, x1_ref, h2_ref, te_ref, tg_ref, *, n_first):
    i = pl.program_id(0)
    tm = x1_ref.shape[0]
    hm = tm // MERGE_SPLIT
    groups = [slice(a * hm, (a + 1) * hm) for a in range(MERGE_SPLIT)]
    oa = _pick(i, n_first, oap_ref, oas_ref).astype(BF16)
    ob = _pick(i, n_first, obp_ref, obs_ref).astype(BF16)
    x = _pick(i, n_first, xp_ref, xs_ref)
    ua = [jnp.dot(oa[g], wua_ref[...], preferred_element_type=F32) for g in groups]
    ub = [jnp.dot(ob[g], wub_ref[...], preferred_element_type=F32) for g in groups]
    merged = [jax.nn.sigmoid(ga_ref[g, :]) * ua[a] + jax.nn.sigmoid(gb_ref[g, :]) * ub[a]
              for a, g in enumerate(groups)]
    x1 = [x[g] + jnp.dot(merged[a].astype(BF16), wo_ref[...], preferred_element_type=F32)
          for a, g in enumerate(groups)]
    h2 = [v * lax.rsqrt(jnp.mean(v * v, axis=-1, keepdims=True) + RMS_EPS) * n2_ref[...] for v in x1]
    h_hi = [v.astype(BF16) for v in h2]
    h_lo = [(v - hi.astype(F32)).astype(BF16) for v, hi in zip(h2, h_hi)]
    l_hi = [jnp.dot(hi, wr_ref[...], preferred_element_type=F32) for hi in h_hi]
    l_lo = [jnp.dot(lo, wr_ref[:, :LANES], preferred_element_type=F32) for lo in h_lo]
    for a, g in enumerate(groups):
        x1_ref[g, :] = x1[a]
        _store_packed_rows(h2_ref, (), a * hm, _pack_bf16_pairs(h2[a]))
        logits = l_hi[a][:, :LANES] + l_hi[a][:, LANES:] + l_lo[a] + br_ref[...]
        te, tg = _topk_softmax(logits)
        te_ref[g, :] = te
        tg_ref[g, :] = tg


def _merge(x_p, x_s, oa_p, oa_s, ob_p, ob_s, p, wua, wub, wo, norm2_w, wr, br):
    n = x_p.shape[0] + x_s.shape[0]
    tm = MERGE_TM
    n_first = x_p.shape[0] // tm
    row = lambda i: (i, 0)
    const = lambda i: (0, 0)
    return pl.pallas_call(
        functools.partial(_merge_kernel, n_first=n_first),
        out_shape=(jax.ShapeDtypeStruct((n, D_MODEL), F32), jax.ShapeDtypeStruct((n * PACK_ROWS, LANES), jnp.uint32),
                   jax.ShapeDtypeStruct((n, LANES), jnp.int32), jax.ShapeDtypeStruct((n, LANES), F32)),
        grid=(n // tm,),
        in_specs=[
            *_split_specs((tm, D_MODEL), n_first),
            *_split_specs((tm, W_A), n_first),
            *_split_specs((tm, W_B), n_first),
            pl.BlockSpec((tm, D_MODEL), lambda i: (i, OFF_GATE_A // D_MODEL)),
            pl.BlockSpec((tm, D_MODEL), lambda i: (i, OFF_GATE_B // D_MODEL)),
            pl.BlockSpec((W_A, D_MODEL), const),
            pl.BlockSpec((W_B, D_MODEL), const),
            pl.BlockSpec((D_MODEL, D_MODEL), const),
            pl.BlockSpec((1, D_MODEL), const),
            pl.BlockSpec((D_MODEL, 2 * LANES), const),
            pl.BlockSpec((1, LANES), const),
        ],
        out_specs=(pl.BlockSpec((tm, D_MODEL), row), pl.BlockSpec((tm * PACK_ROWS, LANES), row),
                   pl.BlockSpec((tm, LANES), row), pl.BlockSpec((tm, LANES), row)),
        compiler_params=pltpu.CompilerParams(
            dimension_semantics=("arbitrary",), vmem_limit_bytes=VMEM_LIMIT),
        name="merge_router",
    )(x_p, x_s, oa_p, oa_s, ob_p, ob_s, p, p, wua, wub, wo, norm2_w.reshape(1, D_MODEL), wr, br)


def _moe_kernel(blk_e_ref, n_used_ref, src_ref, dst_ref, h2_ref, wgu_ref, bgu_ref,
                wdn_ref, bdn_ref, y_hbm, xstage, ybuf, ssem):
    del blk_e_ref
    i = pl.program_id(0)
    n_used = n_used_ref[0]
    bm = MOE_BM
    pr = PACK_ROWS

    def scatter_copy(row, s, r):
        return pltpu.make_async_copy(ybuf.at[s, pl.ds(r * pr, pr), :],
                                     y_hbm.at[pl.ds(pl.multiple_of(row * pr, pr), pr), :], ssem.at[s])

    @pl.when(i == 0)
    def _():
        ybuf[...] = jnp.zeros_like(ybuf)
        n_real = y_hbm.shape[0] - MOE_SUB * bm * pr
        for s in range(MOE_SUB):
            sink = pltpu.make_async_copy(ybuf.at[s], y_hbm.at[pl.ds(n_real + s * bm * pr, bm * pr), :], ssem.at[s])
            sink.start()
            sink.wait()

    @pl.when(i < n_used)
    def _():
        for s in range(MOE_SUB):
            @pl.when(i >= 1)
            def _():
                for r in range(bm):
                    scatter_copy(0, s, r).wait()

            for r in range(bm):
                tok = pl.multiple_of(src_ref[0, 0, s * bm + r] * pr, pr)
                xstage[s, pl.ds(r * pr, pr), :] = h2_ref[pl.ds(tok, pr), :]
            x = _load_packed_rows(xstage, (s,), bm, BF16)
            gu = jnp.dot(x, wgu_ref[...].astype(BF16), preferred_element_type=F32) + bgu_ref[...]
            gate = jnp.minimum(gu[:, :D_FF], SWIGLU_LIMIT)
            up = jnp.clip(gu[:, D_FF:], -SWIGLU_LIMIT, SWIGLU_LIMIT)
            hmid = (up + 1.0) * (gate * jax.nn.sigmoid(SWIGLU_ALPHA * gate))
            y = jnp.dot(hmid.astype(BF16), wdn_ref[...].astype(BF16), preferred_element_type=F32) + bdn_ref[...]
            _store_packed_rows(ybuf, (s,), 0, _pack_bf16_pairs(y))
            for r in range(bm):
                scatter_copy(dst_ref[0, 0, s * bm + r], s, r).start(priority=r % 2)

    @pl.when(i == n_used - 1)
    def _():
        for s in range(MOE_SUB):
            for r in range(bm):
                scatter_copy(0, s, r).wait()


def _moe(blk_e, n_used, src_tok, dst_row, h2, wgu, bgu, wdn, bdn, y_rows):
    n_steps = src_tok.shape[0]
    step_rows = MOE_SUB * MOE_BM
    cur = lambda i, be, nu: (i, 0, 0)
    wsel = lambda i, be, nu: (be[i], 0, 0)
    smem_blk = lambda im: pl.BlockSpec((1, 1, step_rows), im, memory_space=pltpu.SMEM)
    grid_spec = pltpu.PrefetchScalarGridSpec(
        num_scalar_prefetch=2,
        grid=(n_steps,),
        in_specs=[
            smem_blk(cur), smem_blk(cur),
            pl.BlockSpec(h2.shape, lambda i, be, nu: (0, 0), pipeline_mode=pl.Buffered(1)),
            pl.BlockSpec((None, D_MODEL, 2 * D_FF), wsel, pipeline_mode=pl.Buffered(1)),
            pl.BlockSpec((None, 1, 2 * D_FF), wsel),
            pl.BlockSpec((None, D_FF, D_MODEL), wsel, pipeline_mode=pl.Buffered(1)),
            pl.BlockSpec((None, 1, D_MODEL), wsel),
        ],
        out_specs=pl.BlockSpec(memory_space=pl.ANY),
        scratch_shapes=[pltpu.VMEM((MOE_SUB, MOE_BM * PACK_ROWS, LANES), jnp.uint32),
                        pltpu.VMEM((MOE_SUB, MOE_BM * PACK_ROWS, LANES), jnp.uint32),
                        pltpu.SemaphoreType.DMA((MOE_SUB,))],
    )
    return pl.pallas_call(
        _moe_kernel,
        out_shape=jax.ShapeDtypeStruct((y_rows * PACK_ROWS, LANES), jnp.uint32),
        grid_spec=grid_spec,
        compiler_params=pltpu.CompilerParams(
            dimension_semantics=("arbitrary",), vmem_limit_bytes=MOE_VMEM_LIMIT),
        name="moe_experts",
    )(blk_e, n_used, src_tok, dst_row, h2, wgu, bgu, wdn, bdn)


def _route(top_e, n_tok):
    bm = MOE_SUB * MOE_BM
    n_assign = n_tok * TOP_K
    n_blocks = n_assign // bm + N_EXPERTS
    n_dummy = N_EXPERTS * bm
    flat_e = top_e.reshape(-1)
    counts = jnp.sum((flat_e[:, None] == jnp.arange(N_EXPERTS, dtype=jnp.int32)[None, :]).astype(jnp.int32), axis=0)
    padded = (counts + bm - 1) // bm * bm
    pad_end = jnp.cumsum(padded)
    d_e = jnp.arange(n_dummy, dtype=jnp.int32) // bm
    d_active = (jnp.arange(n_dummy, dtype=jnp.int32) % bm) < (padded - counts)[d_e]
    keys = jnp.concatenate([flat_e * 2, jnp.where(d_active, d_e * 2 + 1, 2 * N_EXPERTS)])
    vbits = n_assign.bit_length()
    marker = (1 << vbits) - 1
    vals = jnp.concatenate([jnp.arange(n_assign, dtype=jnp.int32), jnp.full((n_dummy,), marker, jnp.int32)])
    assign = jnp.sort(keys * (1 << vbits) + vals)[:n_blocks * bm] & marker
    rows = jnp.arange(n_blocks * bm, dtype=jnp.int32)
    valid = assign != marker
    src_tok = jnp.where(valid, assign // TOP_K, 0)
    dst_row = jnp.where(valid, (assign % TOP_K) * n_tok + assign // TOP_K, n_assign + rows % bm)
    blk_start = jnp.arange(n_blocks, dtype=jnp.int32) * bm
    blk_e = jnp.minimum(jnp.sum((pad_end[None, :] <= blk_start[:, None]).astype(jnp.int32), axis=1),
                        N_EXPERTS - 1).astype(jnp.int32)
    n_used = (pad_end[-1] // bm).astype(jnp.int32).reshape(1)
    return (blk_e, n_used, src_tok.reshape(n_blocks, 1, bm).astype(jnp.int32),
            dst_row.reshape(n_blocks, 1, bm).astype(jnp.int32))


def _combine_kernel(x1_ref, y0_ref, y1_ref, y2_ref, y3_ref, tg_ref, nf_ref, op_ref, os_ref, *, n_first):
    tg = tg_ref[...]
    tm = x1_ref.shape[0]
    rows = lambda y_ref: _load_packed_rows(y_ref, (), tm, F32)
    acc = rows(y0_ref) * tg[:, 0:1]
    for k, y_ref in enumerate((y1_ref, y2_ref, y3_ref), start=1):
        acc = acc + rows(y_ref) * tg[:, k:k + 1]
    x = x1_ref[...] + acc
    out = x * lax.rsqrt(jnp.mean(x * x, axis=-1, keepdims=True) + RMS_EPS) * nf_ref[...]
    i = pl.program_id(0)

    @pl.when(i < n_first)
    def _():
        op_ref[...] = out

    @pl.when(i >= n_first)
    def _():
        os_ref[...] = out


def _combine(x1, y, tg, norm_f_w, n_p):
    n = x1.shape[0]
    tm = COMBINE_TM
    n_first = n_p // tm
    row = lambda i: (i, 0)
    y_spec = lambda k: pl.BlockSpec((tm * PACK_ROWS, LANES), lambda i: (k * (n // tm) + i, 0))
    return pl.pallas_call(
        functools.partial(_combine_kernel, n_first=n_first),
        out_shape=(jax.ShapeDtypeStruct((n_p, D_MODEL), F32), jax.ShapeDtypeStruct((n - n_p, D_MODEL), F32)),
        grid=(n // tm,),
        in_specs=[pl.BlockSpec((tm, D_MODEL), row), y_spec(0), y_spec(1), y_spec(2), y_spec(3),
                  pl.BlockSpec((tm, LANES), row), pl.BlockSpec((1, D_MODEL), lambda i: (0, 0))],
        out_specs=_split_specs((tm, D_MODEL), n_first),
        compiler_params=pltpu.CompilerParams(
            dimension_semantics=("arbitrary",), vmem_limit_bytes=VMEM_LIMIT),
        name="combine_norm",
    )(x1, y, y, y, y, tg, norm_f_w.reshape(1, D_MODEL))


def _pack_w_in(w_in):
    s = np.cumsum((0,) + IN_SIZES)
    w_in = w_in.astype(BF16)
    qkv, z, al, be, qb, kb, vb, gb, gk, ga, gbt = [w_in[:, s[i]:s[i + 1]] for i in range(len(IN_SIZES))]
    small = jnp.concatenate(
        [al, be, gk, jnp.zeros((D_MODEL, LANES - 2 * N_HEADS - GK_RANK), w_in.dtype)], axis=1)
    return jnp.concatenate([qkv, z, ga, gbt, qb, kb, vb, gb, small], axis=1)


def kernel(x_prompt, x_sample, state_gdn, state_gdn_conv, state_gla, norm1_w, w_in, conv_w, a_log,
           dt_bias, gdn_norm_w, gla_gk_up, gla_gk_b, gla_norm_w, w_up_a, w_up_b, w_o, norm2_w,
           w_router, b_router, w_gate_up, b_gate_up, w_down, b_down, norm_f_w):
    bp, tp, _ = x_prompt.shape
    bs, ts, _ = x_sample.shape
    n_p, n_s = bp * tp, bs * ts
    n = n_p + n_s
    l = 0

    x_p = x_prompt.reshape(n_p, D_MODEL)
    x_s = x_sample.reshape(n_s, D_MODEL)
    p = _inproj(x_p, x_s, norm1_w[l], _pack_w_in(w_in[l]))

    hp = jnp.zeros((8, LANES), F32).at[0, :N_HEADS].set(a_log[l]).at[1, :N_HEADS].set(dt_bias[l])
    wgk = jnp.zeros((LANES, N_HEADS * DK_B), F32).at[2 * N_HEADS:2 * N_HEADS + GK_RANK].set(gla_gk_up[l]).astype(BF16)
    zeros = lambda shape: jnp.zeros(shape, F32)

    oa_p, gdn_p, conv_p = _gdn(p, 0, bp, tp, conv_w[l], hp, gdn_norm_w[l],
                               zeros((bp, CONV_W - 1, CONV_DIM)), zeros((bp, N_HEADS, DK_A, DV_A)))
    oa_s, gdn_s, conv_s = _gdn(p, n_p, bs, ts, conv_w[l], hp, gdn_norm_w[l], state_gdn_conv[l], state_gdn[l])
    ob_p, gla_p = _gla(p, 0, bp, tp, wgk, gla_gk_b[l], gla_norm_w[l], zeros((bp, N_HEADS, DK_B, DV_B)))
    ob_s, gla_s = _gla(p, n_p, bs, ts, wgk, gla_gk_b[l], gla_norm_w[l], state_gla[l])

    wr = jnp.zeros((D_MODEL, LANES), F32).at[:, :N_EXPERTS].set(w_router[l])
    wr_hi = wr.astype(BF16)
    wr = jnp.concatenate([wr_hi, (wr - wr_hi.astype(F32)).astype(BF16)], axis=1)
    br = jnp.zeros((1, LANES), F32).at[0, :N_EXPERTS].set(b_router[l])
    x1, h2, te, tg = _merge(x_p, x_s, oa_p, oa_s, ob_p, ob_s, p, w_up_a[l].astype(BF16),
                            w_up_b[l].astype(BF16), w_o[l].astype(BF16), norm2_w[l], wr, br)

    blk_e, n_used, src_tok, dst_row = _route(te[:, :TOP_K], n)
    y = _moe(blk_e, n_used, src_tok, dst_row, h2, w_gate_up[l],
             b_gate_up[l].reshape(N_EXPERTS, 1, 2 * D_FF), w_down[l],
             b_down[l].reshape(N_EXPERTS, 1, D_MODEL), n * TOP_K + MOE_SUB * MOE_BM)
    out_p, out_s = _combine(x1, y, tg, norm_f_w, n_p)
    y_prompt = out_p.reshape(bp, tp, D_MODEL)
    y_sample = out_s.reshape(bs, ts, D_MODEL)
    return (y_prompt, y_sample, gdn_p[None], conv_p[None], gla_p[None], gdn_s[None], conv_s[None],
            gla_s[None])
```

```python
import functools
import math

import jax
import jax.numpy as jnp
import numpy as np
from jax import lax
from jax.experimental import pallas as pl
from jax.experimental.pallas import tpu as pltpu

F32 = jnp.float32
BF16 = jnp.bfloat16

D_MODEL = 1024
N_HEADS = 4
DK_A = 128
DV_A = 128
W_A = N_HEADS * DV_A
CONV_W = 4
CONV_DIM = 2 * N_HEADS * DK_A + W_A
GDN_CHUNK = 64
DK_B = 64
DV_B = 128
W_B = N_HEADS * DV_B
GK_RANK = 16
GLA_GATE_NORM = 16.0
GLA_CHUNK = 16
N_EXPERTS = 32
TOP_K = 4
D_FF = 1024
SWIGLU_LIMIT = 7.0
SWIGLU_ALPHA = 1.702
RMS_EPS = 1e-6
IN_SIZES = (CONV_DIM, W_A, N_HEADS, N_HEADS, N_HEADS * DK_B, N_HEADS * DK_B, W_B, W_B, GK_RANK,
            D_MODEL, D_MODEL)

LANES = 128

OFF_QKV = 0
OFF_Z = 1536
OFF_GATE_A = 2048
OFF_GATE_B = 3072
OFF_QK_B = 4096
OFF_V_B = 4608
OFF_G_B = 5120
OFF_SMALL = 5632
P_COLS = 5760

INPROJ_TM = 512
GLA_TB = 256
GLA_SEQS_CARRY = 2
GDN_SEGS_CARRY = 4
GDN_SEQS_CARRY = 2
GDN_SEGS_INDEP = 8
MERGE_TM = 512
MERGE_SPLIT = 2
MOE_BM = 256
MOE_SUB = 2
PACK_ROWS = D_MODEL // 2 // LANES
MOE_VMEM_LIMIT = 62 * 1024 * 1024
COMBINE_TM = 256
VMEM_LIMIT = 56 * 1024 * 1024


def _dot(a, b):
    return jnp.dot(a.astype(BF16), b.astype(BF16), preferred_element_type=F32)


def _dot_nt(a, b):
    return lax.dot_general(a.astype(BF16), b.astype(BF16), (((1,), (1,)), ((), ())),
                           preferred_element_type=F32)


def _dot_tn(a, b):
    return lax.dot_general(a, b, (((0,), (0,)), ((), ())), preferred_element_type=F32)


def _split3(x):
    hi = x.astype(BF16)
    r1 = x - hi.astype(F32)
    mid = r1.astype(BF16)
    lo = (r1 - mid.astype(F32)).astype(BF16)
    return hi, mid, lo


def _dot01(m01, x):
    hi, mid, lo = _split3(x)
    d = lambda p: jnp.dot(m01, p, preferred_element_type=F32)
    return d(hi) + d(mid) + d(lo)


def _silu(x):
    return x * jax.nn.sigmoid(x)


def _pack_bf16_pairs(x):
    bits = pltpu.bitcast(x.astype(BF16).astype(F32), jnp.uint32)
    half = D_MODEL // 2
    return (bits[:, :half] >> 16) | (bits[:, half:] & jnp.uint32(0xFFFF0000))


def _store_packed_rows(ref, lead, row0, words):
    n = words.shape[0]
    for c in range(PACK_ROWS):
        ref[(*lead, pl.ds(row0 * PACK_ROWS + c, n, stride=PACK_ROWS), slice(None))] = words[:, c * LANES:(c + 1) * LANES]


def _load_packed_rows(ref, lead, n, dtype):
    words = [ref[(*lead, pl.ds(c, n, stride=PACK_ROWS), slice(None))] for c in range(PACK_ROWS)]
    lo = [pltpu.bitcast(w << 16, F32).astype(dtype) for w in words]
    hi = [pltpu.bitcast(w & jnp.uint32(0xFFFF0000), F32).astype(dtype) for w in words]
    return jnp.concatenate(lo + hi, axis=1)


def _pick(i, n_first, first_ref, second_ref):
    return jnp.where(i < n_first, first_ref[...], second_ref[...])


def _split_specs(block, n_first):
    return (pl.BlockSpec(block, lambda i: (jnp.minimum(i, n_first - 1), 0)),
            pl.BlockSpec(block, lambda i: (jnp.maximum(i - n_first, 0), 0)))


def _inproj_kernel(xp_ref, xs_ref, nw_ref, w_ref, o_ref, *, n_first):
    x = _pick(pl.program_id(0), n_first, xp_ref, xs_ref)
    ms = jnp.mean(x * x, axis=-1, keepdims=True)
    h = (x * lax.rsqrt(ms + RMS_EPS) * nw_ref[...]).astype(BF16)
    o_ref[...] = jnp.dot(h, w_ref[...], preferred_element_type=F32)


def _inproj(x_p, x_s, norm_w, w_packed):
    tm = INPROJ_TM
    n_first = x_p.shape[0] // tm
    n = x_p.shape[0] + x_s.shape[0]
    return pl.pallas_call(
        functools.partial(_inproj_kernel, n_first=n_first),
        out_shape=jax.ShapeDtypeStruct((n, P_COLS), F32),
        grid=(n // tm,),
        in_specs=[
            *_split_specs((tm, D_MODEL), n_first),
            pl.BlockSpec((1, D_MODEL), lambda i: (0, 0)),
            pl.BlockSpec((D_MODEL, P_COLS), lambda i: (0, 0), pipeline_mode=pl.Buffered(1)),
        ],
        out_specs=pl.BlockSpec((tm, P_COLS), lambda i: (i, 0)),
        compiler_params=pltpu.CompilerParams(
            dimension_semantics=("arbitrary",), vmem_limit_bytes=VMEM_LIMIT),
        name="inproj",
    )(x_p, x_s, norm_w.reshape(1, D_MODEL), w_packed)


def _gdn_kernel(*refs, chunk, n_seg, n_seq, carry, n_steps):
    qkv_refs, z_refs, sm_refs = refs[:n_seq], refs[n_seq:2 * n_seq], refs[2 * n_seq:3 * n_seq]
    cw_ref, hp_ref, nw_ref, sconv_ref, s0_ref, o_ref, sout_ref, cout_ref, s_ref, ubuf_ref = refs[3 * n_seq:]
    c_len = chunk
    n_seg_all = n_seq * n_seg
    rows = n_seg_all * c_len
    t = pl.program_id(1)
    hist = CONV_W - 1
    base = 8
    conv_len = n_seg * c_len if carry else c_len
    n_conv = n_seq if carry else n_seg

    def init_history():
        ubuf_ref[:, 0:base, :] = jnp.zeros((n_conv, base, CONV_DIM), F32)
        ubuf_ref[:, base - hist:base, :] = sconv_ref[...]

    if carry:
        @pl.when(t == 0)
        def _():
            s_ref[...] = s0_ref[...]
            init_history()

        @pl.when(t > 0)
        def _():
            ubuf_ref[:, base - hist:base, :] = ubuf_ref[:, base + conv_len - hist:base + conv_len, :]
        for q in range(n_seq):
            ubuf_ref[q, base:base + conv_len, :] = qkv_refs[q][...]
    else:
        init_history()
        ubuf_ref[:, base:base + conv_len, :] = qkv_refs[0][...].reshape(n_conv, conv_len, CONV_DIM)

    cw = cw_ref[...]
    u = ubuf_ref[...]
    u1 = pltpu.roll(u, 1, axis=1)
    v = u1 * cw[0:1, :] + u * cw[1:2, :]
    y = (pltpu.roll(v, 2, axis=1) + u1 * cw[2:3, :]) + u * cw[3:4, :]
    y = _silu(y[:, base:, :]).reshape(rows, CONV_DIM)

    sm = sm_refs[0][...] if n_seq == 1 else jnp.concatenate([r[...] for r in sm_refs], axis=0)
    hp = hp_ref[...]
    lane = lax.broadcasted_iota(jnp.int32, (rows, LANES), 1)
    g_all = -jnp.exp(hp[0:1, :]) * jax.nn.softplus(sm + hp[1:2, :])
    g_all = jnp.where(lane < N_HEADS, g_all, 0.0)
    beta_all = jax.nn.sigmoid(sm)

    sh = int(round(math.log2(c_len)))
    r_b = lax.broadcasted_iota(jnp.int32, (rows, rows), 0)
    c_b = lax.broadcasted_iota(jnp.int32, (rows, rows), 1)
    seg_tril = ((r_b >> sh) == (c_b >> sh)) & (c_b <= r_b)
    gc = _dot01(jnp.where(seg_tril, 1.0, 0.0).astype(BF16), g_all)

    r_i = lax.broadcasted_iota(jnp.int32, (c_len, c_len), 0)
    c_i = lax.broadcasted_iota(jnp.int32, (c_len, c_len), 1)
    tri = c_i <= r_i
    strict = c_i < r_i
    eye = c_i == r_i
    nw = nw_ref[...]
    levels = sh

    qn, kn, kb, rhs, qg, gcol = [], [], [], [], [], []
    for h in range(N_HEADS):
        qh = y[:, h * DK_A:(h + 1) * DK_A]
        kh = y[:, W_A + h * DK_A:W_A + (h + 1) * DK_A]
        vh = y[:, 2 * W_A + h * DV_A:2 * W_A + (h + 1) * DV_A]
        qn_h = qh * lax.rsqrt(jnp.sum(qh * qh, axis=-1, keepdims=True) + 1e-6) * (DK_A ** -0.5)
        kn_h = kh * lax.rsqrt(jnp.sum(kh * kh, axis=-1, keepdims=True) + 1e-6)
        gcol_h = gc[:, h:h + 1]
        eg = jnp.exp(gcol_h)
        bcol = beta_all[:, N_HEADS + h:N_HEADS + h + 1]
        kb_h = kn_h * bcol
        qn.append(qn_h)
        kn.append(kn_h)
        kb.append(kb_h)
        rhs.append(jnp.concatenate([vh * bcol, kb_h * eg], axis=1))
        qg.append(qn_h * eg)
        gcol.append(gcol_h)

    pairs = [(g, h) for g in range(n_seg_all) for h in range(N_HEADS)]
    sl = lambda g: slice(g * c_len, (g + 1) * c_len)

    decay, kd, gl = {}, {}, {}
    for (g, h) in pairs:
        gcol_p = gcol[h][sl(g)]
        gcb = jnp.broadcast_to(gcol_p, (c_len, c_len))
        grow = jnp.sum(jnp.where(eye, gcb, 0.0), axis=0, keepdims=True)
        decay[g, h] = jnp.where(tri, jnp.exp(jnp.where(tri, gcb - grow, 0.0)), 0.0)
        glast = gcol[h][(g + 1) * c_len - 1:(g + 1) * c_len]
        kd[g, h] = kn[h][sl(g)] * jnp.exp(glast - gcol_p)
        gl[g, h] = jnp.exp(glast)

    kq = {p: _dot_nt(jnp.concatenate([kb[p[1]][sl(p[0])], qn[p[1]][sl(p[0])]], axis=0), kn[p[1]][sl(p[0])])
          for p in pairs}
    low = {p: jnp.where(strict, kq[p][:c_len] * decay[p], 0.0) for p in pairs}
    attn = {p: jnp.where(tri, kq[p][c_len:] * decay[p], 0.0) for p in pairs}
    ymat = {p: -low[p] for p in pairs}
    lp = low
    for _ in range(levels - 1):
        lp = {p: _dot(lp[p], lp[p]) for p in pairs}
        prod = {p: _dot(ymat[p], lp[p]) for p in pairs}
        ymat = {p: ymat[p] + lp[p] + prod[p] for p in pairs}
    sol = {}
    for (g, h) in pairs:
        rhs_p = rhs[h][sl(g)]
        sol[g, h] = rhs_p + _dot(ymat[g, h], rhs_p)

    def finish(g, h, o):
        r = o * lax.rsqrt(jnp.mean(o * o, axis=-1, keepdims=True) + RMS_EPS) * nw
        q, loc = (g // n_seg, sl(g % n_seg)) if carry else (0, sl(g))
        gated = r * _silu(z_refs[q][loc, h * DV_A:(h + 1) * DV_A])
        if carry:
            o_ref[q, loc, h * DV_A:(h + 1) * DV_A] = gated
        else:
            o_ref[loc, h * DV_A:(h + 1) * DV_A] = gated

    def advance(group, state):
        ws = {p: _dot(jnp.concatenate([sol[p][:, DV_A:], qg[p[1]][sl(p[0])]], axis=0), state[p]) for p in group}
        v_new = {p: sol[p][:, :DV_A] - ws[p][:c_len] for p in group}
        o = {p: ws[p][c_len:] + _dot(attn[p], v_new[p]) for p in group}
        new = {p: state[p] * gl[p] + _dot_tn(kd[p], v_new[p]) for p in group}
        for p in group:
            finish(p[0], p[1], o[p])
        return new

    if carry:
        chains = [(q, h) for q in range(n_seq) for h in range(N_HEADS)]
        cur = {c: s_ref[c[0], c[1]] for c in chains}
        for g in range(n_seg):
            group = [(q * n_seg + g, h) for (q, h) in chains]
            new = advance(group, {(q * n_seg + g, h): cur[q, h] for (q, h) in chains})
            cur = {(q, h): new[q * n_seg + g, h] for (q, h) in chains}
        for (q, h) in chains:
            s_ref[q, h] = cur[q, h]

        @pl.when(t == n_steps - 1)
        def _():
            for (q, h) in chains:
                sout_ref[q, h] = cur[q, h]
            cout_ref[...] = ubuf_ref[:, base + conv_len - hist:base + conv_len, :]
    else:
        new = advance(pairs, {p: s0_ref[p[0], p[1]] for p in pairs})
        for p in pairs:
            sout_ref[p[0], p[1]] = new[p]
        cout_ref[...] = ubuf_ref[:, base + conv_len - hist:base + conv_len, :]


def _gdn(p, row0, batch, seq, conv_w, hp, norm_w, s_conv, s0):
    chunk = math.gcd(seq, GDN_CHUNK)
    n_chunks = seq // chunk
    carry = n_chunks > 1
    if carry:
        n_seg = math.gcd(n_chunks, GDN_SEGS_CARRY)
        n_seq = math.gcd(batch, GDN_SEQS_CARRY)
        grid = (batch // n_seq, n_chunks // n_seg)
        n_conv, conv_len = n_seq, n_seg * chunk
    else:
        n_seg = math.gcd(batch, GDN_SEGS_INDEP)
        n_seq = 1
        grid = (batch // n_seg, 1)
        n_conv, conv_len = n_seg, chunk
    rows = n_seg * chunk
    n_steps = grid[1]
    blk0 = row0 // rows
    rowblk = lambda q: (lambda b, t: blk0 + (b * n_seq + q) * n_steps + t)
    pcol = lambda width, off: [pl.BlockSpec((rows, width), lambda b, t, rb=rowblk(q): (rb(b, t), off // width))
                               for q in range(n_seq)]
    kern = functools.partial(_gdn_kernel, chunk=chunk, n_seg=n_seg, n_seq=n_seq, carry=carry, n_steps=n_steps)
    if carry:
        o_shape, o_spec = (batch, seq, W_A), pl.BlockSpec((n_seq, rows, W_A), lambda b, t: (b, t, 0))
    else:
        o_shape, o_spec = (batch * seq, W_A), pl.BlockSpec((rows, W_A), lambda b, t: (b, 0))
    o, s_new, c_new = pl.pallas_call(
        kern,
        out_shape=(jax.ShapeDtypeStruct(o_shape, F32),
                   jax.ShapeDtypeStruct((batch, N_HEADS, DK_A, DV_A), F32),
                   jax.ShapeDtypeStruct((batch, CONV_W - 1, CONV_DIM), F32)),
        grid=grid,
        in_specs=[
            *pcol(CONV_DIM, OFF_QKV), *pcol(W_A, OFF_Z), *pcol(LANES, OFF_SMALL),
            pl.BlockSpec((CONV_W, CONV_DIM), lambda b, t: (0, 0)),
            pl.BlockSpec((8, LANES), lambda b, t: (0, 0)),
            pl.BlockSpec((1, DV_A), lambda b, t: (0, 0)),
            pl.BlockSpec((n_conv, CONV_W - 1, CONV_DIM), lambda b, t: (b, 0, 0)),
            pl.BlockSpec((n_conv, N_HEADS, DK_A, DV_A), lambda b, t: (b, 0, 0, 0)),
        ],
        out_specs=(
            o_spec,
            pl.BlockSpec((n_conv, N_HEADS, DK_A, DV_A), lambda b, t: (b, 0, 0, 0)),
            pl.BlockSpec((n_conv, CONV_W - 1, CONV_DIM), lambda b, t: (b, 0, 0)),
        ),
        scratch_shapes=[pltpu.VMEM((n_seq, N_HEADS, DK_A, DV_A), F32),
                        pltpu.VMEM((n_conv, conv_len + 8, CONV_DIM), F32)],
        compiler_params=pltpu.CompilerParams(
            dimension_semantics=("arbitrary", "arbitrary"), vmem_limit_bytes=VMEM_LIMIT),
        name=f"gdn_c{chunk}",
    )(*([p] * (3 * n_seq)), conv_w, hp, norm_w.reshape(1, DV_A), s_conv, s0)
    return o.reshape(batch * seq, W_A), s_new, c_new


def _gla_kernel(*refs, tb, cs, n_seq, carry, n_steps):
    qk_refs, v_refs, g_refs, sm_refs = (refs[k * n_seq:(k + 1) * n_seq] for k in range(4))
    wgk_ref, bgk_ref, nw_ref, s0_ref, o_ref, sout_ref, s_ref = refs[4 * n_seq:]
    t = pl.program_id(1)
    n_sub = tb // cs
    w_kb = N_HEADS * DK_B
    seqs = range(n_seq)
    heads = range(N_HEADS)

    if carry:
        @pl.when(t == 0)
        def _():
            s_ref[...] = s0_ref[...]

    sh = int(round(math.log2(cs)))
    r_i = lax.broadcasted_iota(jnp.int32, (tb, tb), 0)
    c_i = lax.broadcasted_iota(jnp.int32, (tb, tb), 1)
    same = (r_i >> sh) == (c_i >> sh)
    mask = same & (c_i <= r_i)
    mask_bf = jnp.where(mask, 1.0, 0.0).astype(BF16)
    same_bf = jnp.where(same, 1.0, 0.0).astype(BF16)
    onehot = jnp.where((lax.broadcasted_iota(jnp.int32, (tb, n_sub), 0) >> sh)
                       == lax.broadcasted_iota(jnp.int32, (tb, n_sub), 1), 1.0, 0.0).astype(BF16)
    tn = lambda part: lax.dot_general(part, onehot, (((0,), (0,)), ((), ())), preferred_element_type=F32)

    log_a = [jax.nn.log_sigmoid(_dot(sm_refs[q][...], wgk_ref[...]) + bgk_ref[...]) / GLA_GATE_NORM for q in seqs]
    gcum = [_dot01(mask_bf, la) for la in log_a]
    gtot = [_dot01(same_bf, la) for la in log_a]
    glcol = []
    for la in log_a:
        hi, mid, lo = _split3(la)
        glcol.append(jnp.exp(tn(hi) + tn(mid) + tn(lo)))

    nw = nw_ref[...]
    sl = lambda j: slice(j * cs, (j + 1) * cs)
    qg, kg, kd, vh = {}, {}, {}, {}
    for q in seqs:
        qk = qk_refs[q][...]
        for h in heads:
            qh = qk[:, h * DK_B:(h + 1) * DK_B] * (DK_B ** -0.5)
            kh = qk[:, w_kb + h * DK_B:w_kb + (h + 1) * DK_B]
            gh = gcum[q][:, h * DK_B:(h + 1) * DK_B]
            gt = gtot[q][:, h * DK_B:(h + 1) * DK_B]
            qg[q, h] = qh * jnp.exp(gh)
            kg[q, h] = kh * jnp.exp(-gh)
            kd[q, h] = kh * jnp.exp(gt - gh)
            vh[q, h] = v_refs[q][:, h * DV_B:(h + 1) * DV_B]
    chains = [(q, h) for q in seqs for h in heads]
    attn = {c: jnp.where(mask, _dot_nt(qg[c], kg[c]), 0.0) for c in chains}
    o_intra = {c: _dot(attn[c], vh[c]) for c in chains}
    upd = {(j, c): _dot_tn(kd[c][sl(j)], vh[c][sl(j)]) for j in range(n_sub) for c in chains}
    gl = lambda j, c: glcol[c[0]][c[1] * DK_B:(c[1] + 1) * DK_B, j:j + 1]

    o_inter = {c: [] for c in chains}
    if carry:
        cur = {c: s_ref[c[0], c[1]] for c in chains}
        for j in range(n_sub):
            for c in chains:
                o_inter[c].append(_dot(qg[c][sl(j)], cur[c]))
                cur[c] = cur[c] * gl(j, c) + upd[j, c]
        for c in chains:
            s_ref[c[0], c[1]] = cur[c]

        @pl.when(t == n_steps - 1)
        def _():
            for c in chains:
                sout_ref[c[0], c[1]] = cur[c]
    else:
        for j in range(n_sub):
            for c in chains:
                s = s0_ref[j, c[1]]
                o_inter[c].append(_dot(qg[c][sl(j)], s))
                sout_ref[j, c[1]] = s * gl(j, c) + upd[j, c]

    for (q, h) in chains:
        o = o_intra[q, h] + jnp.concatenate(o_inter[q, h], axis=0)
        r = o * lax.rsqrt(jnp.mean(o * o, axis=-1, keepdims=True) + RMS_EPS) * nw
        gated = r * _silu(g_refs[q][:, h * DV_B:(h + 1) * DV_B])
        if carry:
            o_ref[q, :, h * DV_B:(h + 1) * DV_B] = gated
        else:
            o_ref[:, h * DV_B:(h + 1) * DV_B] = gated


def _gla(p, row0, batch, seq, wgk, bgk, norm_w, s0):
    cs = math.gcd(seq, GLA_CHUNK)
    carry = seq > cs
    if carry:
        tb = math.gcd(seq, GLA_TB)
        n_seq = math.gcd(batch, GLA_SEQS_CARRY)
        grid = (batch // n_seq, seq // tb)
        n_state = n_seq
    else:
        tb = math.gcd(batch * seq, GLA_TB)
        n_seq = 1
        n_state = tb // cs
        grid = (batch // n_state, 1)
    n_steps = grid[1]
    blk0 = row0 // tb
    rowblk = lambda q: (lambda b, t: blk0 + (b * n_seq + q) * n_steps + t)
    pcol = lambda width, off: [pl.BlockSpec((tb, width), lambda b, t, rb=rowblk(q): (rb(b, t), off // width))
                               for q in range(n_seq)]
    kern = functools.partial(_gla_kernel, tb=tb, cs=cs, n_seq=n_seq, carry=carry, n_steps=n_steps)
    if carry:
        o_shape, o_spec = (batch, seq, W_B), pl.BlockSpec((n_seq, tb, W_B), lambda b, t: (b, t, 0))
    else:
        o_shape, o_spec = (batch * seq, W_B), pl.BlockSpec((tb, W_B), lambda b, t: (b, 0))
    o, s_new = pl.pallas_call(
        kern,
        out_shape=(jax.ShapeDtypeStruct(o_shape, F32),
                   jax.ShapeDtypeStruct((batch, N_HEADS, DK_B, DV_B), F32)),
        grid=grid,
        in_specs=[
            *pcol(2 * N_HEADS * DK_B, OFF_QK_B), *pcol(W_B, OFF_V_B), *pcol(W_B, OFF_G_B), *pcol(LANES, OFF_SMALL),
            pl.BlockSpec((LANES, N_HEADS * DK_B), lambda b, t: (0, 0)),
            pl.BlockSpec((1, N_HEADS * DK_B), lambda b, t: (0, 0)),
            pl.BlockSpec((1, DV_B), lambda b, t: (0, 0)),
            pl.BlockSpec((n_state, N_HEADS, DK_B, DV_B), lambda b, t: (b, 0, 0, 0)),
        ],
        out_specs=(
            o_spec,
            pl.BlockSpec((n_state, N_HEADS, DK_B, DV_B), lambda b, t: (b, 0, 0, 0)),
        ),
        scratch_shapes=[pltpu.VMEM((n_seq, N_HEADS, DK_B, DV_B), F32)],
        compiler_params=pltpu.CompilerParams(
            dimension_semantics=("arbitrary", "arbitrary"), vmem_limit_bytes=VMEM_LIMIT),
        name=f"gla_c{cs}",
    )(*([p] * (4 * n_seq)), wgk, bgk.reshape(1, N_HEADS * DK_B), norm_w.reshape(1, DV_B), s0)
    return o.reshape(batch * seq, W_B), s_new


def _topk_softmax(logits):
    tm = logits.shape[0]
    lane = lax.broadcasted_iota(jnp.int32, (tm, LANES), 1)
    lane_f = lane.astype(F32)
    neg = jnp.float32(-jnp.inf)
    cur = jnp.where(lane < N_EXPERTS, logits, neg)
    vals, idxs = [], []
    for _ in range(TOP_K):
        m = jnp.max(cur, axis=-1, keepdims=True)
        idx = jnp.min(jnp.where(cur == m, lane_f, float(LANES)), axis=-1, keepdims=True)
        vals.append(m)
        idxs.append(idx)
        cur = jnp.where(lane_f == idx, neg, cur)
    exps = [jnp.exp(v - vals[0]) for v in vals]
    den = exps[0] + exps[1] + exps[2] + exps[3]
    te = jnp.zeros((tm, LANES), F32)
    tg = jnp.zeros((tm, LANES), F32)
    for k in range(TOP_K):
        te = jnp.where(lane == k, idxs[k], te)
        tg = jnp.where(lane == k, exps[k] / den, tg)
    return te.astype(jnp.int32), tg


def _merge_kernel(xp_ref, xs_ref, oap_ref, oas_ref, obp_ref, obs_ref, ga_ref, gb_ref, wua_ref, wub_ref, wo_ref,
                  n2_ref, wr_ref, br_ref, x1_ref, h2_ref, te_ref, tg_ref, *, n_first):
    i = pl.program_id(0)
    tm = x1_ref.shape[0]
    hm = tm // MERGE_SPLIT
    groups = [slice(a * hm, (a + 1) * hm) for a in range(MERGE_SPLIT)]
    oa = _pick(i, n_first, oap_ref, oas_ref).astype(BF16)
    ob = _pick(i, n_first, obp_ref, obs_ref).astype(BF16)
    x = _pick(i, n_first, xp_ref, xs_ref)
    ua = [jnp.dot(oa[g], wua_ref[...], preferred_element_type=F32) for g in groups]
    ub = [jnp.dot(ob[g], wub_ref[...], preferred_element_type=F32) for g in groups]
    merged = [jax.nn.sigmoid(ga_ref[g, :]) * ua[a] + jax.nn.sigmoid(gb_ref[g, :]) * ub[a]
              for a, g in enumerate(groups)]
    x1 = [x[g] + jnp.dot(merged[a].astype(BF16), wo_ref[...], preferred_element_type=F32)
          for a, g in enumerate(groups)]
    h2 = [v * lax.rsqrt(jnp.mean(v * v, axis=-1, keepdims=True) + RMS_EPS) * n2_ref[...] for v in x1]
    h_hi = [v.astype(BF16) for v in h2]
    h_lo = [(v - hi.astype(F32)).astype(BF16) for v, hi in zip(h2, h_hi)]
    l_hi = [jnp.dot(hi, wr_ref[...], preferred_element_type=F32) for hi in h_hi]
    l_lo = [jnp.dot(lo, wr_ref[:, :LANES], preferred_element_type=F32) for lo in h_lo]
    for a, g in enumerate(groups):
        x1_ref[g, :] = x1[a]
        _store_packed_rows(h2_ref, (), a * hm, _pack_bf16_pairs(h2[a]))
        logits = l_hi[a][:, :LANES] + l_hi[a][:, LANES:] + l_lo[a] + br_ref[...]
        te, tg = _topk_softmax(logits)
        te_ref[g, :] = te
        tg_ref[g, :] = tg


def _merge(x_p, x_s, oa_p, oa_s, ob_p, ob_s, p, wua, wub, wo, norm2_w, wr, br):
    n = x_p.shape[0] + x_s.shape[0]
    tm = MERGE_TM
    n_first = x_p.shape[0] // tm
    row = lambda i: (i, 0)
    const = lambda i: (0, 0)
    return pl.pallas_call(
        functools.partial(_merge_kernel, n_first=n_first),
        out_shape=(jax.ShapeDtypeStruct((n, D_MODEL), F32), jax.ShapeDtypeStruct((n * PACK_ROWS, LANES), jnp.uint32),
                   jax.ShapeDtypeStruct((n, LANES), jnp.int32), jax.ShapeDtypeStruct((n, LANES), F32)),
        grid=(n // tm,),
        in_specs=[
            *_split_specs((tm, D_MODEL), n_first),
            *_split_specs((tm, W_A), n_first),
            *_split_specs((tm, W_B), n_first),
            pl.BlockSpec((tm, D_MODEL), lambda i: (i, OFF_GATE_A // D_MODEL)),
            pl.BlockSpec((tm, D_MODEL), lambda i: (i, OFF_GATE_B // D_MODEL)),
            pl.BlockSpec((W_A, D_MODEL), const),
            pl.BlockSpec((W_B, D_MODEL), const),
            pl.BlockSpec((D_MODEL, D_MODEL), const),
            pl.BlockSpec((1, D_MODEL), const),
            pl.BlockSpec((D_MODEL, 2 * LANES), const),
            pl.BlockSpec((1, LANES), const),
        ],
        out_specs=(pl.BlockSpec((tm, D_MODEL), row), pl.BlockSpec((tm * PACK_ROWS, LANES), row),
                   pl.BlockSpec((tm, LANES), row), pl.BlockSpec((tm, LANES), row)),
        compiler_params=pltpu.CompilerParams(
            dimension_semantics=("arbitrary",), vmem_limit_bytes=VMEM_LIMIT),
        name="merge_router",
    )(x_p, x_s, oa_p, oa_s, ob_p, ob_s, p, p, wua, wub, wo, norm2_w.reshape(1, D_MODEL), wr, br)


def _moe_kernel(blk_e_ref, nxt_e_ref, n_used_ref, src_ref, dst_ref, h2_ref, wgu_hbm, bgu_ref,
                wdn_hbm, bdn_ref, y_hbm, xstage, ybuf, wgu_stage, wdn_stage, wgu_bf, wdn_bf, ssem, wsem):
    i = pl.program_id(0)
    n_used = n_used_ref[0]
    bm = MOE_BM
    pr = PACK_ROWS
    e_cur = blk_e_ref[i]

    def weight_copies(e):
        return (pltpu.make_async_copy(wgu_hbm.at[e], wgu_stage, wsem.at[0]),
                pltpu.make_async_copy(wdn_hbm.at[e], wdn_stage, wsem.at[1]))

    @pl.when((i == 0) & (n_used > 0))
    def _():
        for cp in weight_copies(e_cur):
            cp.start()

    @pl.when((i < n_used) & ((i == 0) | (e_cur != blk_e_ref[jnp.maximum(i - 1, 0)])))
    def _():
        for cp in weight_copies(e_cur):
            cp.wait()
        wgu_bf[...] = wgu_stage[...].astype(BF16)
        wdn_bf[...] = wdn_stage[...].astype(BF16)

        @pl.when(nxt_e_ref[i] != e_cur)
        def _():
            for cp in weight_copies(nxt_e_ref[i]):
                cp.start()

    def scatter_copy(row, s, r):
        return pltpu.make_async_copy(ybuf.at[s, pl.ds(r * pr, pr), :],
                                     y_hbm.at[pl.ds(pl.multiple_of(row * pr, pr), pr), :], ssem.at[s])

    @pl.when(i == 0)
    def _():
        ybuf[...] = jnp.zeros_like(ybuf)
        n_real = y_hbm.shape[0] - MOE_SUB * bm * pr
        for s in range(MOE_SUB):
            sink = pltpu.make_async_copy(ybuf.at[s], y_hbm.at[pl.ds(n_real + s * bm * pr, bm * pr), :], ssem.at[s])
            sink.start()
            sink.wait()

    @pl.when(i < n_used)
    def _():
        for s in range(MOE_SUB):
            @pl.when(i >= 1)
            def _():
                for r in range(bm):
                    scatter_copy(0, s, r).wait()

            for r in range(bm):
                tok = pl.multiple_of(src_ref[0, 0, s * bm + r] * pr, pr)
                xstage[s, pl.ds(r * pr, pr), :] = h2_ref[pl.ds(tok, pr), :]
            x = _load_packed_rows(xstage, (s,), bm, BF16)
            gu = jnp.dot(x, wgu_bf[...], preferred_element_type=F32) + bgu_ref[...]
            gate = jnp.minimum(gu[:, :D_FF], SWIGLU_LIMIT)
            up = jnp.clip(gu[:, D_FF:], -SWIGLU_LIMIT, SWIGLU_LIMIT)
            hmid = (up + 1.0) * (gate * jax.nn.sigmoid(SWIGLU_ALPHA * gate))
            y = jnp.dot(hmid.astype(BF16), wdn_bf[...], preferred_element_type=F32) + bdn_ref[...]
            _store_packed_rows(ybuf, (s,), 0, _pack_bf16_pairs(y))
            for r in range(bm):
                scatter_copy(dst_ref[0, 0, s * bm + r], s, r).start(priority=r % 2)

    @pl.when(i == n_used - 1)
    def _():
        for s in range(MOE_SUB):
            for r in range(bm):
                scatter_copy(0, s, r).wait()


def _moe(blk_e, nxt_e, n_used, src_tok, dst_row, h2, wgu, bgu, wdn, bdn, y_rows):
    n_steps = src_tok.shape[0]
    step_rows = MOE_SUB * MOE_BM
    cur = lambda i, be, ne, nu: (i, 0, 0)
    wsel = lambda i, be, ne, nu: (be[i], 0, 0)
    smem_blk = lambda im: pl.BlockSpec((1, 1, step_rows), im, memory_space=pltpu.SMEM)
    grid_spec = pltpu.PrefetchScalarGridSpec(
        num_scalar_prefetch=3,
        grid=(n_steps,),
        in_specs=[
            smem_blk(cur), smem_blk(cur),
            pl.BlockSpec(h2.shape, lambda i, be, ne, nu: (0, 0), pipeline_mode=pl.Buffered(1)),
            pl.BlockSpec(memory_space=pl.ANY),
            pl.BlockSpec((None, 1, 2 * D_FF), wsel),
            pl.BlockSpec(memory_space=pl.ANY),
            pl.BlockSpec((None, 1, D_MODEL), wsel),
        ],
        out_specs=pl.BlockSpec(memory_space=pl.ANY),
        scratch_shapes=[pltpu.VMEM((MOE_SUB, MOE_BM * PACK_ROWS, LANES), jnp.uint32),
                        pltpu.VMEM((MOE_SUB, MOE_BM * PACK_ROWS, LANES), jnp.uint32),
                        pltpu.VMEM((D_MODEL, 2 * D_FF), F32), pltpu.VMEM((D_FF, D_MODEL), F32),
                        pltpu.VMEM((D_MODEL, 2 * D_FF), BF16), pltpu.VMEM((D_FF, D_MODEL), BF16),
                        pltpu.SemaphoreType.DMA((MOE_SUB,)), pltpu.SemaphoreType.DMA((2,))],
    )
    return pl.pallas_call(
        _moe_kernel,
        out_shape=jax.ShapeDtypeStruct((y_rows * PACK_ROWS, LANES), jnp.uint32),
        grid_spec=grid_spec,
        compiler_params=pltpu.CompilerParams(
            dimension_semantics=("arbitrary",), vmem_limit_bytes=MOE_VMEM_LIMIT),
        name="moe_experts",
    )(blk_e, nxt_e, n_used, src_tok, dst_row, h2, wgu, bgu, wdn, bdn)


def _route(top_e, n_tok):
    bm = MOE_SUB * MOE_BM
    n_assign = n_tok * TOP_K
    n_blocks = n_assign // bm + N_EXPERTS
    n_dummy = N_EXPERTS * bm
    flat_e = top_e.reshape(-1)
    counts = jnp.sum((flat_e[:, None] == jnp.arange(N_EXPERTS, dtype=jnp.int32)[None, :]).astype(jnp.int32), axis=0)
    padded = (counts + bm - 1) // bm * bm
    pad_end = jnp.cumsum(padded)
    d_e = jnp.arange(n_dummy, dtype=jnp.int32) // bm
    d_active = (jnp.arange(n_dummy, dtype=jnp.int32) % bm) < (padded - counts)[d_e]
    keys = jnp.concatenate([flat_e * 2, jnp.where(d_active, d_e * 2 + 1, 2 * N_EXPERTS)])
    vbits = n_assign.bit_length()
    marker = (1 << vbits) - 1
    vals = jnp.concatenate([jnp.arange(n_assign, dtype=jnp.int32), jnp.full((n_dummy,), marker, jnp.int32)])
    assign = jnp.sort(keys * (1 << vbits) + vals)[:n_blocks * bm] & marker
    rows = jnp.arange(n_blocks * bm, dtype=jnp.int32)
    valid = assign != marker
    src_tok = jnp.where(valid, assign // TOP_K, 0)
    dst_row = jnp.where(valid, (assign % TOP_K) * n_tok + assign // TOP_K, n_assign + rows % bm)
    blk_start = jnp.arange(n_blocks, dtype=jnp.int32) * bm
    blk_e = jnp.minimum(jnp.sum((pad_end[None, :] <= blk_start[:, None]).astype(jnp.int32), axis=1),
                        N_EXPERTS - 1).astype(jnp.int32)
    n_used = (pad_end[-1] // bm).astype(jnp.int32).reshape(1)
    later = (blk_e[None, :] > blk_e[:, None]) & (blk_start[None, :] < pad_end[-1])
    nxt_e = jnp.min(jnp.where(later, blk_e[None, :], N_EXPERTS), axis=1)
    nxt_e = jnp.where(nxt_e == N_EXPERTS, blk_e, nxt_e).astype(jnp.int32)
    return (blk_e, nxt_e, n_used, src_tok.reshape(n_blocks, 1, bm).astype(jnp.int32),
            dst_row.reshape(n_blocks, 1, bm).astype(jnp.int32))


def _combine_kernel(x1_ref, y0_ref, y1_ref, y2_ref, y3_ref, tg_ref, nf_ref, op_ref, os_ref, *, n_first):
    tg = tg_ref[...]
    tm = x1_ref.shape[0]
    rows = lambda y_ref: _load_packed_rows(y_ref, (), tm, F32)
    acc = rows(y0_ref) * tg[:, 0:1]
    for k, y_ref in enumerate((y1_ref, y2_ref, y3_ref), start=1):
        acc = acc + rows(y_ref) * tg[:, k:k + 1]
    x = x1_ref[...] + acc
    out = x * lax.rsqrt(jnp.mean(x * x, axis=-1, keepdims=True) + RMS_EPS) * nf_ref[...]
    i = pl.program_id(0)

    @pl.when(i < n_first)
    def _():
        op_ref[...] = out

    @pl.when(i >= n_first)
    def _():
        os_ref[...] = out


def _combine(x1, y, tg, norm_f_w, n_p):
    n = x1.shape[0]
    tm = COMBINE_TM
    n_first = n_p // tm
    row = lambda i: (i, 0)
    y_spec = lambda k: pl.BlockSpec((tm * PACK_ROWS, LANES), lambda i: (k * (n // tm) + i, 0))
    return pl.pallas_call(
        functools.partial(_combine_kernel, n_first=n_first),
        out_shape=(jax.ShapeDtypeStruct((n_p, D_MODEL), F32), jax.ShapeDtypeStruct((n - n_p, D_MODEL), F32)),
        grid=(n // tm,),
        in_specs=[pl.BlockSpec((tm, D_MODEL), row), y_spec(0), y_spec(1), y_spec(2), y_spec(3),
                  pl.BlockSpec((tm, LANES), row), pl.BlockSpec((1, D_MODEL), lambda i: (0, 0))],
        out_specs=_split_specs((tm, D_MODEL), n_first),
        compiler_params=pltpu.CompilerParams(
            dimension_semantics=("arbitrary",), vmem_limit_bytes=VMEM_LIMIT),
        name="combine_norm",
    )(x1, y, y, y, y, tg, norm_f_w.reshape(1, D_MODEL))


def _pack_w_in(w_in):
    s = np.cumsum((0,) + IN_SIZES)
    w_in = w_in.astype(BF16)
    qkv, z, al, be, qb, kb, vb, gb, gk, ga, gbt = [w_in[:, s[i]:s[i + 1]] for i in range(len(IN_SIZES))]
    small = jnp.concatenate(
        [al, be, gk, jnp.zeros((D_MODEL, LANES - 2 * N_HEADS - GK_RANK), w_in.dtype)], axis=1)
    return jnp.concatenate([qkv, z, ga, gbt, qb, kb, vb, gb, small], axis=1)


def kernel(x_prompt, x_sample, state_gdn, state_gdn_conv, state_gla, norm1_w, w_in, conv_w, a_log,
           dt_bias, gdn_norm_w, gla_gk_up, gla_gk_b, gla_norm_w, w_up_a, w_up_b, w_o, norm2_w,
           w_router, b_router, w_gate_up, b_gate_up, w_down, b_down, norm_f_w):
    bp, tp, _ = x_prompt.shape
    bs, ts, _ = x_sample.shape
    n_p, n_s = bp * tp, bs * ts
    n = n_p + n_s
    l = 0

    x_p = x_prompt.reshape(n_p, D_MODEL)
    x_s = x_sample.reshape(n_s, D_MODEL)
    p = _inproj(x_p, x_s, norm1_w[l], _pack_w_in(w_in[l]))

    hp = jnp.zeros((8, LANES), F32).at[0, :N_HEADS].set(a_log[l]).at[1, :N_HEADS].set(dt_bias[l])
    wgk = jnp.zeros((LANES, N_HEADS * DK_B), F32).at[2 * N_HEADS:2 * N_HEADS + GK_RANK].set(gla_gk_up[l]).astype(BF16)
    zeros = lambda shape: jnp.zeros(shape, F32)

    oa_p, gdn_p, conv_p = _gdn(p, 0, bp, tp, conv_w[l], hp, gdn_norm_w[l],
                               zeros((bp, CONV_W - 1, CONV_DIM)), zeros((bp, N_HEADS, DK_A, DV_A)))
    oa_s, gdn_s, conv_s = _gdn(p, n_p, bs, ts, conv_w[l], hp, gdn_norm_w[l], state_gdn_conv[l], state_gdn[l])
    ob_p, gla_p = _gla(p, 0, bp, tp, wgk, gla_gk_b[l], gla_norm_w[l], zeros((bp, N_HEADS, DK_B, DV_B)))
    ob_s, gla_s = _gla(p, n_p, bs, ts, wgk, gla_gk_b[l], gla_norm_w[l], state_gla[l])

    wr = jnp.zeros((D_MODEL, LANES), F32).at[:, :N_EXPERTS].set(w_router[l])
    wr_hi = wr.astype(BF16)
    wr = jnp.concatenate([wr_hi, (wr - wr_hi.astype(F32)).astype(BF16)], axis=1)
    br = jnp.zeros((1, LANES), F32).at[0, :N_EXPERTS].set(b_router[l])
    x1, h2, te, tg = _merge(x_p, x_s, oa_p, oa_s, ob_p, ob_s, p, w_up_a[l].astype(BF16),
                            w_up_b[l].astype(BF16), w_o[l].astype(BF16), norm2_w[l], wr, br)

    blk_e, nxt_e, n_used, src_tok, dst_row = _route(te[:, :TOP_K], n)
    y = _moe(blk_e, nxt_e, n_used, src_tok, dst_row, h2, w_gate_up[l],
             b_gate_up[l].reshape(N_EXPERTS, 1, 2 * D_FF), w_down[l],
             b_down[l].reshape(N_EXPERTS, 1, D_MODEL), n * TOP_K + MOE_SUB * MOE_BM)
    out_p, out_s = _combine(x1, y, tg, norm_f_w, n_p)
    y_prompt = out_p.reshape(bp, tp, D_MODEL)
    y_sample = out_s.reshape(bs, ts, D_MODEL)
    return (y_prompt, y_sample, gdn_p[None], conv_p[None], gla_p[None], gdn_s[None], conv_s[None],
            gla_s[None])
```

```python
import functools
import math

import jax
import jax.numpy as jnp
import numpy as np
from jax import lax
from jax.experimental import pallas as pl
from jax.experimental.pallas import tpu as pltpu

F32 = jnp.float32
BF16 = jnp.bfloat16

D_MODEL = 1024
N_HEADS = 4
DK_A = 128
DV_A = 128
W_A = N_HEADS * DV_A
CONV_W = 4
CONV_DIM = 2 * N_HEADS * DK_A + W_A
GDN_CHUNK = 64
DK_B = 64
DV_B = 128
W_B = N_HEADS * DV_B
GK_RANK = 16
GLA_GATE_NORM = 16.0
GLA_CHUNK = 16
N_EXPERTS = 32
TOP_K = 4
D_FF = 1024
SWIGLU_LIMIT = 7.0
SWIGLU_ALPHA = 1.702
RMS_EPS = 1e-6
IN_SIZES = (CONV_DIM, W_A, N_HEADS, N_HEADS, N_HEADS * DK_B, N_HEADS * DK_B, W_B, W_B, GK_RANK,
            D_MODEL, D_MODEL)

LANES = 128

OFF_QKV = 0
OFF_Z = 1536
OFF_GATE_A = 2048
OFF_GATE_B = 3072
OFF_QK_B = 4096
OFF_V_B = 4608
OFF_G_B = 5120
OFF_SMALL = 5632
P_COLS = 5760

INPROJ_TM = 512
GLA_TB = 256
GLA_SEQS_CARRY = 4
GDN_SEGS_CARRY = 4
GDN_SEQS_CARRY = 2
GDN_SEGS_INDEP = 8
MERGE_TM = 512
MERGE_SPLIT = 2
MOE_BM = 256
MOE_SUB = 2
PACK_ROWS = D_MODEL // 2 // LANES
MOE_VMEM_LIMIT = 62 * 1024 * 1024
COMBINE_TM = 256
VMEM_LIMIT = 56 * 1024 * 1024


def _dot(a, b):
    return jnp.dot(a.astype(BF16), b.astype(BF16), preferred_element_type=F32)


def _dot_nt(a, b):
    return lax.dot_general(a.astype(BF16), b.astype(BF16), (((1,), (1,)), ((), ())),
                           preferred_element_type=F32)


def _dot_tn(a, b):
    return lax.dot_general(a.astype(BF16), b.astype(BF16), (((0,), (0,)), ((), ())), preferred_element_type=F32)


def _split3(x):
    hi = x.astype(BF16)
    r1 = x - hi.astype(F32)
    mid = r1.astype(BF16)
    lo = (r1 - mid.astype(F32)).astype(BF16)
    return hi, mid, lo


def _dot01(m01, x):
    hi, mid, lo = _split3(x)
    d = lambda p: jnp.dot(m01, p, preferred_element_type=F32)
    return d(hi) + d(mid) + d(lo)


def _silu(x):
    return x * jax.nn.sigmoid(x)


def _pack_bf16_pairs(x):
    bits = pltpu.bitcast(x.astype(BF16).astype(F32), jnp.uint32)
    half = D_MODEL // 2
    return (bits[:, :half] >> 16) | (bits[:, half:] & jnp.uint32(0xFFFF0000))


def _store_packed_rows(ref, lead, row0, words):
    n = words.shape[0]
    for c in range(PACK_ROWS):
        ref[(*lead, pl.ds(row0 * PACK_ROWS + c, n, stride=PACK_ROWS), slice(None))] = words[:, c * LANES:(c + 1) * LANES]


def _load_packed_rows(ref, lead, n, dtype):
    words = [ref[(*lead, pl.ds(c, n, stride=PACK_ROWS), slice(None))] for c in range(PACK_ROWS)]
    lo = [pltpu.bitcast(w << 16, F32).astype(dtype) for w in words]
    hi = [pltpu.bitcast(w & jnp.uint32(0xFFFF0000), F32).astype(dtype) for w in words]
    return jnp.concatenate(lo + hi, axis=1)


def _pick(i, n_first, first_ref, second_ref):
    return jnp.where(i < n_first, first_ref[...], second_ref[...])


def _split_specs(block, n_first):
    return (pl.BlockSpec(block, lambda i: (jnp.minimum(i, n_first - 1), 0)),
            pl.BlockSpec(block, lambda i: (jnp.maximum(i - n_first, 0), 0)))


def _inproj_kernel(xp_ref, xs_ref, nw_ref, w_ref, o_ref, *, n_first):
    x = _pick(pl.program_id(0), n_first, xp_ref, xs_ref)
    ms = jnp.mean(x * x, axis=-1, keepdims=True)
    h = (x * lax.rsqrt(ms + RMS_EPS) * nw_ref[...]).astype(BF16)
    o_ref[...] = jnp.dot(h, w_ref[...], preferred_element_type=F32)


def _inproj(x_p, x_s, norm_w, w_packed):
    tm = INPROJ_TM
    n_first = x_p.shape[0] // tm
    n = x_p.shape[0] + x_s.shape[0]
    return pl.pallas_call(
        functools.partial(_inproj_kernel, n_first=n_first),
        out_shape=jax.ShapeDtypeStruct((n, P_COLS), F32),
        grid=(n // tm,),
        in_specs=[
            *_split_specs((tm, D_MODEL), n_first),
            pl.BlockSpec((1, D_MODEL), lambda i: (0, 0)),
            pl.BlockSpec((D_MODEL, P_COLS), lambda i: (0, 0), pipeline_mode=pl.Buffered(1)),
        ],
        out_specs=pl.BlockSpec((tm, P_COLS), lambda i: (i, 0)),
        compiler_params=pltpu.CompilerParams(
            dimension_semantics=("arbitrary",), vmem_limit_bytes=VMEM_LIMIT),
        name="inproj",
    )(x_p, x_s, norm_w.reshape(1, D_MODEL), w_packed)


def _gdn_kernel(*refs, chunk, n_seg, n_seq, carry, n_steps):
    qkv_refs, z_refs, sm_refs = refs[:n_seq], refs[n_seq:2 * n_seq], refs[2 * n_seq:3 * n_seq]
    cw_ref, hp_ref, nw_ref, sconv_ref, s0_ref, o_ref, sout_ref, cout_ref, s_ref, ubuf_ref = refs[3 * n_seq:]
    c_len = chunk
    n_seg_all = n_seq * n_seg
    rows = n_seg_all * c_len
    t = pl.program_id(1)
    hist = CONV_W - 1
    base = 8
    conv_len = n_seg * c_len if carry else c_len
    n_conv = n_seq if carry else n_seg

    def init_history():
        ubuf_ref[:, 0:base, :] = jnp.zeros((n_conv, base, CONV_DIM), F32)
        ubuf_ref[:, base - hist:base, :] = sconv_ref[...]

    if carry:
        @pl.when(t == 0)
        def _():
            s_ref[...] = s0_ref[...]
            init_history()

        @pl.when(t > 0)
        def _():
            ubuf_ref[:, base - hist:base, :] = ubuf_ref[:, base + conv_len - hist:base + conv_len, :]
        for q in range(n_seq):
            ubuf_ref[q, base:base + conv_len, :] = qkv_refs[q][...]
    else:
        init_history()
        ubuf_ref[:, base:base + conv_len, :] = qkv_refs[0][...].reshape(n_conv, conv_len, CONV_DIM)

    cw = cw_ref[...]
    u = ubuf_ref[...]
    u1 = pltpu.roll(u, 1, axis=1)
    v = u1 * cw[0:1, :] + u * cw[1:2, :]
    y = (pltpu.roll(v, 2, axis=1) + u1 * cw[2:3, :]) + u * cw[3:4, :]
    y = _silu(y[:, base:, :]).reshape(rows, CONV_DIM)

    sm = sm_refs[0][...] if n_seq == 1 else jnp.concatenate([r[...] for r in sm_refs], axis=0)
    hp = hp_ref[...]
    lane = lax.broadcasted_iota(jnp.int32, (rows, LANES), 1)
    g_all = -jnp.exp(hp[0:1, :]) * jax.nn.softplus(sm + hp[1:2, :])
    g_all = jnp.where(lane < N_HEADS, g_all, 0.0)
    beta_all = jax.nn.sigmoid(sm)

    sh = int(round(math.log2(c_len)))
    r_b = lax.broadcasted_iota(jnp.int32, (rows, rows), 0)
    c_b = lax.broadcasted_iota(jnp.int32, (rows, rows), 1)
    seg_tril = ((r_b >> sh) == (c_b >> sh)) & (c_b <= r_b)
    gc = _dot01(jnp.where(seg_tril, 1.0, 0.0).astype(BF16), g_all)

    r_i = lax.broadcasted_iota(jnp.int32, (c_len, c_len), 0)
    c_i = lax.broadcasted_iota(jnp.int32, (c_len, c_len), 1)
    tri = c_i <= r_i
    strict = c_i < r_i
    eye = c_i == r_i
    nw = nw_ref[...]
    levels = sh

    qn, kn, kb, rhs, qg, gcol = [], [], [], [], [], []
    for h in range(N_HEADS):
        qh = y[:, h * DK_A:(h + 1) * DK_A]
        kh = y[:, W_A + h * DK_A:W_A + (h + 1) * DK_A]
        vh = y[:, 2 * W_A + h * DV_A:2 * W_A + (h + 1) * DV_A]
        qn_h = qh * lax.rsqrt(jnp.sum(qh * qh, axis=-1, keepdims=True) + 1e-6) * (DK_A ** -0.5)
        kn_h = kh * lax.rsqrt(jnp.sum(kh * kh, axis=-1, keepdims=True) + 1e-6)
        gcol_h = gc[:, h:h + 1]
        eg = jnp.exp(gcol_h)
        bcol = beta_all[:, N_HEADS + h:N_HEADS + h + 1]
        kb_h = kn_h * bcol
        qn.append(qn_h)
        kn.append(kn_h)
        kb.append(kb_h)
        rhs.append(jnp.concatenate([vh * bcol, kb_h * eg], axis=1))
        qg.append(qn_h * eg)
        gcol.append(gcol_h)

    pairs = [(g, h) for g in range(n_seg_all) for h in range(N_HEADS)]
    sl = lambda g: slice(g * c_len, (g + 1) * c_len)

    decay, kd, gl = {}, {}, {}
    for (g, h) in pairs:
        gcol_p = gcol[h][sl(g)]
        gcb = jnp.broadcast_to(gcol_p, (c_len, c_len))
        grow = jnp.sum(jnp.where(eye, gcb, 0.0), axis=0, keepdims=True)
        decay[g, h] = jnp.where(tri, jnp.exp(jnp.where(tri, gcb - grow, 0.0)), 0.0)
        glast = gcol[h][(g + 1) * c_len - 1:(g + 1) * c_len]
        kd[g, h] = kn[h][sl(g)] * jnp.exp(glast - gcol_p)
        gl[g, h] = jnp.exp(glast)

    kq = {p: _dot_nt(jnp.concatenate([kb[p[1]][sl(p[0])], qn[p[1]][sl(p[0])]], axis=0), kn[p[1]][sl(p[0])])
          for p in pairs}
    low = {p: jnp.where(strict, kq[p][:c_len] * decay[p], 0.0) for p in pairs}
    attn = {p: jnp.where(tri, kq[p][c_len:] * decay[p], 0.0) for p in pairs}
    ymat = {p: -low[p] for p in pairs}
    lp = low
    for _ in range(levels - 1):
        lp = {p: _dot(lp[p], lp[p]) for p in pairs}
        prod = {p: _dot(ymat[p], lp[p]) for p in pairs}
        ymat = {p: ymat[p] + lp[p] + prod[p] for p in pairs}
    sol = {}
    for (g, h) in pairs:
        rhs_p = rhs[h][sl(g)]
        sol[g, h] = rhs_p + _dot(ymat[g, h], rhs_p)

    def finish(g, h, o):
        r = o * lax.rsqrt(jnp.mean(o * o, axis=-1, keepdims=True) + RMS_EPS) * nw
        q, loc = (g // n_seg, sl(g % n_seg)) if carry else (0, sl(g))
        gated = r * _silu(z_refs[q][loc, h * DV_A:(h + 1) * DV_A])
        if carry:
            o_ref[q, loc, h * DV_A:(h + 1) * DV_A] = gated
        else:
            o_ref[loc, h * DV_A:(h + 1) * DV_A] = gated

    def advance(group, state):
        ws = {p: _dot(jnp.concatenate([sol[p][:, DV_A:], qg[p[1]][sl(p[0])]], axis=0), state[p]) for p in group}
        v_new = {p: sol[p][:, :DV_A] - ws[p][:c_len] for p in group}
        o = {p: ws[p][c_len:] + _dot(attn[p], v_new[p]) for p in group}
        new = {p: state[p] * gl[p] + _dot_tn(kd[p], v_new[p]) for p in group}
        for p in group:
            finish(p[0], p[1], o[p])
        return new

    if carry:
        chains = [(q, h) for q in range(n_seq) for h in range(N_HEADS)]
        cur = {c: s_ref[c[0], c[1]] for c in chains}
        for g in range(n_seg):
            group = [(q * n_seg + g, h) for (q, h) in chains]
            new = advance(group, {(q * n_seg + g, h): cur[q, h] for (q, h) in chains})
            cur = {(q, h): new[q * n_seg + g, h] for (q, h) in chains}
        for (q, h) in chains:
            s_ref[q, h] = cur[q, h]

        @pl.when(t == n_steps - 1)
        def _():
            for (q, h) in chains:
                sout_ref[q, h] = cur[q, h]
            cout_ref[...] = ubuf_ref[:, base + conv_len - hist:base + conv_len, :]
    else:
        new = advance(pairs, {p: s0_ref[p[0], p[1]] for p in pairs})
        for p in pairs:
            sout_ref[p[0], p[1]] = new[p]
        cout_ref[...] = ubuf_ref[:, base + conv_len - hist:base + conv_len, :]


def _gdn(p, row0, batch, seq, conv_w, hp, norm_w, s_conv, s0):
    chunk = math.gcd(seq, GDN_CHUNK)
    n_chunks = seq // chunk
    carry = n_chunks > 1
    if carry:
        n_seg = math.gcd(n_chunks, GDN_SEGS_CARRY)
        n_seq = math.gcd(batch, GDN_SEQS_CARRY)
        grid = (batch // n_seq, n_chunks // n_seg)
        n_conv, conv_len = n_seq, n_seg * chunk
    else:
        n_seg = math.gcd(batch, GDN_SEGS_INDEP)
        n_seq = 1
        grid = (batch // n_seg, 1)
        n_conv, conv_len = n_seg, chunk
    rows = n_seg * chunk
    n_steps = grid[1]
    blk0 = row0 // rows
    rowblk = lambda q: (lambda b, t: blk0 + (b * n_seq + q) * n_steps + t)
    pcol = lambda width, off: [pl.BlockSpec((rows, width), lambda b, t, rb=rowblk(q): (rb(b, t), off // width))
                               for q in range(n_seq)]
    kern = functools.partial(_gdn_kernel, chunk=chunk, n_seg=n_seg, n_seq=n_seq, carry=carry, n_steps=n_steps)
    if carry:
        o_shape, o_spec = (batch, seq, W_A), pl.BlockSpec((n_seq, rows, W_A), lambda b, t: (b, t, 0))
    else:
        o_shape, o_spec = (batch * seq, W_A), pl.BlockSpec((rows, W_A), lambda b, t: (b, 0))
    o, s_new, c_new = pl.pallas_call(
        kern,
        out_shape=(jax.ShapeDtypeStruct(o_shape, F32),
                   jax.ShapeDtypeStruct((batch, N_HEADS, DK_A, DV_A), F32),
                   jax.ShapeDtypeStruct((batch, CONV_W - 1, CONV_DIM), F32)),
        grid=grid,
        in_specs=[
            *pcol(CONV_DIM, OFF_QKV), *pcol(W_A, OFF_Z), *pcol(LANES, OFF_SMALL),
            pl.BlockSpec((CONV_W, CONV_DIM), lambda b, t: (0, 0)),
            pl.BlockSpec((8, LANES), lambda b, t: (0, 0)),
            pl.BlockSpec((1, DV_A), lambda b, t: (0, 0)),
            pl.BlockSpec((n_conv, CONV_W - 1, CONV_DIM), lambda b, t: (b, 0, 0)),
            pl.BlockSpec((n_conv, N_HEADS, DK_A, DV_A), lambda b, t: (b, 0, 0, 0)),
        ],
        out_specs=(
            o_spec,
            pl.BlockSpec((n_conv, N_HEADS, DK_A, DV_A), lambda b, t: (b, 0, 0, 0)),
            pl.BlockSpec((n_conv, CONV_W - 1, CONV_DIM), lambda b, t: (b, 0, 0)),
        ),
        scratch_shapes=[pltpu.VMEM((n_seq, N_HEADS, DK_A, DV_A), F32),
                        pltpu.VMEM((n_conv, conv_len + 8, CONV_DIM), F32)],
        compiler_params=pltpu.CompilerParams(
            dimension_semantics=("arbitrary", "arbitrary"), vmem_limit_bytes=VMEM_LIMIT),
        name=f"gdn_c{chunk}",
    )(*([p] * (3 * n_seq)), conv_w, hp, norm_w.reshape(1, DV_A), s_conv, s0)
    return o.reshape(batch * seq, W_A), s_new, c_new


def _gla_kernel(*refs, tb, cs, n_seq, carry, n_steps):
    qk_refs, v_refs, g_refs, sm_refs = (refs[k * n_seq:(k + 1) * n_seq] for k in range(4))
    wgk_ref, bgk_ref, nw_ref, s0_ref, o_ref, sout_ref, s_ref = refs[4 * n_seq:]
    t = pl.program_id(1)
    n_sub = tb // cs
    w_kb = N_HEADS * DK_B
    seqs = range(n_seq)
    heads = range(N_HEADS)

    if carry:
        @pl.when(t == 0)
        def _():
            s_ref[...] = s0_ref[...]

    sh = int(round(math.log2(cs)))
    r_i = lax.broadcasted_iota(jnp.int32, (tb, tb), 0)
    c_i = lax.broadcasted_iota(jnp.int32, (tb, tb), 1)
    same = (r_i >> sh) == (c_i >> sh)
    mask = same & (c_i <= r_i)
    mask_bf = jnp.where(mask, 1.0, 0.0).astype(BF16)
    same_bf = jnp.where(same, 1.0, 0.0).astype(BF16)
    onehot = jnp.where((lax.broadcasted_iota(jnp.int32, (tb, n_sub), 0) >> sh)
                       == lax.broadcasted_iota(jnp.int32, (tb, n_sub), 1), 1.0, 0.0).astype(BF16)
    tn = lambda part: lax.dot_general(part, onehot, (((0,), (0,)), ((), ())), preferred_element_type=F32)

    log_a = [jax.nn.log_sigmoid(_dot(sm_refs[q][...], wgk_ref[...]) + bgk_ref[...]) / GLA_GATE_NORM for q in seqs]
    gcum = [_dot01(mask_bf, la) for la in log_a]
    gtot = [_dot01(same_bf, la) for la in log_a]
    glcol = []
    for la in log_a:
        hi, mid, lo = _split3(la)
        glcol.append(jnp.exp(tn(hi) + tn(mid) + tn(lo)))

    nw = nw_ref[...]
    sl = lambda j: slice(j * cs, (j + 1) * cs)
    qg, kg, kd, vh = {}, {}, {}, {}
    for q in seqs:
        qk = qk_refs[q][...]
        for h in heads:
            qh = qk[:, h * DK_B:(h + 1) * DK_B] * (DK_B ** -0.5)
            kh = qk[:, w_kb + h * DK_B:w_kb + (h + 1) * DK_B]
            gh = gcum[q][:, h * DK_B:(h + 1) * DK_B]
            gt = gtot[q][:, h * DK_B:(h + 1) * DK_B]
            qg[q, h] = qh * jnp.exp(gh)
            kg[q, h] = kh * jnp.exp(-gh)
            kd[q, h] = kh * jnp.exp(gt - gh)
            vh[q, h] = v_refs[q][:, h * DV_B:(h + 1) * DV_B]
    chains = [(q, h) for q in seqs for h in heads]
    attn = {c: jnp.where(mask, _dot_nt(qg[c], kg[c]), 0.0) for c in chains}
    o_intra = {c: _dot(attn[c], vh[c]) for c in chains}
    upd = {(j, c): _dot_tn(kd[c][sl(j)], vh[c][sl(j)]) for j in range(n_sub) for c in chains}
    gl = lambda j, c: glcol[c[0]][c[1] * DK_B:(c[1] + 1) * DK_B, j:j + 1]

    o_inter = {c: [] for c in chains}
    if carry:
        cur = {c: s_ref[c[0], c[1]] for c in chains}
        for j in range(n_sub):
            for c in chains:
                o_inter[c].append(_dot(qg[c][sl(j)], cur[c]))
                cur[c] = cur[c] * gl(j, c) + upd[j, c]
        for c in chains:
            s_ref[c[0], c[1]] = cur[c]

        @pl.when(t == n_steps - 1)
        def _():
            for c in chains:
                sout_ref[c[0], c[1]] = cur[c]
    else:
        for j in range(n_sub):
            for c in chains:
                s = s0_ref[j, c[1]]
                o_inter[c].append(_dot(qg[c][sl(j)], s))
                sout_ref[j, c[1]] = s * gl(j, c) + upd[j, c]

    for (q, h) in chains:
        o = o_intra[q, h] + jnp.concatenate(o_inter[q, h], axis=0)
        r = o * lax.rsqrt(jnp.mean(o * o, axis=-1, keepdims=True) + RMS_EPS) * nw
        gated = r * _silu(g_refs[q][:, h * DV_B:(h + 1) * DV_B])
        if carry:
            o_ref[q, :, h * DV_B:(h + 1) * DV_B] = gated
        else:
            o_ref[:, h * DV_B:(h + 1) * DV_B] = gated


def _gla(p, row0, batch, seq, wgk, bgk, norm_w, s0):
    cs = math.gcd(seq, GLA_CHUNK)
    carry = seq > cs
    if carry:
        tb = math.gcd(seq, GLA_TB)
        n_seq = math.gcd(batch, GLA_SEQS_CARRY)
        grid = (batch // n_seq, seq // tb)
        n_state = n_seq
    else:
        tb = math.gcd(batch * seq, GLA_TB)
        n_seq = 1
        n_state = tb // cs
        grid = (batch // n_state, 1)
    n_steps = grid[1]
    blk0 = row0 // tb
    rowblk = lambda q: (lambda b, t: blk0 + (b * n_seq + q) * n_steps + t)
    pcol = lambda width, off: [pl.BlockSpec((tb, width), lambda b, t, rb=rowblk(q): (rb(b, t), off // width))
                               for q in range(n_seq)]
    kern = functools.partial(_gla_kernel, tb=tb, cs=cs, n_seq=n_seq, carry=carry, n_steps=n_steps)
    if carry:
        o_shape, o_spec = (batch, seq, W_B), pl.BlockSpec((n_seq, tb, W_B), lambda b, t: (b, t, 0))
    else:
        o_shape, o_spec = (batch * seq, W_B), pl.BlockSpec((tb, W_B), lambda b, t: (b, 0))
    o, s_new = pl.pallas_call(
        kern,
        out_shape=(jax.ShapeDtypeStruct(o_shape, F32),
                   jax.ShapeDtypeStruct((batch, N_HEADS, DK_B, DV_B), F32)),
        grid=grid,
        in_specs=[
            *pcol(2 * N_HEADS * DK_B, OFF_QK_B), *pcol(W_B, OFF_V_B), *pcol(W_B, OFF_G_B), *pcol(LANES, OFF_SMALL),
            pl.BlockSpec((LANES, N_HEADS * DK_B), lambda b, t: (0, 0)),
            pl.BlockSpec((1, N_HEADS * DK_B), lambda b, t: (0, 0)),
            pl.BlockSpec((1, DV_B), lambda b, t: (0, 0)),
            pl.BlockSpec((n_state, N_HEADS, DK_B, DV_B), lambda b, t: (b, 0, 0, 0)),
        ],
        out_specs=(
            o_spec,
            pl.BlockSpec((n_state, N_HEADS, DK_B, DV_B), lambda b, t: (b, 0, 0, 0)),
        ),
        scratch_shapes=[pltpu.VMEM((n_seq, N_HEADS, DK_B, DV_B), F32)],
        compiler_params=pltpu.CompilerParams(
            dimension_semantics=("arbitrary", "arbitrary"), vmem_limit_bytes=VMEM_LIMIT),
        name=f"gla_c{cs}",
    )(*([p] * (4 * n_seq)), wgk, bgk.reshape(1, N_HEADS * DK_B), norm_w.reshape(1, DV_B), s0)
    return o.reshape(batch * seq, W_B), s_new


def _topk_softmax(logits):
    tm = logits.shape[0]
    lane = lax.broadcasted_iota(jnp.int32, (tm, LANES), 1)
    lane_f = lane.astype(F32)
    neg = jnp.float32(-jnp.inf)
    cur = jnp.where(lane < N_EXPERTS, logits, neg)
    vals, idxs = [], []
    for _ in range(TOP_K):
        m = jnp.max(cur, axis=-1, keepdims=True)
        idx = jnp.min(jnp.where(cur == m, lane_f, float(LANES)), axis=-1, keepdims=True)
        vals.append(m)
        idxs.append(idx)
        cur = jnp.where(lane_f == idx, neg, cur)
    exps = [jnp.exp(v - vals[0]) for v in vals]
    den = exps[0] + exps[1] + exps[2] + exps[3]
    te = jnp.zeros((tm, LANES), F32)
    tg = jnp.zeros((tm, LANES), F32)
    for k in range(TOP_K):
        te = jnp.where(lane == k, idxs[k], te)
        tg = jnp.where(lane == k, exps[k] / den, tg)
    return te.astype(jnp.int32), tg


def _merge_kernel(xp_ref, xs_ref, oap_ref, oas_ref, obp_ref, obs_ref, ga_ref, gb_ref, wua_ref, wub_ref, wo_ref,
                  n2_ref, wr_ref, br_ref, x1_ref, h2_ref, te_ref, tg_ref, *, n_first):
    i = pl.program_id(0)
    tm = x1_ref.shape[0]
    hm = tm // MERGE_SPLIT
    groups = [slice(a * hm, (a + 1) * hm) for a in range(MERGE_SPLIT)]
    oa = _pick(i, n_first, oap_ref, oas_ref).astype(BF16)
    ob = _pick(i, n_first, obp_ref, obs_ref).astype(BF16)
    x = _pick(i, n_first, xp_ref, xs_ref)
    ua = [jnp.dot(oa[g], wua_ref[...], preferred_element_type=F32) for g in groups]
    ub = [jnp.dot(ob[g], wub_ref[...], preferred_element_type=F32) for g in groups]
    merged = [jax.nn.sigmoid(ga_ref[g, :]) * ua[a] + jax.nn.sigmoid(gb_ref[g, :]) * ub[a]
              for a, g in enumerate(groups)]
    x1 = [x[g] + jnp.dot(merged[a].astype(BF16), wo_ref[...], preferred_element_type=F32)
          for a, g in enumerate(groups)]
    h2 = [v * lax.rsqrt(jnp.mean(v * v, axis=-1, keepdims=True) + RMS_EPS) * n2_ref[...] for v in x1]
    h_hi = [v.astype(BF16) for v in h2]
    h_lo = [(v - hi.astype(F32)).astype(BF16) for v, hi in zip(h2, h_hi)]
    l_hi = [jnp.dot(hi, wr_ref[...], preferred_element_type=F32) for hi in h_hi]
    l_lo = [jnp.dot(lo, wr_ref[:, :LANES], preferred_element_type=F32) for lo in h_lo]
    for a, g in enumerate(groups):
        x1_ref[g, :] = x1[a]
        _store_packed_rows(h2_ref, (), a * hm, _pack_bf16_pairs(h2[a]))
        logits = l_hi[a][:, :LANES] + l_hi[a][:, LANES:] + l_lo[a] + br_ref[...]
        te, tg = _topk_softmax(logits)
        te_ref[g, :] = te
        tg_ref[g, :] = tg


def _merge(x_p, x_s, oa_p, oa_s, ob_p, ob_s, p, wua, wub, wo, norm2_w, wr, br):
    n = x_p.shape[0] + x_s.shape[0]
    tm = MERGE_TM
    n_first = x_p.shape[0] // tm
    row = lambda i: (i, 0)
    const = lambda i: (0, 0)
    return pl.pallas_call(
        functools.partial(_merge_kernel, n_first=n_first),
        out_shape=(jax.ShapeDtypeStruct((n, D_MODEL), F32), jax.ShapeDtypeStruct((n * PACK_ROWS, LANES), jnp.uint32),
                   jax.ShapeDtypeStruct((n, LANES), jnp.int32), jax.ShapeDtypeStruct((n, LANES), F32)),
        grid=(n // tm,),
        in_specs=[
            *_split_specs((tm, D_MODEL), n_first),
            *_split_specs((tm, W_A), n_first),
            *_split_specs((tm, W_B), n_first),
            pl.BlockSpec((tm, D_MODEL), lambda i: (i, OFF_GATE_A // D_MODEL)),
            pl.BlockSpec((tm, D_MODEL), lambda i: (i, OFF_GATE_B // D_MODEL)),
            pl.BlockSpec((W_A, D_MODEL), const),
            pl.BlockSpec((W_B, D_MODEL), const),
            pl.BlockSpec((D_MODEL, D_MODEL), const),
            pl.BlockSpec((1, D_MODEL), const),
            pl.BlockSpec((D_MODEL, 2 * LANES), const),
            pl.BlockSpec((1, LANES), const),
        ],
        out_specs=(pl.BlockSpec((tm, D_MODEL), row), pl.BlockSpec((tm * PACK_ROWS, LANES), row),
                   pl.BlockSpec((tm, LANES), row), pl.BlockSpec((tm, LANES), row)),
        compiler_params=pltpu.CompilerParams(
            dimension_semantics=("arbitrary",), vmem_limit_bytes=VMEM_LIMIT),
        name="merge_router",
    )(x_p, x_s, oa_p, oa_s, ob_p, ob_s, p, p, wua, wub, wo, norm2_w.reshape(1, D_MODEL), wr, br)


def _moe_kernel(blk_e_ref, nxt_e_ref, n_used_ref, src_ref, dst_ref, h2_ref, wgu_hbm, bgu_ref,
                wdn_hbm, bdn_ref, y_hbm, xstage, ybuf, wgu_stage, wdn_stage, wgu_bf, wdn_bf, ssem, wsem):
    i = pl.program_id(0)
    n_used = n_used_ref[0]
    bm = MOE_BM
    pr = PACK_ROWS
    e_cur = blk_e_ref[i]

    def weight_copies(e):
        return (pltpu.make_async_copy(wgu_hbm.at[e], wgu_stage, wsem.at[0]),
                pltpu.make_async_copy(wdn_hbm.at[e], wdn_stage, wsem.at[1]))

    @pl.when((i == 0) & (n_used > 0))
    def _():
        for cp in weight_copies(e_cur):
            cp.start()

    @pl.when((i < n_used) & ((i == 0) | (e_cur != blk_e_ref[jnp.maximum(i - 1, 0)])))
    def _():
        for cp in weight_copies(e_cur):
            cp.wait()
        wgu_bf[...] = wgu_stage[...].astype(BF16)
        wdn_bf[...] = wdn_stage[...].astype(BF16)

        @pl.when(nxt_e_ref[i] != e_cur)
        def _():
            for cp in weight_copies(nxt_e_ref[i]):
                cp.start()

    def scatter_copy(row, s, r):
        return pltpu.make_async_copy(ybuf.at[s, pl.ds(r * pr, pr), :],
                                     y_hbm.at[pl.ds(pl.multiple_of(row * pr, pr), pr), :], ssem.at[s])

    @pl.when(i == 0)
    def _():
        ybuf[...] = jnp.zeros_like(ybuf)
        n_real = y_hbm.shape[0] - MOE_SUB * bm * pr
        for s in range(MOE_SUB):
            sink = pltpu.make_async_copy(ybuf.at[s], y_hbm.at[pl.ds(n_real + s * bm * pr, bm * pr), :], ssem.at[s])
            sink.start()
            sink.wait()

    @pl.when(i < n_used)
    def _():
        for s in range(MOE_SUB):
            @pl.when(i >= 1)
            def _():
                for r in range(bm):
                    scatter_copy(0, s, r).wait()

            for r in range(bm):
                tok = pl.multiple_of(src_ref[0, 0, s * bm + r] * pr, pr)
                xstage[s, pl.ds(r * pr, pr), :] = h2_ref[pl.ds(tok, pr), :]
            x = _load_packed_rows(xstage, (s,), bm, BF16)
            gu = jnp.dot(x, wgu_bf[...], preferred_element_type=F32) + bgu_ref[...]
            gate = jnp.minimum(gu[:, :D_FF], SWIGLU_LIMIT)
            up = jnp.clip(gu[:, D_FF:], -SWIGLU_LIMIT, SWIGLU_LIMIT)
            hmid = (up + 1.0) * (gate * jax.nn.sigmoid(SWIGLU_ALPHA * gate))
            y = jnp.dot(hmid.astype(BF16), wdn_bf[...], preferred_element_type=F32) + bdn_ref[...]
            _store_packed_rows(ybuf, (s,), 0, _pack_bf16_pairs(y))
            for r in range(bm):
                scatter_copy(dst_ref[0, 0, s * bm + r], s, r).start(priority=r % 2)

    @pl.when(i == n_used - 1)
    def _():
        for s in range(MOE_SUB):
            for r in range(bm):
                scatter_copy(0, s, r).wait()


def _moe(blk_e, nxt_e, n_used, src_tok, dst_row, h2, wgu, bgu, wdn, bdn, y_rows):
    n_steps = src_tok.shape[0]
    step_rows = MOE_SUB * MOE_BM
    cur = lambda i, be, ne, nu: (i, 0, 0)
    wsel = lambda i, be, ne, nu: (be[i], 0, 0)
    smem_blk = lambda im: pl.BlockSpec((1, 1, step_rows), im, memory_space=pltpu.SMEM)
    grid_spec = pltpu.PrefetchScalarGridSpec(
        num_scalar_prefetch=3,
        grid=(n_steps,),
        in_specs=[
            smem_blk(cur), smem_blk(cur),
            pl.BlockSpec(h2.shape, lambda i, be, ne, nu: (0, 0), pipeline_mode=pl.Buffered(1)),
            pl.BlockSpec(memory_space=pl.ANY),
            pl.BlockSpec((None, 1, 2 * D_FF), wsel),
            pl.BlockSpec(memory_space=pl.ANY),
            pl.BlockSpec((None, 1, D_MODEL), wsel),
        ],
        out_specs=pl.BlockSpec(memory_space=pl.ANY),
        scratch_shapes=[pltpu.VMEM((MOE_SUB, MOE_BM * PACK_ROWS, LANES), jnp.uint32),
                        pltpu.VMEM((MOE_SUB, MOE_BM * PACK_ROWS, LANES), jnp.uint32),
                        pltpu.VMEM((D_MODEL, 2 * D_FF), F32), pltpu.VMEM((D_FF, D_MODEL), F32),
                        pltpu.VMEM((D_MODEL, 2 * D_FF), BF16), pltpu.VMEM((D_FF, D_MODEL), BF16),
                        pltpu.SemaphoreType.DMA((MOE_SUB,)), pltpu.SemaphoreType.DMA((2,))],
    )
    return pl.pallas_call(
        _moe_kernel,
        out_shape=jax.ShapeDtypeStruct((y_rows * PACK_ROWS, LANES), jnp.uint32),
        grid_spec=grid_spec,
        compiler_params=pltpu.CompilerParams(
            dimension_semantics=("arbitrary",), vmem_limit_bytes=MOE_VMEM_LIMIT),
        name="moe_experts",
    )(blk_e, nxt_e, n_used, src_tok, dst_row, h2, wgu, bgu, wdn, bdn)


def _route(top_e, n_tok):
    bm = MOE_SUB * MOE_BM
    n_assign = n_tok * TOP_K
    n_blocks = n_assign // bm + N_EXPERTS
    n_dummy = N_EXPERTS * bm
    flat_e = top_e.reshape(-1)
    counts = jnp.sum((flat_e[:, None] == jnp.arange(N_EXPERTS, dtype=jnp.int32)[None, :]).astype(jnp.int32), axis=0)
    padded = (counts + bm - 1) // bm * bm
    pad_end = jnp.cumsum(padded)
    d_e = jnp.arange(n_dummy, dtype=jnp.int32) // bm
    d_active = (jnp.arange(n_dummy, dtype=jnp.int32) % bm) < (padded - counts)[d_e]
    keys = jnp.concatenate([flat_e * 2, jnp.where(d_active, d_e * 2 + 1, 2 * N_EXPERTS)])
    vbits = n_assign.bit_length()
    marker = (1 << vbits) - 1
    vals = jnp.concatenate([jnp.arange(n_assign, dtype=jnp.int32), jnp.full((n_dummy,), marker, jnp.int32)])
    assign = jnp.sort(keys * (1 << vbits) + vals)[:n_blocks * bm] & marker
    rows = jnp.arange(n_blocks * bm, dtype=jnp.int32)
    valid = assign != marker
    src_tok = jnp.where(valid, assign // TOP_K, 0)
    dst_row = jnp.where(valid, (assign % TOP_K) * n_tok + assign // TOP_K, n_assign + rows % bm)
    blk_start = jnp.arange(n_blocks, dtype=jnp.int32) * bm
    blk_e = jnp.minimum(jnp.sum((pad_end[None, :] <= blk_start[:, None]).astype(jnp.int32), axis=1),
                        N_EXPERTS - 1).astype(jnp.int32)
    n_used = (pad_end[-1] // bm).astype(jnp.int32).reshape(1)
    later = (blk_e[None, :] > blk_e[:, None]) & (blk_start[None, :] < pad_end[-1])
    nxt_e = jnp.min(jnp.where(later, blk_e[None, :], N_EXPERTS), axis=1)
    nxt_e = jnp.where(nxt_e == N_EXPERTS, blk_e, nxt_e).astype(jnp.int32)
    return (blk_e, nxt_e, n_used, src_tok.reshape(n_blocks, 1, bm).astype(jnp.int32),
            dst_row.reshape(n_blocks, 1, bm).astype(jnp.int32))


def _combine_kernel(x1_ref, y0_ref, y1_ref, y2_ref, y3_ref, tg_ref, nf_ref, op_ref, os_ref, *, n_first):
    tg = tg_ref[...]
    tm = x1_ref.shape[0]
    rows = lambda y_ref: _load_packed_rows(y_ref, (), tm, F32)
    acc = rows(y0_ref) * tg[:, 0:1]
    for k, y_ref in enumerate((y1_ref, y2_ref, y3_ref), start=1):
        acc = acc + rows(y_ref) * tg[:, k:k + 1]
    x = x1_ref[...] + acc
    out = x * lax.rsqrt(jnp.mean(x * x, axis=-1, keepdims=True) + RMS_EPS) * nf_ref[...]
    i = pl.program_id(0)

    @pl.when(i < n_first)
    def _():
        op_ref[...] = out

    @pl.when(i >= n_first)
    def _():
        os_ref[...] = out


def _combine(x1, y, tg, norm_f_w, n_p):
    n = x1.shape[0]
    tm = COMBINE_TM
    n_first = n_p // tm
    row = lambda i: (i, 0)
    y_spec = lambda k: pl.BlockSpec((tm * PACK_ROWS, LANES), lambda i: (k * (n // tm) + i, 0))
    return pl.pallas_call(
        functools.partial(_combine_kernel, n_first=n_first),
        out_shape=(jax.ShapeDtypeStruct((n_p, D_MODEL), F32), jax.ShapeDtypeStruct((n - n_p, D_MODEL), F32)),
        grid=(n // tm,),
        in_specs=[pl.BlockSpec((tm, D_MODEL), row), y_spec(0), y_spec(1), y_spec(2), y_spec(3),
                  pl.BlockSpec((tm, LANES), row), pl.BlockSpec((1, D_MODEL), lambda i: (0, 0))],
        out_specs=_split_specs((tm, D_MODEL), n_first),
        compiler_params=pltpu.CompilerParams(
            dimension_semantics=("arbitrary",), vmem_limit_bytes=VMEM_LIMIT),
        name="combine_norm",
    )(x1, y, y, y, y, tg, norm_f_w.reshape(1, D_MODEL))


def _pack_w_in(w_in):
    s = np.cumsum((0,) + IN_SIZES)
    w_in = w_in.astype(BF16)
    qkv, z, al, be, qb, kb, vb, gb, gk, ga, gbt = [w_in[:, s[i]:s[i + 1]] for i in range(len(IN_SIZES))]
    small = jnp.concatenate(
        [al, be, gk, jnp.zeros((D_MODEL, LANES - 2 * N_HEADS - GK_RANK), w_in.dtype)], axis=1)
    return jnp.concatenate([qkv, z, ga, gbt, qb, kb, vb, gb, small], axis=1)


def kernel(x_prompt, x_sample, state_gdn, state_gdn_conv, state_gla, norm1_w, w_in, conv_w, a_log,
           dt_bias, gdn_norm_w, gla_gk_up, gla_gk_b, gla_norm_w, w_up_a, w_up_b, w_o, norm2_w,
           w_router, b_router, w_gate_up, b_gate_up, w_down, b_down, norm_f_w):
    bp, tp, _ = x_prompt.shape
    bs, ts, _ = x_sample.shape
    n_p, n_s = bp * tp, bs * ts
    n = n_p + n_s
    l = 0

    x_p = x_prompt.reshape(n_p, D_MODEL)
    x_s = x_sample.reshape(n_s, D_MODEL)
    p = _inproj(x_p, x_s, norm1_w[l], _pack_w_in(w_in[l]))

    hp = jnp.zeros((8, LANES), F32).at[0, :N_HEADS].set(a_log[l]).at[1, :N_HEADS].set(dt_bias[l])
    wgk = jnp.zeros((LANES, N_HEADS * DK_B), F32).at[2 * N_HEADS:2 * N_HEADS + GK_RANK].set(gla_gk_up[l]).astype(BF16)
    zeros = lambda shape: jnp.zeros(shape, F32)

    oa_p, gdn_p, conv_p = _gdn(p, 0, bp, tp, conv_w[l], hp, gdn_norm_w[l],
                               zeros((bp, CONV_W - 1, CONV_DIM)), zeros((bp, N_HEADS, DK_A, DV_A)))
    oa_s, gdn_s, conv_s = _gdn(p, n_p, bs, ts, conv_w[l], hp, gdn_norm_w[l], state_gdn_conv[l], state_gdn[l])
    ob_p, gla_p = _gla(p, 0, bp, tp, wgk, gla_gk_b[l], gla_norm_w[l], zeros((bp, N_HEADS, DK_B, DV_B)))
    ob_s, gla_s = _gla(p, n_p, bs, ts, wgk, gla_gk_b[l], gla_norm_w[l], state_gla[l])

    wr = jnp.zeros((D_MODEL, LANES), F32).at[:, :N_EXPERTS].set(w_router[l])
    wr_hi = wr.astype(BF16)
    wr = jnp.concatenate([wr_hi, (wr - wr_hi.astype(F32)).astype(BF16)], axis=1)
    br = jnp.zeros((1, LANES), F32).at[0, :N_EXPERTS].set(b_router[l])
    x1, h2, te, tg = _merge(x_p, x_s, oa_p, oa_s, ob_p, ob_s, p, w_up_a[l].astype(BF16),
                            w_up_b[l].astype(BF16), w_o[l].astype(BF16), norm2_w[l], wr, br)

    blk_e, nxt_e, n_used, src_tok, dst_row = _route(te[:, :TOP_K], n)
    y = _moe(blk_e, nxt_e, n_used, src_tok, dst_row, h2, w_gate_up[l],
             b_gate_up[l].reshape(N_EXPERTS, 1, 2 * D_FF), w_down[l],
             b_down[l].reshape(N_EXPERTS, 1, D_MODEL), n * TOP_K + MOE_SUB * MOE_BM)
    out_p, out_s = _combine(x1, y, tg, norm_f_w, n_p)
    y_prompt = out_p.reshape(bp, tp, D_MODEL)
    y_sample = out_s.reshape(bs, ts, D_MODEL)
    return (y_prompt, y_sample, gdn_p[None], conv_p[None], gla_p[None], gdn_s[None], conv_s[None],
            gla_s[None])
```

```python
import functools
import math

import jax
import jax.numpy as jnp
import numpy as np
from jax import lax
from jax.experimental import pallas as pl
from jax.experimental.pallas import tpu as pltpu

F32 = jnp.float32
BF16 = jnp.bfloat16

D_MODEL = 1024
N_HEADS = 4
DK_A = 128
DV_A = 128
W_A = N_HEADS * DV_A
CONV_W = 4
CONV_DIM = 2 * N_HEADS * DK_A + W_A
GDN_CHUNK = 64
DK_B = 64
DV_B = 128
W_B = N_HEADS * DV_B
GK_RANK = 16
GLA_GATE_NORM = 16.0
GLA_CHUNK = 16
N_EXPERTS = 32
TOP_K = 4
D_FF = 1024
SWIGLU_LIMIT = 7.0
SWIGLU_ALPHA = 1.702
RMS_EPS = 1e-6
IN_SIZES = (CONV_DIM, W_A, N_HEADS, N_HEADS, N_HEADS * DK_B, N_HEADS * DK_B, W_B, W_B, GK_RANK,
            D_MODEL, D_MODEL)

LANES = 128

OFF_QKV = 0
OFF_Z = 1536
OFF_GATE_A = 2048
OFF_GATE_B = 3072
OFF_QK_B = 4096
OFF_V_B = 4608
OFF_G_B = 5120
OFF_SMALL = 5632
P_COLS = 5760

INPROJ_TM = 512
GLA_TB = 256
GLA_SEQS_CARRY = 4
GDN_SEGS_CARRY = 4
GDN_SEQS_CARRY = 2
GDN_SEGS_INDEP = 8
MERGE_TM = 512
MERGE_SPLIT = 2
MOE_BM = 256
MOE_SUB = 2
PACK_ROWS = D_MODEL // 2 // LANES
MOE_VMEM_LIMIT = 62 * 1024 * 1024
COMBINE_TM = 256
VMEM_LIMIT = 56 * 1024 * 1024


def _dot(a, b):
    return jnp.dot(a.astype(BF16), b.astype(BF16), preferred_element_type=F32)


def _dot_nt(a, b):
    return lax.dot_general(a.astype(BF16), b.astype(BF16), (((1,), (1,)), ((), ())),
                           preferred_element_type=F32)


def _dot_tn(a, b):
    return lax.dot_general(a.astype(BF16), b.astype(BF16), (((0,), (0,)), ((), ())), preferred_element_type=F32)


def _split3(x):
    hi = x.astype(BF16)
    r1 = x - hi.astype(F32)
    mid = r1.astype(BF16)
    lo = (r1 - mid.astype(F32)).astype(BF16)
    return hi, mid, lo


def _dot01(m01, x):
    hi, mid, lo = _split3(x)
    d = lambda p: jnp.dot(m01, p, preferred_element_type=F32)
    return d(hi) + d(mid) + d(lo)


def _silu(x):
    return x * jax.nn.sigmoid(x)


def _pack_bf16_pairs(x):
    bits = pltpu.bitcast(x.astype(BF16).astype(F32), jnp.uint32)
    half = D_MODEL // 2
    return (bits[:, :half] >> 16) | (bits[:, half:] & jnp.uint32(0xFFFF0000))


def _store_packed_rows(ref, lead, row0, words):
    n = words.shape[0]
    for c in range(PACK_ROWS):
        ref[(*lead, pl.ds(row0 * PACK_ROWS + c, n, stride=PACK_ROWS), slice(None))] = words[:, c * LANES:(c + 1) * LANES]


def _load_packed_rows(ref, lead, n, dtype):
    words = [ref[(*lead, pl.ds(c, n, stride=PACK_ROWS), slice(None))] for c in range(PACK_ROWS)]
    lo = [pltpu.bitcast(w << 16, F32).astype(dtype) for w in words]
    hi = [pltpu.bitcast(w & jnp.uint32(0xFFFF0000), F32).astype(dtype) for w in words]
    return jnp.concatenate(lo + hi, axis=1)


def _pick(i, n_first, first_ref, second_ref):
    return jnp.where(i < n_first, first_ref[...], second_ref[...])


def _split_specs(block, n_first):
    return (pl.BlockSpec(block, lambda i: (jnp.minimum(i, n_first - 1), 0)),
            pl.BlockSpec(block, lambda i: (jnp.maximum(i - n_first, 0), 0)))


def _inproj_kernel(xp_ref, xs_ref, nw_ref, w_ref, o_ref, *, n_first):
    x = _pick(pl.program_id(0), n_first, xp_ref, xs_ref)
    ms = jnp.mean(x * x, axis=-1, keepdims=True)
    h = (x * lax.rsqrt(ms + RMS_EPS) * nw_ref[...]).astype(BF16)
    o_ref[...] = jnp.dot(h, w_ref[...], preferred_element_type=F32)


def _inproj(x_p, x_s, norm_w, w_packed):
    tm = INPROJ_TM
    n_first = x_p.shape[0] // tm
    n = x_p.shape[0] + x_s.shape[0]
    return pl.pallas_call(
        functools.partial(_inproj_kernel, n_first=n_first),
        out_shape=jax.ShapeDtypeStruct((n, P_COLS), F32),
        grid=(n // tm,),
        in_specs=[
            *_split_specs((tm, D_MODEL), n_first),
            pl.BlockSpec((1, D_MODEL), lambda i: (0, 0)),
            pl.BlockSpec((D_MODEL, P_COLS), lambda i: (0, 0), pipeline_mode=pl.Buffered(1)),
        ],
        out_specs=pl.BlockSpec((tm, P_COLS), lambda i: (i, 0)),
        compiler_params=pltpu.CompilerParams(
            dimension_semantics=("arbitrary",), vmem_limit_bytes=VMEM_LIMIT),
        name="inproj",
    )(x_p, x_s, norm_w.reshape(1, D_MODEL), w_packed)


def _gdn_kernel(*refs, chunk, n_seg, n_seq, carry, n_steps):
    qkv_refs, z_refs, sm_refs = refs[:n_seq], refs[n_seq:2 * n_seq], refs[2 * n_seq:3 * n_seq]
    cw_ref, hp_ref, nw_ref, sconv_ref, s0_ref, o_ref, sout_ref, cout_ref, s_ref, ubuf_ref = refs[3 * n_seq:]
    c_len = chunk
    n_seg_all = n_seq * n_seg
    rows = n_seg_all * c_len
    t = pl.program_id(1)
    hist = CONV_W - 1
    base = 8
    conv_len = n_seg * c_len if carry else c_len
    n_conv = n_seq if carry else n_seg

    def init_history():
        ubuf_ref[:, 0:base, :] = jnp.zeros((n_conv, base, CONV_DIM), F32)
        ubuf_ref[:, base - hist:base, :] = sconv_ref[...]

    if carry:
        @pl.when(t == 0)
        def _():
            s_ref[...] = s0_ref[...]
            init_history()

        @pl.when(t > 0)
        def _():
            ubuf_ref[:, base - hist:base, :] = ubuf_ref[:, base + conv_len - hist:base + conv_len, :]
        for q in range(n_seq):
            ubuf_ref[q, base:base + conv_len, :] = qkv_refs[q][...]
    else:
        init_history()
        ubuf_ref[:, base:base + conv_len, :] = qkv_refs[0][...].reshape(n_conv, conv_len, CONV_DIM)

    cw = cw_ref[...]
    u = ubuf_ref[...]
    u1 = pltpu.roll(u, 1, axis=1)
    v = u1 * cw[0:1, :] + u * cw[1:2, :]
    y = (pltpu.roll(v, 2, axis=1) + u1 * cw[2:3, :]) + u * cw[3:4, :]
    y = _silu(y[:, base:, :]).reshape(rows, CONV_DIM)

    sm = sm_refs[0][...] if n_seq == 1 else jnp.concatenate([r[...] for r in sm_refs], axis=0)
    hp = hp_ref[...]
    lane = lax.broadcasted_iota(jnp.int32, (rows, LANES), 1)
    g_all = -jnp.exp(hp[0:1, :]) * jax.nn.softplus(sm + hp[1:2, :])
    g_all = jnp.where(lane < N_HEADS, g_all, 0.0)
    beta_all = jax.nn.sigmoid(sm)

    sh = int(round(math.log2(c_len)))
    r_b = lax.broadcasted_iota(jnp.int32, (rows, rows), 0)
    c_b = lax.broadcasted_iota(jnp.int32, (rows, rows), 1)
    seg_tril = ((r_b >> sh) == (c_b >> sh)) & (c_b <= r_b)
    gc = _dot01(jnp.where(seg_tril, 1.0, 0.0).astype(BF16), g_all)

    r_i = lax.broadcasted_iota(jnp.int32, (c_len, c_len), 0)
    c_i = lax.broadcasted_iota(jnp.int32, (c_len, c_len), 1)
    tri = c_i <= r_i
    strict = c_i < r_i
    eye = c_i == r_i
    nw = nw_ref[...]
    levels = sh

    qn, kn, kb, rhs, qg, gcol = [], [], [], [], [], []
    for h in range(N_HEADS):
        qh = y[:, h * DK_A:(h + 1) * DK_A]
        kh = y[:, W_A + h * DK_A:W_A + (h + 1) * DK_A]
        vh = y[:, 2 * W_A + h * DV_A:2 * W_A + (h + 1) * DV_A]
        qn_h = qh * lax.rsqrt(jnp.sum(qh * qh, axis=-1, keepdims=True) + 1e-6) * (DK_A ** -0.5)
        kn_h = kh * lax.rsqrt(jnp.sum(kh * kh, axis=-1, keepdims=True) + 1e-6)
        gcol_h = gc[:, h:h + 1]
        eg = jnp.exp(gcol_h)
        bcol = beta_all[:, N_HEADS + h:N_HEADS + h + 1]
        kb_h = kn_h * bcol
        qn.append(qn_h)
        kn.append(kn_h)
        kb.append(kb_h)
        rhs.append(jnp.concatenate([vh * bcol, kb_h * eg], axis=1))
        qg.append(qn_h * eg)
        gcol.append(gcol_h)

    pairs = [(g, h) for g in range(n_seg_all) for h in range(N_HEADS)]
    sl = lambda g: slice(g * c_len, (g + 1) * c_len)

    decay, kd, gl = {}, {}, {}
    for (g, h) in pairs:
        gcol_p = gcol[h][sl(g)]
        gcb = jnp.broadcast_to(gcol_p, (c_len, c_len))
        grow = jnp.sum(jnp.where(eye, gcb, 0.0), axis=0, keepdims=True)
        decay[g, h] = jnp.where(tri, jnp.exp(jnp.where(tri, gcb - grow, 0.0)), 0.0)
        glast = gcol[h][(g + 1) * c_len - 1:(g + 1) * c_len]
        kd[g, h] = kn[h][sl(g)] * jnp.exp(glast - gcol_p)
        gl[g, h] = jnp.exp(glast)

    kq = {p: _dot_nt(jnp.concatenate([kb[p[1]][sl(p[0])], qn[p[1]][sl(p[0])]], axis=0), kn[p[1]][sl(p[0])])
          for p in pairs}
    low = {p: jnp.where(strict, kq[p][:c_len] * decay[p], 0.0) for p in pairs}
    attn = {p: jnp.where(tri, kq[p][c_len:] * decay[p], 0.0) for p in pairs}
    ymat = {p: -low[p] for p in pairs}
    lp = low
    for _ in range(levels - 1):
        lp = {p: _dot(lp[p], lp[p]) for p in pairs}
        prod = {p: _dot(ymat[p], lp[p]) for p in pairs}
        ymat = {p: ymat[p] + lp[p] + prod[p] for p in pairs}
    sol = {}
    for (g, h) in pairs:
        rhs_p = rhs[h][sl(g)]
        sol[g, h] = rhs_p + _dot(ymat[g, h], rhs_p)

    def finish(g, h, o):
        r = o * lax.rsqrt(jnp.mean(o * o, axis=-1, keepdims=True) + RMS_EPS) * nw
        q, loc = (g // n_seg, sl(g % n_seg)) if carry else (0, sl(g))
        gated = r * _silu(z_refs[q][loc, h * DV_A:(h + 1) * DV_A])
        if carry:
            o_ref[q, loc, h * DV_A:(h + 1) * DV_A] = gated
        else:
            o_ref[loc, h * DV_A:(h + 1) * DV_A] = gated

    def advance(group, state):
        ws = {p: _dot(jnp.concatenate([sol[p][:, DV_A:], qg[p[1]][sl(p[0])]], axis=0), state[p]) for p in group}
        v_new = {p: sol[p][:, :DV_A] - ws[p][:c_len] for p in group}
        o = {p: ws[p][c_len:] + _dot(attn[p], v_new[p]) for p in group}
        new = {p: state[p] * gl[p] + _dot_tn(kd[p], v_new[p]) for p in group}
        for p in group:
            finish(p[0], p[1], o[p])
        return new

    if carry:
        chains = [(q, h) for q in range(n_seq) for h in range(N_HEADS)]
        cur = {c: s_ref[c[0], c[1]] for c in chains}
        for g in range(n_seg):
            group = [(q * n_seg + g, h) for (q, h) in chains]
            new = advance(group, {(q * n_seg + g, h): cur[q, h] for (q, h) in chains})
            cur = {(q, h): new[q * n_seg + g, h] for (q, h) in chains}
        for (q, h) in chains:
            s_ref[q, h] = cur[q, h]

        @pl.when(t == n_steps - 1)
        def _():
            for (q, h) in chains:
                sout_ref[q, h] = cur[q, h]
            cout_ref[...] = ubuf_ref[:, base + conv_len - hist:base + conv_len, :]
    else:
        new = advance(pairs, {p: s0_ref[p[0], p[1]] for p in pairs})
        for p in pairs:
            sout_ref[p[0], p[1]] = new[p]
        cout_ref[...] = ubuf_ref[:, base + conv_len - hist:base + conv_len, :]


def _gdn(p, row0, batch, seq, conv_w, hp, norm_w, s_conv, s0):
    chunk = math.gcd(seq, GDN_CHUNK)
    n_chunks = seq // chunk
    carry = n_chunks > 1
    if carry:
        n_seg = math.gcd(n_chunks, GDN_SEGS_CARRY)
        n_seq = math.gcd(batch, GDN_SEQS_CARRY)
        grid = (batch // n_seq, n_chunks // n_seg)
        n_conv, conv_len = n_seq, n_seg * chunk
    else:
        n_seg = math.gcd(batch, GDN_SEGS_INDEP)
        n_seq = 1
        grid = (batch // n_seg, 1)
        n_conv, conv_len = n_seg, chunk
    rows = n_seg * chunk
    n_steps = grid[1]
    blk0 = row0 // rows
    rowblk = lambda q: (lambda b, t: blk0 + (b * n_seq + q) * n_steps + t)
    pcol = lambda width, off: [pl.BlockSpec((rows, width), lambda b, t, rb=rowblk(q): (rb(b, t), off // width))
                               for q in range(n_seq)]
    kern = functools.partial(_gdn_kernel, chunk=chunk, n_seg=n_seg, n_seq=n_seq, carry=carry, n_steps=n_steps)
    if carry:
        o_shape, o_spec = (batch, seq, W_A), pl.BlockSpec((n_seq, rows, W_A), lambda b, t: (b, t, 0))
    else:
        o_shape, o_spec = (batch * seq, W_A), pl.BlockSpec((rows, W_A), lambda b, t: (b, 0))
    o, s_new, c_new = pl.pallas_call(
        kern,
        out_shape=(jax.ShapeDtypeStruct(o_shape, F32),
                   jax.ShapeDtypeStruct((batch, N_HEADS, DK_A, DV_A), F32),
                   jax.ShapeDtypeStruct((batch, CONV_W - 1, CONV_DIM), F32)),
        grid=grid,
        in_specs=[
            *pcol(CONV_DIM, OFF_QKV), *pcol(W_A, OFF_Z), *pcol(LANES, OFF_SMALL),
            pl.BlockSpec((CONV_W, CONV_DIM), lambda b, t: (0, 0)),
            pl.BlockSpec((8, LANES), lambda b, t: (0, 0)),
            pl.BlockSpec((1, DV_A), lambda b, t: (0, 0)),
            pl.BlockSpec((n_conv, CONV_W - 1, CONV_DIM), lambda b, t: (b, 0, 0)),
            pl.BlockSpec((n_conv, N_HEADS, DK_A, DV_A), lambda b, t: (b, 0, 0, 0)),
        ],
        out_specs=(
            o_spec,
            pl.BlockSpec((n_conv, N_HEADS, DK_A, DV_A), lambda b, t: (b, 0, 0, 0)),
            pl.BlockSpec((n_conv, CONV_W - 1, CONV_DIM), lambda b, t: (b, 0, 0)),
        ),
        scratch_shapes=[pltpu.VMEM((n_seq, N_HEADS, DK_A, DV_A), F32),
                        pltpu.VMEM((n_conv, conv_len + 8, CONV_DIM), F32)],
        compiler_params=pltpu.CompilerParams(
            dimension_semantics=("arbitrary", "arbitrary"), vmem_limit_bytes=VMEM_LIMIT),
        name=f"gdn_c{chunk}",
    )(*([p] * (3 * n_seq)), conv_w, hp, norm_w.reshape(1, DV_A), s_conv, s0)
    return o.reshape(batch * seq, W_A), s_new, c_new


def _gla_kernel(*refs, tb, cs, n_seq, carry, n_steps):
    qk_refs, v_refs, g_refs, sm_refs = (refs[k * n_seq:(k + 1) * n_seq] for k in range(4))
    wgk_ref, bgk_ref, nw_ref, s0_ref, o_ref, sout_ref, s_ref = refs[4 * n_seq:]
    t = pl.program_id(1)
    n_sub = tb // cs
    w_kb = N_HEADS * DK_B
    seqs = range(n_seq)
    heads = range(N_HEADS)

    if carry:
        @pl.when(t == 0)
        def _():
            s_ref[...] = s0_ref[...]

    sh = int(round(math.log2(cs)))
    r_i = lax.broadcasted_iota(jnp.int32, (tb, tb), 0)
    c_i = lax.broadcasted_iota(jnp.int32, (tb, tb), 1)
    same = (r_i >> sh) == (c_i >> sh)
    mask = same & (c_i <= r_i)
    mask_bf = jnp.where(mask, 1.0, 0.0).astype(BF16)
    same_bf = jnp.where(same, 1.0, 0.0).astype(BF16)
    onehot = jnp.where((lax.broadcasted_iota(jnp.int32, (tb, n_sub), 0) >> sh)
                       == lax.broadcasted_iota(jnp.int32, (tb, n_sub), 1), 1.0, 0.0).astype(BF16)
    tn = lambda part: lax.dot_general(part, onehot, (((0,), (0,)), ((), ())), preferred_element_type=F32)

    log_a = [jax.nn.log_sigmoid(_dot(sm_refs[q][...], wgk_ref[...]) + bgk_ref[...]) / GLA_GATE_NORM for q in seqs]
    gcum = [_dot01(mask_bf, la) for la in log_a]
    gtot = [_dot01(same_bf, la) for la in log_a]
    glcol = []
    for la in log_a:
        hi, mid, lo = _split3(la)
        glcol.append(jnp.exp(tn(hi) + tn(mid) + tn(lo)))

    nw = nw_ref[...]
    sl = lambda j: slice(j * cs, (j + 1) * cs)
    qg, kg, kd, vh = {}, {}, {}, {}
    for q in seqs:
        qk = qk_refs[q][...]
        for h in heads:
            qh = qk[:, h * DK_B:(h + 1) * DK_B] * (DK_B ** -0.5)
            kh = qk[:, w_kb + h * DK_B:w_kb + (h + 1) * DK_B]
            gh = gcum[q][:, h * DK_B:(h + 1) * DK_B]
            gt = gtot[q][:, h * DK_B:(h + 1) * DK_B]
            qg[q, h] = qh * jnp.exp(gh)
            kg[q, h] = kh * jnp.exp(-gh)
            kd[q, h] = kh * jnp.exp(gt - gh)
            vh[q, h] = v_refs[q][:, h * DV_B:(h + 1) * DV_B]
    chains = [(q, h) for q in seqs for h in heads]
    attn = {c: jnp.where(mask, _dot_nt(qg[c], kg[c]), 0.0) for c in chains}
    o_intra = {c: _dot(attn[c], vh[c]) for c in chains}
    upd = {(j, c): _dot_tn(kd[c][sl(j)], vh[c][sl(j)]) for j in range(n_sub) for c in chains}
    gl = lambda j, c: glcol[c[0]][c[1] * DK_B:(c[1] + 1) * DK_B, j:j + 1]

    o_inter = {c: [] for c in chains}
    if carry:
        cur = {c: s_ref[c[0], c[1]] for c in chains}
        for j in range(n_sub):
            for c in chains:
                o_inter[c].append(_dot(qg[c][sl(j)], cur[c]))
                cur[c] = cur[c] * gl(j, c) + upd[j, c]
        for c in chains:
            s_ref[c[0], c[1]] = cur[c]

        @pl.when(t == n_steps - 1)
        def _():
            for c in chains:
                sout_ref[c[0], c[1]] = cur[c]
    else:
        for j in range(n_sub):
            for c in chains:
                s = s0_ref[j, c[1]]
                o_inter[c].append(_dot(qg[c][sl(j)], s))
                sout_ref[j, c[1]] = s * gl(j, c) + upd[j, c]

    for (q, h) in chains:
        o = o_intra[q, h] + jnp.concatenate(o_inter[q, h], axis=0)
        r = o * lax.rsqrt(jnp.mean(o * o, axis=-1, keepdims=True) + RMS_EPS) * nw
        gated = r * _silu(g_refs[q][:, h * DV_B:(h + 1) * DV_B])
        if carry:
            o_ref[q, :, h * DV_B:(h + 1) * DV_B] = gated
        else:
            o_ref[:, h * DV_B:(h + 1) * DV_B] = gated


def _gla(p, row0, batch, seq, wgk, bgk, norm_w, s0):
    cs = math.gcd(seq, GLA_CHUNK)
    carry = seq > cs
    if carry:
        tb = math.gcd(seq, GLA_TB)
        n_seq = math.gcd(batch, GLA_SEQS_CARRY)
        grid = (batch // n_seq, seq // tb)
        n_state = n_seq
    else:
        tb = math.gcd(batch * seq, GLA_TB)
        n_seq = 1
        n_state = tb // cs
        grid = (batch // n_state, 1)
    n_steps = grid[1]
    blk0 = row0 // tb
    rowblk = lambda q: (lambda b, t: blk0 + (b * n_seq + q) * n_steps + t)
    pcol = lambda width, off: [pl.BlockSpec((tb, width), lambda b, t, rb=rowblk(q): (rb(b, t), off // width))
                               for q in range(n_seq)]
    kern = functools.partial(_gla_kernel, tb=tb, cs=cs, n_seq=n_seq, carry=carry, n_steps=n_steps)
    if carry:
        o_shape, o_spec = (batch, seq, W_B), pl.BlockSpec((n_seq, tb, W_B), lambda b, t: (b, t, 0))
    else:
        o_shape, o_spec = (batch * seq, W_B), pl.BlockSpec((tb, W_B), lambda b, t: (b, 0))
    o, s_new = pl.pallas_call(
        kern,
        out_shape=(jax.ShapeDtypeStruct(o_shape, F32),
                   jax.ShapeDtypeStruct((batch, N_HEADS, DK_B, DV_B), F32)),
        grid=grid,
        in_specs=[
            *pcol(2 * N_HEADS * DK_B, OFF_QK_B), *pcol(W_B, OFF_V_B), *pcol(W_B, OFF_G_B), *pcol(LANES, OFF_SMALL),
            pl.BlockSpec((LANES, N_HEADS * DK_B), lambda b, t: (0, 0)),
            pl.BlockSpec((1, N_HEADS * DK_B), lambda b, t: (0, 0)),
            pl.BlockSpec((1, DV_B), lambda b, t: (0, 0)),
            pl.BlockSpec((n_state, N_HEADS, DK_B, DV_B), lambda b, t: (b, 0, 0, 0)),
        ],
        out_specs=(
            o_spec,
            pl.BlockSpec((n_state, N_HEADS, DK_B, DV_B), lambda b, t: (b, 0, 0, 0)),
        ),
        scratch_shapes=[pltpu.VMEM((n_seq, N_HEADS, DK_B, DV_B), F32)],
        compiler_params=pltpu.CompilerParams(
            dimension_semantics=("arbitrary", "arbitrary"), vmem_limit_bytes=VMEM_LIMIT),
        name=f"gla_c{cs}",
    )(*([p] * (4 * n_seq)), wgk, bgk.reshape(1, N_HEADS * DK_B), norm_w.reshape(1, DV_B), s0)
    return o.reshape(batch * seq, W_B), s_new


def _topk_softmax(logits):
    tm = logits.shape[0]
    lane = lax.broadcasted_iota(jnp.int32, (tm, LANES), 1)
    lane_f = lane.astype(F32)
    neg = jnp.float32(-jnp.inf)
    cur = jnp.where(lane < N_EXPERTS, logits, neg)
    vals, idxs = [], []
    for _ in range(TOP_K):
        m = jnp.max(cur, axis=-1, keepdims=True)
        idx = jnp.min(jnp.where(cur == m, lane_f, float(LANES)), axis=-1, keepdims=True)
        vals.append(m)
        idxs.append(idx)
        cur = jnp.where(lane_f == idx, neg, cur)
    exps = [jnp.exp(v - vals[0]) for v in vals]
    den = exps[0] + exps[1] + exps[2] + exps[3]
    te = jnp.zeros((tm, LANES), F32)
    tg = jnp.zeros((tm, LANES), F32)
    for k in range(TOP_K):
        te = jnp.where(lane == k, idxs[k], te)
        tg = jnp.where(lane == k, exps[k] / den, tg)
    return te.astype(jnp.int32), tg


def _merge_kernel(xp_ref, xs_ref, oap_ref, oas_ref, obp_ref, obs_ref, ga_ref, gb_ref, wua_ref, wub_ref, wo_ref,
                  n2_ref, wr_ref, br_ref, x1_ref, h2_ref, te_ref, tg_ref, *, n_first):
    i = pl.program_id(0)
    tm = x1_ref.shape[0]
    hm = tm // MERGE_SPLIT
    groups = [slice(a * hm, (a + 1) * hm) for a in range(MERGE_SPLIT)]
    oa = _pick(i, n_first, oap_ref, oas_ref).astype(BF16)
    ob = _pick(i, n_first, obp_ref, obs_ref).astype(BF16)
    x = _pick(i, n_first, xp_ref, xs_ref)
    ua = [jnp.dot(oa[g], wua_ref[...], preferred_element_type=F32) for g in groups]
    ub = [jnp.dot(ob[g], wub_ref[...], preferred_element_type=F32) for g in groups]
    merged = [jax.nn.sigmoid(ga_ref[g, :]) * ua[a] + jax.nn.sigmoid(gb_ref[g, :]) * ub[a]
              for a, g in enumerate(groups)]
    x1 = [x[g] + jnp.dot(merged[a].astype(BF16), wo_ref[...], preferred_element_type=F32)
          for a, g in enumerate(groups)]
    h2 = [v * lax.rsqrt(jnp.mean(v * v, axis=-1, keepdims=True) + RMS_EPS) * n2_ref[...] for v in x1]
    h_hi = [v.astype(BF16) for v in h2]
    h_lo = [(v - hi.astype(F32)).astype(BF16) for v, hi in zip(h2, h_hi)]
    l_hi = [jnp.dot(hi, wr_ref[...], preferred_element_type=F32) for hi in h_hi]
    l_lo = [jnp.dot(lo, wr_ref[:, :LANES], preferred_element_type=F32) for lo in h_lo]
    for a, g in enumerate(groups):
        x1_ref[g, :] = x1[a]
        _store_packed_rows(h2_ref, (), a * hm, _pack_bf16_pairs(h2[a]))
        logits = l_hi[a][:, :LANES] + l_hi[a][:, LANES:] + l_lo[a] + br_ref[...]
        te, tg = _topk_softmax(logits)
        te_ref[g, :] = te
        tg_ref[g, :] = tg


def _merge(x_p, x_s, oa_p, oa_s, ob_p, ob_s, p, wua, wub, wo, norm2_w, wr, br):
    n = x_p.shape[0] + x_s.shape[0]
    tm = MERGE_TM
    n_first = x_p.shape[0] // tm
    row = lambda i: (i, 0)
    const = lambda i: (0, 0)
    return pl.pallas_call(
        functools.partial(_merge_kernel, n_first=n_first),
        out_shape=(jax.ShapeDtypeStruct((n, D_MODEL), F32), jax.ShapeDtypeStruct((n * PACK_ROWS, LANES), jnp.uint32),
                   jax.ShapeDtypeStruct((n, LANES), jnp.int32), jax.ShapeDtypeStruct((n, LANES), F32)),
        grid=(n // tm,),
        in_specs=[
            *_split_specs((tm, D_MODEL), n_first),
            *_split_specs((tm, W_A), n_first),
            *_split_specs((tm, W_B), n_first),
            pl.BlockSpec((tm, D_MODEL), lambda i: (i, OFF_GATE_A // D_MODEL)),
            pl.BlockSpec((tm, D_MODEL), lambda i: (i, OFF_GATE_B // D_MODEL)),
            pl.BlockSpec((W_A, D_MODEL), const),
            pl.BlockSpec((W_B, D_MODEL), const),
            pl.BlockSpec((D_MODEL, D_MODEL), const),
            pl.BlockSpec((1, D_MODEL), const),
            pl.BlockSpec((D_MODEL, 2 * LANES), const),
            pl.BlockSpec((1, LANES), const),
        ],
        out_specs=(pl.BlockSpec((tm, D_MODEL), row), pl.BlockSpec((tm * PACK_ROWS, LANES), row),
                   pl.BlockSpec((tm, LANES), row), pl.BlockSpec((tm, LANES), row)),
        compiler_params=pltpu.CompilerParams(
            dimension_semantics=("arbitrary",), vmem_limit_bytes=VMEM_LIMIT),
        name="merge_router",
    )(x_p, x_s, oa_p, oa_s, ob_p, ob_s, p, p, wua, wub, wo, norm2_w.reshape(1, D_MODEL), wr, br)


def _moe_kernel(blk_e_ref, nxt_e_ref, n_sub_ref, n_used_ref, src_ref, dst_ref, h2_ref, wgu_hbm, bgu_ref,
                wdn_hbm, bdn_ref, y_hbm, xstage, ybuf, wgu_stage, wdn_stage, wgu_bf, wdn_bf, ssem, wsem):
    i = pl.program_id(0)
    n_used = n_used_ref[0]
    bm = MOE_BM
    pr = PACK_ROWS
    e_cur = blk_e_ref[i]

    def weight_copies(e):
        return (pltpu.make_async_copy(wgu_hbm.at[e], wgu_stage, wsem.at[0]),
                pltpu.make_async_copy(wdn_hbm.at[e], wdn_stage, wsem.at[1]))

    @pl.when((i == 0) & (n_used > 0))
    def _():
        for cp in weight_copies(e_cur):
            cp.start()

    @pl.when((i < n_used) & ((i == 0) | (e_cur != blk_e_ref[jnp.maximum(i - 1, 0)])))
    def _():
        for cp in weight_copies(e_cur):
            cp.wait()
        wgu_bf[...] = wgu_stage[...].astype(BF16)
        wdn_bf[...] = wdn_stage[...].astype(BF16)

        @pl.when(nxt_e_ref[i] != e_cur)
        def _():
            for cp in weight_copies(nxt_e_ref[i]):
                cp.start()

    def scatter_copy(row, s, r):
        return pltpu.make_async_copy(ybuf.at[s, pl.ds(r * pr, pr), :],
                                     y_hbm.at[pl.ds(pl.multiple_of(row * pr, pr), pr), :], ssem.at[s])

    @pl.when(i == 0)
    def _():
        ybuf[...] = jnp.zeros_like(ybuf)
        n_real = y_hbm.shape[0] - MOE_SUB * bm * pr
        for s in range(MOE_SUB):
            sink = pltpu.make_async_copy(ybuf.at[s], y_hbm.at[pl.ds(n_real + s * bm * pr, bm * pr), :], ssem.at[s])
            sink.start()
            sink.wait()

    def sub_block(s):
        @pl.when((i >= 1) & (n_sub_ref[jnp.maximum(i - 1, 0)] > s))
        def _():
            for r in range(bm):
                scatter_copy(0, s, r).wait()

        @pl.when(n_sub_ref[i] > s)
        def _():
            for r in range(bm):
                tok = pl.multiple_of(src_ref[0, 0, s * bm + r] * pr, pr)
                xstage[s, pl.ds(r * pr, pr), :] = h2_ref[pl.ds(tok, pr), :]
            x = _load_packed_rows(xstage, (s,), bm, BF16)
            gu = jnp.dot(x, wgu_bf[...], preferred_element_type=F32) + bgu_ref[...]
            gate = jnp.minimum(gu[:, :D_FF], SWIGLU_LIMIT)
            up = jnp.clip(gu[:, D_FF:], -SWIGLU_LIMIT, SWIGLU_LIMIT)
            hmid = (up + 1.0) * (gate * jax.nn.sigmoid(SWIGLU_ALPHA * gate))
            y = jnp.dot(hmid.astype(BF16), wdn_bf[...], preferred_element_type=F32) + bdn_ref[...]
            _store_packed_rows(ybuf, (s,), 0, _pack_bf16_pairs(y))
            for r in range(bm):
                scatter_copy(dst_ref[0, 0, s * bm + r], s, r).start(priority=r % 2)

    @pl.when(i < n_used)
    def _():
        for s in range(MOE_SUB):
            sub_block(s)

    @pl.when(i == n_used - 1)
    def _():
        for s in range(MOE_SUB):
            @pl.when(n_sub_ref[i] > s)
            def _():
                for r in range(bm):
                    scatter_copy(0, s, r).wait()


def _moe(blk_e, nxt_e, n_sub, n_used, src_tok, dst_row, h2, wgu, bgu, wdn, bdn, y_rows):
    n_steps = src_tok.shape[0]
    step_rows = MOE_SUB * MOE_BM
    cur = lambda i, be, ne, ns, nu: (i, 0, 0)
    wsel = lambda i, be, ne, ns, nu: (be[i], 0, 0)
    smem_blk = lambda im: pl.BlockSpec((1, 1, step_rows), im, memory_space=pltpu.SMEM)
    grid_spec = pltpu.PrefetchScalarGridSpec(
        num_scalar_prefetch=4,
        grid=(n_steps,),
        in_specs=[
            smem_blk(cur), smem_blk(cur),
            pl.BlockSpec(h2.shape, lambda i, be, ne, ns, nu: (0, 0), pipeline_mode=pl.Buffered(1)),
            pl.BlockSpec(memory_space=pl.ANY),
            pl.BlockSpec((None, 1, 2 * D_FF), wsel),
            pl.BlockSpec(memory_space=pl.ANY),
            pl.BlockSpec((None, 1, D_MODEL), wsel),
        ],
        out_specs=pl.BlockSpec(memory_space=pl.ANY),
        scratch_shapes=[pltpu.VMEM((MOE_SUB, MOE_BM * PACK_ROWS, LANES), jnp.uint32),
                        pltpu.VMEM((MOE_SUB, MOE_BM * PACK_ROWS, LANES), jnp.uint32),
                        pltpu.VMEM((D_MODEL, 2 * D_FF), F32), pltpu.VMEM((D_FF, D_MODEL), F32),
                        pltpu.VMEM((D_MODEL, 2 * D_FF), BF16), pltpu.VMEM((D_FF, D_MODEL), BF16),
                        pltpu.SemaphoreType.DMA((MOE_SUB,)), pltpu.SemaphoreType.DMA((2,))],
    )
    return pl.pallas_call(
        _moe_kernel,
        out_shape=jax.ShapeDtypeStruct((y_rows * PACK_ROWS, LANES), jnp.uint32),
        grid_spec=grid_spec,
        compiler_params=pltpu.CompilerParams(
            dimension_semantics=("arbitrary",), vmem_limit_bytes=MOE_VMEM_LIMIT),
        name="moe_experts",
    )(blk_e, nxt_e, n_sub, n_used, src_tok, dst_row, h2, wgu, bgu, wdn, bdn)


def _route(top_e, n_tok):
    bm = MOE_SUB * MOE_BM
    n_assign = n_tok * TOP_K
    n_blocks = n_assign // bm + N_EXPERTS
    n_dummy = N_EXPERTS * bm
    flat_e = top_e.reshape(-1)
    counts = jnp.sum((flat_e[:, None] == jnp.arange(N_EXPERTS, dtype=jnp.int32)[None, :]).astype(jnp.int32), axis=0)
    padded = (counts + bm - 1) // bm * bm
    pad_end = jnp.cumsum(padded)
    d_e = jnp.arange(n_dummy, dtype=jnp.int32) // bm
    d_active = (jnp.arange(n_dummy, dtype=jnp.int32) % bm) < (padded - counts)[d_e]
    keys = jnp.concatenate([flat_e * 2, jnp.where(d_active, d_e * 2 + 1, 2 * N_EXPERTS)])
    vbits = n_assign.bit_length()
    marker = (1 << vbits) - 1
    vals = jnp.concatenate([jnp.arange(n_assign, dtype=jnp.int32), jnp.full((n_dummy,), marker, jnp.int32)])
    assign = jnp.sort(keys * (1 << vbits) + vals)[:n_blocks * bm] & marker
    rows = jnp.arange(n_blocks * bm, dtype=jnp.int32)
    valid = assign != marker
    src_tok = jnp.where(valid, assign // TOP_K, 0)
    dst_row = jnp.where(valid, (assign % TOP_K) * n_tok + assign // TOP_K, n_assign + rows % bm)
    blk_start = jnp.arange(n_blocks, dtype=jnp.int32) * bm
    blk_e = jnp.minimum(jnp.sum((pad_end[None, :] <= blk_start[:, None]).astype(jnp.int32), axis=1),
                        N_EXPERTS - 1).astype(jnp.int32)
    n_used = (pad_end[-1] // bm).astype(jnp.int32).reshape(1)
    later = (blk_e[None, :] > blk_e[:, None]) & (blk_start[None, :] < pad_end[-1])
    nxt_e = jnp.min(jnp.where(later, blk_e[None, :], N_EXPERTS), axis=1)
    nxt_e = jnp.where(nxt_e == N_EXPERTS, blk_e, nxt_e).astype(jnp.int32)
    n_sub = jnp.sum(valid.reshape(n_blocks, MOE_SUB, MOE_BM)[:, :, 0].astype(jnp.int32), axis=1)
    return (blk_e, nxt_e, n_sub, n_used, src_tok.reshape(n_blocks, 1, bm).astype(jnp.int32),
            dst_row.reshape(n_blocks, 1, bm).astype(jnp.int32))


def _combine_kernel(x1_ref, y0_ref, y1_ref, y2_ref, y3_ref, tg_ref, nf_ref, op_ref, os_ref, *, n_first):
    tg = tg_ref[...]
    tm = x1_ref.shape[0]
    rows = lambda y_ref: _load_packed_rows(y_ref, (), tm, F32)
    acc = rows(y0_ref) * tg[:, 0:1]
    for k, y_ref in enumerate((y1_ref, y2_ref, y3_ref), start=1):
        acc = acc + rows(y_ref) * tg[:, k:k + 1]
    x = x1_ref[...] + acc
    out = x * lax.rsqrt(jnp.mean(x * x, axis=-1, keepdims=True) + RMS_EPS) * nf_ref[...]
    i = pl.program_id(0)

    @pl.when(i < n_first)
    def _():
        op_ref[...] = out

    @pl.when(i >= n_first)
    def _():
        os_ref[...] = out


def _combine(x1, y, tg, norm_f_w, n_p):
    n = x1.shape[0]
    tm = COMBINE_TM
    n_first = n_p // tm
    row = lambda i: (i, 0)
    y_spec = lambda k: pl.BlockSpec((tm * PACK_ROWS, LANES), lambda i: (k * (n // tm) + i, 0))
    return pl.pallas_call(
        functools.partial(_combine_kernel, n_first=n_first),
        out_shape=(jax.ShapeDtypeStruct((n_p, D_MODEL), F32), jax.ShapeDtypeStruct((n - n_p, D_MODEL), F32)),
        grid=(n // tm,),
        in_specs=[pl.BlockSpec((tm, D_MODEL), row), y_spec(0), y_spec(1), y_spec(2), y_spec(3),
                  pl.BlockSpec((tm, LANES), row), pl.BlockSpec((1, D_MODEL), lambda i: (0, 0))],
        out_specs=_split_specs((tm, D_MODEL), n_first),
        compiler_params=pltpu.CompilerParams(
            dimension_semantics=("arbitrary",), vmem_limit_bytes=VMEM_LIMIT),
        name="combine_norm",
    )(x1, y, y, y, y, tg, norm_f_w.reshape(1, D_MODEL))


def _pack_w_in(w_in):
    s = np.cumsum((0,) + IN_SIZES)
    w_in = w_in.astype(BF16)
    qkv, z, al, be, qb, kb, vb, gb, gk, ga, gbt = [w_in[:, s[i]:s[i + 1]] for i in range(len(IN_SIZES))]
    small = jnp.concatenate(
        [al, be, gk, jnp.zeros((D_MODEL, LANES - 2 * N_HEADS - GK_RANK), w_in.dtype)], axis=1)
    return jnp.concatenate([qkv, z, ga, gbt, qb, kb, vb, gb, small], axis=1)


def kernel(x_prompt, x_sample, state_gdn, state_gdn_conv, state_gla, norm1_w, w_in, conv_w, a_log,
           dt_bias, gdn_norm_w, gla_gk_up, gla_gk_b, gla_norm_w, w_up_a, w_up_b, w_o, norm2_w,
           w_router, b_router, w_gate_up, b_gate_up, w_down, b_down, norm_f_w):
    bp, tp, _ = x_prompt.shape
    bs, ts, _ = x_sample.shape
    n_p, n_s = bp * tp, bs * ts
    n = n_p + n_s
    l = 0

    x_p = x_prompt.reshape(n_p, D_MODEL)
    x_s = x_sample.reshape(n_s, D_MODEL)
    p = _inproj(x_p, x_s, norm1_w[l], _pack_w_in(w_in[l]))

    hp = jnp.zeros((8, LANES), F32).at[0, :N_HEADS].set(a_log[l]).at[1, :N_HEADS].set(dt_bias[l])
    wgk = jnp.zeros((LANES, N_HEADS * DK_B), F32).at[2 * N_HEADS:2 * N_HEADS + GK_RANK].set(gla_gk_up[l]).astype(BF16)
    zeros = lambda shape: jnp.zeros(shape, F32)

    oa_p, gdn_p, conv_p = _gdn(p, 0, bp, tp, conv_w[l], hp, gdn_norm_w[l],
                               zeros((bp, CONV_W - 1, CONV_DIM)), zeros((bp, N_HEADS, DK_A, DV_A)))
    oa_s, gdn_s, conv_s = _gdn(p, n_p, bs, ts, conv_w[l], hp, gdn_norm_w[l], state_gdn_conv[l], state_gdn[l])
    ob_p, gla_p = _gla(p, 0, bp, tp, wgk, gla_gk_b[l], gla_norm_w[l], zeros((bp, N_HEADS, DK_B, DV_B)))
    ob_s, gla_s = _gla(p, n_p, bs, ts, wgk, gla_gk_b[l], gla_norm_w[l], state_gla[l])

    wr = jnp.zeros((D_MODEL, LANES), F32).at[:, :N_EXPERTS].set(w_router[l])
    wr_hi = wr.astype(BF16)
    wr = jnp.concatenate([wr_hi, (wr - wr_hi.astype(F32)).astype(BF16)], axis=1)
    br = jnp.zeros((1, LANES), F32).at[0, :N_EXPERTS].set(b_router[l])
    x1, h2, te, tg = _merge(x_p, x_s, oa_p, oa_s, ob_p, ob_s, p, w_up_a[l].astype(BF16),
                            w_up_b[l].astype(BF16), w_o[l].astype(BF16), norm2_w[l], wr, br)

    blk_e, nxt_e, n_sub, n_used, src_tok, dst_row = _route(te[:, :TOP_K], n)
    y = _moe(blk_e, nxt_e, n_sub, n_used, src_tok, dst_row, h2, w_gate_up[l],
             b_gate_up[l].reshape(N_EXPERTS, 1, 2 * D_FF), w_down[l],
             b_down[l].reshape(N_EXPERTS, 1, D_MODEL), n * TOP_K + MOE_SUB * MOE_BM)
    out_p, out_s = _combine(x1, y, tg, norm_f_w, n_p)
    y_prompt = out_p.reshape(bp, tp, D_MODEL)
    y_sample = out_s.reshape(bs, ts, D_MODEL)
    return (y_prompt, y_sample, gdn_p[None], conv_p[None], gla_p[None], gdn_s[None], conv_s[None],
            gla_s[None])
```

```python
import functools
import math

import jax
import jax.numpy as jnp
import numpy as np
from jax import lax
from jax.experimental import pallas as pl
from jax.experimental.pallas import tpu as pltpu

F32 = jnp.float32
BF16 = jnp.bfloat16

D_MODEL = 1024
N_HEADS = 4
DK_A = 128
DV_A = 128
W_A = N_HEADS * DV_A
CONV_W = 4
CONV_DIM = 2 * N_HEADS * DK_A + W_A
GDN_CHUNK = 64
DK_B = 64
DV_B = 128
W_B = N_HEADS * DV_B
GK_RANK = 16
GLA_GATE_NORM = 16.0
GLA_CHUNK = 16
N_EXPERTS = 32
TOP_K = 4
D_FF = 1024
SWIGLU_LIMIT = 7.0
SWIGLU_ALPHA = 1.702
RMS_EPS = 1e-6
IN_SIZES = (CONV_DIM, W_A, N_HEADS, N_HEADS, N_HEADS * DK_B, N_HEADS * DK_B, W_B, W_B, GK_RANK,
            D_MODEL, D_MODEL)

LANES = 128

OFF_QKV = 0
OFF_Z = 1536
OFF_GATE_A = 2048
OFF_GATE_B = 3072
OFF_QK_B = 4096
OFF_V_B = 4608
OFF_G_B = 5120
OFF_SMALL = 5632
P_COLS = 5760

INPROJ_TM = 512
GLA_TB = 256
GLA_SEQS_CARRY = 4
GDN_SEGS_CARRY = 4
GDN_SEQS_CARRY = 2
GDN_SEGS_INDEP = 8
MERGE_TM = 512
MERGE_SPLIT = 2
MOE_BM = 256
MOE_SUB = 2
PACK_ROWS = D_MODEL // 2 // LANES
MOE_VMEM_LIMIT = 62 * 1024 * 1024
COMBINE_TM = 256
VMEM_LIMIT = 56 * 1024 * 1024


def _dot(a, b):
    return jnp.dot(a.astype(BF16), b.astype(BF16), preferred_element_type=F32)


def _dot_nt(a, b):
    return lax.dot_general(a.astype(BF16), b.astype(BF16), (((1,), (1,)), ((), ())),
                           preferred_element_type=F32)


def _dot_tn(a, b):
    return lax.dot_general(a.astype(BF16), b.astype(BF16), (((0,), (0,)), ((), ())), preferred_element_type=F32)


def _split3(x):
    hi = x.astype(BF16)
    r1 = x - hi.astype(F32)
    mid = r1.astype(BF16)
    lo = (r1 - mid.astype(F32)).astype(BF16)
    return hi, mid, lo


def _dot01(m01, x):
    hi, mid, lo = _split3(x)
    d = lambda p: jnp.dot(m01, p, preferred_element_type=F32)
    return d(hi) + d(mid) + d(lo)


def _silu(x):
    return x * jax.nn.sigmoid(x)


def _pack_bf16_pairs(x):
    bits = pltpu.bitcast(x.astype(BF16).astype(F32), jnp.uint32)
    half = D_MODEL // 2
    return (bits[:, :half] >> 16) | (bits[:, half:] & jnp.uint32(0xFFFF0000))


def _store_packed_rows(ref, lead, row0, words):
    n = words.shape[0]
    for c in range(PACK_ROWS):
        ref[(*lead, pl.ds(row0 * PACK_ROWS + c, n, stride=PACK_ROWS), slice(None))] = words[:, c * LANES:(c + 1) * LANES]


def _load_packed_rows(ref, lead, n, dtype):
    words = [ref[(*lead, pl.ds(c, n, stride=PACK_ROWS), slice(None))] for c in range(PACK_ROWS)]
    lo = [pltpu.bitcast(w << 16, F32).astype(dtype) for w in words]
    hi = [pltpu.bitcast(w & jnp.uint32(0xFFFF0000), F32).astype(dtype) for w in words]
    return jnp.concatenate(lo + hi, axis=1)


def _pick(i, n_first, first_ref, second_ref):
    return jnp.where(i < n_first, first_ref[...], second_ref[...])


def _split_specs(block, n_first):
    return (pl.BlockSpec(block, lambda i: (jnp.minimum(i, n_first - 1), 0)),
            pl.BlockSpec(block, lambda i: (jnp.maximum(i - n_first, 0), 0)))


def _inproj_kernel(xp_ref, xs_ref, nw_ref, w_ref, o_ref, *, n_first):
    x = _pick(pl.program_id(0), n_first, xp_ref, xs_ref)
    ms = jnp.mean(x * x, axis=-1, keepdims=True)
    h = (x * lax.rsqrt(ms + RMS_EPS) * nw_ref[...]).astype(BF16)
    o_ref[...] = jnp.dot(h, w_ref[...], preferred_element_type=F32)


def _inproj(x_p, x_s, norm_w, w_packed):
    tm = INPROJ_TM
    n_first = x_p.shape[0] // tm
    n = x_p.shape[0] + x_s.shape[0]
    return pl.pallas_call(
        functools.partial(_inproj_kernel, n_first=n_first),
        out_shape=jax.ShapeDtypeStruct((n, P_COLS), F32),
        grid=(n // tm,),
        in_specs=[
            *_split_specs((tm, D_MODEL), n_first),
            pl.BlockSpec((1, D_MODEL), lambda i: (0, 0)),
            pl.BlockSpec((D_MODEL, P_COLS), lambda i: (0, 0), pipeline_mode=pl.Buffered(1)),
        ],
        out_specs=pl.BlockSpec((tm, P_COLS), lambda i: (i, 0)),
        compiler_params=pltpu.CompilerParams(
            dimension_semantics=("arbitrary",), vmem_limit_bytes=VMEM_LIMIT),
        name="inproj",
    )(x_p, x_s, norm_w.reshape(1, D_MODEL), w_packed)


def _gdn_kernel(*refs, chunk, n_seg, n_seq, carry, n_steps):
    qkv_refs, z_refs, sm_refs = refs[:n_seq], refs[n_seq:2 * n_seq], refs[2 * n_seq:3 * n_seq]
    cw_ref, hp_ref, nw_ref, sconv_ref, s0_ref, o_ref, sout_ref, cout_ref, s_ref, ubuf_ref = refs[3 * n_seq:]
    c_len = chunk
    n_seg_all = n_seq * n_seg
    rows = n_seg_all * c_len
    t = pl.program_id(1)
    hist = CONV_W - 1
    base = 8
    conv_len = n_seg * c_len if carry else c_len
    n_conv = n_seq if carry else n_seg

    def init_history():
        ubuf_ref[:, 0:base, :] = jnp.zeros((n_conv, base, CONV_DIM), F32)
        ubuf_ref[:, base - hist:base, :] = sconv_ref[...]

    if carry:
        @pl.when(t == 0)
        def _():
            s_ref[...] = s0_ref[...]
            init_history()

        @pl.when(t > 0)
        def _():
            ubuf_ref[:, base - hist:base, :] = ubuf_ref[:, base + conv_len - hist:base + conv_len, :]
        for q in range(n_seq):
            ubuf_ref[q, base:base + conv_len, :] = qkv_refs[q][...]
    else:
        init_history()
        ubuf_ref[:, base:base + conv_len, :] = qkv_refs[0][...].reshape(n_conv, conv_len, CONV_DIM)

    cw = cw_ref[...]
    u = ubuf_ref[...]
    u1 = pltpu.roll(u, 1, axis=1)
    v = u1 * cw[0:1, :] + u * cw[1:2, :]
    y = (pltpu.roll(v, 2, axis=1) + u1 * cw[2:3, :]) + u * cw[3:4, :]
    y = _silu(y[:, base:, :]).reshape(rows, CONV_DIM)

    sm = sm_refs[0][...] if n_seq == 1 else jnp.concatenate([r[...] for r in sm_refs], axis=0)
    hp = hp_ref[...]
    lane = lax.broadcasted_iota(jnp.int32, (rows, LANES), 1)
    g_all = -jnp.exp(hp[0:1, :]) * jax.nn.softplus(sm + hp[1:2, :])
    g_all = jnp.where(lane < N_HEADS, g_all, 0.0)
    beta_all = jax.nn.sigmoid(sm)

    sh = int(round(math.log2(c_len)))
    r_b = lax.broadcasted_iota(jnp.int32, (rows, rows), 0)
    c_b = lax.broadcasted_iota(jnp.int32, (rows, rows), 1)
    seg_tril = ((r_b >> sh) == (c_b >> sh)) & (c_b <= r_b)
    gc = _dot01(jnp.where(seg_tril, 1.0, 0.0).astype(BF16), g_all)

    r_i = lax.broadcasted_iota(jnp.int32, (c_len, c_len), 0)
    c_i = lax.broadcasted_iota(jnp.int32, (c_len, c_len), 1)
    tri = c_i <= r_i
    strict = c_i < r_i
    eye = c_i == r_i
    nw = nw_ref[...]
    levels = sh

    qn, kn, kb, rhs, qg, gcol = [], [], [], [], [], []
    for h in range(N_HEADS):
        qh = y[:, h * DK_A:(h + 1) * DK_A]
        kh = y[:, W_A + h * DK_A:W_A + (h + 1) * DK_A]
        vh = y[:, 2 * W_A + h * DV_A:2 * W_A + (h + 1) * DV_A]
        qn_h = qh * lax.rsqrt(jnp.sum(qh * qh, axis=-1, keepdims=True) + 1e-6) * (DK_A ** -0.5)
        kn_h = kh * lax.rsqrt(jnp.sum(kh * kh, axis=-1, keepdims=True) + 1e-6)
        gcol_h = gc[:, h:h + 1]
        eg = jnp.exp(gcol_h)
        bcol = beta_all[:, N_HEADS + h:N_HEADS + h + 1]
        kb_h = kn_h * bcol
        qn.append(qn_h)
        kn.append(kn_h)
        kb.append(kb_h)
        rhs.append(jnp.concatenate([vh * bcol, kb_h * eg], axis=1))
        qg.append(qn_h * eg)
        gcol.append(gcol_h)

    pairs = [(g, h) for g in range(n_seg_all) for h in range(N_HEADS)]
    sl = lambda g: slice(g * c_len, (g + 1) * c_len)

    decay, kd, gl = {}, {}, {}
    for (g, h) in pairs:
        gcol_p = gcol[h][sl(g)]
        gcb = jnp.broadcast_to(gcol_p, (c_len, c_len))
        grow = jnp.sum(jnp.where(eye, gcb, 0.0), axis=0, keepdims=True)
        decay[g, h] = jnp.where(tri, jnp.exp(jnp.where(tri, gcb - grow, 0.0)), 0.0)
        glast = gcol[h][(g + 1) * c_len - 1:(g + 1) * c_len]
        kd[g, h] = kn[h][sl(g)] * jnp.exp(glast - gcol_p)
        gl[g, h] = jnp.exp(glast)

    kq = {p: _dot_nt(jnp.concatenate([kb[p[1]][sl(p[0])], qn[p[1]][sl(p[0])]], axis=0), kn[p[1]][sl(p[0])])
          for p in pairs}
    low = {p: jnp.where(strict, kq[p][:c_len] * decay[p], 0.0) for p in pairs}
    attn = {p: jnp.where(tri, kq[p][c_len:] * decay[p], 0.0) for p in pairs}
    ymat = {p: -low[p] for p in pairs}
    lp = low
    for _ in range(levels - 1):
        lp = {p: _dot(lp[p], lp[p]) for p in pairs}
        prod = {p: _dot(ymat[p], lp[p]) for p in pairs}
        ymat = {p: ymat[p] + lp[p] + prod[p] for p in pairs}
    sol = {}
    for (g, h) in pairs:
        rhs_p = rhs[h][sl(g)]
        sol[g, h] = rhs_p + _dot(ymat[g, h], rhs_p)

    def finish(g, h, o):
        r = o * lax.rsqrt(jnp.mean(o * o, axis=-1, keepdims=True) + RMS_EPS) * nw
        q, loc = (g // n_seg, sl(g % n_seg)) if carry else (0, sl(g))
        gated = r * _silu(z_refs[q][loc, h * DV_A:(h + 1) * DV_A])
        if carry:
            o_ref[q, loc, h * DV_A:(h + 1) * DV_A] = gated
        else:
            o_ref[loc, h * DV_A:(h + 1) * DV_A] = gated

    def advance(group, state):
        ws = {p: _dot(jnp.concatenate([sol[p][:, DV_A:], qg[p[1]][sl(p[0])]], axis=0), state[p]) for p in group}
        v_new = {p: sol[p][:, :DV_A] - ws[p][:c_len] for p in group}
        o = {p: ws[p][c_len:] + _dot(attn[p], v_new[p]) for p in group}
        new = {p: state[p] * gl[p] + _dot_tn(kd[p], v_new[p]) for p in group}
        for p in group:
            finish(p[0], p[1], o[p])
        return new

    if carry:
        chains = [(q, h) for q in range(n_seq) for h in range(N_HEADS)]
        cur = {c: s_ref[c[0], c[1]] for c in chains}
        for g in range(n_seg):
            group = [(q * n_seg + g, h) for (q, h) in chains]
            new = advance(group, {(q * n_seg + g, h): cur[q, h] for (q, h) in chains})
            cur = {(q, h): new[q * n_seg + g, h] for (q, h) in chains}
        for (q, h) in chains:
            s_ref[q, h] = cur[q, h]

        @pl.when(t == n_steps - 1)
        def _():
            for (q, h) in chains:
                sout_ref[q, h] = cur[q, h]
            cout_ref[...] = ubuf_ref[:, base + conv_len - hist:base + conv_len, :]
    else:
        new = advance(pairs, {p: s0_ref[p[0], p[1]] for p in pairs})
        for p in pairs:
            sout_ref[p[0], p[1]] = new[p]
        cout_ref[...] = ubuf_ref[:, base + conv_len - hist:base + conv_len, :]


def _gdn(p, row0, batch, seq, conv_w, hp, norm_w, s_conv, s0):
    chunk = math.gcd(seq, GDN_CHUNK)
    n_chunks = seq // chunk
    carry = n_chunks > 1
    if carry:
        n_seg = math.gcd(n_chunks, GDN_SEGS_CARRY)
        n_seq = math.gcd(batch, GDN_SEQS_CARRY)
        grid = (batch // n_seq, n_chunks // n_seg)
        n_conv, conv_len = n_seq, n_seg * chunk
    else:
        n_seg = math.gcd(batch, GDN_SEGS_INDEP)
        n_seq = 1
        grid = (batch // n_seg, 1)
        n_conv, conv_len = n_seg, chunk
    rows = n_seg * chunk
    n_steps = grid[1]
    blk0 = row0 // rows
    rowblk = lambda q: (lambda b, t: blk0 + (b * n_seq + q) * n_steps + t)
    pcol = lambda width, off: [pl.BlockSpec((rows, width), lambda b, t, rb=rowblk(q): (rb(b, t), off // width))
                               for q in range(n_seq)]
    kern = functools.partial(_gdn_kernel, chunk=chunk, n_seg=n_seg, n_seq=n_seq, carry=carry, n_steps=n_steps)
    if carry:
        o_shape, o_spec = (batch, seq, W_A), pl.BlockSpec((n_seq, rows, W_A), lambda b, t: (b, t, 0))
    else:
        o_shape, o_spec = (batch * seq, W_A), pl.BlockSpec((rows, W_A), lambda b, t: (b, 0))
    o, s_new, c_new = pl.pallas_call(
        kern,
        out_shape=(jax.ShapeDtypeStruct(o_shape, F32),
                   jax.ShapeDtypeStruct((batch, N_HEADS, DK_A, DV_A), F32),
                   jax.ShapeDtypeStruct((batch, CONV_W - 1, CONV_DIM), F32)),
        grid=grid,
        in_specs=[
            *pcol(CONV_DIM, OFF_QKV), *pcol(W_A, OFF_Z), *pcol(LANES, OFF_SMALL),
            pl.BlockSpec((CONV_W, CONV_DIM), lambda b, t: (0, 0)),
            pl.BlockSpec((8, LANES), lambda b, t: (0, 0)),
            pl.BlockSpec((1, DV_A), lambda b, t: (0, 0)),
            pl.BlockSpec((n_conv, CONV_W - 1, CONV_DIM), lambda b, t: (b, 0, 0)),
            pl.BlockSpec((n_conv, N_HEADS, DK_A, DV_A), lambda b, t: (b, 0, 0, 0)),
        ],
        out_specs=(
            o_spec,
            pl.BlockSpec((n_conv, N_HEADS, DK_A, DV_A), lambda b, t: (b, 0, 0, 0)),
            pl.BlockSpec((n_conv, CONV_W - 1, CONV_DIM), lambda b, t: (b, 0, 0)),
        ),
        scratch_shapes=[pltpu.VMEM((n_seq, N_HEADS, DK_A, DV_A), F32),
                        pltpu.VMEM((n_conv, conv_len + 8, CONV_DIM), F32)],
        compiler_params=pltpu.CompilerParams(
            dimension_semantics=("arbitrary", "arbitrary"), vmem_limit_bytes=VMEM_LIMIT),
        name=f"gdn_c{chunk}",
    )(*([p] * (3 * n_seq)), conv_w, hp, norm_w.reshape(1, DV_A), s_conv, s0)
    return o.reshape(batch * seq, W_A), s_new, c_new


def _gla_kernel(*refs, tb, cs, n_seq, carry, n_steps):
    qk_refs, v_refs, g_refs, sm_refs = (refs[k * n_seq:(k + 1) * n_seq] for k in range(4))
    wgk_ref, bgk_ref, nw_ref, s0_ref, o_ref, sout_ref, s_ref = refs[4 * n_seq:]
    t = pl.program_id(1)
    n_sub = tb // cs
    w_kb = N_HEADS * DK_B
    seqs = range(n_seq)
    heads = range(N_HEADS)

    if carry:
        @pl.when(t == 0)
        def _():
            s_ref[...] = s0_ref[...]

    sh = int(round(math.log2(cs)))
    r_i = lax.broadcasted_iota(jnp.int32, (tb, tb), 0)
    c_i = lax.broadcasted_iota(jnp.int32, (tb, tb), 1)
    same = (r_i >> sh) == (c_i >> sh)
    mask = same & (c_i <= r_i)
    mask_bf = jnp.where(mask, 1.0, 0.0).astype(BF16)
    same_bf = jnp.where(same, 1.0, 0.0).astype(BF16)
    onehot = jnp.where((lax.broadcasted_iota(jnp.int32, (tb, n_sub), 0) >> sh)
                       == lax.broadcasted_iota(jnp.int32, (tb, n_sub), 1), 1.0, 0.0).astype(BF16)
    tn = lambda part: lax.dot_general(part, onehot, (((0,), (0,)), ((), ())), preferred_element_type=F32)

    log_a = [jax.nn.log_sigmoid(_dot(sm_refs[q][...], wgk_ref[...]) + bgk_ref[...]) / GLA_GATE_NORM for q in seqs]
    gcum = [_dot01(mask_bf, la) for la in log_a]
    gtot = [_dot01(same_bf, la) for la in log_a]
    glcol = []
    for la in log_a:
        hi, mid, lo = _split3(la)
        glcol.append(jnp.exp(tn(hi) + tn(mid) + tn(lo)))

    nw = nw_ref[...]
    sl = lambda j: slice(j * cs, (j + 1) * cs)
    qg, kg, kd, vh = {}, {}, {}, {}
    for q in seqs:
        qk = qk_refs[q][...]
        for h in heads:
            qh = qk[:, h * DK_B:(h + 1) * DK_B] * (DK_B ** -0.5)
            kh = qk[:, w_kb + h * DK_B:w_kb + (h + 1) * DK_B]
            gh = gcum[q][:, h * DK_B:(h + 1) * DK_B]
            gt = gtot[q][:, h * DK_B:(h + 1) * DK_B]
            qg[q, h] = qh * jnp.exp(gh)
            kg[q, h] = kh * jnp.exp(-gh)
            kd[q, h] = kh * jnp.exp(gt - gh)
            vh[q, h] = v_refs[q][:, h * DV_B:(h + 1) * DV_B]
    chains = [(q, h) for q in seqs for h in heads]
    attn = {c: jnp.where(mask, _dot_nt(qg[c], kg[c]), 0.0) for c in chains}
    o_intra = {c: _dot(attn[c], vh[c]) for c in chains}
    upd = {(j, c): _dot_tn(kd[c][sl(j)], vh[c][sl(j)]) for j in range(n_sub) for c in chains}
    gl = lambda j, c: glcol[c[0]][c[1] * DK_B:(c[1] + 1) * DK_B, j:j + 1]

    o_inter = {c: [] for c in chains}
    if carry:
        cur = {c: s_ref[c[0], c[1]] for c in chains}
        for j in range(n_sub):
            for c in chains:
                o_inter[c].append(_dot(qg[c][sl(j)], cur[c]))
                cur[c] = cur[c] * gl(j, c) + upd[j, c]
        for c in chains:
            s_ref[c[0], c[1]] = cur[c]

        @pl.when(t == n_steps - 1)
        def _():
            for c in chains:
                sout_ref[c[0], c[1]] = cur[c]
    else:
        for j in range(n_sub):
            for c in chains:
                s = s0_ref[j, c[1]]
                o_inter[c].append(_dot(qg[c][sl(j)], s))
                sout_ref[j, c[1]] = s * gl(j, c) + upd[j, c]

    for (q, h) in chains:
        o = o_intra[q, h] + jnp.concatenate(o_inter[q, h], axis=0)
        r = o * lax.rsqrt(jnp.mean(o * o, axis=-1, keepdims=True) + RMS_EPS) * nw
        gated = r * _silu(g_refs[q][:, h * DV_B:(h + 1) * DV_B])
        if carry:
            o_ref[q, :, h * DV_B:(h + 1) * DV_B] = gated
        else:
            o_ref[:, h * DV_B:(h + 1) * DV_B] = gated


def _gla(p, row0, batch, seq, wgk, bgk, norm_w, s0):
    cs = math.gcd(seq, GLA_CHUNK)
    carry = seq > cs
    if carry:
        tb = math.gcd(seq, GLA_TB)
        n_seq = math.gcd(batch, GLA_SEQS_CARRY)
        grid = (batch // n_seq, seq // tb)
        n_state = n_seq
    else:
        tb = math.gcd(batch * seq, GLA_TB)
        n_seq = 1
        n_state = tb // cs
        grid = (batch // n_state, 1)
    n_steps = grid[1]
    blk0 = row0 // tb
    rowblk = lambda q: (lambda b, t: blk0 + (b * n_seq + q) * n_steps + t)
    pcol = lambda width, off: [pl.BlockSpec((tb, width), lambda b, t, rb=rowblk(q): (rb(b, t), off // width))
                               for q in range(n_seq)]
    kern = functools.partial(_gla_kernel, tb=tb, cs=cs, n_seq=n_seq, carry=carry, n_steps=n_steps)
    if carry:
        o_shape, o_spec = (batch, seq, W_B), pl.BlockSpec((n_seq, tb, W_B), lambda b, t: (b, t, 0))
    else:
        o_shape, o_spec = (batch * seq, W_B), pl.BlockSpec((tb, W_B), lambda b, t: (b, 0))
    o, s_new = pl.pallas_call(
        kern,
        out_shape=(jax.ShapeDtypeStruct(o_shape, F32),
                   jax.ShapeDtypeStruct((batch, N_HEADS, DK_B, DV_B), F32)),
        grid=grid,
        in_specs=[
            *pcol(2 * N_HEADS * DK_B, OFF_QK_B), *pcol(W_B, OFF_V_B), *pcol(W_B, OFF_G_B), *pcol(LANES, OFF_SMALL),
            pl.BlockSpec((LANES, N_HEADS * DK_B), lambda b, t: (0, 0)),
            pl.BlockSpec((1, N_HEADS * DK_B), lambda b, t: (0, 0)),
            pl.BlockSpec((1, DV_B), lambda b, t: (0, 0)),
            pl.BlockSpec((n_state, N_HEADS, DK_B, DV_B), lambda b, t: (b, 0, 0, 0)),
        ],
        out_specs=(
            o_spec,
            pl.BlockSpec((n_state, N_HEADS, DK_B, DV_B), lambda b, t: (b, 0, 0, 0)),
        ),
        scratch_shapes=[pltpu.VMEM((n_seq, N_HEADS, DK_B, DV_B), F32)],
        compiler_params=pltpu.CompilerParams(
            dimension_semantics=("arbitrary", "arbitrary"), vmem_limit_bytes=VMEM_LIMIT),
        name=f"gla_c{cs}",
    )(*([p] * (4 * n_seq)), wgk, bgk.reshape(1, N_HEADS * DK_B), norm_w.reshape(1, DV_B), s0)
    return o.reshape(batch * seq, W_B), s_new


def _topk_softmax(logits):
    tm = logits.shape[0]
    lane = lax.broadcasted_iota(jnp.int32, (tm, LANES), 1)
    lane_f = lane.astype(F32)
    neg = jnp.float32(-jnp.inf)
    cur = jnp.where(lane < N_EXPERTS, logits, neg)
    vals, idxs = [], []
    for _ in range(TOP_K):
        m = jnp.max(cur, axis=-1, keepdims=True)
        idx = jnp.min(jnp.where(cur == m, lane_f, float(LANES)), axis=-1, keepdims=True)
        vals.append(m)
        idxs.append(idx)
        cur = jnp.where(lane_f == idx, neg, cur)
    exps = [jnp.exp(v - vals[0]) for v in vals]
    den = exps[0] + exps[1] + exps[2] + exps[3]
    te = jnp.zeros((tm, LANES), F32)
    tg = jnp.zeros((tm, LANES), F32)
    for k in range(TOP_K):
        te = jnp.where(lane == k, idxs[k], te)
        tg = jnp.where(lane == k, exps[k] / den, tg)
    return te.astype(jnp.int32), tg


def _merge_kernel(xp_ref, xs_ref, oap_ref, oas_ref, obp_ref, obs_ref, ga_ref, gb_ref, wua_ref, wub_ref, wo_ref,
                  n2_ref, wr_ref, br_ref, x1_ref, h2_ref, te_ref, tg_ref, *, n_first):
    i = pl.program_id(0)
    tm = x1_ref.shape[0]
    hm = tm // MERGE_SPLIT
    groups = [slice(a * hm, (a + 1) * hm) for a in range(MERGE_SPLIT)]
    oa = _pick(i, n_first, oap_ref, oas_ref).astype(BF16)
    ob = _pick(i, n_first, obp_ref, obs_ref).astype(BF16)
    x = _pick(i, n_first, xp_ref, xs_ref)
    ua = [jnp.dot(oa[g], wua_ref[...], preferred_element_type=F32) for g in groups]
    ub = [jnp.dot(ob[g], wub_ref[...], preferred_element_type=F32) for g in groups]
    merged = [jax.nn.sigmoid(ga_ref[g, :]) * ua[a] + jax.nn.sigmoid(gb_ref[g, :]) * ub[a]
              for a, g in enumerate(groups)]
    x1 = [x[g] + jnp.dot(merged[a].astype(BF16), wo_ref[...], preferred_element_type=F32)
          for a, g in enumerate(groups)]
    h2 = [v * lax.rsqrt(jnp.mean(v * v, axis=-1, keepdims=True) + RMS_EPS) * n2_ref[...] for v in x1]
    h_hi = [v.astype(BF16) for v in h2]
    h_lo = [(v - hi.astype(F32)).astype(BF16) for v, hi in zip(h2, h_hi)]
    l_hi = [jnp.dot(hi, wr_ref[...], preferred_element_type=F32) for hi in h_hi]
    l_lo = [jnp.dot(lo, wr_ref[:, :LANES], preferred_element_type=F32) for lo in h_lo]
    for a, g in enumerate(groups):
        x1_ref[g, :] = x1[a]
        _store_packed_rows(h2_ref, (), a * hm, _pack_bf16_pairs(h2[a]))
        logits = l_hi[a][:, :LANES] + l_hi[a][:, LANES:] + l_lo[a] + br_ref[...]
        te, tg = _topk_softmax(logits)
        te_ref[g, :] = te
        tg_ref[g, :] = tg


def _merge(x_p, x_s, oa_p, oa_s, ob_p, ob_s, p, wua, wub, wo, norm2_w, wr, br):
    n = x_p.shape[0] + x_s.shape[0]
    tm = MERGE_TM
    n_first = x_p.shape[0] // tm
    row = lambda i: (i, 0)
    const = lambda i: (0, 0)
    return pl.pallas_call(
        functools.partial(_merge_kernel, n_first=n_first),
        out_shape=(jax.ShapeDtypeStruct((n, D_MODEL), F32), jax.ShapeDtypeStruct((n * PACK_ROWS, LANES), jnp.uint32),
                   jax.ShapeDtypeStruct((n, LANES), jnp.int32), jax.ShapeDtypeStruct((n, LANES), F32)),
        grid=(n // tm,),
        in_specs=[
            *_split_specs((tm, D_MODEL), n_first),
            *_split_specs((tm, W_A), n_first),
            *_split_specs((tm, W_B), n_first),
            pl.BlockSpec((tm, D_MODEL), lambda i: (i, OFF_GATE_A // D_MODEL)),
            pl.BlockSpec((tm, D_MODEL), lambda i: (i, OFF_GATE_B // D_MODEL)),
            pl.BlockSpec((W_A, D_MODEL), const),
            pl.BlockSpec((W_B, D_MODEL), const),
            pl.BlockSpec((D_MODEL, D_MODEL), const),
            pl.BlockSpec((1, D_MODEL), const),
            pl.BlockSpec((D_MODEL, 2 * LANES), const),
            pl.BlockSpec((1, LANES), const),
        ],
        out_specs=(pl.BlockSpec((tm, D_MODEL), row), pl.BlockSpec((tm * PACK_ROWS, LANES), row),
                   pl.BlockSpec((tm, LANES), row), pl.BlockSpec((tm, LANES), row)),
        compiler_params=pltpu.CompilerParams(
            dimension_semantics=("arbitrary",), vmem_limit_bytes=VMEM_LIMIT),
        name="merge_router",
    )(x_p, x_s, oa_p, oa_s, ob_p, ob_s, p, p, wua, wub, wo, norm2_w.reshape(1, D_MODEL), wr, br)


def _moe_kernel(blk_e_ref, nxt_e_ref, n_sub_ref, n_used_ref, src_ref, src_nxt_ref, dst_ref, h2_ref, wgu_hbm, bgu_ref,
                wdn_hbm, bdn_ref, y_hbm, xstage, ybuf, wgu_stage, wdn_stage, wgu_bf, wdn_bf, ssem, wsem):
    i = pl.program_id(0)
    n_used = n_used_ref[0]
    bm = MOE_BM
    pr = PACK_ROWS
    e_cur = blk_e_ref[i]

    def weight_copies(e):
        return (pltpu.make_async_copy(wgu_hbm.at[e], wgu_stage, wsem.at[0]),
                pltpu.make_async_copy(wdn_hbm.at[e], wdn_stage, wsem.at[1]))

    @pl.when((i == 0) & (n_used > 0))
    def _():
        for cp in weight_copies(e_cur):
            cp.start()

    @pl.when((i < n_used) & ((i == 0) | (e_cur != blk_e_ref[jnp.maximum(i - 1, 0)])))
    def _():
        for cp in weight_copies(e_cur):
            cp.wait()
        wgu_bf[...] = wgu_stage[...].astype(BF16)
        wdn_bf[...] = wdn_stage[...].astype(BF16)

        @pl.when(nxt_e_ref[i] != e_cur)
        def _():
            for cp in weight_copies(nxt_e_ref[i]):
                cp.start()

    def scatter_copy(row, s, r):
        return pltpu.make_async_copy(ybuf.at[s, pl.ds(r * pr, pr), :],
                                     y_hbm.at[pl.ds(pl.multiple_of(row * pr, pr), pr), :], ssem.at[s])

    @pl.when(i == 0)
    def _():
        ybuf[...] = jnp.zeros_like(ybuf)
        n_real = y_hbm.shape[0] - MOE_SUB * bm * pr
        for s in range(MOE_SUB):
            sink = pltpu.make_async_copy(ybuf.at[s], y_hbm.at[pl.ds(n_real + s * bm * pr, bm * pr), :], ssem.at[s])
            sink.start()
            sink.wait()

    def gather(idx_ref, sub, slot):
        for r in range(bm):
            tok = pl.multiple_of(idx_ref[0, 0, sub * bm + r] * pr, pr)
            xstage[slot, pl.ds(r * pr, pr), :] = h2_ref[pl.ds(tok, pr), :]

    def wait_prev_scatter(s):
        @pl.when((i >= 1) & (n_sub_ref[jnp.maximum(i - 1, 0)] > s))
        def _():
            for r in range(bm):
                scatter_copy(0, s, r).wait()

    def expert_mlp(s):
        x = _load_packed_rows(xstage, (s,), bm, BF16)
        gu = jnp.dot(x, wgu_bf[...], preferred_element_type=F32) + bgu_ref[...]
        gate = jnp.minimum(gu[:, :D_FF], SWIGLU_LIMIT)
        up = jnp.clip(gu[:, D_FF:], -SWIGLU_LIMIT, SWIGLU_LIMIT)
        hmid = (up + 1.0) * (gate * jax.nn.sigmoid(SWIGLU_ALPHA * gate))
        y = jnp.dot(hmid.astype(BF16), wdn_bf[...], preferred_element_type=F32) + bdn_ref[...]
        _store_packed_rows(ybuf, (s,), 0, _pack_bf16_pairs(y))
        for r in range(bm):
            scatter_copy(dst_ref[0, 0, s * bm + r], s, r).start(priority=r % 2)

    assert MOE_SUB == 2

    @pl.when((i == 0) & (n_used > 0))
    def _():
        gather(src_ref, 0, 0)

    @pl.when(i < n_used)
    def _():
        wait_prev_scatter(0)
        gather(src_ref, 1, 1)
        expert_mlp(0)
        wait_prev_scatter(1)

        @pl.when(n_sub_ref[i] > 1)
        def _():
            gather(src_nxt_ref, 0, 0)
            expert_mlp(1)

        @pl.when(n_sub_ref[i] <= 1)
        def _():
            gather(src_nxt_ref, 0, 0)

    @pl.when(i == n_used - 1)
    def _():
        for s in range(MOE_SUB):
            @pl.when(n_sub_ref[i] > s)
            def _():
                for r in range(bm):
                    scatter_copy(0, s, r).wait()


def _moe(blk_e, nxt_e, n_sub, n_used, src_tok, dst_row, h2, wgu, bgu, wdn, bdn, y_rows):
    n_steps = src_tok.shape[0]
    step_rows = MOE_SUB * MOE_BM
    cur = lambda i, be, ne, ns, nu: (i, 0, 0)
    wsel = lambda i, be, ne, ns, nu: (be[i], 0, 0)
    smem_blk = lambda im: pl.BlockSpec((1, 1, step_rows), im, memory_space=pltpu.SMEM)
    grid_spec = pltpu.PrefetchScalarGridSpec(
        num_scalar_prefetch=4,
        grid=(n_steps,),
        in_specs=[
            smem_blk(cur), smem_blk(lambda i, be, ne, ns, nu: (jnp.minimum(i + 1, n_steps - 1), 0, 0)), smem_blk(cur),
            pl.BlockSpec(h2.shape, lambda i, be, ne, ns, nu: (0, 0), pipeline_mode=pl.Buffered(1)),
            pl.BlockSpec(memory_space=pl.ANY),
            pl.BlockSpec((None, 1, 2 * D_FF), wsel),
            pl.BlockSpec(memory_space=pl.ANY),
            pl.BlockSpec((None, 1, D_MODEL), wsel),
        ],
        out_specs=pl.BlockSpec(memory_space=pl.ANY),
        scratch_shapes=[pltpu.VMEM((MOE_SUB, MOE_BM * PACK_ROWS, LANES), jnp.uint32),
                        pltpu.VMEM((MOE_SUB, MOE_BM * PACK_ROWS, LANES), jnp.uint32),
                        pltpu.VMEM((D_MODEL, 2 * D_FF), F32), pltpu.VMEM((D_FF, D_MODEL), F32),
                        pltpu.VMEM((D_MODEL, 2 * D_FF), BF16), pltpu.VMEM((D_FF, D_MODEL), BF16),
                        pltpu.SemaphoreType.DMA((MOE_SUB,)), pltpu.SemaphoreType.DMA((2,))],
    )
    return pl.pallas_call(
        _moe_kernel,
        out_shape=jax.ShapeDtypeStruct((y_rows * PACK_ROWS, LANES), jnp.uint32),
        grid_spec=grid_spec,
        compiler_params=pltpu.CompilerParams(
            dimension_semantics=("arbitrary",), vmem_limit_bytes=MOE_VMEM_LIMIT),
        name="moe_experts",
    )(blk_e, nxt_e, n_sub, n_used, src_tok, src_tok, dst_row, h2, wgu, bgu, wdn, bdn)


def _route(top_e, n_tok):
    bm = MOE_SUB * MOE_BM
    n_assign = n_tok * TOP_K
    n_blocks = n_assign // bm + N_EXPERTS
    n_dummy = N_EXPERTS * bm
    flat_e = top_e.reshape(-1)
    counts = jnp.sum((flat_e[:, None] == jnp.arange(N_EXPERTS, dtype=jnp.int32)[None, :]).astype(jnp.int32), axis=0)
    padded = (counts + bm - 1) // bm * bm
    pad_end = jnp.cumsum(padded)
    d_e = jnp.arange(n_dummy, dtype=jnp.int32) // bm
    d_active = (jnp.arange(n_dummy, dtype=jnp.int32) % bm) < (padded - counts)[d_e]
    keys = jnp.concatenate([flat_e * 2, jnp.where(d_active, d_e * 2 + 1, 2 * N_EXPERTS)])
    vbits = n_assign.bit_length()
    marker = (1 << vbits) - 1
    vals = jnp.concatenate([jnp.arange(n_assign, dtype=jnp.int32), jnp.full((n_dummy,), marker, jnp.int32)])
    assign = jnp.sort(keys * (1 << vbits) + vals)[:n_blocks * bm] & marker
    rows = jnp.arange(n_blocks * bm, dtype=jnp.int32)
    valid = assign != marker
    src_tok = jnp.where(valid, assign // TOP_K, 0)
    dst_row = jnp.where(valid, (assign % TOP_K) * n_tok + assign // TOP_K, n_assign + rows % bm)
    blk_start = jnp.arange(n_blocks, dtype=jnp.int32) * bm
    blk_e = jnp.minimum(jnp.sum((pad_end[None, :] <= blk_start[:, None]).astype(jnp.int32), axis=1),
                        N_EXPERTS - 1).astype(jnp.int32)
    n_used = (pad_end[-1] // bm).astype(jnp.int32).reshape(1)
    later = (blk_e[None, :] > blk_e[:, None]) & (blk_start[None, :] < pad_end[-1])
    nxt_e = jnp.min(jnp.where(later, blk_e[None, :], N_EXPERTS), axis=1)
    nxt_e = jnp.where(nxt_e == N_EXPERTS, blk_e, nxt_e).astype(jnp.int32)
    n_sub = jnp.sum(valid.reshape(n_blocks, MOE_SUB, MOE_BM)[:, :, 0].astype(jnp.int32), axis=1)
    return (blk_e, nxt_e, n_sub, n_used, src_tok.reshape(n_blocks, 1, bm).astype(jnp.int32),
            dst_row.reshape(n_blocks, 1, bm).astype(jnp.int32))


def _combine_kernel(x1_ref, y0_ref, y1_ref, y2_ref, y3_ref, tg_ref, nf_ref, op_ref, os_ref, *, n_first):
    tg = tg_ref[...]
    tm = x1_ref.shape[0]
    rows = lambda y_ref: _load_packed_rows(y_ref, (), tm, F32)
    acc = rows(y0_ref) * tg[:, 0:1]
    for k, y_ref in enumerate((y1_ref, y2_ref, y3_ref), start=1):
        acc = acc + rows(y_ref) * tg[:, k:k + 1]
    x = x1_ref[...] + acc
    out = x * lax.rsqrt(jnp.mean(x * x, axis=-1, keepdims=True) + RMS_EPS) * nf_ref[...]
    i = pl.program_id(0)

    @pl.when(i < n_first)
    def _():
        op_ref[...] = out

    @pl.when(i >= n_first)
    def _():
        os_ref[...] = out


def _combine(x1, y, tg, norm_f_w, n_p):
    n = x1.shape[0]
    tm = COMBINE_TM
    n_first = n_p // tm
    row = lambda i: (i, 0)
    y_spec = lambda k: pl.BlockSpec((tm * PACK_ROWS, LANES), lambda i: (k * (n // tm) + i, 0))
    return pl.pallas_call(
        functools.partial(_combine_kernel, n_first=n_first),
        out_shape=(jax.ShapeDtypeStruct((n_p, D_MODEL), F32), jax.ShapeDtypeStruct((n - n_p, D_MODEL), F32)),
        grid=(n // tm,),
        in_specs=[pl.BlockSpec((tm, D_MODEL), row), y_spec(0), y_spec(1), y_spec(2), y_spec(3),
                  pl.BlockSpec((tm, LANES), row), pl.BlockSpec((1, D_MODEL), lambda i: (0, 0))],
        out_specs=_split_specs((tm, D_MODEL), n_first),
        compiler_params=pltpu.CompilerParams(
            dimension_semantics=("arbitrary",), vmem_limit_bytes=VMEM_LIMIT),
        name="combine_norm",
    )(x1, y, y, y, y, tg, norm_f_w.reshape(1, D_MODEL))


def _pack_w_in(w_in):
    s = np.cumsum((0,) + IN_SIZES)
    w_in = w_in.astype(BF16)
    qkv, z, al, be, qb, kb, vb, gb, gk, ga, gbt = [w_in[:, s[i]:s[i + 1]] for i in range(len(IN_SIZES))]
    small = jnp.concatenate(
        [al, be, gk, jnp.zeros((D_MODEL, LANES - 2 * N_HEADS - GK_RANK), w_in.dtype)], axis=1)
    return jnp.concatenate([qkv, z, ga, gbt, qb, kb, vb, gb, small], axis=1)


def kernel(x_prompt, x_sample, state_gdn, state_gdn_conv, state_gla, norm1_w, w_in, conv_w, a_log,
           dt_bias, gdn_norm_w, gla_gk_up, gla_gk_b, gla_norm_w, w_up_a, w_up_b, w_o, norm2_w,
           w_router, b_router, w_gate_up, b_gate_up, w_down, b_down, norm_f_w):
    bp, tp, _ = x_prompt.shape
    bs, ts, _ = x_sample.shape
    n_p, n_s = bp * tp, bs * ts
    n = n_p + n_s
    l = 0

    x_p = x_prompt.reshape(n_p, D_MODEL)
    x_s = x_sample.reshape(n_s, D_MODEL)
    p = _inproj(x_p, x_s, norm1_w[l], _pack_w_in(w_in[l]))

    hp = jnp.zeros((8, LANES), F32).at[0, :N_HEADS].set(a_log[l]).at[1, :N_HEADS].set(dt_bias[l])
    wgk = jnp.zeros((LANES, N_HEADS * DK_B), F32).at[2 * N_HEADS:2 * N_HEADS + GK_RANK].set(gla_gk_up[l]).astype(BF16)
    zeros = lambda shape: jnp.zeros(shape, F32)

    oa_p, gdn_p, conv_p = _gdn(p, 0, bp, tp, conv_w[l], hp, gdn_norm_w[l],
                               zeros((bp, CONV_W - 1, CONV_DIM)), zeros((bp, N_HEADS, DK_A, DV_A)))
    oa_s, gdn_s, conv_s = _gdn(p, n_p, bs, ts, conv_w[l], hp, gdn_norm_w[l], state_gdn_conv[l], state_gdn[l])
    ob_p, gla_p = _gla(p, 0, bp, tp, wgk, gla_gk_b[l], gla_norm_w[l], zeros((bp, N_HEADS, DK_B, DV_B)))
    ob_s, gla_s = _gla(p, n_p, bs, ts, wgk, gla_gk_b[l], gla_norm_w[l], state_gla[l])

    wr = jnp.zeros((D_MODEL, LANES), F32).at[:, :N_EXPERTS].set(w_router[l])
    wr_hi = wr.astype(BF16)
    wr = jnp.concatenate([wr_hi, (wr - wr_hi.astype(F32)).astype(BF16)], axis=1)
    br = jnp.zeros((1, LANES), F32).at[0, :N_EXPERTS].set(b_router[l])
    x1, h2, te, tg = _merge(x_p, x_s, oa_p, oa_s, ob_p, ob_s, p, w_up_a[l].astype(BF16),
                            w_up_b[l].astype(BF16), w_o[l].astype(BF16), norm2_w[l], wr, br)

    blk_e, nxt_e, n_sub, n_used, src_tok, dst_row = _route(te[:, :TOP_K], n)
    y = _moe(blk_e, nxt_e, n_sub, n_used, src_tok, dst_row, h2, w_gate_up[l],
             b_gate_up[l].reshape(N_EXPERTS, 1, 2 * D_FF), w_down[l],
             b_down[l].reshape(N_EXPERTS, 1, D_MODEL), n * TOP_K + MOE_SUB * MOE_BM)
    out_p, out_s = _combine(x1, y, tg, norm_f_w, n_p)
    y_prompt = out_p.reshape(bp, tp, D_MODEL)
    y_sample = out_s.reshape(bs, ts, D_MODEL)
    return (y_prompt, y_sample, gdn_p[None], conv_p[None], gla_p[None], gdn_s[None], conv_s[None],
            gla_s[None])
```

```python
import functools
import math

import jax
import jax.numpy as jnp
import numpy as np
from jax import lax
from jax.experimental import pallas as pl
from jax.experimental.pallas import tpu as pltpu

F32 = jnp.float32
BF16 = jnp.bfloat16

D_MODEL = 1024
N_HEADS = 4
DK_A = 128
DV_A = 128
W_A = N_HEADS * DV_A
CONV_W = 4
CONV_DIM = 2 * N_HEADS * DK_A + W_A
GDN_CHUNK = 64
DK_B = 64
DV_B = 128
W_B = N_HEADS * DV_B
GK_RANK = 16
GLA_GATE_NORM = 16.0
GLA_CHUNK = 16
N_EXPERTS = 32
TOP_K = 4
D_FF = 1024
SWIGLU_LIMIT = 7.0
SWIGLU_ALPHA = 1.702
RMS_EPS = 1e-6
IN_SIZES = (CONV_DIM, W_A, N_HEADS, N_HEADS, N_HEADS * DK_B, N_HEADS * DK_B, W_B, W_B, GK_RANK,
            D_MODEL, D_MODEL)

LANES = 128

OFF_QKV = 0
OFF_Z = 1536
OFF_GATE_A = 2048
OFF_GATE_B = 3072
OFF_QK_B = 4096
OFF_V_B = 4608
OFF_G_B = 5120
OFF_SMALL = 5632
P_COLS = 5760

INPROJ_TM = 512
GLA_TB = 256
GLA_SEQS_CARRY = 4
GDN_SEGS_CARRY = 4
GDN_SEQS_CARRY = 2
GDN_SEGS_INDEP = 8
MERGE_TM = 512
MERGE_SPLIT = 2
MOE_BM = 256
SCATTER_CHUNK = 32
MOE_SUB = 2
PACK_ROWS = D_MODEL // 2 // LANES
MOE_VMEM_LIMIT = 62 * 1024 * 1024
COMBINE_TM = 256
VMEM_LIMIT = 56 * 1024 * 1024


def _dot(a, b):
    return jnp.dot(a.astype(BF16), b.astype(BF16), preferred_element_type=F32)


def _dot_nt(a, b):
    return lax.dot_general(a.astype(BF16), b.astype(BF16), (((1,), (1,)), ((), ())),
                           preferred_element_type=F32)


def _dot_tn(a, b):
    return lax.dot_general(a.astype(BF16), b.astype(BF16), (((0,), (0,)), ((), ())), preferred_element_type=F32)


def _split3(x):
    hi = x.astype(BF16)
    r1 = x - hi.astype(F32)
    mid = r1.astype(BF16)
    lo = (r1 - mid.astype(F32)).astype(BF16)
    return hi, mid, lo


def _dot01(m01, x):
    hi, mid, lo = _split3(x)
    d = lambda p: jnp.dot(m01, p, preferred_element_type=F32)
    return d(hi) + d(mid) + d(lo)


def _silu(x):
    return x * jax.nn.sigmoid(x)


def _pack_bf16_pairs(x):
    bits = pltpu.bitcast(x.astype(BF16).astype(F32), jnp.uint32)
    half = D_MODEL // 2
    return (bits[:, :half] >> 16) | (bits[:, half:] & jnp.uint32(0xFFFF0000))


def _store_packed_rows(ref, lead, row0, words):
    n = words.shape[0]
    for c in range(PACK_ROWS):
        ref[(*lead, pl.ds(row0 * PACK_ROWS + c, n, stride=PACK_ROWS), slice(None))] = words[:, c * LANES:(c + 1) * LANES]


def _load_packed_rows(ref, lead, n, dtype):
    words = [ref[(*lead, pl.ds(c, n, stride=PACK_ROWS), slice(None))] for c in range(PACK_ROWS)]
    lo = [pltpu.bitcast(w << 16, F32).astype(dtype) for w in words]
    hi = [pltpu.bitcast(w & jnp.uint32(0xFFFF0000), F32).astype(dtype) for w in words]
    return jnp.concatenate(lo + hi, axis=1)


def _pick(i, n_first, first_ref, second_ref):
    return jnp.where(i < n_first, first_ref[...], second_ref[...])


def _split_specs(block, n_first):
    return (pl.BlockSpec(block, lambda i: (jnp.minimum(i, n_first - 1), 0)),
            pl.BlockSpec(block, lambda i: (jnp.maximum(i - n_first, 0), 0)))


def _inproj_kernel(xp_ref, xs_ref, nw_ref, w_ref, o_ref, *, n_first):
    x = _pick(pl.program_id(0), n_first, xp_ref, xs_ref)
    ms = jnp.mean(x * x, axis=-1, keepdims=True)
    h = (x * lax.rsqrt(ms + RMS_EPS) * nw_ref[...]).astype(BF16)
    o_ref[...] = jnp.dot(h, w_ref[...], preferred_element_type=F32)


def _inproj(x_p, x_s, norm_w, w_packed):
    tm = INPROJ_TM
    n_first = x_p.shape[0] // tm
    n = x_p.shape[0] + x_s.shape[0]
    return pl.pallas_call(
        functools.partial(_inproj_kernel, n_first=n_first),
        out_shape=jax.ShapeDtypeStruct((n, P_COLS), F32),
        grid=(n // tm,),
        in_specs=[
            *_split_specs((tm, D_MODEL), n_first),
            pl.BlockSpec((1, D_MODEL), lambda i: (0, 0)),
            pl.BlockSpec((D_MODEL, P_COLS), lambda i: (0, 0), pipeline_mode=pl.Buffered(1)),
        ],
        out_specs=pl.BlockSpec((tm, P_COLS), lambda i: (i, 0)),
        compiler_params=pltpu.CompilerParams(
            dimension_semantics=("arbitrary",), vmem_limit_bytes=VMEM_LIMIT),
        name="inproj",
    )(x_p, x_s, norm_w.reshape(1, D_MODEL), w_packed)


def _gdn_kernel(*refs, chunk, n_seg, n_seq, carry, n_steps):
    qkv_refs, z_refs, sm_refs = refs[:n_seq], refs[n_seq:2 * n_seq], refs[2 * n_seq:3 * n_seq]
    cw_ref, hp_ref, nw_ref, sconv_ref, s0_ref, o_ref, sout_ref, cout_ref, s_ref, ubuf_ref = refs[3 * n_seq:]
    c_len = chunk
    n_seg_all = n_seq * n_seg
    rows = n_seg_all * c_len
    t = pl.program_id(1)
    hist = CONV_W - 1
    base = 8
    conv_len = n_seg * c_len if carry else c_len
    n_conv = n_seq if carry else n_seg

    def init_history():
        ubuf_ref[:, 0:base, :] = jnp.zeros((n_conv, base, CONV_DIM), F32)
        ubuf_ref[:, base - hist:base, :] = sconv_ref[...]

    if carry:
        @pl.when(t == 0)
        def _():
            s_ref[...] = s0_ref[...]
            init_history()

        @pl.when(t > 0)
        def _():
            ubuf_ref[:, base - hist:base, :] = ubuf_ref[:, base + conv_len - hist:base + conv_len, :]
        for q in range(n_seq):
            ubuf_ref[q, base:base + conv_len, :] = qkv_refs[q][...]
    else:
        init_history()
        ubuf_ref[:, base:base + conv_len, :] = qkv_refs[0][...].reshape(n_conv, conv_len, CONV_DIM)

    cw = cw_ref[...]
    u = ubuf_ref[...]
    u1 = pltpu.roll(u, 1, axis=1)
    v = u1 * cw[0:1, :] + u * cw[1:2, :]
    y = (pltpu.roll(v, 2, axis=1) + u1 * cw[2:3, :]) + u * cw[3:4, :]
    y = _silu(y[:, base:, :]).reshape(rows, CONV_DIM)

    sm = sm_refs[0][...] if n_seq == 1 else jnp.concatenate([r[...] for r in sm_refs], axis=0)
    hp = hp_ref[...]
    lane = lax.broadcasted_iota(jnp.int32, (rows, LANES), 1)
    g_all = -jnp.exp(hp[0:1, :]) * jax.nn.softplus(sm + hp[1:2, :])
    g_all = jnp.where(lane < N_HEADS, g_all, 0.0)
    beta_all = jax.nn.sigmoid(sm)

    sh = int(round(math.log2(c_len)))
    r_b = lax.broadcasted_iota(jnp.int32, (rows, rows), 0)
    c_b = lax.broadcasted_iota(jnp.int32, (rows, rows), 1)
    seg_tril = ((r_b >> sh) == (c_b >> sh)) & (c_b <= r_b)
    gc = _dot01(jnp.where(seg_tril, 1.0, 0.0).astype(BF16), g_all)

    r_i = lax.broadcasted_iota(jnp.int32, (c_len, c_len), 0)
    c_i = lax.broadcasted_iota(jnp.int32, (c_len, c_len), 1)
    tri = c_i <= r_i
    strict = c_i < r_i
    eye = c_i == r_i
    nw = nw_ref[...]
    levels = sh

    qn, kn, kb, rhs, qg, gcol = [], [], [], [], [], []
    for h in range(N_HEADS):
        qh = y[:, h * DK_A:(h + 1) * DK_A]
        kh = y[:, W_A + h * DK_A:W_A + (h + 1) * DK_A]
        vh = y[:, 2 * W_A + h * DV_A:2 * W_A + (h + 1) * DV_A]
        qn_h = qh * lax.rsqrt(jnp.sum(qh * qh, axis=-1, keepdims=True) + 1e-6) * (DK_A ** -0.5)
        kn_h = kh * lax.rsqrt(jnp.sum(kh * kh, axis=-1, keepdims=True) + 1e-6)
        gcol_h = gc[:, h:h + 1]
        eg = jnp.exp(gcol_h)
        bcol = beta_all[:, N_HEADS + h:N_HEADS + h + 1]
        kb_h = kn_h * bcol
        qn.append(qn_h)
        kn.append(kn_h)
        kb.append(kb_h)
        rhs.append(jnp.concatenate([vh * bcol, kb_h * eg], axis=1))
        qg.append(qn_h * eg)
        gcol.append(gcol_h)

    pairs = [(g, h) for g in range(n_seg_all) for h in range(N_HEADS)]
    sl = lambda g: slice(g * c_len, (g + 1) * c_len)

    decay, kd, gl = {}, {}, {}
    for (g, h) in pairs:
        gcol_p = gcol[h][sl(g)]
        gcb = jnp.broadcast_to(gcol_p, (c_len, c_len))
        grow = jnp.sum(jnp.where(eye, gcb, 0.0), axis=0, keepdims=True)
        decay[g, h] = jnp.where(tri, jnp.exp(jnp.where(tri, gcb - grow, 0.0)), 0.0)
        glast = gcol[h][(g + 1) * c_len - 1:(g + 1) * c_len]
        kd[g, h] = kn[h][sl(g)] * jnp.exp(glast - gcol_p)
        gl[g, h] = jnp.exp(glast)

    kq = {p: _dot_nt(jnp.concatenate([kb[p[1]][sl(p[0])], qn[p[1]][sl(p[0])]], axis=0), kn[p[1]][sl(p[0])])
          for p in pairs}
    low = {p: jnp.where(strict, kq[p][:c_len] * decay[p], 0.0) for p in pairs}
    attn = {p: jnp.where(tri, kq[p][c_len:] * decay[p], 0.0) for p in pairs}
    ymat = {p: -low[p] for p in pairs}
    lp = low
    for _ in range(levels - 1):
        lp = {p: _dot(lp[p], lp[p]) for p in pairs}
        prod = {p: _dot(ymat[p], lp[p]) for p in pairs}
        ymat = {p: ymat[p] + lp[p] + prod[p] for p in pairs}
    sol = {}
    for (g, h) in pairs:
        rhs_p = rhs[h][sl(g)]
        sol[g, h] = rhs_p + _dot(ymat[g, h], rhs_p)

    def finish(g, h, o):
        r = o * lax.rsqrt(jnp.mean(o * o, axis=-1, keepdims=True) + RMS_EPS) * nw
        q, loc = (g // n_seg, sl(g % n_seg)) if carry else (0, sl(g))
        gated = r * _silu(z_refs[q][loc, h * DV_A:(h + 1) * DV_A])
        if carry:
            o_ref[q, loc, h * DV_A:(h + 1) * DV_A] = gated
        else:
            o_ref[loc, h * DV_A:(h + 1) * DV_A] = gated

    def advance(group, state):
        ws = {p: _dot(jnp.concatenate([sol[p][:, DV_A:], qg[p[1]][sl(p[0])]], axis=0), state[p]) for p in group}
        v_new = {p: sol[p][:, :DV_A] - ws[p][:c_len] for p in group}
        o = {p: ws[p][c_len:] + _dot(attn[p], v_new[p]) for p in group}
        new = {p: state[p] * gl[p] + _dot_tn(kd[p], v_new[p]) for p in group}
        for p in group:
            finish(p[0], p[1], o[p])
        return new

    if carry:
        chains = [(q, h) for q in range(n_seq) for h in range(N_HEADS)]
        cur = {c: s_ref[c[0], c[1]] for c in chains}
        for g in range(n_seg):
            group = [(q * n_seg + g, h) for (q, h) in chains]
            new = advance(group, {(q * n_seg + g, h): cur[q, h] for (q, h) in chains})
            cur = {(q, h): new[q * n_seg + g, h] for (q, h) in chains}
        for (q, h) in chains:
            s_ref[q, h] = cur[q, h]

        @pl.when(t == n_steps - 1)
        def _():
            for (q, h) in chains:
                sout_ref[q, h] = cur[q, h]
            cout_ref[...] = ubuf_ref[:, base + conv_len - hist:base + conv_len, :]
    else:
        new = advance(pairs, {p: s0_ref[p[0], p[1]] for p in pairs})
        for p in pairs:
            sout_ref[p[0], p[1]] = new[p]
        cout_ref[...] = ubuf_ref[:, base + conv_len - hist:base + conv_len, :]


def _gdn(p, row0, batch, seq, conv_w, hp, norm_w, s_conv, s0):
    chunk = math.gcd(seq, GDN_CHUNK)
    n_chunks = seq // chunk
    carry = n_chunks > 1
    if carry:
        n_seg = math.gcd(n_chunks, GDN_SEGS_CARRY)
        n_seq = math.gcd(batch, GDN_SEQS_CARRY)
        grid = (batch // n_seq, n_chunks // n_seg)
        n_conv, conv_len = n_seq, n_seg * chunk
    else:
        n_seg = math.gcd(batch, GDN_SEGS_INDEP)
        n_seq = 1
        grid = (batch // n_seg, 1)
        n_conv, conv_len = n_seg, chunk
    rows = n_seg * chunk
    n_steps = grid[1]
    blk0 = row0 // rows
    rowblk = lambda q: (lambda b, t: blk0 + (b * n_seq + q) * n_steps + t)
    pcol = lambda width, off: [pl.BlockSpec((rows, width), lambda b, t, rb=rowblk(q): (rb(b, t), off // width))
                               for q in range(n_seq)]
    kern = functools.partial(_gdn_kernel, chunk=chunk, n_seg=n_seg, n_seq=n_seq, carry=carry, n_steps=n_steps)
    if carry:
        o_shape, o_spec = (batch, seq, W_A), pl.BlockSpec((n_seq, rows, W_A), lambda b, t: (b, t, 0))
    else:
        o_shape, o_spec = (batch * seq, W_A), pl.BlockSpec((rows, W_A), lambda b, t: (b, 0))
    o, s_new, c_new = pl.pallas_call(
        kern,
        out_shape=(jax.ShapeDtypeStruct(o_shape, F32),
                   jax.ShapeDtypeStruct((batch, N_HEADS, DK_A, DV_A), F32),
                   jax.ShapeDtypeStruct((batch, CONV_W - 1, CONV_DIM), F32)),
        grid=grid,
        in_specs=[
            *pcol(CONV_DIM, OFF_QKV), *pcol(W_A, OFF_Z), *pcol(LANES, OFF_SMALL),
            pl.BlockSpec((CONV_W, CONV_DIM), lambda b, t: (0, 0)),
            pl.BlockSpec((8, LANES), lambda b, t: (0, 0)),
            pl.BlockSpec((1, DV_A), lambda b, t: (0, 0)),
            pl.BlockSpec((n_conv, CONV_W - 1, CONV_DIM), lambda b, t: (b, 0, 0)),
            pl.BlockSpec((n_conv, N_HEADS, DK_A, DV_A), lambda b, t: (b, 0, 0, 0)),
        ],
        out_specs=(
            o_spec,
            pl.BlockSpec((n_conv, N_HEADS, DK_A, DV_A), lambda b, t: (b, 0, 0, 0)),
            pl.BlockSpec((n_conv, CONV_W - 1, CONV_DIM), lambda b, t: (b, 0, 0)),
        ),
        scratch_shapes=[pltpu.VMEM((n_seq, N_HEADS, DK_A, DV_A), F32),
                        pltpu.VMEM((n_conv, conv_len + 8, CONV_DIM), F32)],
        compiler_params=pltpu.CompilerParams(
            dimension_semantics=("arbitrary", "arbitrary"), vmem_limit_bytes=VMEM_LIMIT),
        name=f"gdn_c{chunk}",
    )(*([p] * (3 * n_seq)), conv_w, hp, norm_w.reshape(1, DV_A), s_conv, s0)
    return o.reshape(batch * seq, W_A), s_new, c_new


def _gla_kernel(*refs, tb, cs, n_seq, carry, n_steps):
    qk_refs, v_refs, g_refs, sm_refs = (refs[k * n_seq:(k + 1) * n_seq] for k in range(4))
    wgk_ref, bgk_ref, nw_ref, s0_ref, o_ref, sout_ref, s_ref = refs[4 * n_seq:]
    t = pl.program_id(1)
    n_sub = tb // cs
    w_kb = N_HEADS * DK_B
    seqs = range(n_seq)
    heads = range(N_HEADS)

    if carry:
        @pl.when(t == 0)
        def _():
            s_ref[...] = s0_ref[...]

    sh = int(round(math.log2(cs)))
    r_i = lax.broadcasted_iota(jnp.int32, (tb, tb), 0)
    c_i = lax.broadcasted_iota(jnp.int32, (tb, tb), 1)
    same = (r_i >> sh) == (c_i >> sh)
    mask = same & (c_i <= r_i)
    mask_bf = jnp.where(mask, 1.0, 0.0).astype(BF16)
    same_bf = jnp.where(same, 1.0, 0.0).astype(BF16)
    onehot = jnp.where((lax.broadcasted_iota(jnp.int32, (tb, n_sub), 0) >> sh)
                       == lax.broadcasted_iota(jnp.int32, (tb, n_sub), 1), 1.0, 0.0).astype(BF16)
    tn = lambda part: lax.dot_general(part, onehot, (((0,), (0,)), ((), ())), preferred_element_type=F32)

    log_a = [jax.nn.log_sigmoid(_dot(sm_refs[q][...], wgk_ref[...]) + bgk_ref[...]) / GLA_GATE_NORM for q in seqs]
    gcum = [_dot01(mask_bf, la) for la in log_a]
    gtot = [_dot01(same_bf, la) for la in log_a]
    glcol = []
    for la in log_a:
        hi, mid, lo = _split3(la)
        glcol.append(jnp.exp(tn(hi) + tn(mid) + tn(lo)))

    nw = nw_ref[...]
    sl = lambda j: slice(j * cs, (j + 1) * cs)
    qg, kg, kd, vh = {}, {}, {}, {}
    for q in seqs:
        qk = qk_refs[q][...]
        for h in heads:
            qh = qk[:, h * DK_B:(h + 1) * DK_B] * (DK_B ** -0.5)
            kh = qk[:, w_kb + h * DK_B:w_kb + (h + 1) * DK_B]
            gh = gcum[q][:, h * DK_B:(h + 1) * DK_B]
            gt = gtot[q][:, h * DK_B:(h + 1) * DK_B]
            qg[q, h] = qh * jnp.exp(gh)
            kg[q, h] = kh * jnp.exp(-gh)
            kd[q, h] = kh * jnp.exp(gt - gh)
            vh[q, h] = v_refs[q][:, h * DV_B:(h + 1) * DV_B]
    chains = [(q, h) for q in seqs for h in heads]
    attn = {c: jnp.where(mask, _dot_nt(qg[c], kg[c]), 0.0) for c in chains}
    o_intra = {c: _dot(attn[c], vh[c]) for c in chains}
    upd = {(j, c): _dot_tn(kd[c][sl(j)], vh[c][sl(j)]) for j in range(n_sub) for c in chains}
    gl = lambda j, c: glcol[c[0]][c[1] * DK_B:(c[1] + 1) * DK_B, j:j + 1]

    o_inter = {c: [] for c in chains}
    if carry:
        cur = {c: s_ref[c[0], c[1]] for c in chains}
        for j in range(n_sub):
            for c in chains:
                o_inter[c].append(_dot(qg[c][sl(j)], cur[c]))
                cur[c] = cur[c] * gl(j, c) + upd[j, c]
        for c in chains:
            s_ref[c[0], c[1]] = cur[c]

        @pl.when(t == n_steps - 1)
        def _():
            for c in chains:
                sout_ref[c[0], c[1]] = cur[c]
    else:
        for j in range(n_sub):
            for c in chains:
                s = s0_ref[j, c[1]]
                o_inter[c].append(_dot(qg[c][sl(j)], s))
                sout_ref[j, c[1]] = s * gl(j, c) + upd[j, c]

    for (q, h) in chains:
        o = o_intra[q, h] + jnp.concatenate(o_inter[q, h], axis=0)
        r = o * lax.rsqrt(jnp.mean(o * o, axis=-1, keepdims=True) + RMS_EPS) * nw
        gated = r * _silu(g_refs[q][:, h * DV_B:(h + 1) * DV_B])
        if carry:
            o_ref[q, :, h * DV_B:(h + 1) * DV_B] = gated
        else:
            o_ref[:, h * DV_B:(h + 1) * DV_B] = gated


def _gla(p, row0, batch, seq, wgk, bgk, norm_w, s0):
    cs = math.gcd(seq, GLA_CHUNK)
    carry = seq > cs
    if carry:
        tb = math.gcd(seq, GLA_TB)
        n_seq = math.gcd(batch, GLA_SEQS_CARRY)
        grid = (batch // n_seq, seq // tb)
        n_state = n_seq
    else:
        tb = math.gcd(batch * seq, GLA_TB)
        n_seq = 1
        n_state = tb // cs
        grid = (batch // n_state, 1)
    n_steps = grid[1]
    blk0 = row0 // tb
    rowblk = lambda q: (lambda b, t: blk0 + (b * n_seq + q) * n_steps + t)
    pcol = lambda width, off: [pl.BlockSpec((tb, width), lambda b, t, rb=rowblk(q): (rb(b, t), off // width))
                               for q in range(n_seq)]
    kern = functools.partial(_gla_kernel, tb=tb, cs=cs, n_seq=n_seq, carry=carry, n_steps=n_steps)
    if carry:
        o_shape, o_spec = (batch, seq, W_B), pl.BlockSpec((n_seq, tb, W_B), lambda b, t: (b, t, 0))
    else:
        o_shape, o_spec = (batch * seq, W_B), pl.BlockSpec((tb, W_B), lambda b, t: (b, 0))
    o, s_new = pl.pallas_call(
        kern,
        out_shape=(jax.ShapeDtypeStruct(o_shape, F32),
                   jax.ShapeDtypeStruct((batch, N_HEADS, DK_B, DV_B), F32)),
        grid=grid,
        in_specs=[
            *pcol(2 * N_HEADS * DK_B, OFF_QK_B), *pcol(W_B, OFF_V_B), *pcol(W_B, OFF_G_B), *pcol(LANES, OFF_SMALL),
            pl.BlockSpec((LANES, N_HEADS * DK_B), lambda b, t: (0, 0)),
            pl.BlockSpec((1, N_HEADS * DK_B), lambda b, t: (0, 0)),
            pl.BlockSpec((1, DV_B), lambda b, t: (0, 0)),
            pl.BlockSpec((n_state, N_HEADS, DK_B, DV_B), lambda b, t: (b, 0, 0, 0)),
        ],
        out_specs=(
            o_spec,
            pl.BlockSpec((n_state, N_HEADS, DK_B, DV_B), lambda b, t: (b, 0, 0, 0)),
        ),
        scratch_shapes=[pltpu.VMEM((n_seq, N_HEADS, DK_B, DV_B), F32)],
        compiler_params=pltpu.CompilerParams(
            dimension_semantics=("arbitrary", "arbitrary"), vmem_limit_bytes=VMEM_LIMIT),
        name=f"gla_c{cs}",
    )(*([p] * (4 * n_seq)), wgk, bgk.reshape(1, N_HEADS * DK_B), norm_w.reshape(1, DV_B), s0)
    return o.reshape(batch * seq, W_B), s_new


def _topk_softmax(logits):
    tm = logits.shape[0]
    lane = lax.broadcasted_iota(jnp.int32, (tm, LANES), 1)
    lane_f = lane.astype(F32)
    neg = jnp.float32(-jnp.inf)
    cur = jnp.where(lane < N_EXPERTS, logits, neg)
    vals, idxs = [], []
    for _ in range(TOP_K):
        m = jnp.max(cur, axis=-1, keepdims=True)
        idx = jnp.min(jnp.where(cur == m, lane_f, float(LANES)), axis=-1, keepdims=True)
        vals.append(m)
        idxs.append(idx)
        cur = jnp.where(lane_f == idx, neg, cur)
    exps = [jnp.exp(v - vals[0]) for v in vals]
    den = exps[0] + exps[1] + exps[2] + exps[3]
    te = jnp.zeros((tm, LANES), F32)
    tg = jnp.zeros((tm, LANES), F32)
    for k in range(TOP_K):
        te = jnp.where(lane == k, idxs[k], te)
        tg = jnp.where(lane == k, exps[k] / den, tg)
    return te.astype(jnp.int32), tg


def _merge_kernel(xp_ref, xs_ref, oap_ref, oas_ref, obp_ref, obs_ref, ga_ref, gb_ref, wua_ref, wub_ref, wo_ref,
                  n2_ref, wr_ref, br_ref, x1_ref, h2_ref, te_ref, tg_ref, *, n_first):
    i = pl.program_id(0)
    tm = x1_ref.shape[0]
    hm = tm // MERGE_SPLIT
    groups = [slice(a * hm, (a + 1) * hm) for a in range(MERGE_SPLIT)]
    oa = _pick(i, n_first, oap_ref, oas_ref).astype(BF16)
    ob = _pick(i, n_first, obp_ref, obs_ref).astype(BF16)
    x = _pick(i, n_first, xp_ref, xs_ref)
    ua = [jnp.dot(oa[g], wua_ref[...], preferred_element_type=F32) for g in groups]
    ub = [jnp.dot(ob[g], wub_ref[...], preferred_element_type=F32) for g in groups]
    merged = [jax.nn.sigmoid(ga_ref[g, :]) * ua[a] + jax.nn.sigmoid(gb_ref[g, :]) * ub[a]
              for a, g in enumerate(groups)]
    x1 = [x[g] + jnp.dot(merged[a].astype(BF16), wo_ref[...], preferred_element_type=F32)
          for a, g in enumerate(groups)]
    h2 = [v * lax.rsqrt(jnp.mean(v * v, axis=-1, keepdims=True) + RMS_EPS) * n2_ref[...] for v in x1]
    h_hi = [v.astype(BF16) for v in h2]
    h_lo = [(v - hi.astype(F32)).astype(BF16) for v, hi in zip(h2, h_hi)]
    l_hi = [jnp.dot(hi, wr_ref[...], preferred_element_type=F32) for hi in h_hi]
    l_lo = [jnp.dot(lo, wr_ref[:, :LANES], preferred_element_type=F32) for lo in h_lo]
    for a, g in enumerate(groups):
        x1_ref[g, :] = x1[a]
        _store_packed_rows(h2_ref, (), a * hm, _pack_bf16_pairs(h2[a]))
        logits = l_hi[a][:, :LANES] + l_hi[a][:, LANES:] + l_lo[a] + br_ref[...]
        te, tg = _topk_softmax(logits)
        te_ref[g, :] = te
        tg_ref[g, :] = tg


def _merge(x_p, x_s, oa_p, oa_s, ob_p, ob_s, p, wua, wub, wo, norm2_w, wr, br):
    n = x_p.shape[0] + x_s.shape[0]
    tm = MERGE_TM
    n_first = x_p.shape[0] // tm
    row = lambda i: (i, 0)
    const = lambda i: (0, 0)
    return pl.pallas_call(
        functools.partial(_merge_kernel, n_first=n_first),
        out_shape=(jax.ShapeDtypeStruct((n, D_MODEL), F32), jax.ShapeDtypeStruct((n * PACK_ROWS, LANES), jnp.uint32),
                   jax.ShapeDtypeStruct((n, LANES), jnp.int32), jax.ShapeDtypeStruct((n, LANES), F32)),
        grid=(n // tm,),
        in_specs=[
            *_split_specs((tm, D_MODEL), n_first),
            *_split_specs((tm, W_A), n_first),
            *_split_specs((tm, W_B), n_first),
            pl.BlockSpec((tm, D_MODEL), lambda i: (i, OFF_GATE_A // D_MODEL)),
            pl.BlockSpec((tm, D_MODEL), lambda i: (i, OFF_GATE_B // D_MODEL)),
            pl.BlockSpec((W_A, D_MODEL), const),
            pl.BlockSpec((W_B, D_MODEL), const),
            pl.BlockSpec((D_MODEL, D_MODEL), const),
            pl.BlockSpec((1, D_MODEL), const),
            pl.BlockSpec((D_MODEL, 2 * LANES), const),
            pl.BlockSpec((1, LANES), const),
        ],
        out_specs=(pl.BlockSpec((tm, D_MODEL), row), pl.BlockSpec((tm * PACK_ROWS, LANES), row),
                   pl.BlockSpec((tm, LANES), row), pl.BlockSpec((tm, LANES), row)),
        compiler_params=pltpu.CompilerParams(
            dimension_semantics=("arbitrary",), vmem_limit_bytes=VMEM_LIMIT),
        name="merge_router",
    )(x_p, x_s, oa_p, oa_s, ob_p, ob_s, p, p, wua, wub, wo, norm2_w.reshape(1, D_MODEL), wr, br)


def _moe_kernel(blk_e_ref, nxt_e_ref, n_sub_ref, n_used_ref, src_ref, src_nxt_ref, dst_ref, h2_ref, wgu_hbm, bgu_ref,
                wdn_hbm, bdn_ref, y_hbm, xstage, ybuf, wgu_stage, wdn_stage, wgu_bf, wdn_bf, ssem, wsem):
    i = pl.program_id(0)
    n_used = n_used_ref[0]
    bm = MOE_BM
    pr = PACK_ROWS
    e_cur = blk_e_ref[i]

    def weight_copies(e):
        return (pltpu.make_async_copy(wgu_hbm.at[e], wgu_stage, wsem.at[0]),
                pltpu.make_async_copy(wdn_hbm.at[e], wdn_stage, wsem.at[1]))

    @pl.when((i == 0) & (n_used > 0))
    def _():
        for cp in weight_copies(e_cur):
            cp.start()

    @pl.when((i < n_used) & ((i == 0) | (e_cur != blk_e_ref[jnp.maximum(i - 1, 0)])))
    def _():
        for cp in weight_copies(e_cur):
            cp.wait()
        wgu_bf[...] = wgu_stage[...].astype(BF16)
        wdn_bf[...] = wdn_stage[...].astype(BF16)

        @pl.when(nxt_e_ref[i] != e_cur)
        def _():
            for cp in weight_copies(nxt_e_ref[i]):
                cp.start()

    def scatter_copy(row, s, r):
        return pltpu.make_async_copy(ybuf.at[s, pl.ds(r * pr, pr), :],
                                     y_hbm.at[pl.ds(pl.multiple_of(row * pr, pr), pr), :], ssem.at[s])

    @pl.when(i == 0)
    def _():
        ybuf[...] = jnp.zeros_like(ybuf)
        n_real = y_hbm.shape[0] - MOE_SUB * bm * pr
        for s in range(MOE_SUB):
            sink = pltpu.make_async_copy(ybuf.at[s], y_hbm.at[pl.ds(n_real + s * bm * pr, bm * pr), :], ssem.at[s])
            sink.start()
            sink.wait()

    def gather(idx_ref, sub, slot):
        for r in range(bm):
            tok = pl.multiple_of(idx_ref[0, 0, sub * bm + r] * pr, pr)
            xstage[slot, pl.ds(r * pr, pr), :] = h2_ref[pl.ds(tok, pr), :]

    def scatter_chunks(n_rows, fn):
        for c in range(bm // SCATTER_CHUNK):
            @pl.when(n_rows > c * SCATTER_CHUNK)
            def _():
                for r in range(c * SCATTER_CHUNK, (c + 1) * SCATTER_CHUNK):
                    fn(r)

    def wait_prev_scatter(s):
        prev_rows = jnp.where(i >= 1, n_sub_ref[jnp.maximum(i - 1, 0) * MOE_SUB + s], 0)
        scatter_chunks(prev_rows, lambda r: scatter_copy(0, s, r).wait())

    def expert_mlp(s):
        x = _load_packed_rows(xstage, (s,), bm, BF16)
        gu = jnp.dot(x, wgu_bf[...], preferred_element_type=F32) + bgu_ref[...]
        gate = jnp.minimum(gu[:, :D_FF], SWIGLU_LIMIT)
        up = jnp.clip(gu[:, D_FF:], -SWIGLU_LIMIT, SWIGLU_LIMIT)
        hmid = (up + 1.0) * (gate * jax.nn.sigmoid(SWIGLU_ALPHA * gate))
        y = jnp.dot(hmid.astype(BF16), wdn_bf[...], preferred_element_type=F32) + bdn_ref[...]
        _store_packed_rows(ybuf, (s,), 0, _pack_bf16_pairs(y))
        scatter_chunks(n_sub_ref[i * MOE_SUB + s],
                       lambda r: scatter_copy(dst_ref[0, 0, s * bm + r], s, r).start(priority=r % 2))

    assert MOE_SUB == 2

    @pl.when((i == 0) & (n_used > 0))
    def _():
        gather(src_ref, 0, 0)

    @pl.when(i < n_used)
    def _():
        wait_prev_scatter(0)
        gather(src_ref, 1, 1)
        expert_mlp(0)
        wait_prev_scatter(1)

        @pl.when(n_sub_ref[i * MOE_SUB + 1] > 0)
        def _():
            gather(src_nxt_ref, 0, 0)
            expert_mlp(1)

        @pl.when(n_sub_ref[i * MOE_SUB + 1] <= 0)
        def _():
            gather(src_nxt_ref, 0, 0)

    @pl.when(i == n_used - 1)
    def _():
        for s in range(MOE_SUB):
            scatter_chunks(n_sub_ref[i * MOE_SUB + s], lambda r, s=s: scatter_copy(0, s, r).wait())


def _moe(blk_e, nxt_e, n_sub, n_used, src_tok, dst_row, h2, wgu, bgu, wdn, bdn, y_rows):
    n_steps = src_tok.shape[0]
    step_rows = MOE_SUB * MOE_BM
    cur = lambda i, be, ne, ns, nu: (i, 0, 0)
    wsel = lambda i, be, ne, ns, nu: (be[i], 0, 0)
    smem_blk = lambda im: pl.BlockSpec((1, 1, step_rows), im, memory_space=pltpu.SMEM)
    grid_spec = pltpu.PrefetchScalarGridSpec(
        num_scalar_prefetch=4,
        grid=(n_steps,),
        in_specs=[
            smem_blk(cur), smem_blk(lambda i, be, ne, ns, nu: (jnp.minimum(i + 1, n_steps - 1), 0, 0)), smem_blk(cur),
            pl.BlockSpec(h2.shape, lambda i, be, ne, ns, nu: (0, 0), pipeline_mode=pl.Buffered(1)),
            pl.BlockSpec(memory_space=pl.ANY),
            pl.BlockSpec((None, 1, 2 * D_FF), wsel),
            pl.BlockSpec(memory_space=pl.ANY),
            pl.BlockSpec((None, 1, D_MODEL), wsel),
        ],
        out_specs=pl.BlockSpec(memory_space=pl.ANY),
        scratch_shapes=[pltpu.VMEM((MOE_SUB, MOE_BM * PACK_ROWS, LANES), jnp.uint32),
                        pltpu.VMEM((MOE_SUB, MOE_BM * PACK_ROWS, LANES), jnp.uint32),
                        pltpu.VMEM((D_MODEL, 2 * D_FF), F32), pltpu.VMEM((D_FF, D_MODEL), F32),
                        pltpu.VMEM((D_MODEL, 2 * D_FF), BF16), pltpu.VMEM((D_FF, D_MODEL), BF16),
                        pltpu.SemaphoreType.DMA((MOE_SUB,)), pltpu.SemaphoreType.DMA((2,))],
    )
    return pl.pallas_call(
        _moe_kernel,
        out_shape=jax.ShapeDtypeStruct((y_rows * PACK_ROWS, LANES), jnp.uint32),
        grid_spec=grid_spec,
        compiler_params=pltpu.CompilerParams(
            dimension_semantics=("arbitrary",), vmem_limit_bytes=MOE_VMEM_LIMIT),
        name="moe_experts",
    )(blk_e, nxt_e, n_sub, n_used, src_tok, src_tok, dst_row, h2, wgu, bgu, wdn, bdn)


def _route(top_e, n_tok):
    bm = MOE_SUB * MOE_BM
    n_assign = n_tok * TOP_K
    n_blocks = n_assign // bm + N_EXPERTS
    n_dummy = N_EXPERTS * bm
    flat_e = top_e.reshape(-1)
    counts = jnp.sum((flat_e[:, None] == jnp.arange(N_EXPERTS, dtype=jnp.int32)[None, :]).astype(jnp.int32), axis=0)
    padded = (counts + bm - 1) // bm * bm
    pad_end = jnp.cumsum(padded)
    d_e = jnp.arange(n_dummy, dtype=jnp.int32) // bm
    d_active = (jnp.arange(n_dummy, dtype=jnp.int32) % bm) < (padded - counts)[d_e]
    keys = jnp.concatenate([flat_e * 2, jnp.where(d_active, d_e * 2 + 1, 2 * N_EXPERTS)])
    vbits = n_assign.bit_length()
    marker = (1 << vbits) - 1
    vals = jnp.concatenate([jnp.arange(n_assign, dtype=jnp.int32), jnp.full((n_dummy,), marker, jnp.int32)])
    assign = jnp.sort(keys * (1 << vbits) + vals)[:n_blocks * bm] & marker
    rows = jnp.arange(n_blocks * bm, dtype=jnp.int32)
    valid = assign != marker
    src_tok = jnp.where(valid, assign // TOP_K, 0)
    dst_row = jnp.where(valid, (assign % TOP_K) * n_tok + assign // TOP_K, n_assign + rows % bm)
    blk_start = jnp.arange(n_blocks, dtype=jnp.int32) * bm
    blk_e = jnp.minimum(jnp.sum((pad_end[None, :] <= blk_start[:, None]).astype(jnp.int32), axis=1),
                        N_EXPERTS - 1).astype(jnp.int32)
    n_used = (pad_end[-1] // bm).astype(jnp.int32).reshape(1)
    later = (blk_e[None, :] > blk_e[:, None]) & (blk_start[None, :] < pad_end[-1])
    nxt_e = jnp.min(jnp.where(later, blk_e[None, :], N_EXPERTS), axis=1)
    nxt_e = jnp.where(nxt_e == N_EXPERTS, blk_e, nxt_e).astype(jnp.int32)
    n_sub = jnp.sum(valid.reshape(n_blocks * MOE_SUB, MOE_BM).astype(jnp.int32), axis=1)
    return (blk_e, nxt_e, n_sub, n_used, src_tok.reshape(n_blocks, 1, bm).astype(jnp.int32),
            dst_row.reshape(n_blocks, 1, bm).astype(jnp.int32))


def _combine_kernel(x1_ref, y0_ref, y1_ref, y2_ref, y3_ref, tg_ref, nf_ref, op_ref, os_ref, *, n_first):
    tg = tg_ref[...]
    tm = x1_ref.shape[0]
    rows = lambda y_ref: _load_packed_rows(y_ref, (), tm, F32)
    acc = rows(y0_ref) * tg[:, 0:1]
    for k, y_ref in enumerate((y1_ref, y2_ref, y3_ref), start=1):
        acc = acc + rows(y_ref) * tg[:, k:k + 1]
    x = x1_ref[...] + acc
    out = x * lax.rsqrt(jnp.mean(x * x, axis=-1, keepdims=True) + RMS_EPS) * nf_ref[...]
    i = pl.program_id(0)

    @pl.when(i < n_first)
    def _():
        op_ref[...] = out

    @pl.when(i >= n_first)
    def _():
        os_ref[...] = out


def _combine(x1, y, tg, norm_f_w, n_p):
    n = x1.shape[0]
    tm = COMBINE_TM
    n_first = n_p // tm
    row = lambda i: (i, 0)
    y_spec = lambda k: pl.BlockSpec((tm * PACK_ROWS, LANES), lambda i: (k * (n // tm) + i, 0))
    return pl.pallas_call(
        functools.partial(_combine_kernel, n_first=n_first),
        out_shape=(jax.ShapeDtypeStruct((n_p, D_MODEL), F32), jax.ShapeDtypeStruct((n - n_p, D_MODEL), F32)),
        grid=(n // tm,),
        in_specs=[pl.BlockSpec((tm, D_MODEL), row), y_spec(0), y_spec(1), y_spec(2), y_spec(3),
                  pl.BlockSpec((tm, LANES), row), pl.BlockSpec((1, D_MODEL), lambda i: (0, 0))],
        out_specs=_split_specs((tm, D_MODEL), n_first),
        compiler_params=pltpu.CompilerParams(
            dimension_semantics=("arbitrary",), vmem_limit_bytes=VMEM_LIMIT),
        name="combine_norm",
    )(x1, y, y, y, y, tg, norm_f_w.reshape(1, D_MODEL))


def _pack_w_in(w_in):
    s = np.cumsum((0,) + IN_SIZES)
    w_in = w_in.astype(BF16)
    qkv, z, al, be, qb, kb, vb, gb, gk, ga, gbt = [w_in[:, s[i]:s[i + 1]] for i in range(len(IN_SIZES))]
    small = jnp.concatenate(
        [al, be, gk, jnp.zeros((D_MODEL, LANES - 2 * N_HEADS - GK_RANK), w_in.dtype)], axis=1)
    return jnp.concatenate([qkv, z, ga, gbt, qb, kb, vb, gb, small], axis=1)


def kernel(x_prompt, x_sample, state_gdn, state_gdn_conv, state_gla, norm1_w, w_in, conv_w, a_log,
           dt_bias, gdn_norm_w, gla_gk_up, gla_gk_b, gla_norm_w, w_up_a, w_up_b, w_o, norm2_w,
           w_router, b_router, w_gate_up, b_gate_up, w_down, b_down, norm_f_w):
    bp, tp, _ = x_prompt.shape
    bs, ts, _ = x_sample.shape
    n_p, n_s = bp * tp, bs * ts
    n = n_p + n_s
    l = 0

    x_p = x_prompt.reshape(n_p, D_MODEL)
    x_s = x_sample.reshape(n_s, D_MODEL)
    p = _inproj(x_p, x_s, norm1_w[l], _pack_w_in(w_in[l]))

    hp = jnp.zeros((8, LANES), F32).at[0, :N_HEADS].set(a_log[l]).at[1, :N_HEADS].set(dt_bias[l])
    wgk = jnp.zeros((LANES, N_HEADS * DK_B), F32).at[2 * N_HEADS:2 * N_HEADS + GK_RANK].set(gla_gk_up[l]).astype(BF16)
    zeros = lambda shape: jnp.zeros(shape, F32)

    oa_p, gdn_p, conv_p = _gdn(p, 0, bp, tp, conv_w[l], hp, gdn_norm_w[l],
                               zeros((bp, CONV_W - 1, CONV_DIM)), zeros((bp, N_HEADS, DK_A, DV_A)))
    oa_s, gdn_s, conv_s = _gdn(p, n_p, bs, ts, conv_w[l], hp, gdn_norm_w[l], state_gdn_conv[l], state_gdn[l])
    ob_p, gla_p = _gla(p, 0, bp, tp, wgk, gla_gk_b[l], gla_norm_w[l], zeros((bp, N_HEADS, DK_B, DV_B)))
    ob_s, gla_s = _gla(p, n_p, bs, ts, wgk, gla_gk_b[l], gla_norm_w[l], state_gla[l])

    wr = jnp.zeros((D_MODEL, LANES), F32).at[:, :N_EXPERTS].set(w_router[l])
    wr_hi = wr.astype(BF16)
    wr = jnp.concatenate([wr_hi, (wr - wr_hi.astype(F32)).astype(BF16)], axis=1)
    br = jnp.zeros((1, LANES), F32).at[0, :N_EXPERTS].set(b_router[l])
    x1, h2, te, tg = _merge(x_p, x_s, oa_p, oa_s, ob_p, ob_s, p, w_up_a[l].astype(BF16),
                            w_up_b[l].astype(BF16), w_o[l].astype(BF16), norm2_w[l], wr, br)

    blk_e, nxt_e, n_sub, n_used, src_tok, dst_row = _route(te[:, :TOP_K], n)
    y = _moe(blk_e, nxt_e, n_sub, n_used, src_tok, dst_row, h2, w_gate_up[l],
             b_gate_up[l].reshape(N_EXPERTS, 1, 2 * D_FF), w_down[l],
             b_down[l].reshape(N_EXPERTS, 1, D_MODEL), n * TOP_K + MOE_SUB * MOE_BM)
    out_p, out_s = _combine(x1, y, tg, norm_f_w, n_p)
    y_prompt = out_p.reshape(bp, tp, D_MODEL)
    y_sample = out_s.reshape(bs, ts, D_MODEL)
    return (y_prompt, y_sample, gdn_p[None], conv_p[None], gla_p[None], gdn_s[None], conv_s[None],
            gla_s[None])
```

```python
import functools
import math

import jax
import jax.numpy as jnp
import numpy as np
from jax import lax
from jax.experimental import pallas as pl
from jax.experimental.pallas import tpu as pltpu

F32 = jnp.float32
BF16 = jnp.bfloat16

D_MODEL = 1024
N_HEADS = 4
DK_A = 128
DV_A = 128
W_A = N_HEADS * DV_A
CONV_W = 4
CONV_DIM = 2 * N_HEADS * DK_A + W_A
GDN_CHUNK = 64
DK_B = 64
DV_B = 128
W_B = N_HEADS * DV_B
GK_RANK = 16
GLA_GATE_NORM = 16.0
GLA_CHUNK = 16
N_EXPERTS = 32
TOP_K = 4
D_FF = 1024
SWIGLU_LIMIT = 7.0
SWIGLU_ALPHA = 1.702
RMS_EPS = 1e-6
IN_SIZES = (CONV_DIM, W_A, N_HEADS, N_HEADS, N_HEADS * DK_B, N_HEADS * DK_B, W_B, W_B, GK_RANK,
            D_MODEL, D_MODEL)

LANES = 128

OFF_QKV = 0
OFF_Z = 1536
OFF_GATE_A = 2048
OFF_GATE_B = 3072
OFF_QK_B = 4096
OFF_V_B = 4608
OFF_G_B = 5120
OFF_SMALL = 5632
P_COLS = 5760

INPROJ_TM = 512
GLA_TB = 256
GLA_SEQS_CARRY = 4
GDN_SEGS_CARRY = 4
GDN_SEQS_CARRY = 2
GDN_SEGS_INDEP = 16
MERGE_TM = 512
MERGE_SPLIT = 2
MOE_BM = 256
MOE_SUB = 2
PACK_ROWS = D_MODEL // 2 // LANES
MOE_VMEM_LIMIT = 62 * 1024 * 1024
COMBINE_TM = 512
VMEM_LIMIT = 56 * 1024 * 1024


def _dot(a, b):
    return jnp.dot(a.astype(BF16), b.astype(BF16), preferred_element_type=F32)


def _dot_nt(a, b):
    return lax.dot_general(a.astype(BF16), b.astype(BF16), (((1,), (1,)), ((), ())),
                           preferred_element_type=F32)


def _dot_tn(a, b):
    return lax.dot_general(a.astype(BF16), b.astype(BF16), (((0,), (0,)), ((), ())), preferred_element_type=F32)


def _split3(x):
    hi = x.astype(BF16)
    r1 = x - hi.astype(F32)
    mid = r1.astype(BF16)
    lo = (r1 - mid.astype(F32)).astype(BF16)
    return hi, mid, lo


def _dot01(m01, x):
    hi, mid, lo = _split3(x)
    d = lambda p: jnp.dot(m01, p, preferred_element_type=F32)
    return d(hi) + d(mid) + d(lo)


def _silu(x):
    return x * jax.nn.sigmoid(x)


def _pack_bf16_pairs(x):
    bits = pltpu.bitcast(x.astype(BF16).astype(F32), jnp.uint32)
    half = D_MODEL // 2
    return (bits[:, :half] >> 16) | (bits[:, half:] & jnp.uint32(0xFFFF0000))


def _store_packed_rows(ref, lead, row0, words):
    n = words.shape[0]
    for c in range(PACK_ROWS):
        ref[(*lead, pl.ds(row0 * PACK_ROWS + c, n, stride=PACK_ROWS), slice(None))] = words[:, c * LANES:(c + 1) * LANES]


def _load_packed_rows(ref, lead, n, dtype):
    words = [ref[(*lead, pl.ds(c, n, stride=PACK_ROWS), slice(None))] for c in range(PACK_ROWS)]
    lo = [pltpu.bitcast(w << 16, F32).astype(dtype) for w in words]
    hi = [pltpu.bitcast(w & jnp.uint32(0xFFFF0000), F32).astype(dtype) for w in words]
    return jnp.concatenate(lo + hi, axis=1)


def _pick(i, n_first, first_ref, second_ref):
    return jnp.where(i < n_first, first_ref[...], second_ref[...])


def _split_specs(block, n_first):
    return (pl.BlockSpec(block, lambda i: (jnp.minimum(i, n_first - 1), 0)),
            pl.BlockSpec(block, lambda i: (jnp.maximum(i - n_first, 0), 0)))


def _inproj_kernel(xp_ref, xs_ref, nw_ref, w_ref, o_ref, *, n_first):
    x = _pick(pl.program_id(0), n_first, xp_ref, xs_ref)
    ms = jnp.mean(x * x, axis=-1, keepdims=True)
    h = (x * lax.rsqrt(ms + RMS_EPS) * nw_ref[...]).astype(BF16)
    o_ref[...] = jnp.dot(h, w_ref[...], preferred_element_type=F32)


def _inproj(x_p, x_s, norm_w, w_packed):
    tm = INPROJ_TM
    n_first = x_p.shape[0] // tm
    n = x_p.shape[0] + x_s.shape[0]
    return pl.pallas_call(
        functools.partial(_inproj_kernel, n_first=n_first),
        out_shape=jax.ShapeDtypeStruct((n, P_COLS), F32),
        grid=(n // tm,),
        in_specs=[
            *_split_specs((tm, D_MODEL), n_first),
            pl.BlockSpec((1, D_MODEL), lambda i: (0, 0)),
            pl.BlockSpec((D_MODEL, P_COLS), lambda i: (0, 0), pipeline_mode=pl.Buffered(1)),
        ],
        out_specs=pl.BlockSpec((tm, P_COLS), lambda i: (i, 0)),
        compiler_params=pltpu.CompilerParams(
            dimension_semantics=("arbitrary",), vmem_limit_bytes=VMEM_LIMIT),
        name="inproj",
    )(x_p, x_s, norm_w.reshape(1, D_MODEL), w_packed)


def _gdn_kernel(*refs, chunk, n_seg, n_seq, carry, n_steps):
    qkv_refs, z_refs, sm_refs = refs[:n_seq], refs[n_seq:2 * n_seq], refs[2 * n_seq:3 * n_seq]
    cw_ref, hp_ref, nw_ref, sconv_ref, s0_ref, o_ref, sout_ref, cout_ref, s_ref, ubuf_ref = refs[3 * n_seq:]
    c_len = chunk
    n_seg_all = n_seq * n_seg
    rows = n_seg_all * c_len
    t = pl.program_id(1)
    hist = CONV_W - 1
    base = 8
    conv_len = n_seg * c_len if carry else c_len
    n_conv = n_seq if carry else n_seg

    def init_history():
        ubuf_ref[:, 0:base, :] = jnp.zeros((n_conv, base, CONV_DIM), F32)
        ubuf_ref[:, base - hist:base, :] = sconv_ref[...]

    if carry:
        @pl.when(t == 0)
        def _():
            s_ref[...] = s0_ref[...]
            init_history()

        @pl.when(t > 0)
        def _():
            ubuf_ref[:, base - hist:base, :] = ubuf_ref[:, base + conv_len - hist:base + conv_len, :]
        for q in range(n_seq):
            ubuf_ref[q, base:base + conv_len, :] = qkv_refs[q][...]
    else:
        init_history()
        ubuf_ref[:, base:base + conv_len, :] = qkv_refs[0][...].reshape(n_conv, conv_len, CONV_DIM)

    cw = cw_ref[...]
    u = ubuf_ref[...]
    u1 = pltpu.roll(u, 1, axis=1)
    v = u1 * cw[0:1, :] + u * cw[1:2, :]
    y = (pltpu.roll(v, 2, axis=1) + u1 * cw[2:3, :]) + u * cw[3:4, :]
    y = _silu(y[:, base:, :]).reshape(rows, CONV_DIM)

    sm = sm_refs[0][...] if n_seq == 1 else jnp.concatenate([r[...] for r in sm_refs], axis=0)
    hp = hp_ref[...]
    lane = lax.broadcasted_iota(jnp.int32, (rows, LANES), 1)
    g_all = -jnp.exp(hp[0:1, :]) * jax.nn.softplus(sm + hp[1:2, :])
    g_all = jnp.where(lane < N_HEADS, g_all, 0.0)
    beta_all = jax.nn.sigmoid(sm)

    sh = int(round(math.log2(c_len)))
    r_b = lax.broadcasted_iota(jnp.int32, (rows, rows), 0)
    c_b = lax.broadcasted_iota(jnp.int32, (rows, rows), 1)
    seg_tril = ((r_b >> sh) == (c_b >> sh)) & (c_b <= r_b)
    gc = _dot01(jnp.where(seg_tril, 1.0, 0.0).astype(BF16), g_all)

    r_i = lax.broadcasted_iota(jnp.int32, (c_len, c_len), 0)
    c_i = lax.broadcasted_iota(jnp.int32, (c_len, c_len), 1)
    tri = c_i <= r_i
    strict = c_i < r_i
    eye = c_i == r_i
    nw = nw_ref[...]
    levels = sh

    qn, kn, kb, rhs, qg, gcol = [], [], [], [], [], []
    for h in range(N_HEADS):
        qh = y[:, h * DK_A:(h + 1) * DK_A]
        kh = y[:, W_A + h * DK_A:W_A + (h + 1) * DK_A]
        vh = y[:, 2 * W_A + h * DV_A:2 * W_A + (h + 1) * DV_A]
        qn_h = qh * lax.rsqrt(jnp.sum(qh * qh, axis=-1, keepdims=True) + 1e-6) * (DK_A ** -0.5)
        kn_h = kh * lax.rsqrt(jnp.sum(kh * kh, axis=-1, keepdims=True) + 1e-6)
        gcol_h = gc[:, h:h + 1]
        eg = jnp.exp(gcol_h)
        bcol = beta_all[:, N_HEADS + h:N_HEADS + h + 1]
        kb_h = kn_h * bcol
        qn.append(qn_h)
        kn.append(kn_h)
        kb.append(kb_h)
        rhs.append(jnp.concatenate([vh * bcol, kb_h * eg], axis=1))
        qg.append(qn_h * eg)
        gcol.append(gcol_h)

    pairs = [(g, h) for g in range(n_seg_all) for h in range(N_HEADS)]
    sl = lambda g: slice(g * c_len, (g + 1) * c_len)

    decay, kd, gl = {}, {}, {}
    for (g, h) in pairs:
        gcol_p = gcol[h][sl(g)]
        gcb = jnp.broadcast_to(gcol_p, (c_len, c_len))
        grow = jnp.sum(jnp.where(eye, gcb, 0.0), axis=0, keepdims=True)
        decay[g, h] = jnp.where(tri, jnp.exp(jnp.where(tri, gcb - grow, 0.0)), 0.0)
        glast = gcol[h][(g + 1) * c_len - 1:(g + 1) * c_len]
        kd[g, h] = kn[h][sl(g)] * jnp.exp(glast - gcol_p)
        gl[g, h] = jnp.exp(glast)

    kq = {p: _dot_nt(jnp.concatenate([kb[p[1]][sl(p[0])], qn[p[1]][sl(p[0])]], axis=0), kn[p[1]][sl(p[0])])
          for p in pairs}
    low = {p: jnp.where(strict, kq[p][:c_len] * decay[p], 0.0) for p in pairs}
    attn = {p: jnp.where(tri, kq[p][c_len:] * decay[p], 0.0) for p in pairs}
    ymat = {p: -low[p] for p in pairs}
    lp = low
    for _ in range(levels - 1):
        lp = {p: _dot(lp[p], lp[p]) for p in pairs}
        prod = {p: _dot(ymat[p], lp[p]) for p in pairs}
        ymat = {p: ymat[p] + lp[p] + prod[p] for p in pairs}
    sol = {}
    for (g, h) in pairs:
        rhs_p = rhs[h][sl(g)]
        sol[g, h] = rhs_p + _dot(ymat[g, h], rhs_p)

    def finish(g, h, o):
        r = o * lax.rsqrt(jnp.mean(o * o, axis=-1, keepdims=True) + RMS_EPS) * nw
        q, loc = (g // n_seg, sl(g % n_seg)) if carry else (0, sl(g))
        gated = r * _silu(z_refs[q][loc, h * DV_A:(h + 1) * DV_A])
        if carry:
            o_ref[q, loc, h * DV_A:(h + 1) * DV_A] = gated
        else:
            o_ref[loc, h * DV_A:(h + 1) * DV_A] = gated

    def advance(group, state):
        ws = {p: _dot(jnp.concatenate([sol[p][:, DV_A:], qg[p[1]][sl(p[0])]], axis=0), state[p]) for p in group}
        v_new = {p: sol[p][:, :DV_A] - ws[p][:c_len] for p in group}
        o = {p: ws[p][c_len:] + _dot(attn[p], v_new[p]) for p in group}
        new = {p: state[p] * gl[p] + _dot_tn(kd[p], v_new[p]) for p in group}
        for p in group:
            finish(p[0], p[1], o[p])
        return new

    if carry:
        chains = [(q, h) for q in range(n_seq) for h in range(N_HEADS)]
        cur = {c: s_ref[c[0], c[1]] for c in chains}
        for g in range(n_seg):
            group = [(q * n_seg + g, h) for (q, h) in chains]
            new = advance(group, {(q * n_seg + g, h): cur[q, h] for (q, h) in chains})
            cur = {(q, h): new[q * n_seg + g, h] for (q, h) in chains}
        for (q, h) in chains:
            s_ref[q, h] = cur[q, h]

        @pl.when(t == n_steps - 1)
        def _():
            for (q, h) in chains:
                sout_ref[q, h] = cur[q, h]
            cout_ref[...] = ubuf_ref[:, base + conv_len - hist:base + conv_len, :]
    else:
        new = advance(pairs, {p: s0_ref[p[0], p[1]] for p in pairs})
        for p in pairs:
            sout_ref[p[0], p[1]] = new[p]
        cout_ref[...] = ubuf_ref[:, base + conv_len - hist:base + conv_len, :]


def _gdn(p, row0, batch, seq, conv_w, hp, norm_w, s_conv, s0):
    chunk = math.gcd(seq, GDN_CHUNK)
    n_chunks = seq // chunk
    carry = n_chunks > 1
    if carry:
        n_seg = math.gcd(n_chunks, GDN_SEGS_CARRY)
        n_seq = math.gcd(batch, GDN_SEQS_CARRY)
        grid = (batch // n_seq, n_chunks // n_seg)
        n_conv, conv_len = n_seq, n_seg * chunk
    else:
        n_seg = math.gcd(batch, GDN_SEGS_INDEP)
        n_seq = 1
        grid = (batch // n_seg, 1)
        n_conv, conv_len = n_seg, chunk
    rows = n_seg * chunk
    n_steps = grid[1]
    blk0 = row0 // rows
    rowblk = lambda q: (lambda b, t: blk0 + (b * n_seq + q) * n_steps + t)
    pcol = lambda width, off: [pl.BlockSpec((rows, width), lambda b, t, rb=rowblk(q): (rb(b, t), off // width))
                               for q in range(n_seq)]
    kern = functools.partial(_gdn_kernel, chunk=chunk, n_seg=n_seg, n_seq=n_seq, carry=carry, n_steps=n_steps)
    if carry:
        o_shape, o_spec = (batch, seq, W_A), pl.BlockSpec((n_seq, rows, W_A), lambda b, t: (b, t, 0))
    else:
        o_shape, o_spec = (batch * seq, W_A), pl.BlockSpec((rows, W_A), lambda b, t: (b, 0))
    o, s_new, c_new = pl.pallas_call(
        kern,
        out_shape=(jax.ShapeDtypeStruct(o_shape, F32),
                   jax.ShapeDtypeStruct((batch, N_HEADS, DK_A, DV_A), F32),
                   jax.ShapeDtypeStruct((batch, CONV_W - 1, CONV_DIM), F32)),
        grid=grid,
        in_specs=[
            *pcol(CONV_DIM, OFF_QKV), *pcol(W_A, OFF_Z), *pcol(LANES, OFF_SMALL),
            pl.BlockSpec((CONV_W, CONV_DIM), lambda b, t: (0, 0)),
            pl.BlockSpec((8, LANES), lambda b, t: (0, 0)),
            pl.BlockSpec((1, DV_A), lambda b, t: (0, 0)),
            pl.BlockSpec((n_conv, CONV_W - 1, CONV_DIM), lambda b, t: (b, 0, 0)),
            pl.BlockSpec((n_conv, N_HEADS, DK_A, DV_A), lambda b, t: (b, 0, 0, 0)),
        ],
        out_specs=(
            o_spec,
            pl.BlockSpec((n_conv, N_HEADS, DK_A, DV_A), lambda b, t: (b, 0, 0, 0)),
            pl.BlockSpec((n_conv, CONV_W - 1, CONV_DIM), lambda b, t: (b, 0, 0)),
        ),
        scratch_shapes=[pltpu.VMEM((n_seq, N_HEADS, DK_A, DV_A), F32),
                        pltpu.VMEM((n_conv, conv_len + 8, CONV_DIM), F32)],
        compiler_params=pltpu.CompilerParams(
            dimension_semantics=("arbitrary", "arbitrary"), vmem_limit_bytes=VMEM_LIMIT),
        name=f"gdn_c{chunk}",
    )(*([p] * (3 * n_seq)), conv_w, hp, norm_w.reshape(1, DV_A), s_conv, s0)
    return o.reshape(batch * seq, W_A), s_new, c_new


def _gla_kernel(*refs, tb, cs, n_seq, carry, n_steps):
    qk_refs, v_refs, g_refs, sm_refs = (refs[k * n_seq:(k + 1) * n_seq] for k in range(4))
    wgk_ref, bgk_ref, nw_ref, s0_ref, o_ref, sout_ref, s_ref = refs[4 * n_seq:]
    t = pl.program_id(1)
    n_sub = tb // cs
    w_kb = N_HEADS * DK_B
    seqs = range(n_seq)
    heads = range(N_HEADS)

    if carry:
        @pl.when(t == 0)
        def _():
            s_ref[...] = s0_ref[...]

    sh = int(round(math.log2(cs)))
    r_i = lax.broadcasted_iota(jnp.int32, (tb, tb), 0)
    c_i = lax.broadcasted_iota(jnp.int32, (tb, tb), 1)
    same = (r_i >> sh) == (c_i >> sh)
    mask = same & (c_i <= r_i)
    mask_bf = jnp.where(mask, 1.0, 0.0).astype(BF16)
    same_bf = jnp.where(same, 1.0, 0.0).astype(BF16)
    onehot = jnp.where((lax.broadcasted_iota(jnp.int32, (tb, n_sub), 0) >> sh)
                       == lax.broadcasted_iota(jnp.int32, (tb, n_sub), 1), 1.0, 0.0).astype(BF16)
    tn = lambda part: lax.dot_general(part, onehot, (((0,), (0,)), ((), ())), preferred_element_type=F32)

    log_a = [jax.nn.log_sigmoid(_dot(sm_refs[q][...], wgk_ref[...]) + bgk_ref[...]) / GLA_GATE_NORM for q in seqs]
    gcum = [_dot01(mask_bf, la) for la in log_a]
    gtot = [_dot01(same_bf, la) for la in log_a]
    glcol = []
    for la in log_a:
        hi, mid, lo = _split3(la)
        glcol.append(jnp.exp(tn(hi) + tn(mid) + tn(lo)))

    nw = nw_ref[...]
    sl = lambda j: slice(j * cs, (j + 1) * cs)
    qg, kg, kd, vh = {}, {}, {}, {}
    for q in seqs:
        qk = qk_refs[q][...]
        for h in heads:
            qh = qk[:, h * DK_B:(h + 1) * DK_B] * (DK_B ** -0.5)
            kh = qk[:, w_kb + h * DK_B:w_kb + (h + 1) * DK_B]
            gh = gcum[q][:, h * DK_B:(h + 1) * DK_B]
            gt = gtot[q][:, h * DK_B:(h + 1) * DK_B]
            qg[q, h] = qh * jnp.exp(gh)
            kg[q, h] = kh * jnp.exp(-gh)
            kd[q, h] = kh * jnp.exp(gt - gh)
            vh[q, h] = v_refs[q][:, h * DV_B:(h + 1) * DV_B]
    chains = [(q, h) for q in seqs for h in heads]
    attn = {c: jnp.where(mask, _dot_nt(qg[c], kg[c]), 0.0) for c in chains}
    o_intra = {c: _dot(attn[c], vh[c]) for c in chains}
    upd = {(j, c): _dot_tn(kd[c][sl(j)], vh[c][sl(j)]) for j in range(n_sub) for c in chains}
    gl = lambda j, c: glcol[c[0]][c[1] * DK_B:(c[1] + 1) * DK_B, j:j + 1]

    o_inter = {c: [] for c in chains}
    if carry:
        cur = {c: s_ref[c[0], c[1]] for c in chains}
        for j in range(n_sub):
            for c in chains:
                o_inter[c].append(_dot(qg[c][sl(j)], cur[c]))
                cur[c] = cur[c] * gl(j, c) + upd[j, c]
        for c in chains:
            s_ref[c[0], c[1]] = cur[c]

        @pl.when(t == n_steps - 1)
        def _():
            for c in chains:
                sout_ref[c[0], c[1]] = cur[c]
    else:
        for j in range(n_sub):
            for c in chains:
                s = s0_ref[j, c[1]]
                o_inter[c].append(_dot(qg[c][sl(j)], s))
                sout_ref[j, c[1]] = s * gl(j, c) + upd[j, c]

    for (q, h) in chains:
        o = o_intra[q, h] + jnp.concatenate(o_inter[q, h], axis=0)
        r = o * lax.rsqrt(jnp.mean(o * o, axis=-1, keepdims=True) + RMS_EPS) * nw
        gated = r * _silu(g_refs[q][:, h * DV_B:(h + 1) * DV_B])
        if carry:
            o_ref[q, :, h * DV_B:(h + 1) * DV_B] = gated
        else:
            o_ref[:, h * DV_B:(h + 1) * DV_B] = gated


def _gla(p, row0, batch, seq, wgk, bgk, norm_w, s0):
    cs = math.gcd(seq, GLA_CHUNK)
    carry = seq > cs
    if carry:
        tb = math.gcd(seq, GLA_TB)
        n_seq = math.gcd(batch, GLA_SEQS_CARRY)
        grid = (batch // n_seq, seq // tb)
        n_state = n_seq
    else:
        tb = math.gcd(batch * seq, GLA_TB)
        n_seq = 1
        n_state = tb // cs
        grid = (batch // n_state, 1)
    n_steps = grid[1]
    blk0 = row0 // tb
    rowblk = lambda q: (lambda b, t: blk0 + (b * n_seq + q) * n_steps + t)
    pcol = lambda width, off: [pl.BlockSpec((tb, width), lambda b, t, rb=rowblk(q): (rb(b, t), off // width))
                               for q in range(n_seq)]
    kern = functools.partial(_gla_kernel, tb=tb, cs=cs, n_seq=n_seq, carry=carry, n_steps=n_steps)
    if carry:
        o_shape, o_spec = (batch, seq, W_B), pl.BlockSpec((n_seq, tb, W_B), lambda b, t: (b, t, 0))
    else:
        o_shape, o_spec = (batch * seq, W_B), pl.BlockSpec((tb, W_B), lambda b, t: (b, 0))
    o, s_new = pl.pallas_call(
        kern,
        out_shape=(jax.ShapeDtypeStruct(o_shape, F32),
                   jax.ShapeDtypeStruct((batch, N_HEADS, DK_B, DV_B), F32)),
        grid=grid,
        in_specs=[
            *pcol(2 * N_HEADS * DK_B, OFF_QK_B), *pcol(W_B, OFF_V_B), *pcol(W_B, OFF_G_B), *pcol(LANES, OFF_SMALL),
            pl.BlockSpec((LANES, N_HEADS * DK_B), lambda b, t: (0, 0)),
            pl.BlockSpec((1, N_HEADS * DK_B), lambda b, t: (0, 0)),
            pl.BlockSpec((1, DV_B), lambda b, t: (0, 0)),
            pl.BlockSpec((n_state, N_HEADS, DK_B, DV_B), lambda b, t: (b, 0, 0, 0)),
        ],
        out_specs=(
            o_spec,
            pl.BlockSpec((n_state, N_HEADS, DK_B, DV_B), lambda b, t: (b, 0, 0, 0)),
        ),
        scratch_shapes=[pltpu.VMEM((n_seq, N_HEADS, DK_B, DV_B), F32)],
        compiler_params=pltpu.CompilerParams(
            dimension_semantics=("arbitrary", "arbitrary"), vmem_limit_bytes=VMEM_LIMIT),
        name=f"gla_c{cs}",
    )(*([p] * (4 * n_seq)), wgk, bgk.reshape(1, N_HEADS * DK_B), norm_w.reshape(1, DV_B), s0)
    return o.reshape(batch * seq, W_B), s_new


def _topk_softmax(logits):
    tm = logits.shape[0]
    lane = lax.broadcasted_iota(jnp.int32, (tm, LANES), 1)
    lane_f = lane.astype(F32)
    neg = jnp.float32(-jnp.inf)
    cur = jnp.where(lane < N_EXPERTS, logits, neg)
    vals, idxs = [], []
    for _ in range(TOP_K):
        m = jnp.max(cur, axis=-1, keepdims=True)
        idx = jnp.min(jnp.where(cur == m, lane_f, float(LANES)), axis=-1, keepdims=True)
        vals.append(m)
        idxs.append(idx)
        cur = jnp.where(lane_f == idx, neg, cur)
    exps = [jnp.exp(v - vals[0]) for v in vals]
    den = exps[0] + exps[1] + exps[2] + exps[3]
    te = jnp.zeros((tm, LANES), F32)
    tg = jnp.zeros((tm, LANES), F32)
    for k in range(TOP_K):
        te = jnp.where(lane == k, idxs[k], te)
        tg = jnp.where(lane == k, exps[k] / den, tg)
    return te.astype(jnp.int32), tg


def _merge_kernel(xp_ref, xs_ref, oap_ref, oas_ref, obp_ref, obs_ref, ga_ref, gb_ref, wua_ref, wub_ref, wo_ref,
                  n2_ref, wr_ref, br_ref, x1_ref, h2_ref, te_ref, tg_ref, *, n_first):
    i = pl.program_id(0)
    tm = x1_ref.shape[0]
    hm = tm // MERGE_SPLIT
    groups = [slice(a * hm, (a + 1) * hm) for a in range(MERGE_SPLIT)]
    oa = _pick(i, n_first, oap_ref, oas_ref).astype(BF16)
    ob = _pick(i, n_first, obp_ref, obs_ref).astype(BF16)
    x = _pick(i, n_first, xp_ref, xs_ref)
    ua = [jnp.dot(oa[g], wua_ref[...], preferred_element_type=F32) for g in groups]
    ub = [jnp.dot(ob[g], wub_ref[...], preferred_element_type=F32) for g in groups]
    merged = [jax.nn.sigmoid(ga_ref[g, :]) * ua[a] + jax.nn.sigmoid(gb_ref[g, :]) * ub[a]
              for a, g in enumerate(groups)]
    x1 = [x[g] + jnp.dot(merged[a].astype(BF16), wo_ref[...], preferred_element_type=F32)
          for a, g in enumerate(groups)]
    h2 = [v * lax.rsqrt(jnp.mean(v * v, axis=-1, keepdims=True) + RMS_EPS) * n2_ref[...] for v in x1]
    h_hi = [v.astype(BF16) for v in h2]
    h_lo = [(v - hi.astype(F32)).astype(BF16) for v, hi in zip(h2, h_hi)]
    l_hi = [jnp.dot(hi, wr_ref[...], preferred_element_type=F32) for hi in h_hi]
    l_lo = [jnp.dot(lo, wr_ref[:, :LANES], preferred_element_type=F32) for lo in h_lo]
    for a, g in enumerate(groups):
        x1_ref[g, :] = x1[a]
        _store_packed_rows(h2_ref, (), a * hm, _pack_bf16_pairs(h2[a]))
        logits = l_hi[a][:, :LANES] + l_hi[a][:, LANES:] + l_lo[a] + br_ref[...]
        te, tg = _topk_softmax(logits)
        te_ref[g, :] = te
        tg_ref[g, :] = tg


def _merge(x_p, x_s, oa_p, oa_s, ob_p, ob_s, p, wua, wub, wo, norm2_w, wr, br):
    n = x_p.shape[0] + x_s.shape[0]
    tm = MERGE_TM
    n_first = x_p.shape[0] // tm
    row = lambda i: (i, 0)
    const = lambda i: (0, 0)
    return pl.pallas_call(
        functools.partial(_merge_kernel, n_first=n_first),
        out_shape=(jax.ShapeDtypeStruct((n, D_MODEL), F32), jax.ShapeDtypeStruct((n * PACK_ROWS, LANES), jnp.uint32),
                   jax.ShapeDtypeStruct((n, LANES), jnp.int32), jax.ShapeDtypeStruct((n, LANES), F32)),
        grid=(n // tm,),
        in_specs=[
            *_split_specs((tm, D_MODEL), n_first),
            *_split_specs((tm, W_A), n_first),
            *_split_specs((tm, W_B), n_first),
            pl.BlockSpec((tm, D_MODEL), lambda i: (i, OFF_GATE_A // D_MODEL)),
            pl.BlockSpec((tm, D_MODEL), lambda i: (i, OFF_GATE_B // D_MODEL)),
            pl.BlockSpec((W_A, D_MODEL), const),
            pl.BlockSpec((W_B, D_MODEL), const),
            pl.BlockSpec((D_MODEL, D_MODEL), const),
            pl.BlockSpec((1, D_MODEL), const),
            pl.BlockSpec((D_MODEL, 2 * LANES), const),
            pl.BlockSpec((1, LANES), const),
        ],
        out_specs=(pl.BlockSpec((tm, D_MODEL), row), pl.BlockSpec((tm * PACK_ROWS, LANES), row),
                   pl.BlockSpec((tm, LANES), row), pl.BlockSpec((tm, LANES), row)),
        compiler_params=pltpu.CompilerParams(
            dimension_semantics=("arbitrary",), vmem_limit_bytes=VMEM_LIMIT),
        name="merge_router",
    )(x_p, x_s, oa_p, oa_s, ob_p, ob_s, p, p, wua, wub, wo, norm2_w.reshape(1, D_MODEL), wr, br)


def _moe_kernel(blk_e_ref, nxt_e_ref, n_sub_ref, n_used_ref, src_ref, src_nxt_ref, dst_ref, h2_ref, wgu_hbm, bgu_ref,
                wdn_hbm, bdn_ref, y_hbm, xstage, ybuf, wgu_stage, wdn_stage, wgu_bf, wdn_bf, ssem, wsem):
    i = pl.program_id(0)
    n_used = n_used_ref[0]
    bm = MOE_BM
    pr = PACK_ROWS
    e_cur = blk_e_ref[i]

    def weight_copies(e):
        return (pltpu.make_async_copy(wgu_hbm.at[e], wgu_stage, wsem.at[0]),
                pltpu.make_async_copy(wdn_hbm.at[e], wdn_stage, wsem.at[1]))

    @pl.when((i == 0) & (n_used > 0))
    def _():
        for cp in weight_copies(e_cur):
            cp.start()

    @pl.when((i < n_used) & ((i == 0) | (e_cur != blk_e_ref[jnp.maximum(i - 1, 0)])))
    def _():
        for cp in weight_copies(e_cur):
            cp.wait()
        wgu_bf[...] = wgu_stage[...].astype(BF16)
        wdn_bf[...] = wdn_stage[...].astype(BF16)

        @pl.when(nxt_e_ref[i] != e_cur)
        def _():
            for cp in weight_copies(nxt_e_ref[i]):
                cp.start()

    def scatter_copy(row, s, r):
        return pltpu.make_async_copy(ybuf.at[s, pl.ds(r * pr, pr), :],
                                     y_hbm.at[pl.ds(pl.multiple_of(row * pr, pr), pr), :], ssem.at[s])

    @pl.when(i == 0)
    def _():
        ybuf[...] = jnp.zeros_like(ybuf)
        n_real = y_hbm.shape[0] - MOE_SUB * bm * pr
        for s in range(MOE_SUB):
            sink = pltpu.make_async_copy(ybuf.at[s], y_hbm.at[pl.ds(n_real + s * bm * pr, bm * pr), :], ssem.at[s])
            sink.start()
            sink.wait()

    def gather(idx_ref, sub, slot):
        for r in range(bm):
            tok = pl.multiple_of(idx_ref[0, 0, sub * bm + r] * pr, pr)
            xstage[slot, pl.ds(r * pr, pr), :] = h2_ref[pl.ds(tok, pr), :]

    def wait_prev_scatter(s):
        @pl.when((i >= 1) & (n_sub_ref[jnp.maximum(i - 1, 0)] > s))
        def _():
            for r in range(bm):
                scatter_copy(0, s, r).wait()

    def expert_mlp(s):
        x = _load_packed_rows(xstage, (s,), bm, BF16)
        gu = jnp.dot(x, wgu_bf[...], preferred_element_type=F32) + bgu_ref[...]
        gate = jnp.minimum(gu[:, :D_FF], SWIGLU_LIMIT)
        up = jnp.clip(gu[:, D_FF:], -SWIGLU_LIMIT, SWIGLU_LIMIT)
        hmid = (up + 1.0) * (gate * jax.nn.sigmoid(SWIGLU_ALPHA * gate))
        y = jnp.dot(hmid.astype(BF16), wdn_bf[...], preferred_element_type=F32) + bdn_ref[...]
        _store_packed_rows(ybuf, (s,), 0, _pack_bf16_pairs(y))
        for r in range(bm):
            scatter_copy(dst_ref[0, 0, s * bm + r], s, r).start(priority=r % 2)

    assert MOE_SUB == 2

    @pl.when((i == 0) & (n_used > 0))
    def _():
        gather(src_ref, 0, 0)

    @pl.when(i < n_used)
    def _():
        wait_prev_scatter(0)
        gather(src_ref, 1, 1)
        expert_mlp(0)
        wait_prev_scatter(1)

        @pl.when(n_sub_ref[i] > 1)
        def _():
            gather(src_nxt_ref, 0, 0)
            expert_mlp(1)

        @pl.when(n_sub_ref[i] <= 1)
        def _():
            gather(src_nxt_ref, 0, 0)

    @pl.when(i == n_used - 1)
    def _():
        for s in range(MOE_SUB):
            @pl.when(n_sub_ref[i] > s)
            def _():
                for r in range(bm):
                    scatter_copy(0, s, r).wait()


def _moe(blk_e, nxt_e, n_sub, n_used, src_tok, dst_row, h2, wgu, bgu, wdn, bdn, y_rows):
    n_steps = src_tok.shape[0]
    step_rows = MOE_SUB * MOE_BM
    cur = lambda i, be, ne, ns, nu: (i, 0, 0)
    wsel = lambda i, be, ne, ns, nu: (be[i], 0, 0)
    smem_blk = lambda im: pl.BlockSpec((1, 1, step_rows), im, memory_space=pltpu.SMEM)
    grid_spec = pltpu.PrefetchScalarGridSpec(
        num_scalar_prefetch=4,
        grid=(n_steps,),
        in_specs=[
            smem_blk(cur), smem_blk(lambda i, be, ne, ns, nu: (jnp.minimum(i + 1, n_steps - 1), 0, 0)), smem_blk(cur),
            pl.BlockSpec(h2.shape, lambda i, be, ne, ns, nu: (0, 0), pipeline_mode=pl.Buffered(1)),
            pl.BlockSpec(memory_space=pl.ANY),
            pl.BlockSpec((None, 1, 2 * D_FF), wsel),
            pl.BlockSpec(memory_space=pl.ANY),
            pl.BlockSpec((None, 1, D_MODEL), wsel),
        ],
        out_specs=pl.BlockSpec(memory_space=pl.ANY),
        scratch_shapes=[pltpu.VMEM((MOE_SUB, MOE_BM * PACK_ROWS, LANES), jnp.uint32),
                        pltpu.VMEM((MOE_SUB, MOE_BM * PACK_ROWS, LANES), jnp.uint32),
                        pltpu.VMEM((D_MODEL, 2 * D_FF), F32), pltpu.VMEM((D_FF, D_MODEL), F32),
                        pltpu.VMEM((D_MODEL, 2 * D_FF), BF16), pltpu.VMEM((D_FF, D_MODEL), BF16),
                        pltpu.SemaphoreType.DMA((MOE_SUB,)), pltpu.SemaphoreType.DMA((2,))],
    )
    return pl.pallas_call(
        _moe_kernel,
        out_shape=jax.ShapeDtypeStruct((y_rows * PACK_ROWS, LANES), jnp.uint32),
        grid_spec=grid_spec,
        compiler_params=pltpu.CompilerParams(
            dimension_semantics=("arbitrary",), vmem_limit_bytes=MOE_VMEM_LIMIT),
        name="moe_experts",
    )(blk_e, nxt_e, n_sub, n_used, src_tok, src_tok, dst_row, h2, wgu, bgu, wdn, bdn)


def _route(top_e, n_tok):
    bm = MOE_SUB * MOE_BM
    n_assign = n_tok * TOP_K
    n_blocks = n_assign // bm + N_EXPERTS
    n_dummy = N_EXPERTS * bm
    flat_e = top_e.reshape(-1)
    counts = jnp.sum((flat_e[:, None] == jnp.arange(N_EXPERTS, dtype=jnp.int32)[None, :]).astype(jnp.int32), axis=0)
    padded = (counts + bm - 1) // bm * bm
    pad_end = jnp.cumsum(padded)
    d_e = jnp.arange(n_dummy, dtype=jnp.int32) // bm
    d_active = (jnp.arange(n_dummy, dtype=jnp.int32) % bm) < (padded - counts)[d_e]
    keys = jnp.concatenate([flat_e * 2, jnp.where(d_active, d_e * 2 + 1, 2 * N_EXPERTS)])
    vbits = n_assign.bit_length()
    marker = (1 << vbits) - 1
    vals = jnp.concatenate([jnp.arange(n_assign, dtype=jnp.int32), jnp.full((n_dummy,), marker, jnp.int32)])
    assign = jnp.sort(keys * (1 << vbits) + vals)[:n_blocks * bm] & marker
    rows = jnp.arange(n_blocks * bm, dtype=jnp.int32)
    valid = assign != marker
    src_tok = jnp.where(valid, assign // TOP_K, 0)
    dst_row = jnp.where(valid, (assign % TOP_K) * n_tok + assign // TOP_K, n_assign + rows % bm)
    blk_start = jnp.arange(n_blocks, dtype=jnp.int32) * bm
    blk_e = jnp.minimum(jnp.sum((pad_end[None, :] <= blk_start[:, None]).astype(jnp.int32), axis=1),
                        N_EXPERTS - 1).astype(jnp.int32)
    n_used = (pad_end[-1] // bm).astype(jnp.int32).reshape(1)
    later = (blk_e[None, :] > blk_e[:, None]) & (blk_start[None, :] < pad_end[-1])
    nxt_e = jnp.min(jnp.where(later, blk_e[None, :], N_EXPERTS), axis=1)
    nxt_e = jnp.where(nxt_e == N_EXPERTS, blk_e, nxt_e).astype(jnp.int32)
    n_sub = jnp.sum(valid.reshape(n_blocks, MOE_SUB, MOE_BM)[:, :, 0].astype(jnp.int32), axis=1)
    return (blk_e, nxt_e, n_sub, n_used, src_tok.reshape(n_blocks, 1, bm).astype(jnp.int32),
            dst_row.reshape(n_blocks, 1, bm).astype(jnp.int32))


def _combine_kernel(x1_ref, y0_ref, y1_ref, y2_ref, y3_ref, tg_ref, nf_ref, op_ref, os_ref, *, n_first):
    tg = tg_ref[...]
    tm = x1_ref.shape[0]
    rows = lambda y_ref: _load_packed_rows(y_ref, (), tm, F32)
    acc = rows(y0_ref) * tg[:, 0:1]
    for k, y_ref in enumerate((y1_ref, y2_ref, y3_ref), start=1):
        acc = acc + rows(y_ref) * tg[:, k:k + 1]
    x = x1_ref[...] + acc
    out = x * lax.rsqrt(jnp.mean(x * x, axis=-1, keepdims=True) + RMS_EPS) * nf_ref[...]
    i = pl.program_id(0)

    @pl.when(i < n_first)
    def _():
        op_ref[...] = out

    @pl.when(i >= n_first)
    def _():
        os_ref[...] = out


def _combine(x1, y, tg, norm_f_w, n_p):
    n = x1.shape[0]
    tm = COMBINE_TM
    n_first = n_p // tm
    row = lambda i: (i, 0)
    y_spec = lambda k: pl.BlockSpec((tm * PACK_ROWS, LANES), lambda i: (k * (n // tm) + i, 0))
    return pl.pallas_call(
        functools.partial(_combine_kernel, n_first=n_first),
        out_shape=(jax.ShapeDtypeStruct((n_p, D_MODEL), F32), jax.ShapeDtypeStruct((n - n_p, D_MODEL), F32)),
        grid=(n // tm,),
        in_specs=[pl.BlockSpec((tm, D_MODEL), row), y_spec(0), y_spec(1), y_spec(2), y_spec(3),
                  pl.BlockSpec((tm, LANES), row), pl.BlockSpec((1, D_MODEL), lambda i: (0, 0))],
        out_specs=_split_specs((tm, D_MODEL), n_first),
        compiler_params=pltpu.CompilerParams(
            dimension_semantics=("arbitrary",), vmem_limit_bytes=VMEM_LIMIT),
        name="combine_norm",
    )(x1, y, y, y, y, tg, norm_f_w.reshape(1, D_MODEL))


def _pack_w_in(w_in):
    s = np.cumsum((0,) + IN_SIZES)
    w_in = w_in.astype(BF16)
    qkv, z, al, be, qb, kb, vb, gb, gk, ga, gbt = [w_in[:, s[i]:s[i + 1]] for i in range(len(IN_SIZES))]
    small = jnp.concatenate(
        [al, be, gk, jnp.zeros((D_MODEL, LANES - 2 * N_HEADS - GK_RANK), w_in.dtype)], axis=1)
    return jnp.concatenate([qkv, z, ga, gbt, qb, kb, vb, gb, small], axis=1)


def kernel(x_prompt, x_sample, state_gdn, state_gdn_conv, state_gla, norm1_w, w_in, conv_w, a_log,
           dt_bias, gdn_norm_w, gla_gk_up, gla_gk_b, gla_norm_w, w_up_a, w_up_b, w_o, norm2_w,
           w_router, b_router, w_gate_up, b_gate_up, w_down, b_down, norm_f_w):
    bp, tp, _ = x_prompt.shape
    bs, ts, _ = x_sample.shape
    n_p, n_s = bp * tp, bs * ts
    n = n_p + n_s
    l = 0

    x_p = x_prompt.reshape(n_p, D_MODEL)
    x_s = x_sample.reshape(n_s, D_MODEL)
    p = _inproj(x_p, x_s, norm1_w[l], _pack_w_in(w_in[l]))

    hp = jnp.zeros((8, LANES), F32).at[0, :N_HEADS].set(a_log[l]).at[1, :N_HEADS].set(dt_bias[l])
    wgk = jnp.zeros((LANES, N_HEADS * DK_B), F32).at[2 * N_HEADS:2 * N_HEADS + GK_RANK].set(gla_gk_up[l]).astype(BF16)
    zeros = lambda shape: jnp.zeros(shape, F32)

    oa_p, gdn_p, conv_p = _gdn(p, 0, bp, tp, conv_w[l], hp, gdn_norm_w[l],
                               zeros((bp, CONV_W - 1, CONV_DIM)), zeros((bp, N_HEADS, DK_A, DV_A)))
    oa_s, gdn_s, conv_s = _gdn(p, n_p, bs, ts, conv_w[l], hp, gdn_norm_w[l], state_gdn_conv[l], state_gdn[l])
    ob_p, gla_p = _gla(p, 0, bp, tp, wgk, gla_gk_b[l], gla_norm_w[l], zeros((bp, N_HEADS, DK_B, DV_B)))
    ob_s, gla_s = _gla(p, n_p, bs, ts, wgk, gla_gk_b[l], gla_norm_w[l], state_gla[l])

    wr = jnp.zeros((D_MODEL, LANES), F32).at[:, :N_EXPERTS].set(w_router[l])
    wr_hi = wr.astype(BF16)
    wr = jnp.concatenate([wr_hi, (wr - wr_hi.astype(F32)).astype(BF16)], axis=1)
    br = jnp.zeros((1, LANES), F32).at[0, :N_EXPERTS].set(b_router[l])
    x1, h2, te, tg = _merge(x_p, x_s, oa_p, oa_s, ob_p, ob_s, p, w_up_a[l].astype(BF16),
                            w_up_b[l].astype(BF16), w_o[l].astype(BF16), norm2_w[l], wr, br)

    blk_e, nxt_e, n_sub, n_used, src_tok, dst_row = _route(te[:, :TOP_K], n)
    y = _moe(blk_e, nxt_e, n_sub, n_used, src_tok, dst_row, h2, w_gate_up[l],
             b_gate_up[l].reshape(N_EXPERTS, 1, 2 * D_FF), w_down[l],
             b_down[l].reshape(N_EXPERTS, 1, D_MODEL), n * TOP_K + MOE_SUB * MOE_BM)
    out_p, out_s = _combine(x1, y, tg, norm_f_w, n_p)
    y_prompt = out_p.reshape(bp, tp, D_MODEL)
    y_sample = out_s.reshape(bs, ts, D_MODEL)
    return (y_prompt, y_sample, gdn_p[None], conv_p[None], gla_p[None], gdn_s[None], conv_s[None],
            gla_s[None])
```

```python
import functools
import math

import jax
import jax.numpy as jnp
import numpy as np
from jax import lax
from jax.experimental import pallas as pl
from jax.experimental.pallas import tpu as pltpu

F32 = jnp.float32
BF16 = jnp.bfloat16

D_MODEL = 1024
N_HEADS = 4
DK_A = 128
DV_A = 128
W_A = N_HEADS * DV_A
CONV_W = 4
CONV_DIM = 2 * N_HEADS * DK_A + W_A
GDN_CHUNK = 64
DK_B = 64
DV_B = 128
W_B = N_HEADS * DV_B
GK_RANK = 16
GLA_GATE_NORM = 16.0
GLA_CHUNK = 16
N_EXPERTS = 32
TOP_K = 4
D_FF = 1024
SWIGLU_LIMIT = 7.0
SWIGLU_ALPHA = 1.702
RMS_EPS = 1e-6
IN_SIZES = (CONV_DIM, W_A, N_HEADS, N_HEADS, N_HEADS * DK_B, N_HEADS * DK_B, W_B, W_B, GK_RANK,
            D_MODEL, D_MODEL)

LANES = 128

OFF_QKV = 0
OFF_Z = 1536
OFF_QK_B = 2048
OFF_V_B = 2560
OFF_G_B = 3072
OFF_SMALL = 3584
P_MAIN = 3712
P_GATES = 2 * D_MODEL
P_COLS = P_MAIN + P_GATES

INPROJ_TM = 512
GLA_TB = 256
GLA_SEQS_CARRY = 4
GDN_SEGS_CARRY = 4
GDN_SEQS_CARRY = 2
GDN_SEGS_INDEP = 16
MERGE_TM = 512
MERGE_SPLIT = 2
MOE_BM = 256
MOE_SUB = 2
PACK_ROWS = D_MODEL // 2 // LANES
MOE_VMEM_LIMIT = 62 * 1024 * 1024
COMBINE_TM = 512
VMEM_LIMIT = 56 * 1024 * 1024


def _dot(a, b):
    return jnp.dot(a.astype(BF16), b.astype(BF16), preferred_element_type=F32)


def _dot_nt(a, b):
    return lax.dot_general(a.astype(BF16), b.astype(BF16), (((1,), (1,)), ((), ())),
                           preferred_element_type=F32)


def _dot_tn(a, b):
    return lax.dot_general(a.astype(BF16), b.astype(BF16), (((0,), (0,)), ((), ())), preferred_element_type=F32)


def _split3(x):
    hi = x.astype(BF16)
    r1 = x - hi.astype(F32)
    mid = r1.astype(BF16)
    lo = (r1 - mid.astype(F32)).astype(BF16)
    return hi, mid, lo


def _dot01(m01, x):
    hi, mid, lo = _split3(x)
    d = lambda p: jnp.dot(m01, p, preferred_element_type=F32)
    return d(hi) + d(mid) + d(lo)


def _silu(x):
    return x * jax.nn.sigmoid(x)


def _pack_bf16_pairs(x):
    bits = pltpu.bitcast(x.astype(BF16).astype(F32), jnp.uint32)
    half = D_MODEL // 2
    return (bits[:, :half] >> 16) | (bits[:, half:] & jnp.uint32(0xFFFF0000))


def _store_packed_rows(ref, lead, row0, words):
    n = words.shape[0]
    for c in range(PACK_ROWS):
        ref[(*lead, pl.ds(row0 * PACK_ROWS + c, n, stride=PACK_ROWS), slice(None))] = words[:, c * LANES:(c + 1) * LANES]


def _load_packed_rows(ref, lead, n, dtype):
    words = [ref[(*lead, pl.ds(c, n, stride=PACK_ROWS), slice(None))] for c in range(PACK_ROWS)]
    lo = [pltpu.bitcast(w << 16, F32).astype(dtype) for w in words]
    hi = [pltpu.bitcast(w & jnp.uint32(0xFFFF0000), F32).astype(dtype) for w in words]
    return jnp.concatenate(lo + hi, axis=1)


def _pick(i, n_first, first_ref, second_ref):
    return jnp.where(i < n_first, first_ref[...], second_ref[...])


def _split_specs(block, n_first):
    return (pl.BlockSpec(block, lambda i: (jnp.minimum(i, n_first - 1), 0)),
            pl.BlockSpec(block, lambda i: (jnp.maximum(i - n_first, 0), 0)))


def _inproj_kernel(xp_ref, xs_ref, nw_ref, w_ref, o_ref, og_ref, *, n_first):
    x = _pick(pl.program_id(0), n_first, xp_ref, xs_ref)
    ms = jnp.mean(x * x, axis=-1, keepdims=True)
    h = (x * lax.rsqrt(ms + RMS_EPS) * nw_ref[...]).astype(BF16)
    acc = jnp.dot(h, w_ref[...], preferred_element_type=F32)
    o_ref[...] = acc[:, :P_MAIN]
    og_ref[...] = acc[:, P_MAIN:].astype(BF16)


def _inproj(x_p, x_s, norm_w, w_packed):
    tm = INPROJ_TM
    n_first = x_p.shape[0] // tm
    n = x_p.shape[0] + x_s.shape[0]
    return pl.pallas_call(
        functools.partial(_inproj_kernel, n_first=n_first),
        out_shape=(jax.ShapeDtypeStruct((n, P_MAIN), F32), jax.ShapeDtypeStruct((n, P_GATES), BF16)),
        grid=(n // tm,),
        in_specs=[
            *_split_specs((tm, D_MODEL), n_first),
            pl.BlockSpec((1, D_MODEL), lambda i: (0, 0)),
            pl.BlockSpec((D_MODEL, P_COLS), lambda i: (0, 0), pipeline_mode=pl.Buffered(1)),
        ],
        out_specs=(pl.BlockSpec((tm, P_MAIN), lambda i: (i, 0)), pl.BlockSpec((tm, P_GATES), lambda i: (i, 0))),
        compiler_params=pltpu.CompilerParams(
            dimension_semantics=("arbitrary",), vmem_limit_bytes=VMEM_LIMIT),
        name="inproj",
    )(x_p, x_s, norm_w.reshape(1, D_MODEL), w_packed)


def _gdn_kernel(*refs, chunk, n_seg, n_seq, carry, n_steps):
    qkv_refs, z_refs, sm_refs = refs[:n_seq], refs[n_seq:2 * n_seq], refs[2 * n_seq:3 * n_seq]
    cw_ref, hp_ref, nw_ref, sconv_ref, s0_ref, o_ref, sout_ref, cout_ref, s_ref, ubuf_ref = refs[3 * n_seq:]
    c_len = chunk
    n_seg_all = n_seq * n_seg
    rows = n_seg_all * c_len
    t = pl.program_id(1)
    hist = CONV_W - 1
    base = 8
    conv_len = n_seg * c_len if carry else c_len
    n_conv = n_seq if carry else n_seg

    def init_history():
        ubuf_ref[:, 0:base, :] = jnp.zeros((n_conv, base, CONV_DIM), F32)
        ubuf_ref[:, base - hist:base, :] = sconv_ref[...]

    if carry:
        @pl.when(t == 0)
        def _():
            s_ref[...] = s0_ref[...]
            init_history()

        @pl.when(t > 0)
        def _():
            ubuf_ref[:, base - hist:base, :] = ubuf_ref[:, base + conv_len - hist:base + conv_len, :]
        for q in range(n_seq):
            ubuf_ref[q, base:base + conv_len, :] = qkv_refs[q][...]
    else:
        init_history()
        ubuf_ref[:, base:base + conv_len, :] = qkv_refs[0][...].reshape(n_conv, conv_len, CONV_DIM)

    cw = cw_ref[...]
    u = ubuf_ref[...]
    u1 = pltpu.roll(u, 1, axis=1)
    v = u1 * cw[0:1, :] + u * cw[1:2, :]
    y = (pltpu.roll(v, 2, axis=1) + u1 * cw[2:3, :]) + u * cw[3:4, :]
    y = _silu(y[:, base:, :]).reshape(rows, CONV_DIM)

    sm = sm_refs[0][...] if n_seq == 1 else jnp.concatenate([r[...] for r in sm_refs], axis=0)
    hp = hp_ref[...]
    lane = lax.broadcasted_iota(jnp.int32, (rows, LANES), 1)
    g_all = -jnp.exp(hp[0:1, :]) * jax.nn.softplus(sm + hp[1:2, :])
    g_all = jnp.where(lane < N_HEADS, g_all, 0.0)
    beta_all = jax.nn.sigmoid(sm)

    sh = int(round(math.log2(c_len)))
    r_b = lax.broadcasted_iota(jnp.int32, (rows, rows), 0)
    c_b = lax.broadcasted_iota(jnp.int32, (rows, rows), 1)
    seg_tril = ((r_b >> sh) == (c_b >> sh)) & (c_b <= r_b)
    gc = _dot01(jnp.where(seg_tril, 1.0, 0.0).astype(BF16), g_all)

    r_i = lax.broadcasted_iota(jnp.int32, (c_len, c_len), 0)
    c_i = lax.broadcasted_iota(jnp.int32, (c_len, c_len), 1)
    tri = c_i <= r_i
    strict = c_i < r_i
    eye = c_i == r_i
    nw = nw_ref[...]
    levels = sh

    qn, kn, kb, rhs, qg, gcol = [], [], [], [], [], []
    for h in range(N_HEADS):
        qh = y[:, h * DK_A:(h + 1) * DK_A]
        kh = y[:, W_A + h * DK_A:W_A + (h + 1) * DK_A]
        vh = y[:, 2 * W_A + h * DV_A:2 * W_A + (h + 1) * DV_A]
        qn_h = qh * lax.rsqrt(jnp.sum(qh * qh, axis=-1, keepdims=True) + 1e-6) * (DK_A ** -0.5)
        kn_h = kh * lax.rsqrt(jnp.sum(kh * kh, axis=-1, keepdims=True) + 1e-6)
        gcol_h = gc[:, h:h + 1]
        eg = jnp.exp(gcol_h)
        bcol = beta_all[:, N_HEADS + h:N_HEADS + h + 1]
        kb_h = kn_h * bcol
        qn.append(qn_h)
        kn.append(kn_h)
        kb.append(kb_h)
        rhs.append(jnp.concatenate([vh * bcol, kb_h * eg], axis=1))
        qg.append(qn_h * eg)
        gcol.append(gcol_h)

    pairs = [(g, h) for g in range(n_seg_all) for h in range(N_HEADS)]
    sl = lambda g: slice(g * c_len, (g + 1) * c_len)

    decay, kd, gl = {}, {}, {}
    for (g, h) in pairs:
        gcol_p = gcol[h][sl(g)]
        gcb = jnp.broadcast_to(gcol_p, (c_len, c_len))
        grow = jnp.sum(jnp.where(eye, gcb, 0.0), axis=0, keepdims=True)
        decay[g, h] = jnp.where(tri, jnp.exp(jnp.where(tri, gcb - grow, 0.0)), 0.0)
        glast = gcol[h][(g + 1) * c_len - 1:(g + 1) * c_len]
        kd[g, h] = kn[h][sl(g)] * jnp.exp(glast - gcol_p)
        gl[g, h] = jnp.exp(glast)

    kq = {p: _dot_nt(jnp.concatenate([kb[p[1]][sl(p[0])], qn[p[1]][sl(p[0])]], axis=0), kn[p[1]][sl(p[0])])
          for p in pairs}
    low = {p: jnp.where(strict, kq[p][:c_len] * decay[p], 0.0) for p in pairs}
    attn = {p: jnp.where(tri, kq[p][c_len:] * decay[p], 0.0) for p in pairs}
    ymat = {p: -low[p] for p in pairs}
    lp = low
    for _ in range(levels - 1):
        lp = {p: _dot(lp[p], lp[p]) for p in pairs}
        prod = {p: _dot(ymat[p], lp[p]) for p in pairs}
        ymat = {p: ymat[p] + lp[p] + prod[p] for p in pairs}
    sol = {}
    for (g, h) in pairs:
        rhs_p = rhs[h][sl(g)]
        sol[g, h] = rhs_p + _dot(ymat[g, h], rhs_p)

    def finish(g, h, o):
        r = o * lax.rsqrt(jnp.mean(o * o, axis=-1, keepdims=True) + RMS_EPS) * nw
        q, loc = (g // n_seg, sl(g % n_seg)) if carry else (0, sl(g))
        gated = r * _silu(z_refs[q][loc, h * DV_A:(h + 1) * DV_A])
        if carry:
            o_ref[q, loc, h * DV_A:(h + 1) * DV_A] = gated
        else:
            o_ref[loc, h * DV_A:(h + 1) * DV_A] = gated

    def advance(group, state):
        ws = {p: _dot(jnp.concatenate([sol[p][:, DV_A:], qg[p[1]][sl(p[0])]], axis=0), state[p]) for p in group}
        v_new = {p: sol[p][:, :DV_A] - ws[p][:c_len] for p in group}
        o = {p: ws[p][c_len:] + _dot(attn[p], v_new[p]) for p in group}
        new = {p: state[p] * gl[p] + _dot_tn(kd[p], v_new[p]) for p in group}
        for p in group:
            finish(p[0], p[1], o[p])
        return new

    if carry:
        chains = [(q, h) for q in range(n_seq) for h in range(N_HEADS)]
        cur = {c: s_ref[c[0], c[1]] for c in chains}
        for g in range(n_seg):
            group = [(q * n_seg + g, h) for (q, h) in chains]
            new = advance(group, {(q * n_seg + g, h): cur[q, h] for (q, h) in chains})
            cur = {(q, h): new[q * n_seg + g, h] for (q, h) in chains}
        for (q, h) in chains:
            s_ref[q, h] = cur[q, h]

        @pl.when(t == n_steps - 1)
        def _():
            for (q, h) in chains:
                sout_ref[q, h] = cur[q, h]
            cout_ref[...] = ubuf_ref[:, base + conv_len - hist:base + conv_len, :]
    else:
        new = advance(pairs, {p: s0_ref[p[0], p[1]] for p in pairs})
        for p in pairs:
            sout_ref[p[0], p[1]] = new[p]
        cout_ref[...] = ubuf_ref[:, base + conv_len - hist:base + conv_len, :]


def _gdn(p, row0, batch, seq, conv_w, hp, norm_w, s_conv, s0):
    chunk = math.gcd(seq, GDN_CHUNK)
    n_chunks = seq // chunk
    carry = n_chunks > 1
    if carry:
        n_seg = math.gcd(n_chunks, GDN_SEGS_CARRY)
        n_seq = math.gcd(batch, GDN_SEQS_CARRY)
        grid = (batch // n_seq, n_chunks // n_seg)
        n_conv, conv_len = n_seq, n_seg * chunk
    else:
        n_seg = math.gcd(batch, GDN_SEGS_INDEP)
        n_seq = 1
        grid = (batch // n_seg, 1)
        n_conv, conv_len = n_seg, chunk
    rows = n_seg * chunk
    n_steps = grid[1]
    blk0 = row0 // rows
    rowblk = lambda q: (lambda b, t: blk0 + (b * n_seq + q) * n_steps + t)
    pcol = lambda width, off: [pl.BlockSpec((rows, width), lambda b, t, rb=rowblk(q): (rb(b, t), off // width))
                               for q in range(n_seq)]
    kern = functools.partial(_gdn_kernel, chunk=chunk, n_seg=n_seg, n_seq=n_seq, carry=carry, n_steps=n_steps)
    if carry:
        o_shape, o_spec = (batch, seq, W_A), pl.BlockSpec((n_seq, rows, W_A), lambda b, t: (b, t, 0))
    else:
        o_shape, o_spec = (batch * seq, W_A), pl.BlockSpec((rows, W_A), lambda b, t: (b, 0))
    o, s_new, c_new = pl.pallas_call(
        kern,
        out_shape=(jax.ShapeDtypeStruct(o_shape, F32),
                   jax.ShapeDtypeStruct((batch, N_HEADS, DK_A, DV_A), F32),
                   jax.ShapeDtypeStruct((batch, CONV_W - 1, CONV_DIM), F32)),
        grid=grid,
        in_specs=[
            *pcol(CONV_DIM, OFF_QKV), *pcol(W_A, OFF_Z), *pcol(LANES, OFF_SMALL),
            pl.BlockSpec((CONV_W, CONV_DIM), lambda b, t: (0, 0)),
            pl.BlockSpec((8, LANES), lambda b, t: (0, 0)),
            pl.BlockSpec((1, DV_A), lambda b, t: (0, 0)),
            pl.BlockSpec((n_conv, CONV_W - 1, CONV_DIM), lambda b, t: (b, 0, 0)),
            pl.BlockSpec((n_conv, N_HEADS, DK_A, DV_A), lambda b, t: (b, 0, 0, 0)),
        ],
        out_specs=(
            o_spec,
            pl.BlockSpec((n_conv, N_HEADS, DK_A, DV_A), lambda b, t: (b, 0, 0, 0)),
            pl.BlockSpec((n_conv, CONV_W - 1, CONV_DIM), lambda b, t: (b, 0, 0)),
        ),
        scratch_shapes=[pltpu.VMEM((n_seq, N_HEADS, DK_A, DV_A), F32),
                        pltpu.VMEM((n_conv, conv_len + 8, CONV_DIM), F32)],
        compiler_params=pltpu.CompilerParams(
            dimension_semantics=("arbitrary", "arbitrary"), vmem_limit_bytes=VMEM_LIMIT),
        name=f"gdn_c{chunk}",
    )(*([p] * (3 * n_seq)), conv_w, hp, norm_w.reshape(1, DV_A), s_conv, s0)
    return o.reshape(batch * seq, W_A), s_new, c_new


def _gla_kernel(*refs, tb, cs, n_seq, carry, n_steps):
    qk_refs, v_refs, g_refs, sm_refs = (refs[k * n_seq:(k + 1) * n_seq] for k in range(4))
    wgk_ref, bgk_ref, nw_ref, s0_ref, o_ref, sout_ref, s_ref = refs[4 * n_seq:]
    t = pl.program_id(1)
    n_sub = tb // cs
    w_kb = N_HEADS * DK_B
    seqs = range(n_seq)
    heads = range(N_HEADS)

    if carry:
        @pl.when(t == 0)
        def _():
            s_ref[...] = s0_ref[...]

    sh = int(round(math.log2(cs)))
    r_i = lax.broadcasted_iota(jnp.int32, (tb, tb), 0)
    c_i = lax.broadcasted_iota(jnp.int32, (tb, tb), 1)
    same = (r_i >> sh) == (c_i >> sh)
    mask = same & (c_i <= r_i)
    mask_bf = jnp.where(mask, 1.0, 0.0).astype(BF16)
    same_bf = jnp.where(same, 1.0, 0.0).astype(BF16)
    onehot = jnp.where((lax.broadcasted_iota(jnp.int32, (tb, n_sub), 0) >> sh)
                       == lax.broadcasted_iota(jnp.int32, (tb, n_sub), 1), 1.0, 0.0).astype(BF16)
    tn = lambda part: lax.dot_general(part, onehot, (((0,), (0,)), ((), ())), preferred_element_type=F32)

    log_a = [jax.nn.log_sigmoid(_dot(sm_refs[q][...], wgk_ref[...]) + bgk_ref[...]) / GLA_GATE_NORM for q in seqs]
    gcum = [_dot01(mask_bf, la) for la in log_a]
    gtot = [_dot01(same_bf, la) for la in log_a]
    glcol = []
    for la in log_a:
        hi, mid, lo = _split3(la)
        glcol.append(jnp.exp(tn(hi) + tn(mid) + tn(lo)))

    nw = nw_ref[...]
    sl = lambda j: slice(j * cs, (j + 1) * cs)
    qg, kg, kd, vh = {}, {}, {}, {}
    for q in seqs:
        qk = qk_refs[q][...]
        for h in heads:
            qh = qk[:, h * DK_B:(h + 1) * DK_B] * (DK_B ** -0.5)
            kh = qk[:, w_kb + h * DK_B:w_kb + (h + 1) * DK_B]
            gh = gcum[q][:, h * DK_B:(h + 1) * DK_B]
            gt = gtot[q][:, h * DK_B:(h + 1) * DK_B]
            qg[q, h] = qh * jnp.exp(gh)
            kg[q, h] = kh * jnp.exp(-gh)
            kd[q, h] = kh * jnp.exp(gt - gh)
            vh[q, h] = v_refs[q][:, h * DV_B:(h + 1) * DV_B]
    chains = [(q, h) for q in seqs for h in heads]
    attn = {c: jnp.where(mask, _dot_nt(qg[c], kg[c]), 0.0) for c in chains}
    o_intra = {c: _dot(attn[c], vh[c]) for c in chains}
    upd = {(j, c): _dot_tn(kd[c][sl(j)], vh[c][sl(j)]) for j in range(n_sub) for c in chains}
    gl = lambda j, c: glcol[c[0]][c[1] * DK_B:(c[1] + 1) * DK_B, j:j + 1]

    o_inter = {c: [] for c in chains}
    if carry:
        cur = {c: s_ref[c[0], c[1]] for c in chains}
        for j in range(n_sub):
            for c in chains:
                o_inter[c].append(_dot(qg[c][sl(j)], cur[c]))
                cur[c] = cur[c] * gl(j, c) + upd[j, c]
        for c in chains:
            s_ref[c[0], c[1]] = cur[c]

        @pl.when(t == n_steps - 1)
        def _():
            for c in chains:
                sout_ref[c[0], c[1]] = cur[c]
    else:
        for j in range(n_sub):
            for c in chains:
                s = s0_ref[j, c[1]]
                o_inter[c].append(_dot(qg[c][sl(j)], s))
                sout_ref[j, c[1]] = s * gl(j, c) + upd[j, c]

    for (q, h) in chains:
        o = o_intra[q, h] + jnp.concatenate(o_inter[q, h], axis=0)
        r = o * lax.rsqrt(jnp.mean(o * o, axis=-1, keepdims=True) + RMS_EPS) * nw
        gated = r * _silu(g_refs[q][:, h * DV_B:(h + 1) * DV_B])
        if carry:
            o_ref[q, :, h * DV_B:(h + 1) * DV_B] = gated
        else:
            o_ref[:, h * DV_B:(h + 1) * DV_B] = gated


def _gla(p, row0, batch, seq, wgk, bgk, norm_w, s0):
    cs = math.gcd(seq, GLA_CHUNK)
    carry = seq > cs
    if carry:
        tb = math.gcd(seq, GLA_TB)
        n_seq = math.gcd(batch, GLA_SEQS_CARRY)
        grid = (batch // n_seq, seq // tb)
        n_state = n_seq
    else:
        tb = math.gcd(batch * seq, GLA_TB)
        n_seq = 1
        n_state = tb // cs
        grid = (batch // n_state, 1)
    n_steps = grid[1]
    blk0 = row0 // tb
    rowblk = lambda q: (lambda b, t: blk0 + (b * n_seq + q) * n_steps + t)
    pcol = lambda width, off: [pl.BlockSpec((tb, width), lambda b, t, rb=rowblk(q): (rb(b, t), off // width))
                               for q in range(n_seq)]
    kern = functools.partial(_gla_kernel, tb=tb, cs=cs, n_seq=n_seq, carry=carry, n_steps=n_steps)
    if carry:
        o_shape, o_spec = (batch, seq, W_B), pl.BlockSpec((n_seq, tb, W_B), lambda b, t: (b, t, 0))
    else:
        o_shape, o_spec = (batch * seq, W_B), pl.BlockSpec((tb, W_B), lambda b, t: (b, 0))
    o, s_new = pl.pallas_call(
        kern,
        out_shape=(jax.ShapeDtypeStruct(o_shape, F32),
                   jax.ShapeDtypeStruct((batch, N_HEADS, DK_B, DV_B), F32)),
        grid=grid,
        in_specs=[
            *pcol(2 * N_HEADS * DK_B, OFF_QK_B), *pcol(W_B, OFF_V_B), *pcol(W_B, OFF_G_B), *pcol(LANES, OFF_SMALL),
            pl.BlockSpec((LANES, N_HEADS * DK_B), lambda b, t: (0, 0)),
            pl.BlockSpec((1, N_HEADS * DK_B), lambda b, t: (0, 0)),
            pl.BlockSpec((1, DV_B), lambda b, t: (0, 0)),
            pl.BlockSpec((n_state, N_HEADS, DK_B, DV_B), lambda b, t: (b, 0, 0, 0)),
        ],
        out_specs=(
            o_spec,
            pl.BlockSpec((n_state, N_HEADS, DK_B, DV_B), lambda b, t: (b, 0, 0, 0)),
        ),
        scratch_shapes=[pltpu.VMEM((n_seq, N_HEADS, DK_B, DV_B), F32)],
        compiler_params=pltpu.CompilerParams(
            dimension_semantics=("arbitrary", "arbitrary"), vmem_limit_bytes=VMEM_LIMIT),
        name=f"gla_c{cs}",
    )(*([p] * (4 * n_seq)), wgk, bgk.reshape(1, N_HEADS * DK_B), norm_w.reshape(1, DV_B), s0)
    return o.reshape(batch * seq, W_B), s_new


def _topk_softmax(logits):
    tm = logits.shape[0]
    lane = lax.broadcasted_iota(jnp.int32, (tm, LANES), 1)
    lane_f = lane.astype(F32)
    neg = jnp.float32(-jnp.inf)
    cur = jnp.where(lane < N_EXPERTS, logits, neg)
    vals, idxs = [], []
    for _ in range(TOP_K):
        m = jnp.max(cur, axis=-1, keepdims=True)
        idx = jnp.min(jnp.where(cur == m, lane_f, float(LANES)), axis=-1, keepdims=True)
        vals.append(m)
        idxs.append(idx)
        cur = jnp.where(lane_f == idx, neg, cur)
    exps = [jnp.exp(v - vals[0]) for v in vals]
    den = exps[0] + exps[1] + exps[2] + exps[3]
    te = jnp.zeros((tm, LANES), F32)
    tg = jnp.zeros((tm, LANES), F32)
    for k in range(TOP_K):
        te = jnp.where(lane == k, idxs[k], te)
        tg = jnp.where(lane == k, exps[k] / den, tg)
    return te.astype(jnp.int32), tg


def _merge_kernel(xp_ref, xs_ref, oap_ref, oas_ref, obp_ref, obs_ref, ga_ref, gb_ref, wua_ref, wub_ref, wo_ref,
                  n2_ref, wr_ref, br_ref, x1_ref, h2_ref, te_ref, tg_ref, *, n_first):
    i = pl.program_id(0)
    tm = x1_ref.shape[0]
    hm = tm // MERGE_SPLIT
    groups = [slice(a * hm, (a + 1) * hm) for a in range(MERGE_SPLIT)]
    oa = _pick(i, n_first, oap_ref, oas_ref).astype(BF16)
    ob = _pick(i, n_first, obp_ref, obs_ref).astype(BF16)
    x = _pick(i, n_first, xp_ref, xs_ref)
    ua = [jnp.dot(oa[g], wua_ref[...], preferred_element_type=F32) for g in groups]
    ub = [jnp.dot(ob[g], wub_ref[...], preferred_element_type=F32) for g in groups]
    merged = [jax.nn.sigmoid(ga_ref[g, :].astype(F32)) * ua[a] + jax.nn.sigmoid(gb_ref[g, :].astype(F32)) * ub[a]
              for a, g in enumerate(groups)]
    x1 = [x[g] + jnp.dot(merged[a].astype(BF16), wo_ref[...], preferred_element_type=F32)
          for a, g in enumerate(groups)]
    h2 = [v * lax.rsqrt(jnp.mean(v * v, axis=-1, keepdims=True) + RMS_EPS) * n2_ref[...] for v in x1]
    h_hi = [v.astype(BF16) for v in h2]
    h_lo = [(v - hi.astype(F32)).astype(BF16) for v, hi in zip(h2, h_hi)]
    l_hi = [jnp.dot(hi, wr_ref[...], preferred_element_type=F32) for hi in h_hi]
    l_lo = [jnp.dot(lo, wr_ref[:, :LANES], preferred_element_type=F32) for lo in h_lo]
    for a, g in enumerate(groups):
        x1_ref[g, :] = x1[a]
        _store_packed_rows(h2_ref, (), a * hm, _pack_bf16_pairs(h2[a]))
        logits = l_hi[a][:, :LANES] + l_hi[a][:, LANES:] + l_lo[a] + br_ref[...]
        te, tg = _topk_softmax(logits)
        te_ref[g, :] = te
        tg_ref[g, :] = tg


def _merge(x_p, x_s, oa_p, oa_s, ob_p, ob_s, p_gates, wua, wub, wo, norm2_w, wr, br):
    n = x_p.shape[0] + x_s.shape[0]
    tm = MERGE_TM
    n_first = x_p.shape[0] // tm
    row = lambda i: (i, 0)
    const = lambda i: (0, 0)
    return pl.pallas_call(
        functools.partial(_merge_kernel, n_first=n_first),
        out_shape=(jax.ShapeDtypeStruct((n, D_MODEL), F32), jax.ShapeDtypeStruct((n * PACK_ROWS, LANES), jnp.uint32),
                   jax.ShapeDtypeStruct((n, LANES), jnp.int32), jax.ShapeDtypeStruct((n, LANES), F32)),
        grid=(n // tm,),
        in_specs=[
            *_split_specs((tm, D_MODEL), n_first),
            *_split_specs((tm, W_A), n_first),
            *_split_specs((tm, W_B), n_first),
            pl.BlockSpec((tm, D_MODEL), lambda i: (i, 0)),
            pl.BlockSpec((tm, D_MODEL), lambda i: (i, 1)),
            pl.BlockSpec((W_A, D_MODEL), const),
            pl.BlockSpec((W_B, D_MODEL), const),
            pl.BlockSpec((D_MODEL, D_MODEL), const),
            pl.BlockSpec((1, D_MODEL), const),
            pl.BlockSpec((D_MODEL, 2 * LANES), const),
            pl.BlockSpec((1, LANES), const),
        ],
        out_specs=(pl.BlockSpec((tm, D_MODEL), row), pl.BlockSpec((tm * PACK_ROWS, LANES), row),
                   pl.BlockSpec((tm, LANES), row), pl.BlockSpec((tm, LANES), row)),
        compiler_params=pltpu.CompilerParams(
            dimension_semantics=("arbitrary",), vmem_limit_bytes=VMEM_LIMIT),
        name="merge_router",
    )(x_p, x_s, oa_p, oa_s, ob_p, ob_s, p_gates, p_gates, wua, wub, wo, norm2_w.reshape(1, D_MODEL), wr, br)


def _moe_kernel(blk_e_ref, nxt_e_ref, n_sub_ref, n_used_ref, src_ref, src_nxt_ref, dst_ref, h2_ref, wgu_hbm, bgu_ref,
                wdn_hbm, bdn_ref, y_hbm, xstage, ybuf, wgu_stage, wdn_stage, wgu_bf, wdn_bf, ssem, wsem):
    i = pl.program_id(0)
    n_used = n_used_ref[0]
    bm = MOE_BM
    pr = PACK_ROWS
    e_cur = blk_e_ref[i]

    def weight_copies(e):
        return (pltpu.make_async_copy(wgu_hbm.at[e], wgu_stage, wsem.at[0]),
                pltpu.make_async_copy(wdn_hbm.at[e], wdn_stage, wsem.at[1]))

    @pl.when((i == 0) & (n_used > 0))
    def _():
        for cp in weight_copies(e_cur):
            cp.start()

    @pl.when((i < n_used) & ((i == 0) | (e_cur != blk_e_ref[jnp.maximum(i - 1, 0)])))
    def _():
        for cp in weight_copies(e_cur):
            cp.wait()
        wgu_bf[...] = wgu_stage[...].astype(BF16)
        wdn_bf[...] = wdn_stage[...].astype(BF16)

        @pl.when(nxt_e_ref[i] != e_cur)
        def _():
            for cp in weight_copies(nxt_e_ref[i]):
                cp.start()

    def scatter_copy(row, s, r):
        return pltpu.make_async_copy(ybuf.at[s, pl.ds(r * pr, pr), :],
                                     y_hbm.at[pl.ds(pl.multiple_of(row * pr, pr), pr), :], ssem.at[s])

    @pl.when(i == 0)
    def _():
        ybuf[...] = jnp.zeros_like(ybuf)
        n_real = y_hbm.shape[0] - MOE_SUB * bm * pr
        for s in range(MOE_SUB):
            sink = pltpu.make_async_copy(ybuf.at[s], y_hbm.at[pl.ds(n_real + s * bm * pr, bm * pr), :], ssem.at[s])
            sink.start()
            sink.wait()

    def gather(idx_ref, sub, slot):
        for r in range(bm):
            tok = pl.multiple_of(idx_ref[0, 0, sub * bm + r] * pr, pr)
            xstage[slot, pl.ds(r * pr, pr), :] = h2_ref[pl.ds(tok, pr), :]

    def wait_prev_scatter(s):
        @pl.when((i >= 1) & (n_sub_ref[jnp.maximum(i - 1, 0)] > s))
        def _():
            for r in range(bm):
                scatter_copy(0, s, r).wait()

    def expert_mlp(s):
        x = _load_packed_rows(xstage, (s,), bm, BF16)
        gu = jnp.dot(x, wgu_bf[...], preferred_element_type=F32) + bgu_ref[...]
        gate = jnp.minimum(gu[:, :D_FF], SWIGLU_LIMIT)
        up = jnp.clip(gu[:, D_FF:], -SWIGLU_LIMIT, SWIGLU_LIMIT)
        hmid = (up + 1.0) * (gate * jax.nn.sigmoid(SWIGLU_ALPHA * gate))
        y = jnp.dot(hmid.astype(BF16), wdn_bf[...], preferred_element_type=F32) + bdn_ref[...]
        _store_packed_rows(ybuf, (s,), 0, _pack_bf16_pairs(y))
        for r in range(bm):
            scatter_copy(dst_ref[0, 0, s * bm + r], s, r).start(priority=r % 2)

    assert MOE_SUB == 2

    @pl.when((i == 0) & (n_used > 0))
    def _():
        gather(src_ref, 0, 0)

    @pl.when(i < n_used)
    def _():
        wait_prev_scatter(0)
        gather(src_ref, 1, 1)
        expert_mlp(0)
        wait_prev_scatter(1)

        @pl.when(n_sub_ref[i] > 1)
        def _():
            gather(src_nxt_ref, 0, 0)
            expert_mlp(1)

        @pl.when(n_sub_ref[i] <= 1)
        def _():
            gather(src_nxt_ref, 0, 0)

    @pl.when(i == n_used - 1)
    def _():
        for s in range(MOE_SUB):
            @pl.when(n_sub_ref[i] > s)
            def _():
                for r in range(bm):
                    scatter_copy(0, s, r).wait()


def _moe(blk_e, nxt_e, n_sub, n_used, src_tok, dst_row, h2, wgu, bgu, wdn, bdn, y_rows):
    n_steps = src_tok.shape[0]
    step_rows = MOE_SUB * MOE_BM
    cur = lambda i, be, ne, ns, nu: (i, 0, 0)
    wsel = lambda i, be, ne, ns, nu: (be[i], 0, 0)
    smem_blk = lambda im: pl.BlockSpec((1, 1, step_rows), im, memory_space=pltpu.SMEM)
    grid_spec = pltpu.PrefetchScalarGridSpec(
        num_scalar_prefetch=4,
        grid=(n_steps,),
        in_specs=[
            smem_blk(cur), smem_blk(lambda i, be, ne, ns, nu: (jnp.minimum(i + 1, n_steps - 1), 0, 0)), smem_blk(cur),
            pl.BlockSpec(h2.shape, lambda i, be, ne, ns, nu: (0, 0), pipeline_mode=pl.Buffered(1)),
            pl.BlockSpec(memory_space=pl.ANY),
            pl.BlockSpec((None, 1, 2 * D_FF), wsel),
            pl.BlockSpec(memory_space=pl.ANY),
            pl.BlockSpec((None, 1, D_MODEL), wsel),
        ],
        out_specs=pl.BlockSpec(memory_space=pl.ANY),
        scratch_shapes=[pltpu.VMEM((MOE_SUB, MOE_BM * PACK_ROWS, LANES), jnp.uint32),
                        pltpu.VMEM((MOE_SUB, MOE_BM * PACK_ROWS, LANES), jnp.uint32),
                        pltpu.VMEM((D_MODEL, 2 * D_FF), F32), pltpu.VMEM((D_FF, D_MODEL), F32),
                        pltpu.VMEM((D_MODEL, 2 * D_FF), BF16), pltpu.VMEM((D_FF, D_MODEL), BF16),
                        pltpu.SemaphoreType.DMA((MOE_SUB,)), pltpu.SemaphoreType.DMA((2,))],
    )
    return pl.pallas_call(
        _moe_kernel,
        out_shape=jax.ShapeDtypeStruct((y_rows * PACK_ROWS, LANES), jnp.uint32),
        grid_spec=grid_spec,
        compiler_params=pltpu.CompilerParams(
            dimension_semantics=("arbitrary",), vmem_limit_bytes=MOE_VMEM_LIMIT),
        name="moe_experts",
    )(blk_e, nxt_e, n_sub, n_used, src_tok, src_tok, dst_row, h2, wgu, bgu, wdn, bdn)


def _route(top_e, n_tok):
    bm = MOE_SUB * MOE_BM
    n_assign = n_tok * TOP_K
    n_blocks = n_assign // bm + N_EXPERTS
    n_dummy = N_EXPERTS * bm
    flat_e = top_e.reshape(-1)
    counts = jnp.sum((flat_e[:, None] == jnp.arange(N_EXPERTS, dtype=jnp.int32)[None, :]).astype(jnp.int32), axis=0)
    padded = (counts + bm - 1) // bm * bm
    pad_end = jnp.cumsum(padded)
    d_e = jnp.arange(n_dummy, dtype=jnp.int32) // bm
    d_active = (jnp.arange(n_dummy, dtype=jnp.int32) % bm) < (padded - counts)[d_e]
    keys = jnp.concatenate([flat_e * 2, jnp.where(d_active, d_e * 2 + 1, 2 * N_EXPERTS)])
    vbits = n_assign.bit_length()
    marker = (1 << vbits) - 1
    vals = jnp.concatenate([jnp.arange(n_assign, dtype=jnp.int32), jnp.full((n_dummy,), marker, jnp.int32)])
    assign = jnp.sort(keys * (1 << vbits) + vals)[:n_blocks * bm] & marker
    rows = jnp.arange(n_blocks * bm, dtype=jnp.int32)
    valid = assign != marker
    src_tok = jnp.where(valid, assign // TOP_K, 0)
    dst_row = jnp.where(valid, (assign % TOP_K) * n_tok + assign // TOP_K, n_assign + rows % bm)
    blk_start = jnp.arange(n_blocks, dtype=jnp.int32) * bm
    blk_e = jnp.minimum(jnp.sum((pad_end[None, :] <= blk_start[:, None]).astype(jnp.int32), axis=1),
                        N_EXPERTS - 1).astype(jnp.int32)
    n_used = (pad_end[-1] // bm).astype(jnp.int32).reshape(1)
    later = (blk_e[None, :] > blk_e[:, None]) & (blk_start[None, :] < pad_end[-1])
    nxt_e = jnp.min(jnp.where(later, blk_e[None, :], N_EXPERTS), axis=1)
    nxt_e = jnp.where(nxt_e == N_EXPERTS, blk_e, nxt_e).astype(jnp.int32)
    n_sub = jnp.sum(valid.reshape(n_blocks, MOE_SUB, MOE_BM)[:, :, 0].astype(jnp.int32), axis=1)
    return (blk_e, nxt_e, n_sub, n_used, src_tok.reshape(n_blocks, 1, bm).astype(jnp.int32),
            dst_row.reshape(n_blocks, 1, bm).astype(jnp.int32))


def _combine_kernel(x1_ref, y0_ref, y1_ref, y2_ref, y3_ref, tg_ref, nf_ref, op_ref, os_ref, *, n_first):
    tg = tg_ref[...]
    tm = x1_ref.shape[0]
    rows = lambda y_ref: _load_packed_rows(y_ref, (), tm, F32)
    acc = rows(y0_ref) * tg[:, 0:1]
    for k, y_ref in enumerate((y1_ref, y2_ref, y3_ref), start=1):
        acc = acc + rows(y_ref) * tg[:, k:k + 1]
    x = x1_ref[...] + acc
    out = x * lax.rsqrt(jnp.mean(x * x, axis=-1, keepdims=True) + RMS_EPS) * nf_ref[...]
    i = pl.program_id(0)

    @pl.when(i < n_first)
    def _():
        op_ref[...] = out

    @pl.when(i >= n_first)
    def _():
        os_ref[...] = out


def _combine(x1, y, tg, norm_f_w, n_p):
    n = x1.shape[0]
    tm = COMBINE_TM
    n_first = n_p // tm
    row = lambda i: (i, 0)
    y_spec = lambda k: pl.BlockSpec((tm * PACK_ROWS, LANES), lambda i: (k * (n // tm) + i, 0))
    return pl.pallas_call(
        functools.partial(_combine_kernel, n_first=n_first),
        out_shape=(jax.ShapeDtypeStruct((n_p, D_MODEL), F32), jax.ShapeDtypeStruct((n - n_p, D_MODEL), F32)),
        grid=(n // tm,),
        in_specs=[pl.BlockSpec((tm, D_MODEL), row), y_spec(0), y_spec(1), y_spec(2), y_spec(3),
                  pl.BlockSpec((tm, LANES), row), pl.BlockSpec((1, D_MODEL), lambda i: (0, 0))],
        out_specs=_split_specs((tm, D_MODEL), n_first),
        compiler_params=pltpu.CompilerParams(
            dimension_semantics=("arbitrary",), vmem_limit_bytes=VMEM_LIMIT),
        name="combine_norm",
    )(x1, y, y, y, y, tg, norm_f_w.reshape(1, D_MODEL))


def _pack_w_in(w_in):
    s = np.cumsum((0,) + IN_SIZES)
    w_in = w_in.astype(BF16)
    qkv, z, al, be, qb, kb, vb, gb, gk, ga, gbt = [w_in[:, s[i]:s[i + 1]] for i in range(len(IN_SIZES))]
    small = jnp.concatenate(
        [al, be, gk, jnp.zeros((D_MODEL, LANES - 2 * N_HEADS - GK_RANK), w_in.dtype)], axis=1)
    return jnp.concatenate([qkv, z, qb, kb, vb, gb, small, ga, gbt], axis=1)


def kernel(x_prompt, x_sample, state_gdn, state_gdn_conv, state_gla, norm1_w, w_in, conv_w, a_log,
           dt_bias, gdn_norm_w, gla_gk_up, gla_gk_b, gla_norm_w, w_up_a, w_up_b, w_o, norm2_w,
           w_router, b_router, w_gate_up, b_gate_up, w_down, b_down, norm_f_w):
    bp, tp, _ = x_prompt.shape
    bs, ts, _ = x_sample.shape
    n_p, n_s = bp * tp, bs * ts
    n = n_p + n_s
    l = 0

    x_p = x_prompt.reshape(n_p, D_MODEL)
    x_s = x_sample.reshape(n_s, D_MODEL)
    p, p_gates = _inproj(x_p, x_s, norm1_w[l], _pack_w_in(w_in[l]))

    hp = jnp.zeros((8, LANES), F32).at[0, :N_HEADS].set(a_log[l]).at[1, :N_HEADS].set(dt_bias[l])
    wgk = jnp.zeros((LANES, N_HEADS * DK_B), F32).at[2 * N_HEADS:2 * N_HEADS + GK_RANK].set(gla_gk_up[l]).astype(BF16)
    zeros = lambda shape: jnp.zeros(shape, F32)

    oa_p, gdn_p, conv_p = _gdn(p, 0, bp, tp, conv_w[l], hp, gdn_norm_w[l],
                               zeros((bp, CONV_W - 1, CONV_DIM)), zeros((bp, N_HEADS, DK_A, DV_A)))
    oa_s, gdn_s, conv_s = _gdn(p, n_p, bs, ts, conv_w[l], hp, gdn_norm_w[l], state_gdn_conv[l], state_gdn[l])
    ob_p, gla_p = _gla(p, 0, bp, tp, wgk, gla_gk_b[l], gla_norm_w[l], zeros((bp, N_HEADS, DK_B, DV_B)))
    ob_s, gla_s = _gla(p, n_p, bs, ts, wgk, gla_gk_b[l], gla_norm_w[l], state_gla[l])

    wr = jnp.zeros((D_MODEL, LANES), F32).at[:, :N_EXPERTS].set(w_router[l])
    wr_hi = wr.astype(BF16)
    wr = jnp.concatenate([wr_hi, (wr - wr_hi.astype(F32)).astype(BF16)], axis=1)
    br = jnp.zeros((1, LANES), F32).at[0, :N_EXPERTS].set(b_router[l])
    x1, h2, te, tg = _merge(x_p, x_s, oa_p, oa_s, ob_p, ob_s, p_gates, w_up_a[l].astype(BF16),
                            w_up_b[l].astype(BF16), w_o[l].astype(BF16), norm2_w[l], wr, br)

    blk_e, nxt_e, n_sub, n_used, src_tok, dst_row = _route(te[:, :TOP_K], n)
    y = _moe(blk_e, nxt_e, n_sub, n_used, src_tok, dst_row, h2, w_gate_up[l],
             b_gate_up[l].reshape(N_EXPERTS, 1, 2 * D_FF), w_down[l],
             b_down[l].reshape(N_EXPERTS, 1, D_MODEL), n * TOP_K + MOE_SUB * MOE_BM)
    out_p, out_s = _combine(x1, y, tg, norm_f_w, n_p)
    y_prompt = out_p.reshape(bp, tp, D_MODEL)
    y_sample = out_s.reshape(bs, ts, D_MODEL)
    return (y_prompt, y_sample, gdn_p[None], conv_p[None], gla_p[None], gdn_s[None], conv_s[None],
            gla_s[None])
```

```python
import functools
import math

import jax
import jax.numpy as jnp
import numpy as np
from jax import lax
from jax.experimental import pallas as pl
from jax.experimental.pallas import tpu as pltpu

F32 = jnp.float32
BF16 = jnp.bfloat16

D_MODEL = 1024
N_HEADS = 4
DK_A = 128
DV_A = 128
W_A = N_HEADS * DV_A
CONV_W = 4
CONV_DIM = 2 * N_HEADS * DK_A + W_A
GDN_CHUNK = 64
DK_B = 64
DV_B = 128
W_B = N_HEADS * DV_B
GK_RANK = 16
GLA_GATE_NORM = 16.0
GLA_CHUNK = 16
N_EXPERTS = 32
TOP_K = 4
D_FF = 1024
SWIGLU_LIMIT = 7.0
SWIGLU_ALPHA = 1.702
RMS_EPS = 1e-6
IN_SIZES = (CONV_DIM, W_A, N_HEADS, N_HEADS, N_HEADS * DK_B, N_HEADS * DK_B, W_B, W_B, GK_RANK,
            D_MODEL, D_MODEL)

LANES = 128

OFF_QKV = 0
OFF_Z = 1536
OFF_GATE_A = 2048
OFF_GATE_B = 3072
OFF_QK_B = 4096
OFF_V_B = 4608
OFF_G_B = 5120
OFF_SMALL = 5632
P_COLS = 5760

INPROJ_TM = 512
GLA_TB = 256
GLA_SEQS_CARRY = 4
GDN_SEGS_CARRY = 4
GDN_SEQS_CARRY = 2
GDN_SEGS_INDEP = 16
MERGE_TM = 512
MERGE_SPLIT = 2
MOE_BM = 256
MOE_SUB = 4
PACK_ROWS = D_MODEL // 2 // LANES
MOE_VMEM_LIMIT = 62 * 1024 * 1024
COMBINE_TM = 512
VMEM_LIMIT = 56 * 1024 * 1024


def _dot(a, b):
    return jnp.dot(a.astype(BF16), b.astype(BF16), preferred_element_type=F32)


def _dot_nt(a, b):
    return lax.dot_general(a.astype(BF16), b.astype(BF16), (((1,), (1,)), ((), ())),
                           preferred_element_type=F32)


def _dot_tn(a, b):
    return lax.dot_general(a.astype(BF16), b.astype(BF16), (((0,), (0,)), ((), ())), preferred_element_type=F32)


def _split3(x):
    hi = x.astype(BF16)
    r1 = x - hi.astype(F32)
    mid = r1.astype(BF16)
    lo = (r1 - mid.astype(F32)).astype(BF16)
    return hi, mid, lo


def _dot01(m01, x):
    hi, mid, lo = _split3(x)
    d = lambda p: jnp.dot(m01, p, preferred_element_type=F32)
    return d(hi) + d(mid) + d(lo)


def _silu(x):
    return x * jax.nn.sigmoid(x)


def _pack_bf16_pairs(x):
    bits = pltpu.bitcast(x.astype(BF16).astype(F32), jnp.uint32)
    half = D_MODEL // 2
    return (bits[:, :half] >> 16) | (bits[:, half:] & jnp.uint32(0xFFFF0000))


def _store_packed_rows(ref, lead, row0, words):
    n = words.shape[0]
    for c in range(PACK_ROWS):
        ref[(*lead, pl.ds(row0 * PACK_ROWS + c, n, stride=PACK_ROWS), slice(None))] = words[:, c * LANES:(c + 1) * LANES]


def _load_packed_rows(ref, lead, n, dtype):
    words = [ref[(*lead, pl.ds(c, n, stride=PACK_ROWS), slice(None))] for c in range(PACK_ROWS)]
    lo = [pltpu.bitcast(w << 16, F32).astype(dtype) for w in words]
    hi = [pltpu.bitcast(w & jnp.uint32(0xFFFF0000), F32).astype(dtype) for w in words]
    return jnp.concatenate(lo + hi, axis=1)


def _pick(i, n_first, first_ref, second_ref):
    return jnp.where(i < n_first, first_ref[...], second_ref[...])


def _split_specs(block, n_first):
    return (pl.BlockSpec(block, lambda i: (jnp.minimum(i, n_first - 1), 0)),
            pl.BlockSpec(block, lambda i: (jnp.maximum(i - n_first, 0), 0)))


def _inproj_kernel(xp_ref, xs_ref, nw_ref, w_ref, o_ref, *, n_first):
    x = _pick(pl.program_id(0), n_first, xp_ref, xs_ref)
    ms = jnp.mean(x * x, axis=-1, keepdims=True)
    h = (x * lax.rsqrt(ms + RMS_EPS) * nw_ref[...]).astype(BF16)
    o_ref[...] = jnp.dot(h, w_ref[...], preferred_element_type=F32)


def _inproj(x_p, x_s, norm_w, w_packed):
    tm = INPROJ_TM
    n_first = x_p.shape[0] // tm
    n = x_p.shape[0] + x_s.shape[0]
    return pl.pallas_call(
        functools.partial(_inproj_kernel, n_first=n_first),
        out_shape=jax.ShapeDtypeStruct((n, P_COLS), F32),
        grid=(n // tm,),
        in_specs=[
            *_split_specs((tm, D_MODEL), n_first),
            pl.BlockSpec((1, D_MODEL), lambda i: (0, 0)),
            pl.BlockSpec((D_MODEL, P_COLS), lambda i: (0, 0), pipeline_mode=pl.Buffered(1)),
        ],
        out_specs=pl.BlockSpec((tm, P_COLS), lambda i: (i, 0)),
        compiler_params=pltpu.CompilerParams(
            dimension_semantics=("arbitrary",), vmem_limit_bytes=VMEM_LIMIT),
        name="inproj",
    )(x_p, x_s, norm_w.reshape(1, D_MODEL), w_packed)


def _gdn_kernel(*refs, chunk, n_seg, n_seq, carry, n_steps):
    qkv_refs, z_refs, sm_refs = refs[:n_seq], refs[n_seq:2 * n_seq], refs[2 * n_seq:3 * n_seq]
    cw_ref, hp_ref, nw_ref, sconv_ref, s0_ref, o_ref, sout_ref, cout_ref, s_ref, ubuf_ref = refs[3 * n_seq:]
    c_len = chunk
    n_seg_all = n_seq * n_seg
    rows = n_seg_all * c_len
    t = pl.program_id(1)
    hist = CONV_W - 1
    base = 8
    conv_len = n_seg * c_len if carry else c_len
    n_conv = n_seq if carry else n_seg

    def init_history():
        ubuf_ref[:, 0:base, :] = jnp.zeros((n_conv, base, CONV_DIM), F32)
        ubuf_ref[:, base - hist:base, :] = sconv_ref[...]

    if carry:
        @pl.when(t == 0)
        def _():
            s_ref[...] = s0_ref[...]
            init_history()

        @pl.when(t > 0)
        def _():
            ubuf_ref[:, base - hist:base, :] = ubuf_ref[:, base + conv_len - hist:base + conv_len, :]
        for q in range(n_seq):
            ubuf_ref[q, base:base + conv_len, :] = qkv_refs[q][...]
    else:
        init_history()
        ubuf_ref[:, base:base + conv_len, :] = qkv_refs[0][...].reshape(n_conv, conv_len, CONV_DIM)

    cw = cw_ref[...]
    u = ubuf_ref[...]
    u1 = pltpu.roll(u, 1, axis=1)
    v = u1 * cw[0:1, :] + u * cw[1:2, :]
    y = (pltpu.roll(v, 2, axis=1) + u1 * cw[2:3, :]) + u * cw[3:4, :]
    y = _silu(y[:, base:, :]).reshape(rows, CONV_DIM)

    sm = sm_refs[0][...] if n_seq == 1 else jnp.concatenate([r[...] for r in sm_refs], axis=0)
    hp = hp_ref[...]
    lane = lax.broadcasted_iota(jnp.int32, (rows, LANES), 1)
    g_all = -jnp.exp(hp[0:1, :]) * jax.nn.softplus(sm + hp[1:2, :])
    g_all = jnp.where(lane < N_HEADS, g_all, 0.0)
    beta_all = jax.nn.sigmoid(sm)

    sh = int(round(math.log2(c_len)))
    r_b = lax.broadcasted_iota(jnp.int32, (rows, rows), 0)
    c_b = lax.broadcasted_iota(jnp.int32, (rows, rows), 1)
    seg_tril = ((r_b >> sh) == (c_b >> sh)) & (c_b <= r_b)
    gc = _dot01(jnp.where(seg_tril, 1.0, 0.0).astype(BF16), g_all)

    r_i = lax.broadcasted_iota(jnp.int32, (c_len, c_len), 0)
    c_i = lax.broadcasted_iota(jnp.int32, (c_len, c_len), 1)
    tri = c_i <= r_i
    strict = c_i < r_i
    eye = c_i == r_i
    nw = nw_ref[...]
    levels = sh

    qn, kn, kb, rhs, qg, gcol = [], [], [], [], [], []
    for h in range(N_HEADS):
        qh = y[:, h * DK_A:(h + 1) * DK_A]
        kh = y[:, W_A + h * DK_A:W_A + (h + 1) * DK_A]
        vh = y[:, 2 * W_A + h * DV_A:2 * W_A + (h + 1) * DV_A]
        qn_h = qh * lax.rsqrt(jnp.sum(qh * qh, axis=-1, keepdims=True) + 1e-6) * (DK_A ** -0.5)
        kn_h = kh * lax.rsqrt(jnp.sum(kh * kh, axis=-1, keepdims=True) + 1e-6)
        gcol_h = gc[:, h:h + 1]
        eg = jnp.exp(gcol_h)
        bcol = beta_all[:, N_HEADS + h:N_HEADS + h + 1]
        kb_h = kn_h * bcol
        qn.append(qn_h)
        kn.append(kn_h)
        kb.append(kb_h)
        rhs.append(jnp.concatenate([vh * bcol, kb_h * eg], axis=1))
        qg.append(qn_h * eg)
        gcol.append(gcol_h)

    pairs = [(g, h) for g in range(n_seg_all) for h in range(N_HEADS)]
    sl = lambda g: slice(g * c_len, (g + 1) * c_len)

    decay, kd, gl = {}, {}, {}
    for (g, h) in pairs:
        gcol_p = gcol[h][sl(g)]
        gcb = jnp.broadcast_to(gcol_p, (c_len, c_len))
        grow = jnp.sum(jnp.where(eye, gcb, 0.0), axis=0, keepdims=True)
        decay[g, h] = jnp.where(tri, jnp.exp(jnp.where(tri, gcb - grow, 0.0)), 0.0)
        glast = gcol[h][(g + 1) * c_len - 1:(g + 1) * c_len]
        kd[g, h] = kn[h][sl(g)] * jnp.exp(glast - gcol_p)
        gl[g, h] = jnp.exp(glast)

    kq = {p: _dot_nt(jnp.concatenate([kb[p[1]][sl(p[0])], qn[p[1]][sl(p[0])]], axis=0), kn[p[1]][sl(p[0])])
          for p in pairs}
    low = {p: jnp.where(strict, kq[p][:c_len] * decay[p], 0.0) for p in pairs}
    attn = {p: jnp.where(tri, kq[p][c_len:] * decay[p], 0.0) for p in pairs}
    ymat = {p: -low[p] for p in pairs}
    lp = low
    for _ in range(levels - 1):
        lp = {p: _dot(lp[p], lp[p]) for p in pairs}
        prod = {p: _dot(ymat[p], lp[p]) for p in pairs}
        ymat = {p: ymat[p] + lp[p] + prod[p] for p in pairs}
    sol = {}
    for (g, h) in pairs:
        rhs_p = rhs[h][sl(g)]
        sol[g, h] = rhs_p + _dot(ymat[g, h], rhs_p)

    def finish(g, h, o):
        r = o * lax.rsqrt(jnp.mean(o * o, axis=-1, keepdims=True) + RMS_EPS) * nw
        q, loc = (g // n_seg, sl(g % n_seg)) if carry else (0, sl(g))
        gated = r * _silu(z_refs[q][loc, h * DV_A:(h + 1) * DV_A])
        if carry:
            o_ref[q, loc, h * DV_A:(h + 1) * DV_A] = gated
        else:
            o_ref[loc, h * DV_A:(h + 1) * DV_A] = gated

    def advance(group, state):
        ws = {p: _dot(jnp.concatenate([sol[p][:, DV_A:], qg[p[1]][sl(p[0])]], axis=0), state[p]) for p in group}
        v_new = {p: sol[p][:, :DV_A] - ws[p][:c_len] for p in group}
        o = {p: ws[p][c_len:] + _dot(attn[p], v_new[p]) for p in group}
        new = {p: state[p] * gl[p] + _dot_tn(kd[p], v_new[p]) for p in group}
        for p in group:
            finish(p[0], p[1], o[p])
        return new

    if carry:
        chains = [(q, h) for q in range(n_seq) for h in range(N_HEADS)]
        cur = {c: s_ref[c[0], c[1]] for c in chains}
        for g in range(n_seg):
            group = [(q * n_seg + g, h) for (q, h) in chains]
            new = advance(group, {(q * n_seg + g, h): cur[q, h] for (q, h) in chains})
            cur = {(q, h): new[q * n_seg + g, h] for (q, h) in chains}
        for (q, h) in chains:
            s_ref[q, h] = cur[q, h]

        @pl.when(t == n_steps - 1)
        def _():
            for (q, h) in chains:
                sout_ref[q, h] = cur[q, h]
            cout_ref[...] = ubuf_ref[:, base + conv_len - hist:base + conv_len, :]
    else:
        new = advance(pairs, {p: s0_ref[p[0], p[1]] for p in pairs})
        for p in pairs:
            sout_ref[p[0], p[1]] = new[p]
        cout_ref[...] = ubuf_ref[:, base + conv_len - hist:base + conv_len, :]


def _gdn(p, row0, batch, seq, conv_w, hp, norm_w, s_conv, s0):
    chunk = math.gcd(seq, GDN_CHUNK)
    n_chunks = seq // chunk
    carry = n_chunks > 1
    if carry:
        n_seg = math.gcd(n_chunks, GDN_SEGS_CARRY)
        n_seq = math.gcd(batch, GDN_SEQS_CARRY)
        grid = (batch // n_seq, n_chunks // n_seg)
        n_conv, conv_len = n_seq, n_seg * chunk
    else:
        n_seg = math.gcd(batch, GDN_SEGS_INDEP)
        n_seq = 1
        grid = (batch // n_seg, 1)
        n_conv, conv_len = n_seg, chunk
    rows = n_seg * chunk
    n_steps = grid[1]
    blk0 = row0 // rows
    rowblk = lambda q: (lambda b, t: blk0 + (b * n_seq + q) * n_steps + t)
    pcol = lambda width, off: [pl.BlockSpec((rows, width), lambda b, t, rb=rowblk(q): (rb(b, t), off // width))
                               for q in range(n_seq)]
    kern = functools.partial(_gdn_kernel, chunk=chunk, n_seg=n_seg, n_seq=n_seq, carry=carry, n_steps=n_steps)
    if carry:
        o_shape, o_spec = (batch, seq, W_A), pl.BlockSpec((n_seq, rows, W_A), lambda b, t: (b, t, 0))
    else:
        o_shape, o_spec = (batch * seq, W_A), pl.BlockSpec((rows, W_A), lambda b, t: (b, 0))
    o, s_new, c_new = pl.pallas_call(
        kern,
        out_shape=(jax.ShapeDtypeStruct(o_shape, F32),
                   jax.ShapeDtypeStruct((batch, N_HEADS, DK_A, DV_A), F32),
                   jax.ShapeDtypeStruct((batch, CONV_W - 1, CONV_DIM), F32)),
        grid=grid,
        in_specs=[
            *pcol(CONV_DIM, OFF_QKV), *pcol(W_A, OFF_Z), *pcol(LANES, OFF_SMALL),
            pl.BlockSpec((CONV_W, CONV_DIM), lambda b, t: (0, 0)),
            pl.BlockSpec((8, LANES), lambda b, t: (0, 0)),
            pl.BlockSpec((1, DV_A), lambda b, t: (0, 0)),
            pl.BlockSpec((n_conv, CONV_W - 1, CONV_DIM), lambda b, t: (b, 0, 0)),
            pl.BlockSpec((n_conv, N_HEADS, DK_A, DV_A), lambda b, t: (b, 0, 0, 0)),
        ],
        out_specs=(
            o_spec,
            pl.BlockSpec((n_conv, N_HEADS, DK_A, DV_A), lambda b, t: (b, 0, 0, 0)),
            pl.BlockSpec((n_conv, CONV_W - 1, CONV_DIM), lambda b, t: (b, 0, 0)),
        ),
        scratch_shapes=[pltpu.VMEM((n_seq, N_HEADS, DK_A, DV_A), F32),
                        pltpu.VMEM((n_conv, conv_len + 8, CONV_DIM), F32)],
        compiler_params=pltpu.CompilerParams(
            dimension_semantics=("arbitrary", "arbitrary"), vmem_limit_bytes=VMEM_LIMIT),
        name=f"gdn_c{chunk}",
    )(*([p] * (3 * n_seq)), conv_w, hp, norm_w.reshape(1, DV_A), s_conv, s0)
    return o.reshape(batch * seq, W_A), s_new, c_new


def _gla_kernel(*refs, tb, cs, n_seq, carry, n_steps):
    qk_refs, v_refs, g_refs, sm_refs = (refs[k * n_seq:(k + 1) * n_seq] for k in range(4))
    wgk_ref, bgk_ref, nw_ref, s0_ref, o_ref, sout_ref, s_ref = refs[4 * n_seq:]
    t = pl.program_id(1)
    n_sub = tb // cs
    w_kb = N_HEADS * DK_B
    seqs = range(n_seq)
    heads = range(N_HEADS)

    if carry:
        @pl.when(t == 0)
        def _():
            s_ref[...] = s0_ref[...]

    sh = int(round(math.log2(cs)))
    r_i = lax.broadcasted_iota(jnp.int32, (tb, tb), 0)
    c_i = lax.broadcasted_iota(jnp.int32, (tb, tb), 1)
    same = (r_i >> sh) == (c_i >> sh)
    mask = same & (c_i <= r_i)
    mask_bf = jnp.where(mask, 1.0, 0.0).astype(BF16)
    same_bf = jnp.where(same, 1.0, 0.0).astype(BF16)
    onehot = jnp.where((lax.broadcasted_iota(jnp.int32, (tb, n_sub), 0) >> sh)
                       == lax.broadcasted_iota(jnp.int32, (tb, n_sub), 1), 1.0, 0.0).astype(BF16)
    tn = lambda part: lax.dot_general(part, onehot, (((0,), (0,)), ((), ())), preferred_element_type=F32)

    log_a = [jax.nn.log_sigmoid(_dot(sm_refs[q][...], wgk_ref[...]) + bgk_ref[...]) / GLA_GATE_NORM for q in seqs]
    gcum = [_dot01(mask_bf, la) for la in log_a]
    gtot = [_dot01(same_bf, la) for la in log_a]
    glcol = []
    for la in log_a:
        hi, mid, lo = _split3(la)
        glcol.append(jnp.exp(tn(hi) + tn(mid) + tn(lo)))

    nw = nw_ref[...]
    sl = lambda j: slice(j * cs, (j + 1) * cs)
    qg, kg, kd, vh = {}, {}, {}, {}
    for q in seqs:
        qk = qk_refs[q][...]
        for h in heads:
            qh = qk[:, h * DK_B:(h + 1) * DK_B] * (DK_B ** -0.5)
            kh = qk[:, w_kb + h * DK_B:w_kb + (h + 1) * DK_B]
            gh = gcum[q][:, h * DK_B:(h + 1) * DK_B]
            gt = gtot[q][:, h * DK_B:(h + 1) * DK_B]
            qg[q, h] = qh * jnp.exp(gh)
            kg[q, h] = kh * jnp.exp(-gh)
            kd[q, h] = kh * jnp.exp(gt - gh)
            vh[q, h] = v_refs[q][:, h * DV_B:(h + 1) * DV_B]
    chains = [(q, h) for q in seqs for h in heads]
    attn = {c: jnp.where(mask, _dot_nt(qg[c], kg[c]), 0.0) for c in chains}
    o_intra = {c: _dot(attn[c], vh[c]) for c in chains}
    upd = {(j, c): _dot_tn(kd[c][sl(j)], vh[c][sl(j)]) for j in range(n_sub) for c in chains}
    gl = lambda j, c: glcol[c[0]][c[1] * DK_B:(c[1] + 1) * DK_B, j:j + 1]

    o_inter = {c: [] for c in chains}
    if carry:
        cur = {c: s_ref[c[0], c[1]] for c in chains}
        for j in range(n_sub):
            for c in chains:
                o_inter[c].append(_dot(qg[c][sl(j)], cur[c]))
                cur[c] = cur[c] * gl(j, c) + upd[j, c]
        for c in chains:
            s_ref[c[0], c[1]] = cur[c]

        @pl.when(t == n_steps - 1)
        def _():
            for c in chains:
                sout_ref[c[0], c[1]] = cur[c]
    else:
        for j in range(n_sub):
            for c in chains:
                s = s0_ref[j, c[1]]
                o_inter[c].append(_dot(qg[c][sl(j)], s))
                sout_ref[j, c[1]] = s * gl(j, c) + upd[j, c]

    for (q, h) in chains:
        o = o_intra[q, h] + jnp.concatenate(o_inter[q, h], axis=0)
        r = o * lax.rsqrt(jnp.mean(o * o, axis=-1, keepdims=True) + RMS_EPS) * nw
        gated = r * _silu(g_refs[q][:, h * DV_B:(h + 1) * DV_B])
        if carry:
            o_ref[q, :, h * DV_B:(h + 1) * DV_B] = gated
        else:
            o_ref[:, h * DV_B:(h + 1) * DV_B] = gated


def _gla(p, row0, batch, seq, wgk, bgk, norm_w, s0):
    cs = math.gcd(seq, GLA_CHUNK)
    carry = seq > cs
    if carry:
        tb = math.gcd(seq, GLA_TB)
        n_seq = math.gcd(batch, GLA_SEQS_CARRY)
        grid = (batch // n_seq, seq // tb)
        n_state = n_seq
    else:
        tb = math.gcd(batch * seq, GLA_TB)
        n_seq = 1
        n_state = tb // cs
        grid = (batch // n_state, 1)
    n_steps = grid[1]
    blk0 = row0 // tb
    rowblk = lambda q: (lambda b, t: blk0 + (b * n_seq + q) * n_steps + t)
    pcol = lambda width, off: [pl.BlockSpec((tb, width), lambda b, t, rb=rowblk(q): (rb(b, t), off // width))
                               for q in range(n_seq)]
    kern = functools.partial(_gla_kernel, tb=tb, cs=cs, n_seq=n_seq, carry=carry, n_steps=n_steps)
    if carry:
        o_shape, o_spec = (batch, seq, W_B), pl.BlockSpec((n_seq, tb, W_B), lambda b, t: (b, t, 0))
    else:
        o_shape, o_spec = (batch * seq, W_B), pl.BlockSpec((tb, W_B), lambda b, t: (b, 0))
    o, s_new = pl.pallas_call(
        kern,
        out_shape=(jax.ShapeDtypeStruct(o_shape, F32),
                   jax.ShapeDtypeStruct((batch, N_HEADS, DK_B, DV_B), F32)),
        grid=grid,
        in_specs=[
            *pcol(2 * N_HEADS * DK_B, OFF_QK_B), *pcol(W_B, OFF_V_B), *pcol(W_B, OFF_G_B), *pcol(LANES, OFF_SMALL),
            pl.BlockSpec((LANES, N_HEADS * DK_B), lambda b, t: (0, 0)),
            pl.BlockSpec((1, N_HEADS * DK_B), lambda b, t: (0, 0)),
            pl.BlockSpec((1, DV_B), lambda b, t: (0, 0)),
            pl.BlockSpec((n_state, N_HEADS, DK_B, DV_B), lambda b, t: (b, 0, 0, 0)),
        ],
        out_specs=(
            o_spec,
            pl.BlockSpec((n_state, N_HEADS, DK_B, DV_B), lambda b, t: (b, 0, 0, 0)),
        ),
        scratch_shapes=[pltpu.VMEM((n_seq, N_HEADS, DK_B, DV_B), F32)],
        compiler_params=pltpu.CompilerParams(
            dimension_semantics=("arbitrary", "arbitrary"), vmem_limit_bytes=VMEM_LIMIT),
        name=f"gla_c{cs}",
    )(*([p] * (4 * n_seq)), wgk, bgk.reshape(1, N_HEADS * DK_B), norm_w.reshape(1, DV_B), s0)
    return o.reshape(batch * seq, W_B), s_new


def _topk_softmax(logits):
    tm = logits.shape[0]
    lane = lax.broadcasted_iota(jnp.int32, (tm, LANES), 1)
    lane_f = lane.astype(F32)
    neg = jnp.float32(-jnp.inf)
    cur = jnp.where(lane < N_EXPERTS, logits, neg)
    vals, idxs = [], []
    for _ in range(TOP_K):
        m = jnp.max(cur, axis=-1, keepdims=True)
        idx = jnp.min(jnp.where(cur == m, lane_f, float(LANES)), axis=-1, keepdims=True)
        vals.append(m)
        idxs.append(idx)
        cur = jnp.where(lane_f == idx, neg, cur)
    exps = [jnp.exp(v - vals[0]) for v in vals]
    den = exps[0] + exps[1] + exps[2] + exps[3]
    te = jnp.zeros((tm, LANES), F32)
    tg = jnp.zeros((tm, LANES), F32)
    for k in range(TOP_K):
        te = jnp.where(lane == k, idxs[k], te)
        tg = jnp.where(lane == k, exps[k] / den, tg)
    return te.astype(jnp.int32), tg


def _merge_kernel(xp_ref, xs_ref, oap_ref, oas_ref, obp_ref, obs_ref, ga_ref, gb_ref, wua_ref, wub_ref, wo_ref,
                  n2_ref, wr_ref, br_ref, x1_ref, h2_ref, te_ref, tg_ref, *, n_first):
    i = pl.program_id(0)
    tm = x1_ref.shape[0]
    hm = tm // MERGE_SPLIT
    groups = [slice(a * hm, (a + 1) * hm) for a in range(MERGE_SPLIT)]
    oa = _pick(i, n_first, oap_ref, oas_ref).astype(BF16)
    ob = _pick(i, n_first, obp_ref, obs_ref).astype(BF16)
    x = _pick(i, n_first, xp_ref, xs_ref)
    ua = [jnp.dot(oa[g], wua_ref[...], preferred_element_type=F32) for g in groups]
    ub = [jnp.dot(ob[g], wub_ref[...], preferred_element_type=F32) for g in groups]
    merged = [jax.nn.sigmoid(ga_ref[g, :]) * ua[a] + jax.nn.sigmoid(gb_ref[g, :]) * ub[a]
              for a, g in enumerate(groups)]
    x1 = [x[g] + jnp.dot(merged[a].astype(BF16), wo_ref[...], preferred_element_type=F32)
          for a, g in enumerate(groups)]
    h2 = [v * lax.rsqrt(jnp.mean(v * v, axis=-1, keepdims=True) + RMS_EPS) * n2_ref[...] for v in x1]
    h_hi = [v.astype(BF16) for v in h2]
    h_lo = [(v - hi.astype(F32)).astype(BF16) for v, hi in zip(h2, h_hi)]
    l_hi = [jnp.dot(hi, wr_ref[...], preferred_element_type=F32) for hi in h_hi]
    l_lo = [jnp.dot(lo, wr_ref[:, :LANES], preferred_element_type=F32) for lo in h_lo]
    for a, g in enumerate(groups):
        x1_ref[g, :] = x1[a]
        _store_packed_rows(h2_ref, (), a * hm, _pack_bf16_pairs(h2[a]))
        logits = l_hi[a][:, :LANES] + l_hi[a][:, LANES:] + l_lo[a] + br_ref[...]
        te, tg = _topk_softmax(logits)
        te_ref[g, :] = te
        tg_ref[g, :] = tg


def _merge(x_p, x_s, oa_p, oa_s, ob_p, ob_s, p, wua, wub, wo, norm2_w, wr, br):
    n = x_p.shape[0] + x_s.shape[0]
    tm = MERGE_TM
    n_first = x_p.shape[0] // tm
    row = lambda i: (i, 0)
    const = lambda i: (0, 0)
    return pl.pallas_call(
        functools.partial(_merge_kernel, n_first=n_first),
        out_shape=(jax.ShapeDtypeStruct((n, D_MODEL), F32), jax.ShapeDtypeStruct((n * PACK_ROWS, LANES), jnp.uint32),
                   jax.ShapeDtypeStruct((n, LANES), jnp.int32), jax.ShapeDtypeStruct((n, LANES), F32)),
        grid=(n // tm,),
        in_specs=[
            *_split_specs((tm, D_MODEL), n_first),
            *_split_specs((tm, W_A), n_first),
            *_split_specs((tm, W_B), n_first),
            pl.BlockSpec((tm, D_MODEL), lambda i: (i, OFF_GATE_A // D_MODEL)),
            pl.BlockSpec((tm, D_MODEL), lambda i: (i, OFF_GATE_B // D_MODEL)),
            pl.BlockSpec((W_A, D_MODEL), const),
            pl.BlockSpec((W_B, D_MODEL), const),
            pl.BlockSpec((D_MODEL, D_MODEL), const),
            pl.BlockSpec((1, D_MODEL), const),
            pl.BlockSpec((D_MODEL, 2 * LANES), const),
            pl.BlockSpec((1, LANES), const),
        ],
        out_specs=(pl.BlockSpec((tm, D_MODEL), row), pl.BlockSpec((tm * PACK_ROWS, LANES), row),
                   pl.BlockSpec((tm, LANES), row), pl.BlockSpec((tm, LANES), row)),
        compiler_params=pltpu.CompilerParams(
            dimension_semantics=("arbitrary",), vmem_limit_bytes=VMEM_LIMIT),
        name="merge_router",
    )(x_p, x_s, oa_p, oa_s, ob_p, ob_s, p, p, wua, wub, wo, norm2_w.reshape(1, D_MODEL), wr, br)


def _moe_kernel(blk_e_ref, nxt_e_ref, n_sub_ref, n_used_ref, src_ref, dst_ref, h2_ref, wgu_hbm, bgu_ref,
                wdn_hbm, bdn_ref, y_hbm, xstage, ybuf, wgu_stage, wdn_stage, wgu_bf, wdn_bf, ssem, wsem):
    i = pl.program_id(0)
    n_used = n_used_ref[0]
    bm = MOE_BM
    pr = PACK_ROWS
    e_cur = blk_e_ref[i]

    def weight_copies(e):
        return (pltpu.make_async_copy(wgu_hbm.at[e], wgu_stage, wsem.at[0]),
                pltpu.make_async_copy(wdn_hbm.at[e], wdn_stage, wsem.at[1]))

    @pl.when((i == 0) & (n_used > 0))
    def _():
        for cp in weight_copies(e_cur):
            cp.start()

    @pl.when((i < n_used) & ((i == 0) | (e_cur != blk_e_ref[jnp.maximum(i - 1, 0)])))
    def _():
        for cp in weight_copies(e_cur):
            cp.wait()
        wgu_bf[...] = wgu_stage[...].astype(BF16)
        wdn_bf[...] = wdn_stage[...].astype(BF16)

        @pl.when(nxt_e_ref[i] != e_cur)
        def _():
            for cp in weight_copies(nxt_e_ref[i]):
                cp.start()

    def scatter_copy(row, s, r):
        return pltpu.make_async_copy(ybuf.at[s, pl.ds(r * pr, pr), :],
                                     y_hbm.at[pl.ds(pl.multiple_of(row * pr, pr), pr), :], ssem.at[s])

    @pl.when(i == 0)
    def _():
        ybuf[...] = jnp.zeros_like(ybuf)
        n_real = y_hbm.shape[0] - MOE_SUB * bm * pr
        for s in range(MOE_SUB):
            sink = pltpu.make_async_copy(ybuf.at[s], y_hbm.at[pl.ds(n_real + s * bm * pr, bm * pr), :], ssem.at[s])
            sink.start()
            sink.wait()

    def sub_block(s):
        @pl.when((i >= 1) & (n_sub_ref[jnp.maximum(i - 1, 0)] > s))
        def _():
            for r in range(bm):
                scatter_copy(0, s, r).wait()

        @pl.when(n_sub_ref[i] > s)
        def _():
            for r in range(bm):
                tok = pl.multiple_of(src_ref[0, 0, s * bm + r] * pr, pr)
                xstage[s, pl.ds(r * pr, pr), :] = h2_ref[pl.ds(tok, pr), :]
            x = _load_packed_rows(xstage, (s,), bm, BF16)
            gu = jnp.dot(x, wgu_bf[...], preferred_element_type=F32) + bgu_ref[...]
            gate = jnp.minimum(gu[:, :D_FF], SWIGLU_LIMIT)
            up = jnp.clip(gu[:, D_FF:], -SWIGLU_LIMIT, SWIGLU_LIMIT)
            hmid = (up + 1.0) * (gate * jax.nn.sigmoid(SWIGLU_ALPHA * gate))
            y = jnp.dot(hmid.astype(BF16), wdn_bf[...], preferred_element_type=F32) + bdn_ref[...]
            _store_packed_rows(ybuf, (s,), 0, _pack_bf16_pairs(y))
            for r in range(bm):
                scatter_copy(dst_ref[0, 0, s * bm + r], s, r).start(priority=r % 2)

    @pl.when(i < n_used)
    def _():
        for s in range(MOE_SUB):
            sub_block(s)

    @pl.when(i == n_used - 1)
    def _():
        for s in range(MOE_SUB):
            @pl.when(n_sub_ref[i] > s)
            def _():
                for r in range(bm):
                    scatter_copy(0, s, r).wait()


def _moe(blk_e, nxt_e, n_sub, n_used, src_tok, dst_row, h2, wgu, bgu, wdn, bdn, y_rows):
    n_steps = src_tok.shape[0]
    step_rows = MOE_SUB * MOE_BM
    cur = lambda i, be, ne, ns, nu: (i, 0, 0)
    wsel = lambda i, be, ne, ns, nu: (be[i], 0, 0)
    smem_blk = lambda im: pl.BlockSpec((1, 1, step_rows), im, memory_space=pltpu.SMEM)
    grid_spec = pltpu.PrefetchScalarGridSpec(
        num_scalar_prefetch=4,
        grid=(n_steps,),
        in_specs=[
            smem_blk(cur), smem_blk(cur),
            pl.BlockSpec(h2.shape, lambda i, be, ne, ns, nu: (0, 0), pipeline_mode=pl.Buffered(1)),
            pl.BlockSpec(memory_space=pl.ANY),
            pl.BlockSpec((None, 1, 2 * D_FF), wsel),
            pl.BlockSpec(memory_space=pl.ANY),
            pl.BlockSpec((None, 1, D_MODEL), wsel),
        ],
        out_specs=pl.BlockSpec(memory_space=pl.ANY),
        scratch_shapes=[pltpu.VMEM((MOE_SUB, MOE_BM * PACK_ROWS, LANES), jnp.uint32),
                        pltpu.VMEM((MOE_SUB, MOE_BM * PACK_ROWS, LANES), jnp.uint32),
                        pltpu.VMEM((D_MODEL, 2 * D_FF), F32), pltpu.VMEM((D_FF, D_MODEL), F32),
                        pltpu.VMEM((D_MODEL, 2 * D_FF), BF16), pltpu.VMEM((D_FF, D_MODEL), BF16),
                        pltpu.SemaphoreType.DMA((MOE_SUB,)), pltpu.SemaphoreType.DMA((2,))],
    )
    return pl.pallas_call(
        _moe_kernel,
        out_shape=jax.ShapeDtypeStruct((y_rows * PACK_ROWS, LANES), jnp.uint32),
        grid_spec=grid_spec,
        compiler_params=pltpu.CompilerParams(
            dimension_semantics=("arbitrary",), vmem_limit_bytes=MOE_VMEM_LIMIT),
        name="moe_experts",
    )(blk_e, nxt_e, n_sub, n_used, src_tok, dst_row, h2, wgu, bgu, wdn, bdn)


def _route(top_e, n_tok):
    bm = MOE_SUB * MOE_BM
    n_assign = n_tok * TOP_K
    n_blocks = n_assign // bm + N_EXPERTS
    n_dummy = N_EXPERTS * bm
    flat_e = top_e.reshape(-1)
    counts = jnp.sum((flat_e[:, None] == jnp.arange(N_EXPERTS, dtype=jnp.int32)[None, :]).astype(jnp.int32), axis=0)
    padded = (counts + bm - 1) // bm * bm
    pad_end = jnp.cumsum(padded)
    d_e = jnp.arange(n_dummy, dtype=jnp.int32) // bm
    d_active = (jnp.arange(n_dummy, dtype=jnp.int32) % bm) < (padded - counts)[d_e]
    keys = jnp.concatenate([flat_e * 2, jnp.where(d_active, d_e * 2 + 1, 2 * N_EXPERTS)])
    vbits = n_assign.bit_length()
    marker = (1 << vbits) - 1
    vals = jnp.concatenate([jnp.arange(n_assign, dtype=jnp.int32), jnp.full((n_dummy,), marker, jnp.int32)])
    assign = jnp.sort(keys * (1 << vbits) + vals)[:n_blocks * bm] & marker
    rows = jnp.arange(n_blocks * bm, dtype=jnp.int32)
    valid = assign != marker
    src_tok = jnp.where(valid, assign // TOP_K, 0)
    dst_row = jnp.where(valid, (assign % TOP_K) * n_tok + assign // TOP_K, n_assign + rows % bm)
    blk_start = jnp.arange(n_blocks, dtype=jnp.int32) * bm
    blk_e = jnp.minimum(jnp.sum((pad_end[None, :] <= blk_start[:, None]).astype(jnp.int32), axis=1),
                        N_EXPERTS - 1).astype(jnp.int32)
    n_used = (pad_end[-1] // bm).astype(jnp.int32).reshape(1)
    later = (blk_e[None, :] > blk_e[:, None]) & (blk_start[None, :] < pad_end[-1])
    nxt_e = jnp.min(jnp.where(later, blk_e[None, :], N_EXPERTS), axis=1)
    nxt_e = jnp.where(nxt_e == N_EXPERTS, blk_e, nxt_e).astype(jnp.int32)
    n_sub = jnp.sum(valid.reshape(n_blocks, MOE_SUB, MOE_BM)[:, :, 0].astype(jnp.int32), axis=1)
    return (blk_e, nxt_e, n_sub, n_used, src_tok.reshape(n_blocks, 1, bm).astype(jnp.int32),
            dst_row.reshape(n_blocks, 1, bm).astype(jnp.int32))


def _combine_kernel(x1_ref, y0_ref, y1_ref, y2_ref, y3_ref, tg_ref, nf_ref, op_ref, os_ref, *, n_first):
    tg = tg_ref[...]
    tm = x1_ref.shape[0]
    rows = lambda y_ref: _load_packed_rows(y_ref, (), tm, F32)
    acc = rows(y0_ref) * tg[:, 0:1]
    for k, y_ref in enumerate((y1_ref, y2_ref, y3_ref), start=1):
        acc = acc + rows(y_ref) * tg[:, k:k + 1]
    x = x1_ref[...] + acc
    out = x * lax.rsqrt(jnp.mean(x * x, axis=-1, keepdims=True) + RMS_EPS) * nf_ref[...]
    i = pl.program_id(0)

    @pl.when(i < n_first)
    def _():
        op_ref[...] = out

    @pl.when(i >= n_first)
    def _():
        os_ref[...] = out


def _combine(x1, y, tg, norm_f_w, n_p):
    n = x1.shape[0]
    tm = COMBINE_TM
    n_first = n_p // tm
    row = lambda i: (i, 0)
    y_spec = lambda k: pl.BlockSpec((tm * PACK_ROWS, LANES), lambda i: (k * (n // tm) + i, 0))
    return pl.pallas_call(
        functools.partial(_combine_kernel, n_first=n_first),
        out_shape=(jax.ShapeDtypeStruct((n_p, D_MODEL), F32), jax.ShapeDtypeStruct((n - n_p, D_MODEL), F32)),
        grid=(n // tm,),
        in_specs=[pl.BlockSpec((tm, D_MODEL), row), y_spec(0), y_spec(1), y_spec(2), y_spec(3),
                  pl.BlockSpec((tm, LANES), row), pl.BlockSpec((1, D_MODEL), lambda i: (0, 0))],
        out_specs=_split_specs((tm, D_MODEL), n_first),
        compiler_params=pltpu.CompilerParams(
            dimension_semantics=("arbitrary",), vmem_limit_bytes=VMEM_LIMIT),
        name="combine_norm",
    )(x1, y, y, y, y, tg, norm_f_w.reshape(1, D_MODEL))


def _pack_w_in(w_in):
    s = np.cumsum((0,) + IN_SIZES)
    w_in = w_in.astype(BF16)
    qkv, z, al, be, qb, kb, vb, gb, gk, ga, gbt = [w_in[:, s[i]:s[i + 1]] for i in range(len(IN_SIZES))]
    small = jnp.concatenate(
        [al, be, gk, jnp.zeros((D_MODEL, LANES - 2 * N_HEADS - GK_RANK), w_in.dtype)], axis=1)
    return jnp.concatenate([qkv, z, ga, gbt, qb, kb, vb, gb, small], axis=1)


def kernel(x_prompt, x_sample, state_gdn, state_gdn_conv, state_gla, norm1_w, w_in, conv_w, a_log,
           dt_bias, gdn_norm_w, gla_gk_up, gla_gk_b, gla_norm_w, w_up_a, w_up_b, w_o, norm2_w,
           w_router, b_router, w_gate_up, b_gate_up, w_down, b_down, norm_f_w):
    bp, tp, _ = x_prompt.shape
    bs, ts, _ = x_sample.shape
    n_p, n_s = bp * tp, bs * ts
    n = n_p + n_s
    l = 0

    x_p = x_prompt.reshape(n_p, D_MODEL)
    x_s = x_sample.reshape(n_s, D_MODEL)
    p = _inproj(x_p, x_s, norm1_w[l], _pack_w_in(w_in[l]))

    hp = jnp.zeros((8, LANES), F32).at[0, :N_HEADS].set(a_log[l]).at[1, :N_HEADS].set(dt_bias[l])
    wgk = jnp.zeros((LANES, N_HEADS * DK_B), F32).at[2 * N_HEADS:2 * N_HEADS + GK_RANK].set(gla_gk_up[l]).astype(BF16)
    zeros = lambda shape: jnp.zeros(shape, F32)

    oa_p, gdn_p, conv_p = _gdn(p, 0, bp, tp, conv_w[l], hp, gdn_norm_w[l],
                               zeros((bp, CONV_W - 1, CONV_DIM)), zeros((bp, N_HEADS, DK_A, DV_A)))
    oa_s, gdn_s, conv_s = _gdn(p, n_p, bs, ts, conv_w[l], hp, gdn_norm_w[l], state_gdn_conv[l], state_gdn[l])
    ob_p, gla_p = _gla(p, 0, bp, tp, wgk, gla_gk_b[l], gla_norm_w[l], zeros((bp, N_HEADS, DK_B, DV_B)))
    ob_s, gla_s = _gla(p, n_p, bs, ts, wgk, gla_gk_b[l], gla_norm_w[l], state_gla[l])

    wr = jnp.zeros((D_MODEL, LANES), F32).at[:, :N_EXPERTS].set(w_router[l])
    wr_hi = wr.astype(BF16)
    wr = jnp.concatenate([wr_hi, (wr - wr_hi.astype(F32)).astype(BF16)], axis=1)
    br = jnp.zeros((1, LANES), F32).at[0, :N_EXPERTS].set(b_router[l])
    x1, h2, te, tg = _merge(x_p, x_s, oa_p, oa_s, ob_p, ob_s, p, w_up_a[l].astype(BF16),
                            w_up_b[l].astype(BF16), w_o[l].astype(BF16), norm2_w[l], wr, br)

    blk_e, nxt_e, n_sub, n_used, src_tok, dst_row = _route(te[:, :TOP_K], n)
    y = _moe(blk_e, nxt_e, n_sub, n_used, src_tok, dst_row, h2, w_gate_up[l],
             b_gate_up[l].reshape(N_EXPERTS, 1, 2 * D_FF), w_down[l],
             b_down[l].reshape(N_EXPERTS, 1, D_MODEL), n * TOP_K + MOE_SUB * MOE_BM)
    out_p, out_s = _combine(x1, y, tg, norm_f_w, n_p)
    y_prompt = out_p.reshape(bp, tp, D_MODEL)
    y_sample = out_s.reshape(bs, ts, D_MODEL)
    return (y_prompt, y_sample, gdn_p[None], conv_p[None], gla_p[None], gdn_s[None], conv_s[None],
            gla_s[None])
```

```python
import functools
import math

import jax
import jax.numpy as jnp
import numpy as np
from jax import lax
from jax.experimental import pallas as pl
from jax.experimental.pallas import tpu as pltpu

F32 = jnp.float32
BF16 = jnp.bfloat16

D_MODEL = 1024
N_HEADS = 4
DK_A = 128
DV_A = 128
W_A = N_HEADS * DV_A
CONV_W = 4
CONV_DIM = 2 * N_HEADS * DK_A + W_A
GDN_CHUNK = 64
DK_B = 64
DV_B = 128
W_B = N_HEADS * DV_B
GK_RANK = 16
GLA_GATE_NORM = 16.0
GLA_CHUNK = 16
N_EXPERTS = 32
TOP_K = 4
D_FF = 1024
SWIGLU_LIMIT = 7.0
SWIGLU_ALPHA = 1.702
RMS_EPS = 1e-6
IN_SIZES = (CONV_DIM, W_A, N_HEADS, N_HEADS, N_HEADS * DK_B, N_HEADS * DK_B, W_B, W_B, GK_RANK,
            D_MODEL, D_MODEL)

LANES = 128

OFF_QKV = 0
OFF_Z = 1536
OFF_GATE_A = 2048
OFF_GATE_B = 3072
OFF_QK_B = 4096
OFF_V_B = 4608
OFF_G_B = 5120
OFF_SMALL = 5632
P_COLS = 5760

INPROJ_TM = 512
GLA_TB = 256
GLA_SEQS_CARRY = 4
GDN_SEGS_CARRY = 4
GDN_SEQS_CARRY = 2
GDN_SEGS_INDEP = 16
MERGE_TM = 512
MERGE_SPLIT = 2
MOE_BM = 256
MOE_SUB = 2
PACK_ROWS = D_MODEL // 2 // LANES
MOE_VMEM_LIMIT = 62 * 1024 * 1024
COMBINE_TM = 512
VMEM_LIMIT = 56 * 1024 * 1024


def _dot(a, b):
    return jnp.dot(a.astype(BF16), b.astype(BF16), preferred_element_type=F32)


def _dot_nt(a, b):
    return lax.dot_general(a.astype(BF16), b.astype(BF16), (((1,), (1,)), ((), ())),
                           preferred_element_type=F32)


def _dot_tn(a, b):
    return lax.dot_general(a.astype(BF16), b.astype(BF16), (((0,), (0,)), ((), ())), preferred_element_type=F32)


def _split3(x):
    hi = x.astype(BF16)
    r1 = x - hi.astype(F32)
    mid = r1.astype(BF16)
    lo = (r1 - mid.astype(F32)).astype(BF16)
    return hi, mid, lo


def _dot01(m01, x):
    hi, mid, lo = _split3(x)
    d = lambda p: jnp.dot(m01, p, preferred_element_type=F32)
    return d(hi) + d(mid) + d(lo)


def _silu(x):
    return x * jax.nn.sigmoid(x)


def _pack_bf16_pairs(x):
    bits = pltpu.bitcast(x.astype(BF16).astype(F32), jnp.uint32)
    half = D_MODEL // 2
    return (bits[:, :half] >> 16) | (bits[:, half:] & jnp.uint32(0xFFFF0000))


def _store_packed_rows(ref, lead, row0, words):
    n = words.shape[0]
    for c in range(PACK_ROWS):
        ref[(*lead, pl.ds(row0 * PACK_ROWS + c, n, stride=PACK_ROWS), slice(None))] = words[:, c * LANES:(c + 1) * LANES]


def _load_packed_rows(ref, lead, n, dtype):
    words = [ref[(*lead, pl.ds(c, n, stride=PACK_ROWS), slice(None))] for c in range(PACK_ROWS)]
    lo = [pltpu.bitcast(w << 16, F32).astype(dtype) for w in words]
    hi = [pltpu.bitcast(w & jnp.uint32(0xFFFF0000), F32).astype(dtype) for w in words]
    return jnp.concatenate(lo + hi, axis=1)


def _pick(i, n_first, first_ref, second_ref):
    return jnp.where(i < n_first, first_ref[...], second_ref[...])


def _split_specs(block, n_first):
    return (pl.BlockSpec(block, lambda i: (jnp.minimum(i, n_first - 1), 0)),
            pl.BlockSpec(block, lambda i: (jnp.maximum(i - n_first, 0), 0)))


def _inproj_kernel(xp_ref, xs_ref, nw_ref, w_ref, o_ref, *, n_first):
    x = _pick(pl.program_id(0), n_first, xp_ref, xs_ref)
    ms = jnp.mean(x * x, axis=-1, keepdims=True)
    h = (x * lax.rsqrt(ms + RMS_EPS) * nw_ref[...]).astype(BF16)
    o_ref[...] = jnp.dot(h, w_ref[...], preferred_element_type=F32)


def _inproj(x_p, x_s, norm_w, w_packed):
    tm = INPROJ_TM
    n_first = x_p.shape[0] // tm
    n = x_p.shape[0] + x_s.shape[0]
    return pl.pallas_call(
        functools.partial(_inproj_kernel, n_first=n_first),
        out_shape=jax.ShapeDtypeStruct((n, P_COLS), F32),
        grid=(n // tm,),
        in_specs=[
            *_split_specs((tm, D_MODEL), n_first),
            pl.BlockSpec((1, D_MODEL), lambda i: (0, 0)),
            pl.BlockSpec((D_MODEL, P_COLS), lambda i: (0, 0), pipeline_mode=pl.Buffered(1)),
        ],
        out_specs=pl.BlockSpec((tm, P_COLS), lambda i: (i, 0)),
        compiler_params=pltpu.CompilerParams(
            dimension_semantics=("arbitrary",), vmem_limit_bytes=VMEM_LIMIT),
        name="inproj",
    )(x_p, x_s, norm_w.reshape(1, D_MODEL), w_packed)


def _gdn_kernel(*refs, chunk, n_seg, n_seq, carry, n_steps):
    qkv_refs, z_refs, sm_refs = refs[:n_seq], refs[n_seq:2 * n_seq], refs[2 * n_seq:3 * n_seq]
    cw_ref, hp_ref, nw_ref, sconv_ref, s0_ref, o_ref, sout_ref, cout_ref, s_ref, ubuf_ref = refs[3 * n_seq:]
    c_len = chunk
    n_seg_all = n_seq * n_seg
    rows = n_seg_all * c_len
    t = pl.program_id(1)
    hist = CONV_W - 1
    base = 8
    conv_len = n_seg * c_len if carry else c_len
    n_conv = n_seq if carry else n_seg

    def init_history():
        ubuf_ref[:, 0:base, :] = jnp.zeros((n_conv, base, CONV_DIM), F32)
        ubuf_ref[:, base - hist:base, :] = sconv_ref[...]

    if carry:
        @pl.when(t == 0)
        def _():
            s_ref[...] = s0_ref[...]
            init_history()

        @pl.when(t > 0)
        def _():
            ubuf_ref[:, base - hist:base, :] = ubuf_ref[:, base + conv_len - hist:base + conv_len, :]
        for q in range(n_seq):
            ubuf_ref[q, base:base + conv_len, :] = qkv_refs[q][...]
    else:
        init_history()
        ubuf_ref[:, base:base + conv_len, :] = qkv_refs[0][...].reshape(n_conv, conv_len, CONV_DIM)

    cw = cw_ref[...]
    u = ubuf_ref[...]
    u1 = pltpu.roll(u, 1, axis=1)
    v = u1 * cw[0:1, :] + u * cw[1:2, :]
    y = (pltpu.roll(v, 2, axis=1) + u1 * cw[2:3, :]) + u * cw[3:4, :]
    y = _silu(y[:, base:, :]).reshape(rows, CONV_DIM)

    sm = sm_refs[0][...] if n_seq == 1 else jnp.concatenate([r[...] for r in sm_refs], axis=0)
    hp = hp_ref[...]
    lane = lax.broadcasted_iota(jnp.int32, (rows, LANES), 1)
    g_all = -jnp.exp(hp[0:1, :]) * jax.nn.softplus(sm + hp[1:2, :])
    g_all = jnp.where(lane < N_HEADS, g_all, 0.0)
    beta_all = jax.nn.sigmoid(sm)

    sh = int(round(math.log2(c_len)))
    r_b = lax.broadcasted_iota(jnp.int32, (rows, rows), 0)
    c_b = lax.broadcasted_iota(jnp.int32, (rows, rows), 1)
    seg_tril = ((r_b >> sh) == (c_b >> sh)) & (c_b <= r_b)
    gc = _dot01(jnp.where(seg_tril, 1.0, 0.0).astype(BF16), g_all)

    r_i = lax.broadcasted_iota(jnp.int32, (c_len, c_len), 0)
    c_i = lax.broadcasted_iota(jnp.int32, (c_len, c_len), 1)
    tri = c_i <= r_i
    strict = c_i < r_i
    eye = c_i == r_i
    nw = nw_ref[...]
    levels = sh

    qn, kn, kb, rhs, qg, gcol = [], [], [], [], [], []
    for h in range(N_HEADS):
        qh = y[:, h * DK_A:(h + 1) * DK_A]
        kh = y[:, W_A + h * DK_A:W_A + (h + 1) * DK_A]
        vh = y[:, 2 * W_A + h * DV_A:2 * W_A + (h + 1) * DV_A]
        qn_h = qh * lax.rsqrt(jnp.sum(qh * qh, axis=-1, keepdims=True) + 1e-6) * (DK_A ** -0.5)
        kn_h = kh * lax.rsqrt(jnp.sum(kh * kh, axis=-1, keepdims=True) + 1e-6)
        gcol_h = gc[:, h:h + 1]
        eg = jnp.exp(gcol_h)
        bcol = beta_all[:, N_HEADS + h:N_HEADS + h + 1]
        kb_h = kn_h * bcol
        qn.append(qn_h)
        kn.append(kn_h)
        kb.append(kb_h)
        rhs.append(jnp.concatenate([vh * bcol, kb_h * eg], axis=1))
        qg.append(qn_h * eg)
        gcol.append(gcol_h)

    pairs = [(g, h) for g in range(n_seg_all) for h in range(N_HEADS)]
    sl = lambda g: slice(g * c_len, (g + 1) * c_len)

    decay, kd, gl = {}, {}, {}
    for (g, h) in pairs:
        gcol_p = gcol[h][sl(g)]
        gcb = jnp.broadcast_to(gcol_p, (c_len, c_len))
        grow = jnp.sum(jnp.where(eye, gcb, 0.0), axis=0, keepdims=True)
        decay[g, h] = jnp.where(tri, jnp.exp(jnp.where(tri, gcb - grow, 0.0)), 0.0)
        glast = gcol[h][(g + 1) * c_len - 1:(g + 1) * c_len]
        kd[g, h] = kn[h][sl(g)] * jnp.exp(glast - gcol_p)
        gl[g, h] = jnp.exp(glast)

    kq = {p: _dot_nt(jnp.concatenate([kb[p[1]][sl(p[0])], qn[p[1]][sl(p[0])]], axis=0), kn[p[1]][sl(p[0])])
          for p in pairs}
    low = {p: jnp.where(strict, kq[p][:c_len] * decay[p], 0.0) for p in pairs}
    attn = {p: jnp.where(tri, kq[p][c_len:] * decay[p], 0.0) for p in pairs}
    ymat = {p: -low[p] for p in pairs}
    lp = low
    for _ in range(levels - 1):
        lp = {p: _dot(lp[p], lp[p]) for p in pairs}
        prod = {p: _dot(ymat[p], lp[p]) for p in pairs}
        ymat = {p: ymat[p] + lp[p] + prod[p] for p in pairs}
    sol = {}
    for (g, h) in pairs:
        rhs_p = rhs[h][sl(g)]
        sol[g, h] = rhs_p + _dot(ymat[g, h], rhs_p)

    def finish(g, h, o):
        r = o * lax.rsqrt(jnp.mean(o * o, axis=-1, keepdims=True) + RMS_EPS) * nw
        q, loc = (g // n_seg, sl(g % n_seg)) if carry else (0, sl(g))
        gated = r * _silu(z_refs[q][loc, h * DV_A:(h + 1) * DV_A])
        if carry:
            o_ref[q, loc, h * DV_A:(h + 1) * DV_A] = gated
        else:
            o_ref[loc, h * DV_A:(h + 1) * DV_A] = gated

    def advance(group, state):
        ws = {p: _dot(jnp.concatenate([sol[p][:, DV_A:], qg[p[1]][sl(p[0])]], axis=0), state[p]) for p in group}
        v_new = {p: sol[p][:, :DV_A] - ws[p][:c_len] for p in group}
        o = {p: ws[p][c_len:] + _dot(attn[p], v_new[p]) for p in group}
        new = {p: state[p] * gl[p] + _dot_tn(kd[p], v_new[p]) for p in group}
        for p in group:
            finish(p[0], p[1], o[p])
        return new

    if carry:
        chains = [(q, h) for q in range(n_seq) for h in range(N_HEADS)]
        cur = {c: s_ref[c[0], c[1]] for c in chains}
        for g in range(n_seg):
            group = [(q * n_seg + g, h) for (q, h) in chains]
            new = advance(group, {(q * n_seg + g, h): cur[q, h] for (q, h) in chains})
            cur = {(q, h): new[q * n_seg + g, h] for (q, h) in chains}
        for (q, h) in chains:
            s_ref[q, h] = cur[q, h]

        @pl.when(t == n_steps - 1)
        def _():
            for (q, h) in chains:
                sout_ref[q, h] = cur[q, h]
            cout_ref[...] = ubuf_ref[:, base + conv_len - hist:base + conv_len, :]
    else:
        new = advance(pairs, {p: s0_ref[p[0], p[1]] for p in pairs})
        for p in pairs:
            sout_ref[p[0], p[1]] = new[p]
        cout_ref[...] = ubuf_ref[:, base + conv_len - hist:base + conv_len, :]


def _gdn(p, row0, batch, seq, conv_w, hp, norm_w, s_conv, s0):
    chunk = math.gcd(seq, GDN_CHUNK)
    n_chunks = seq // chunk
    carry = n_chunks > 1
    if carry:
        n_seg = math.gcd(n_chunks, GDN_SEGS_CARRY)
        n_seq = math.gcd(batch, GDN_SEQS_CARRY)
        grid = (batch // n_seq, n_chunks // n_seg)
        n_conv, conv_len = n_seq, n_seg * chunk
    else:
        n_seg = math.gcd(batch, GDN_SEGS_INDEP)
        n_seq = 1
        grid = (batch // n_seg, 1)
        n_conv, conv_len = n_seg, chunk
    rows = n_seg * chunk
    n_steps = grid[1]
    blk0 = row0 // rows
    rowblk = lambda q: (lambda b, t: blk0 + (b * n_seq + q) * n_steps + t)
    pcol = lambda width, off: [pl.BlockSpec((rows, width), lambda b, t, rb=rowblk(q): (rb(b, t), off // width))
                               for q in range(n_seq)]
    kern = functools.partial(_gdn_kernel, chunk=chunk, n_seg=n_seg, n_seq=n_seq, carry=carry, n_steps=n_steps)
    if carry:
        o_shape, o_spec = (batch, seq, W_A), pl.BlockSpec((n_seq, rows, W_A), lambda b, t: (b, t, 0))
    else:
        o_shape, o_spec = (batch * seq, W_A), pl.BlockSpec((rows, W_A), lambda b, t: (b, 0))
    o, s_new, c_new = pl.pallas_call(
        kern,
        out_shape=(jax.ShapeDtypeStruct(o_shape, F32),
                   jax.ShapeDtypeStruct((batch, N_HEADS, DK_A, DV_A), F32),
                   jax.ShapeDtypeStruct((batch, CONV_W - 1, CONV_DIM), F32)),
        grid=grid,
        in_specs=[
            *pcol(CONV_DIM, OFF_QKV), *pcol(W_A, OFF_Z), *pcol(LANES, OFF_SMALL),
            pl.BlockSpec((CONV_W, CONV_DIM), lambda b, t: (0, 0)),
            pl.BlockSpec((8, LANES), lambda b, t: (0, 0)),
            pl.BlockSpec((1, DV_A), lambda b, t: (0, 0)),
            pl.BlockSpec((n_conv, CONV_W - 1, CONV_DIM), lambda b, t: (b, 0, 0)),
            pl.BlockSpec((n_conv, N_HEADS, DK_A, DV_A), lambda b, t: (b, 0, 0, 0)),
        ],
        out_specs=(
            o_spec,
            pl.BlockSpec((n_conv, N_HEADS, DK_A, DV_A), lambda b, t: (b, 0, 0, 0)),
            pl.BlockSpec((n_conv, CONV_W - 1, CONV_DIM), lambda b, t: (b, 0, 0)),
        ),
        scratch_shapes=[pltpu.VMEM((n_seq, N_HEADS, DK_A, DV_A), F32),
                        pltpu.VMEM((n_conv, conv_len + 8, CONV_DIM), F32)],
        compiler_params=pltpu.CompilerParams(
            dimension_semantics=("arbitrary", "arbitrary"), vmem_limit_bytes=VMEM_LIMIT),
        name=f"gdn_c{chunk}",
    )(*([p] * (3 * n_seq)), conv_w, hp, norm_w.reshape(1, DV_A), s_conv, s0)
    return o.reshape(batch * seq, W_A), s_new, c_new


def _gla_kernel(*refs, tb, cs, n_seq, carry, n_steps):
    qk_refs, v_refs, g_refs, sm_refs = (refs[k * n_seq:(k + 1) * n_seq] for k in range(4))
    wgk_ref, bgk_ref, nw_ref, s0_ref, o_ref, sout_ref, s_ref = refs[4 * n_seq:]
    t = pl.program_id(1)
    n_sub = tb // cs
    w_kb = N_HEADS * DK_B
    seqs = range(n_seq)
    heads = range(N_HEADS)

    if carry:
        @pl.when(t == 0)
        def _():
            s_ref[...] = s0_ref[...]

    sh = int(round(math.log2(cs)))
    r_i = lax.broadcasted_iota(jnp.int32, (tb, tb), 0)
    c_i = lax.broadcasted_iota(jnp.int32, (tb, tb), 1)
    same = (r_i >> sh) == (c_i >> sh)
    mask = same & (c_i <= r_i)
    mask_bf = jnp.where(mask, 1.0, 0.0).astype(BF16)
    same_bf = jnp.where(same, 1.0, 0.0).astype(BF16)
    onehot = jnp.where((lax.broadcasted_iota(jnp.int32, (tb, n_sub), 0) >> sh)
                       == lax.broadcasted_iota(jnp.int32, (tb, n_sub), 1), 1.0, 0.0).astype(BF16)
    tn = lambda part: lax.dot_general(part, onehot, (((0,), (0,)), ((), ())), preferred_element_type=F32)

    log_a = [jax.nn.log_sigmoid(_dot(sm_refs[q][...], wgk_ref[...]) + bgk_ref[...]) / GLA_GATE_NORM for q in seqs]
    gcum = [_dot01(mask_bf, la) for la in log_a]
    gtot = [_dot01(same_bf, la) for la in log_a]
    glcol = []
    for la in log_a:
        hi, mid, lo = _split3(la)
        glcol.append(jnp.exp(tn(hi) + tn(mid) + tn(lo)))

    nw = nw_ref[...]
    sl = lambda j: slice(j * cs, (j + 1) * cs)
    qg, kg, kd, vh = {}, {}, {}, {}
    for q in seqs:
        qk = qk_refs[q][...]
        for h in heads:
            qh = qk[:, h * DK_B:(h + 1) * DK_B] * (DK_B ** -0.5)
            kh = qk[:, w_kb + h * DK_B:w_kb + (h + 1) * DK_B]
            gh = gcum[q][:, h * DK_B:(h + 1) * DK_B]
            gt = gtot[q][:, h * DK_B:(h + 1) * DK_B]
            qg[q, h] = qh * jnp.exp(gh)
            kg[q, h] = kh * jnp.exp(-gh)
            kd[q, h] = kh * jnp.exp(gt - gh)
            vh[q, h] = v_refs[q][:, h * DV_B:(h + 1) * DV_B]
    chains = [(q, h) for q in seqs for h in heads]
    attn = {c: jnp.where(mask, _dot_nt(qg[c], kg[c]), 0.0) for c in chains}
    o_intra = {c: _dot(attn[c], vh[c]) for c in chains}
    upd = {(j, c): _dot_tn(kd[c][sl(j)], vh[c][sl(j)]) for j in range(n_sub) for c in chains}
    gl = lambda j, c: glcol[c[0]][c[1] * DK_B:(c[1] + 1) * DK_B, j:j + 1]

    o_inter = {c: [] for c in chains}
    if carry:
        cur = {c: s_ref[c[0], c[1]] for c in chains}
        for j in range(n_sub):
            for c in chains:
                o_inter[c].append(_dot(qg[c][sl(j)], cur[c]))
                cur[c] = cur[c] * gl(j, c) + upd[j, c]
        for c in chains:
            s_ref[c[0], c[1]] = cur[c]

        @pl.when(t == n_steps - 1)
        def _():
            for c in chains:
                sout_ref[c[0], c[1]] = cur[c]
    else:
        for j in range(n_sub):
            for c in chains:
                s = s0_ref[j, c[1]]
                o_inter[c].append(_dot(qg[c][sl(j)], s))
                sout_ref[j, c[1]] = s * gl(j, c) + upd[j, c]

    for (q, h) in chains:
        o = o_intra[q, h] + jnp.concatenate(o_inter[q, h], axis=0)
        r = o * lax.rsqrt(jnp.mean(o * o, axis=-1, keepdims=True) + RMS_EPS) * nw
        gated = r * _silu(g_refs[q][:, h * DV_B:(h + 1) * DV_B])
        if carry:
            o_ref[q, :, h * DV_B:(h + 1) * DV_B] = gated
        else:
            o_ref[:, h * DV_B:(h + 1) * DV_B] = gated


def _gla(p, row0, batch, seq, wgk, bgk, norm_w, s0):
    cs = math.gcd(seq, GLA_CHUNK)
    carry = seq > cs
    if carry:
        tb = math.gcd(seq, GLA_TB)
        n_seq = math.gcd(batch, GLA_SEQS_CARRY)
        grid = (batch // n_seq, seq // tb)
        n_state = n_seq
    else:
        tb = math.gcd(batch * seq, GLA_TB)
        n_seq = 1
        n_state = tb // cs
        grid = (batch // n_state, 1)
    n_steps = grid[1]
    blk0 = row0 // tb
    rowblk = lambda q: (lambda b, t: blk0 + (b * n_seq + q) * n_steps + t)
    pcol = lambda width, off: [pl.BlockSpec((tb, width), lambda b, t, rb=rowblk(q): (rb(b, t), off // width))
                               for q in range(n_seq)]
    kern = functools.partial(_gla_kernel, tb=tb, cs=cs, n_seq=n_seq, carry=carry, n_steps=n_steps)
    if carry:
        o_shape, o_spec = (batch, seq, W_B), pl.BlockSpec((n_seq, tb, W_B), lambda b, t: (b, t, 0))
    else:
        o_shape, o_spec = (batch * seq, W_B), pl.BlockSpec((tb, W_B), lambda b, t: (b, 0))
    o, s_new = pl.pallas_call(
        kern,
        out_shape=(jax.ShapeDtypeStruct(o_shape, F32),
                   jax.ShapeDtypeStruct((batch, N_HEADS, DK_B, DV_B), F32)),
        grid=grid,
        in_specs=[
            *pcol(2 * N_HEADS * DK_B, OFF_QK_B), *pcol(W_B, OFF_V_B), *pcol(W_B, OFF_G_B), *pcol(LANES, OFF_SMALL),
            pl.BlockSpec((LANES, N_HEADS * DK_B), lambda b, t: (0, 0)),
            pl.BlockSpec((1, N_HEADS * DK_B), lambda b, t: (0, 0)),
            pl.BlockSpec((1, DV_B), lambda b, t: (0, 0)),
            pl.BlockSpec((n_state, N_HEADS, DK_B, DV_B), lambda b, t: (b, 0, 0, 0)),
        ],
        out_specs=(
            o_spec,
            pl.BlockSpec((n_state, N_HEADS, DK_B, DV_B), lambda b, t: (b, 0, 0, 0)),
        ),
        scratch_shapes=[pltpu.VMEM((n_seq, N_HEADS, DK_B, DV_B), F32)],
        compiler_params=pltpu.CompilerParams(
            dimension_semantics=("arbitrary", "arbitrary"), vmem_limit_bytes=VMEM_LIMIT),
        name=f"gla_c{cs}",
    )(*([p] * (4 * n_seq)), wgk, bgk.reshape(1, N_HEADS * DK_B), norm_w.reshape(1, DV_B), s0)
    return o.reshape(batch * seq, W_B), s_new


def _topk_softmax(logits):
    tm = logits.shape[0]
    lane = lax.broadcasted_iota(jnp.int32, (tm, LANES), 1)
    lane_f = lane.astype(F32)
    neg = jnp.float32(-jnp.inf)
    cur = jnp.where(lane < N_EXPERTS, logits, neg)
    vals, idxs = [], []
    for _ in range(TOP_K):
        m = jnp.max(cur, axis=-1, keepdims=True)
        idx = jnp.min(jnp.where(cur == m, lane_f, float(LANES)), axis=-1, keepdims=True)
        vals.append(m)
        idxs.append(idx)
        cur = jnp.where(lane_f == idx, neg, cur)
    exps = [jnp.exp(v - vals[0]) for v in vals]
    den = exps[0] + exps[1] + exps[2] + exps[3]
    te = jnp.zeros((tm, LANES), F32)
    tg = jnp.zeros((tm, LANES), F32)
    for k in range(TOP_K):
        te = jnp.where(lane == k, idxs[k], te)
        tg = jnp.where(lane == k, exps[k] / den, tg)
    return te.astype(jnp.int32), tg


def _merge_kernel(xp_ref, xs_ref, oap_ref, oas_ref, obp_ref, obs_ref, ga_ref, gb_ref, wua_ref, wub_ref, wo_ref,
                  n2_ref, wr_ref, br_ref, x1_ref, h2_ref, te_ref, tg_ref, *, n_first):
    i = pl.program_id(0)
    tm = x1_ref.shape[0]
    hm = tm // MERGE_SPLIT
    groups = [slice(a * hm, (a + 1) * hm) for a in range(MERGE_SPLIT)]
    oa = _pick(i, n_first, oap_ref, oas_ref).astype(BF16)
    ob = _pick(i, n_first, obp_ref, obs_ref).astype(BF16)
    x = _pick(i, n_first, xp_ref, xs_ref)
    ua = [jnp.dot(oa[g], wua_ref[...], preferred_element_type=F32) for g in groups]
    ub = [jnp.dot(ob[g], wub_ref[...], preferred_element_type=F32) for g in groups]
    merged = [jax.nn.sigmoid(ga_ref[g, :]) * ua[a] + jax.nn.sigmoid(gb_ref[g, :]) * ub[a]
              for a, g in enumerate(groups)]
    x1 = [x[g] + jnp.dot(merged[a].astype(BF16), wo_ref[...], preferred_element_type=F32)
          for a, g in enumerate(groups)]
    h2 = [v * lax.rsqrt(jnp.mean(v * v, axis=-1, keepdims=True) + RMS_EPS) * n2_ref[...] for v in x1]
    h_hi = [v.astype(BF16) for v in h2]
    h_lo = [(v - hi.astype(F32)).astype(BF16) for v, hi in zip(h2, h_hi)]
    l_hi = [jnp.dot(hi, wr_ref[...], preferred_element_type=F32) for hi in h_hi]
    l_lo = [jnp.dot(lo, wr_ref[:, :LANES], preferred_element_type=F32) for lo in h_lo]
    for a, g in enumerate(groups):
        x1_ref[g, :] = x1[a]
        _store_packed_rows(h2_ref, (), a * hm, _pack_bf16_pairs(h2[a]))
        logits = l_hi[a][:, :LANES] + l_hi[a][:, LANES:] + l_lo[a] + br_ref[...]
        te, tg = _topk_softmax(logits)
        te_ref[g, :] = te
        tg_ref[g, :] = tg


def _merge(x_p, x_s, oa_p, oa_s, ob_p, ob_s, p, wua, wub, wo, norm2_w, wr, br):
    n = x_p.shape[0] + x_s.shape[0]
    tm = MERGE_TM
    n_first = x_p.shape[0] // tm
    row = lambda i: (i, 0)
    const = lambda i: (0, 0)
    return pl.pallas_call(
        functools.partial(_merge_kernel, n_first=n_first),
        out_shape=(jax.ShapeDtypeStruct((n, D_MODEL), F32), jax.ShapeDtypeStruct((n * PACK_ROWS, LANES), jnp.uint32),
                   jax.ShapeDtypeStruct((n, LANES), jnp.int32), jax.ShapeDtypeStruct((n, LANES), F32)),
        grid=(n // tm,),
        in_specs=[
            *_split_specs((tm, D_MODEL), n_first),
            *_split_specs((tm, W_A), n_first),
            *_split_specs((tm, W_B), n_first),
            pl.BlockSpec((tm, D_MODEL), lambda i: (i, OFF_GATE_A // D_MODEL)),
            pl.BlockSpec((tm, D_MODEL), lambda i: (i, OFF_GATE_B // D_MODEL)),
            pl.BlockSpec((W_A, D_MODEL), const),
            pl.BlockSpec((W_B, D_MODEL), const),
            pl.BlockSpec((D_MODEL, D_MODEL), const),
            pl.BlockSpec((1, D_MODEL), const),
            pl.BlockSpec((D_MODEL, 2 * LANES), const),
            pl.BlockSpec((1, LANES), const),
        ],
        out_specs=(pl.BlockSpec((tm, D_MODEL), row), pl.BlockSpec((tm * PACK_ROWS, LANES), row),
                   pl.BlockSpec((tm, LANES), row), pl.BlockSpec((tm, LANES), row)),
        compiler_params=pltpu.CompilerParams(
            dimension_semantics=("arbitrary",), vmem_limit_bytes=VMEM_LIMIT),
        name="merge_router",
    )(x_p, x_s, oa_p, oa_s, ob_p, ob_s, p, p, wua, wub, wo, norm2_w.reshape(1, D_MODEL), wr, br)


def _moe_kernel(blk_e_ref, nxt_e_ref, n_sub_ref, n_used_ref, src_ref, src_nxt_ref, dst_ref, h2_ref, wgu_hbm, bgu_ref,
                wdn_hbm, bdn_ref, y_hbm, xstage, ybuf, wgu_stage, wdn_stage, wgu_bf, wdn_bf, ssem, wsem):
    i = pl.program_id(0)
    n_used = n_used_ref[0]
    bm = MOE_BM
    pr = PACK_ROWS
    e_cur = blk_e_ref[i]

    def weight_copies(e):
        return (pltpu.make_async_copy(wgu_hbm.at[e], wgu_stage, wsem.at[0]),
                pltpu.make_async_copy(wdn_hbm.at[e], wdn_stage, wsem.at[1]))

    @pl.when((i == 0) & (n_used > 0))
    def _():
        for cp in weight_copies(e_cur):
            cp.start()

    @pl.when((i < n_used) & ((i == 0) | (e_cur != blk_e_ref[jnp.maximum(i - 1, 0)])))
    def _():
        for cp in weight_copies(e_cur):
            cp.wait()
        wgu_bf[...] = wgu_stage[...].astype(BF16)
        wdn_bf[...] = wdn_stage[...].astype(BF16)

        @pl.when(nxt_e_ref[i] != e_cur)
        def _():
            for cp in weight_copies(nxt_e_ref[i]):
                cp.start(priority=1)

    def scatter_copy(row, s, r):
        return pltpu.make_async_copy(ybuf.at[s, pl.ds(r * pr, pr), :],
                                     y_hbm.at[pl.ds(pl.multiple_of(row * pr, pr), pr), :], ssem.at[s])

    @pl.when(i == 0)
    def _():
        ybuf[...] = jnp.zeros_like(ybuf)
        n_real = y_hbm.shape[0] - MOE_SUB * bm * pr
        for s in range(MOE_SUB):
            sink = pltpu.make_async_copy(ybuf.at[s], y_hbm.at[pl.ds(n_real + s * bm * pr, bm * pr), :], ssem.at[s])
            sink.start()
            sink.wait()

    def gather(idx_ref, sub, slot):
        for r in range(bm):
            tok = pl.multiple_of(idx_ref[0, 0, sub * bm + r] * pr, pr)
            xstage[slot, pl.ds(r * pr, pr), :] = h2_ref[pl.ds(tok, pr), :]

    def wait_prev_scatter(s):
        @pl.when((i >= 1) & (n_sub_ref[jnp.maximum(i - 1, 0)] > s))
        def _():
            for r in range(bm):
                scatter_copy(0, s, r).wait()

    def expert_mlp(s):
        x = _load_packed_rows(xstage, (s,), bm, BF16)
        gu = jnp.dot(x, wgu_bf[...], preferred_element_type=F32) + bgu_ref[...]
        gate = jnp.minimum(gu[:, :D_FF], SWIGLU_LIMIT)
        up = jnp.clip(gu[:, D_FF:], -SWIGLU_LIMIT, SWIGLU_LIMIT)
        hmid = (up + 1.0) * (gate * jax.nn.sigmoid(SWIGLU_ALPHA * gate))
        y = jnp.dot(hmid.astype(BF16), wdn_bf[...], preferred_element_type=F32) + bdn_ref[...]
        _store_packed_rows(ybuf, (s,), 0, _pack_bf16_pairs(y))
        for r in range(bm):
            scatter_copy(dst_ref[0, 0, s * bm + r], s, r).start()

    assert MOE_SUB == 2

    @pl.when((i == 0) & (n_used > 0))
    def _():
        gather(src_ref, 0, 0)

    @pl.when(i < n_used)
    def _():
        wait_prev_scatter(0)
        gather(src_ref, 1, 1)
        expert_mlp(0)
        wait_prev_scatter(1)

        @pl.when(n_sub_ref[i] > 1)
        def _():
            gather(src_nxt_ref, 0, 0)
            expert_mlp(1)

        @pl.when(n_sub_ref[i] <= 1)
        def _():
            gather(src_nxt_ref, 0, 0)

    @pl.when(i == n_used - 1)
    def _():
        for s in range(MOE_SUB):
            @pl.when(n_sub_ref[i] > s)
            def _():
                for r in range(bm):
                    scatter_copy(0, s, r).wait()


def _moe(blk_e, nxt_e, n_sub, n_used, src_tok, dst_row, h2, wgu, bgu, wdn, bdn, y_rows):
    n_steps = src_tok.shape[0]
    step_rows = MOE_SUB * MOE_BM
    cur = lambda i, be, ne, ns, nu: (i, 0, 0)
    wsel = lambda i, be, ne, ns, nu: (be[i], 0, 0)
    smem_blk = lambda im: pl.BlockSpec((1, 1, step_rows), im, memory_space=pltpu.SMEM)
    grid_spec = pltpu.PrefetchScalarGridSpec(
        num_scalar_prefetch=4,
        grid=(n_steps,),
        in_specs=[
            smem_blk(cur), smem_blk(lambda i, be, ne, ns, nu: (jnp.minimum(i + 1, n_steps - 1), 0, 0)), smem_blk(cur),
            pl.BlockSpec(h2.shape, lambda i, be, ne, ns, nu: (0, 0), pipeline_mode=pl.Buffered(1)),
            pl.BlockSpec(memory_space=pl.ANY),
            pl.BlockSpec((None, 1, 2 * D_FF), wsel),
            pl.BlockSpec(memory_space=pl.ANY),
            pl.BlockSpec((None, 1, D_MODEL), wsel),
        ],
        out_specs=pl.BlockSpec(memory_space=pl.ANY),
        scratch_shapes=[pltpu.VMEM((MOE_SUB, MOE_BM * PACK_ROWS, LANES), jnp.uint32),
                        pltpu.VMEM((MOE_SUB, MOE_BM * PACK_ROWS, LANES), jnp.uint32),
                        pltpu.VMEM((D_MODEL, 2 * D_FF), F32), pltpu.VMEM((D_FF, D_MODEL), F32),
                        pltpu.VMEM((D_MODEL, 2 * D_FF), BF16), pltpu.VMEM((D_FF, D_MODEL), BF16),
                        pltpu.SemaphoreType.DMA((MOE_SUB,)), pltpu.SemaphoreType.DMA((2,))],
    )
    return pl.pallas_call(
        _moe_kernel,
        out_shape=jax.ShapeDtypeStruct((y_rows * PACK_ROWS, LANES), jnp.uint32),
        grid_spec=grid_spec,
        compiler_params=pltpu.CompilerParams(
            dimension_semantics=("arbitrary",), vmem_limit_bytes=MOE_VMEM_LIMIT),
        name="moe_experts",
    )(blk_e, nxt_e, n_sub, n_used, src_tok, src_tok, dst_row, h2, wgu, bgu, wdn, bdn)


def _route(top_e, n_tok):
    bm = MOE_SUB * MOE_BM
    n_assign = n_tok * TOP_K
    n_blocks = n_assign // bm + N_EXPERTS
    n_dummy = N_EXPERTS * bm
    flat_e = top_e.reshape(-1)
    counts = jnp.sum((flat_e[:, None] == jnp.arange(N_EXPERTS, dtype=jnp.int32)[None, :]).astype(jnp.int32), axis=0)
    padded = (counts + bm - 1) // bm * bm
    pad_end = jnp.cumsum(padded)
    d_e = jnp.arange(n_dummy, dtype=jnp.int32) // bm
    d_active = (jnp.arange(n_dummy, dtype=jnp.int32) % bm) < (padded - counts)[d_e]
    keys = jnp.concatenate([flat_e * 2, jnp.where(d_active, d_e * 2 + 1, 2 * N_EXPERTS)])
    vbits = n_assign.bit_length()
    marker = (1 << vbits) - 1
    vals = jnp.concatenate([jnp.arange(n_assign, dtype=jnp.int32), jnp.full((n_dummy,), marker, jnp.int32)])
    assign = jnp.sort(keys * (1 << vbits) + vals)[:n_blocks * bm] & marker
    rows = jnp.arange(n_blocks * bm, dtype=jnp.int32)
    valid = assign != marker
    src_tok = jnp.where(valid, assign // TOP_K, 0)
    dst_row = jnp.where(valid, (assign % TOP_K) * n_tok + assign // TOP_K, n_assign + rows % bm)
    blk_start = jnp.arange(n_blocks, dtype=jnp.int32) * bm
    blk_e = jnp.minimum(jnp.sum((pad_end[None, :] <= blk_start[:, None]).astype(jnp.int32), axis=1),
                        N_EXPERTS - 1).astype(jnp.int32)
    n_used = (pad_end[-1] // bm).astype(jnp.int32).reshape(1)
    later = (blk_e[None, :] > blk_e[:, None]) & (blk_start[None, :] < pad_end[-1])
    nxt_e = jnp.min(jnp.where(later, blk_e[None, :], N_EXPERTS), axis=1)
    nxt_e = jnp.where(nxt_e == N_EXPERTS, blk_e, nxt_e).astype(jnp.int32)
    n_sub = jnp.sum(valid.reshape(n_blocks, MOE_SUB, MOE_BM)[:, :, 0].astype(jnp.int32), axis=1)
    return (blk_e, nxt_e, n_sub, n_used, src_tok.reshape(n_blocks, 1, bm).astype(jnp.int32),
            dst_row.reshape(n_blocks, 1, bm).astype(jnp.int32))


def _combine_kernel(x1_ref, y0_ref, y1_ref, y2_ref, y3_ref, tg_ref, nf_ref, op_ref, os_ref, *, n_first):
    tg = tg_ref[...]
    tm = x1_ref.shape[0]
    rows = lambda y_ref: _load_packed_rows(y_ref, (), tm, F32)
    acc = rows(y0_ref) * tg[:, 0:1]
    for k, y_ref in enumerate((y1_ref, y2_ref, y3_ref), start=1):
        acc = acc + rows(y_ref) * tg[:, k:k + 1]
    x = x1_ref[...] + acc
    out = x * lax.rsqrt(jnp.mean(x * x, axis=-1, keepdims=True) + RMS_EPS) * nf_ref[...]
    i = pl.program_id(0)

    @pl.when(i < n_first)
    def _():
        op_ref[...] = out

    @pl.when(i >= n_first)
    def _():
        os_ref[...] = out


def _combine(x1, y, tg, norm_f_w, n_p):
    n = x1.shape[0]
    tm = COMBINE_TM
    n_first = n_p // tm
    row = lambda i: (i, 0)
    y_spec = lambda k: pl.BlockSpec((tm * PACK_ROWS, LANES), lambda i: (k * (n // tm) + i, 0))
    return pl.pallas_call(
        functools.partial(_combine_kernel, n_first=n_first),
        out_shape=(jax.ShapeDtypeStruct((n_p, D_MODEL), F32), jax.ShapeDtypeStruct((n - n_p, D_MODEL), F32)),
        grid=(n // tm,),
        in_specs=[pl.BlockSpec((tm, D_MODEL), row), y_spec(0), y_spec(1), y_spec(2), y_spec(3),
                  pl.BlockSpec((tm, LANES), row), pl.BlockSpec((1, D_MODEL), lambda i: (0, 0))],
        out_specs=_split_specs((tm, D_MODEL), n_first),
        compiler_params=pltpu.CompilerParams(
            dimension_semantics=("arbitrary",), vmem_limit_bytes=VMEM_LIMIT),
        name="combine_norm",
    )(x1, y, y, y, y, tg, norm_f_w.reshape(1, D_MODEL))


def _pack_w_in(w_in):
    s = np.cumsum((0,) + IN_SIZES)
    w_in = w_in.astype(BF16)
    qkv, z, al, be, qb, kb, vb, gb, gk, ga, gbt = [w_in[:, s[i]:s[i + 1]] for i in range(len(IN_SIZES))]
    small = jnp.concatenate(
        [al, be, gk, jnp.zeros((D_MODEL, LANES - 2 * N_HEADS - GK_RANK), w_in.dtype)], axis=1)
    return jnp.concatenate([qkv, z, ga, gbt, qb, kb, vb, gb, small], axis=1)


def kernel(x_prompt, x_sample, state_gdn, state_gdn_conv, state_gla, norm1_w, w_in, conv_w, a_log,
           dt_bias, gdn_norm_w, gla_gk_up, gla_gk_b, gla_norm_w, w_up_a, w_up_b, w_o, norm2_w,
           w_router, b_router, w_gate_up, b_gate_up, w_down, b_down, norm_f_w):
    bp, tp, _ = x_prompt.shape
    bs, ts, _ = x_sample.shape
    n_p, n_s = bp * tp, bs * ts
    n = n_p + n_s
    l = 0

    x_p = x_prompt.reshape(n_p, D_MODEL)
    x_s = x_sample.reshape(n_s, D_MODEL)
    p = _inproj(x_p, x_s, norm1_w[l], _pack_w_in(w_in[l]))

    hp = jnp.zeros((8, LANES), F32).at[0, :N_HEADS].set(a_log[l]).at[1, :N_HEADS].set(dt_bias[l])
    wgk = jnp.zeros((LANES, N_HEADS * DK_B), F32).at[2 * N_HEADS:2 * N_HEADS + GK_RANK].set(gla_gk_up[l]).astype(BF16)
    zeros = lambda shape: jnp.zeros(shape, F32)

    oa_p, gdn_p, conv_p = _gdn(p, 0, bp, tp, conv_w[l], hp, gdn_norm_w[l],
                               zeros((bp, CONV_W - 1, CONV_DIM)), zeros((bp, N_HEADS, DK_A, DV_A)))
    oa_s, gdn_s, conv_s = _gdn(p, n_p, bs, ts, conv_w[l], hp, gdn_norm_w[l], state_gdn_conv[l], state_gdn[l])
    ob_p, gla_p = _gla(p, 0, bp, tp, wgk, gla_gk_b[l], gla_norm_w[l], zeros((bp, N_HEADS, DK_B, DV_B)))
    ob_s, gla_s = _gla(p, n_p, bs, ts, wgk, gla_gk_b[l], gla_norm_w[l], state_gla[l])

    wr = jnp.zeros((D_MODEL, LANES), F32).at[:, :N_EXPERTS].set(w_router[l])
    wr_hi = wr.astype(BF16)
    wr = jnp.concatenate([wr_hi, (wr - wr_hi.astype(F32)).astype(BF16)], axis=1)
    br = jnp.zeros((1, LANES), F32).at[0, :N_EXPERTS].set(b_router[l])
    x1, h2, te, tg = _merge(x_p, x_s, oa_p, oa_s, ob_p, ob_s, p, w_up_a[l].astype(BF16),
                            w_up_b[l].astype(BF16), w_o[l].astype(BF16), norm2_w[l], wr, br)

    blk_e, nxt_e, n_sub, n_used, src_tok, dst_row = _route(te[:, :TOP_K], n)
    y = _moe(blk_e, nxt_e, n_sub, n_used, src_tok, dst_row, h2, w_gate_up[l],
             b_gate_up[l].reshape(N_EXPERTS, 1, 2 * D_FF), w_down[l],
             b_down[l].reshape(N_EXPERTS, 1, D_MODEL), n * TOP_K + MOE_SUB * MOE_BM)
    out_p, out_s = _combine(x1, y, tg, norm_f_w, n_p)
    y_prompt = out_p.reshape(bp, tp, D_MODEL)
    y_sample = out_s.reshape(bs, ts, D_MODEL)
    return (y_prompt, y_sample, gdn_p[None], conv_p[None], gla_p[None], gdn_s[None], conv_s[None],
            gla_s[None])
```

```python
import functools
import math

import jax
import jax.numpy as jnp
import numpy as np
from jax import lax
from jax.experimental import pallas as pl
from jax.experimental.pallas import tpu as pltpu

F32 = jnp.float32
BF16 = jnp.bfloat16

D_MODEL = 1024
N_HEADS = 4
DK_A = 128
DV_A = 128
W_A = N_HEADS * DV_A
CONV_W = 4
CONV_DIM = 2 * N_HEADS * DK_A + W_A
GDN_CHUNK = 64
DK_B = 64
DV_B = 128
W_B = N_HEADS * DV_B
GK_RANK = 16
GLA_GATE_NORM = 16.0
GLA_CHUNK = 16
N_EXPERTS = 32
TOP_K = 4
D_FF = 1024
SWIGLU_LIMIT = 7.0
SWIGLU_ALPHA = 1.702
RMS_EPS = 1e-6
IN_SIZES = (CONV_DIM, W_A, N_HEADS, N_HEADS, N_HEADS * DK_B, N_HEADS * DK_B, W_B, W_B, GK_RANK,
            D_MODEL, D_MODEL)

LANES = 128

OFF_QKV = 0
OFF_Z = 1536
OFF_GATE_A = 2048
OFF_GATE_B = 3072
OFF_QK_B = 4096
OFF_V_B = 4608
OFF_G_B = 5120
OFF_SMALL = 5632
P_COLS = 5760

INPROJ_TM = 512
GLA_TB = 256
GLA_SEQS_CARRY = 4
GDN_SEGS_CARRY = 4
GDN_SEQS_CARRY = 2
GDN_SEGS_INDEP = 16
MERGE_TM = 512
MERGE_SPLIT = 2
MERGE_IN_BUFFERS = 3
MOE_BM = 256
MOE_SUB = 2
PACK_ROWS = D_MODEL // 2 // LANES
MOE_VMEM_LIMIT = 62 * 1024 * 1024
COMBINE_TM = 512
VMEM_LIMIT = 56 * 1024 * 1024


def _dot(a, b):
    return jnp.dot(a.astype(BF16), b.astype(BF16), preferred_element_type=F32)


def _dot_nt(a, b):
    return lax.dot_general(a.astype(BF16), b.astype(BF16), (((1,), (1,)), ((), ())),
                           preferred_element_type=F32)


def _dot_tn(a, b):
    return lax.dot_general(a.astype(BF16), b.astype(BF16), (((0,), (0,)), ((), ())), preferred_element_type=F32)


def _split3(x):
    hi = x.astype(BF16)
    r1 = x - hi.astype(F32)
    mid = r1.astype(BF16)
    lo = (r1 - mid.astype(F32)).astype(BF16)
    return hi, mid, lo


def _dot01(m01, x):
    hi, mid, lo = _split3(x)
    d = lambda p: jnp.dot(m01, p, preferred_element_type=F32)
    return d(hi) + d(mid) + d(lo)


def _silu(x):
    return x * jax.nn.sigmoid(x)


def _pack_bf16_pairs(x):
    bits = pltpu.bitcast(x.astype(BF16).astype(F32), jnp.uint32)
    half = D_MODEL // 2
    return (bits[:, :half] >> 16) | (bits[:, half:] & jnp.uint32(0xFFFF0000))


def _store_packed_rows(ref, lead, row0, words):
    n = words.shape[0]
    for c in range(PACK_ROWS):
        ref[(*lead, pl.ds(row0 * PACK_ROWS + c, n, stride=PACK_ROWS), slice(None))] = words[:, c * LANES:(c + 1) * LANES]


def _load_packed_rows(ref, lead, n, dtype):
    words = [ref[(*lead, pl.ds(c, n, stride=PACK_ROWS), slice(None))] for c in range(PACK_ROWS)]
    lo = [pltpu.bitcast(w << 16, F32).astype(dtype) for w in words]
    hi = [pltpu.bitcast(w & jnp.uint32(0xFFFF0000), F32).astype(dtype) for w in words]
    return jnp.concatenate(lo + hi, axis=1)


def _pick(i, n_first, first_ref, second_ref):
    return jnp.where(i < n_first, first_ref[...], second_ref[...])


def _split_specs(block, n_first):
    return (pl.BlockSpec(block, lambda i: (jnp.minimum(i, n_first - 1), 0)),
            pl.BlockSpec(block, lambda i: (jnp.maximum(i - n_first, 0), 0)))


def _inproj_kernel(xp_ref, xs_ref, nw_ref, w_ref, o_ref, *, n_first):
    x = _pick(pl.program_id(0), n_first, xp_ref, xs_ref)
    ms = jnp.mean(x * x, axis=-1, keepdims=True)
    h = (x * lax.rsqrt(ms + RMS_EPS) * nw_ref[...]).astype(BF16)
    o_ref[...] = jnp.dot(h, w_ref[...], preferred_element_type=F32)


def _inproj(x_p, x_s, norm_w, w_packed):
    tm = INPROJ_TM
    n_first = x_p.shape[0] // tm
    n = x_p.shape[0] + x_s.shape[0]
    return pl.pallas_call(
        functools.partial(_inproj_kernel, n_first=n_first),
        out_shape=jax.ShapeDtypeStruct((n, P_COLS), F32),
        grid=(n // tm,),
        in_specs=[
            *_split_specs((tm, D_MODEL), n_first),
            pl.BlockSpec((1, D_MODEL), lambda i: (0, 0)),
            pl.BlockSpec((D_MODEL, P_COLS), lambda i: (0, 0), pipeline_mode=pl.Buffered(1)),
        ],
        out_specs=pl.BlockSpec((tm, P_COLS), lambda i: (i, 0)),
        compiler_params=pltpu.CompilerParams(
            dimension_semantics=("arbitrary",), vmem_limit_bytes=VMEM_LIMIT),
        name="inproj",
    )(x_p, x_s, norm_w.reshape(1, D_MODEL), w_packed)


def _gdn_kernel(*refs, chunk, n_seg, n_seq, carry, n_steps):
    qkv_refs, z_refs, sm_refs = refs[:n_seq], refs[n_seq:2 * n_seq], refs[2 * n_seq:3 * n_seq]
    cw_ref, hp_ref, nw_ref, sconv_ref, s0_ref, o_ref, sout_ref, cout_ref, s_ref, ubuf_ref = refs[3 * n_seq:]
    c_len = chunk
    n_seg_all = n_seq * n_seg
    rows = n_seg_all * c_len
    t = pl.program_id(1)
    hist = CONV_W - 1
    base = 8
    conv_len = n_seg * c_len if carry else c_len
    n_conv = n_seq if carry else n_seg

    def init_history():
        ubuf_ref[:, 0:base, :] = jnp.zeros((n_conv, base, CONV_DIM), F32)
        ubuf_ref[:, base - hist:base, :] = sconv_ref[...]

    if carry:
        @pl.when(t == 0)
        def _():
            s_ref[...] = s0_ref[...]
            init_history()

        @pl.when(t > 0)
        def _():
            ubuf_ref[:, base - hist:base, :] = ubuf_ref[:, base + conv_len - hist:base + conv_len, :]
        for q in range(n_seq):
            ubuf_ref[q, base:base + conv_len, :] = qkv_refs[q][...]
    else:
        init_history()
        ubuf_ref[:, base:base + conv_len, :] = qkv_refs[0][...].reshape(n_conv, conv_len, CONV_DIM)

    cw = cw_ref[...]
    u = ubuf_ref[...]
    u1 = pltpu.roll(u, 1, axis=1)
    v = u1 * cw[0:1, :] + u * cw[1:2, :]
    y = (pltpu.roll(v, 2, axis=1) + u1 * cw[2:3, :]) + u * cw[3:4, :]
    y = _silu(y[:, base:, :]).reshape(rows, CONV_DIM)

    sm = sm_refs[0][...] if n_seq == 1 else jnp.concatenate([r[...] for r in sm_refs], axis=0)
    hp = hp_ref[...]
    lane = lax.broadcasted_iota(jnp.int32, (rows, LANES), 1)
    g_all = -jnp.exp(hp[0:1, :]) * jax.nn.softplus(sm + hp[1:2, :])
    g_all = jnp.where(lane < N_HEADS, g_all, 0.0)
    beta_all = jax.nn.sigmoid(sm)

    sh = int(round(math.log2(c_len)))
    r_b = lax.broadcasted_iota(jnp.int32, (rows, rows), 0)
    c_b = lax.broadcasted_iota(jnp.int32, (rows, rows), 1)
    seg_tril = ((r_b >> sh) == (c_b >> sh)) & (c_b <= r_b)
    gc = _dot01(jnp.where(seg_tril, 1.0, 0.0).astype(BF16), g_all)

    r_i = lax.broadcasted_iota(jnp.int32, (c_len, c_len), 0)
    c_i = lax.broadcasted_iota(jnp.int32, (c_len, c_len), 1)
    tri = c_i <= r_i
    strict = c_i < r_i
    eye = c_i == r_i
    nw = nw_ref[...]
    levels = sh

    qn, kn, kb, rhs, qg, gcol = [], [], [], [], [], []
    for h in range(N_HEADS):
        qh = y[:, h * DK_A:(h + 1) * DK_A]
        kh = y[:, W_A + h * DK_A:W_A + (h + 1) * DK_A]
        vh = y[:, 2 * W_A + h * DV_A:2 * W_A + (h + 1) * DV_A]
        qn_h = qh * lax.rsqrt(jnp.sum(qh * qh, axis=-1, keepdims=True) + 1e-6) * (DK_A ** -0.5)
        kn_h = kh * lax.rsqrt(jnp.sum(kh * kh, axis=-1, keepdims=True) + 1e-6)
        gcol_h = gc[:, h:h + 1]
        eg = jnp.exp(gcol_h)
        bcol = beta_all[:, N_HEADS + h:N_HEADS + h + 1]
        kb_h = kn_h * bcol
        qn.append(qn_h)
        kn.append(kn_h)
        kb.append(kb_h)
        rhs.append(jnp.concatenate([vh * bcol, kb_h * eg], axis=1))
        qg.append(qn_h * eg)
        gcol.append(gcol_h)

    pairs = [(g, h) for g in range(n_seg_all) for h in range(N_HEADS)]
    sl = lambda g: slice(g * c_len, (g + 1) * c_len)

    decay, kd, gl = {}, {}, {}
    for (g, h) in pairs:
        gcol_p = gcol[h][sl(g)]
        gcb = jnp.broadcast_to(gcol_p, (c_len, c_len))
        grow = jnp.sum(jnp.where(eye, gcb, 0.0), axis=0, keepdims=True)
        decay[g, h] = jnp.where(tri, jnp.exp(jnp.where(tri, gcb - grow, 0.0)), 0.0)
        glast = gcol[h][(g + 1) * c_len - 1:(g + 1) * c_len]
        kd[g, h] = kn[h][sl(g)] * jnp.exp(glast - gcol_p)
        gl[g, h] = jnp.exp(glast)

    kq = {p: _dot_nt(jnp.concatenate([kb[p[1]][sl(p[0])], qn[p[1]][sl(p[0])]], axis=0), kn[p[1]][sl(p[0])])
          for p in pairs}
    low = {p: jnp.where(strict, kq[p][:c_len] * decay[p], 0.0) for p in pairs}
    attn = {p: jnp.where(tri, kq[p][c_len:] * decay[p], 0.0) for p in pairs}
    ymat = {p: -low[p] for p in pairs}
    lp = low
    for _ in range(levels - 1):
        lp = {p: _dot(lp[p], lp[p]) for p in pairs}
        prod = {p: _dot(ymat[p], lp[p]) for p in pairs}
        ymat = {p: ymat[p] + lp[p] + prod[p] for p in pairs}
    sol = {}
    for (g, h) in pairs:
        rhs_p = rhs[h][sl(g)]
        sol[g, h] = rhs_p + _dot(ymat[g, h], rhs_p)

    def finish(g, h, o):
        r = o * lax.rsqrt(jnp.mean(o * o, axis=-1, keepdims=True) + RMS_EPS) * nw
        q, loc = (g // n_seg, sl(g % n_seg)) if carry else (0, sl(g))
        gated = r * _silu(z_refs[q][loc, h * DV_A:(h + 1) * DV_A])
        if carry:
            o_ref[q, loc, h * DV_A:(h + 1) * DV_A] = gated
        else:
            o_ref[loc, h * DV_A:(h + 1) * DV_A] = gated

    def advance(group, state):
        ws = {p: _dot(jnp.concatenate([sol[p][:, DV_A:], qg[p[1]][sl(p[0])]], axis=0), state[p]) for p in group}
        v_new = {p: sol[p][:, :DV_A] - ws[p][:c_len] for p in group}
        o = {p: ws[p][c_len:] + _dot(attn[p], v_new[p]) for p in group}
        new = {p: state[p] * gl[p] + _dot_tn(kd[p], v_new[p]) for p in group}
        for p in group:
            finish(p[0], p[1], o[p])
        return new

    if carry:
        chains = [(q, h) for q in range(n_seq) for h in range(N_HEADS)]
        cur = {c: s_ref[c[0], c[1]] for c in chains}
        for g in range(n_seg):
            group = [(q * n_seg + g, h) for (q, h) in chains]
            new = advance(group, {(q * n_seg + g, h): cur[q, h] for (q, h) in chains})
            cur = {(q, h): new[q * n_seg + g, h] for (q, h) in chains}
        for (q, h) in chains:
            s_ref[q, h] = cur[q, h]

        @pl.when(t == n_steps - 1)
        def _():
            for (q, h) in chains:
                sout_ref[q, h] = cur[q, h]
            cout_ref[...] = ubuf_ref[:, base + conv_len - hist:base + conv_len, :]
    else:
        new = advance(pairs, {p: s0_ref[p[0], p[1]] for p in pairs})
        for p in pairs:
            sout_ref[p[0], p[1]] = new[p]
        cout_ref[...] = ubuf_ref[:, base + conv_len - hist:base + conv_len, :]


def _gdn(p, row0, batch, seq, conv_w, hp, norm_w, s_conv, s0):
    chunk = math.gcd(seq, GDN_CHUNK)
    n_chunks = seq // chunk
    carry = n_chunks > 1
    if carry:
        n_seg = math.gcd(n_chunks, GDN_SEGS_CARRY)
        n_seq = math.gcd(batch, GDN_SEQS_CARRY)
        grid = (batch // n_seq, n_chunks // n_seg)
        n_conv, conv_len = n_seq, n_seg * chunk
    else:
        n_seg = math.gcd(batch, GDN_SEGS_INDEP)
        n_seq = 1
        grid = (batch // n_seg, 1)
        n_conv, conv_len = n_seg, chunk
    rows = n_seg * chunk
    n_steps = grid[1]
    blk0 = row0 // rows
    rowblk = lambda q: (lambda b, t: blk0 + (b * n_seq + q) * n_steps + t)
    pcol = lambda width, off: [pl.BlockSpec((rows, width), lambda b, t, rb=rowblk(q): (rb(b, t), off // width))
                               for q in range(n_seq)]
    kern = functools.partial(_gdn_kernel, chunk=chunk, n_seg=n_seg, n_seq=n_seq, carry=carry, n_steps=n_steps)
    if carry:
        o_shape, o_spec = (batch, seq, W_A), pl.BlockSpec((n_seq, rows, W_A), lambda b, t: (b, t, 0))
    else:
        o_shape, o_spec = (batch * seq, W_A), pl.BlockSpec((rows, W_A), lambda b, t: (b, 0))
    o, s_new, c_new = pl.pallas_call(
        kern,
        out_shape=(jax.ShapeDtypeStruct(o_shape, F32),
                   jax.ShapeDtypeStruct((batch, N_HEADS, DK_A, DV_A), F32),
                   jax.ShapeDtypeStruct((batch, CONV_W - 1, CONV_DIM), F32)),
        grid=grid,
        in_specs=[
            *pcol(CONV_DIM, OFF_QKV), *pcol(W_A, OFF_Z), *pcol(LANES, OFF_SMALL),
            pl.BlockSpec((CONV_W, CONV_DIM), lambda b, t: (0, 0)),
            pl.BlockSpec((8, LANES), lambda b, t: (0, 0)),
            pl.BlockSpec((1, DV_A), lambda b, t: (0, 0)),
            pl.BlockSpec((n_conv, CONV_W - 1, CONV_DIM), lambda b, t: (b, 0, 0)),
            pl.BlockSpec((n_conv, N_HEADS, DK_A, DV_A), lambda b, t: (b, 0, 0, 0)),
        ],
        out_specs=(
            o_spec,
            pl.BlockSpec((n_conv, N_HEADS, DK_A, DV_A), lambda b, t: (b, 0, 0, 0)),
            pl.BlockSpec((n_conv, CONV_W - 1, CONV_DIM), lambda b, t: (b, 0, 0)),
        ),
        scratch_shapes=[pltpu.VMEM((n_seq, N_HEADS, DK_A, DV_A), F32),
                        pltpu.VMEM((n_conv, conv_len + 8, CONV_DIM), F32)],
        compiler_params=pltpu.CompilerParams(
            dimension_semantics=("arbitrary", "arbitrary"), vmem_limit_bytes=VMEM_LIMIT),
        name=f"gdn_c{chunk}",
    )(*([p] * (3 * n_seq)), conv_w, hp, norm_w.reshape(1, DV_A), s_conv, s0)
    return o.reshape(batch * seq, W_A), s_new, c_new


def _gla_kernel(*refs, tb, cs, n_seq, carry, n_steps):
    qk_refs, v_refs, g_refs, sm_refs = (refs[k * n_seq:(k + 1) * n_seq] for k in range(4))
    wgk_ref, bgk_ref, nw_ref, s0_ref, o_ref, sout_ref, s_ref = refs[4 * n_seq:]
    t = pl.program_id(1)
    n_sub = tb // cs
    w_kb = N_HEADS * DK_B
    seqs = range(n_seq)
    heads = range(N_HEADS)

    if carry:
        @pl.when(t == 0)
        def _():
            s_ref[...] = s0_ref[...]

    sh = int(round(math.log2(cs)))
    r_i = lax.broadcasted_iota(jnp.int32, (tb, tb), 0)
    c_i = lax.broadcasted_iota(jnp.int32, (tb, tb), 1)
    same = (r_i >> sh) == (c_i >> sh)
    mask = same & (c_i <= r_i)
    mask_bf = jnp.where(mask, 1.0, 0.0).astype(BF16)
    same_bf = jnp.where(same, 1.0, 0.0).astype(BF16)
    onehot = jnp.where((lax.broadcasted_iota(jnp.int32, (tb, n_sub), 0) >> sh)
                       == lax.broadcasted_iota(jnp.int32, (tb, n_sub), 1), 1.0, 0.0).astype(BF16)
    tn = lambda part: lax.dot_general(part, onehot, (((0,), (0,)), ((), ())), preferred_element_type=F32)

    log_a = [jax.nn.log_sigmoid(_dot(sm_refs[q][...], wgk_ref[...]) + bgk_ref[...]) / GLA_GATE_NORM for q in seqs]
    gcum = [_dot01(mask_bf, la) for la in log_a]
    gtot = [_dot01(same_bf, la) for la in log_a]
    glcol = []
    for la in log_a:
        hi, mid, lo = _split3(la)
        glcol.append(jnp.exp(tn(hi) + tn(mid) + tn(lo)))

    nw = nw_ref[...]
    sl = lambda j: slice(j * cs, (j + 1) * cs)
    qg, kg, kd, vh = {}, {}, {}, {}
    for q in seqs:
        qk = qk_refs[q][...]
        for h in heads:
            qh = qk[:, h * DK_B:(h + 1) * DK_B] * (DK_B ** -0.5)
            kh = qk[:, w_kb + h * DK_B:w_kb + (h + 1) * DK_B]
            gh = gcum[q][:, h * DK_B:(h + 1) * DK_B]
            gt = gtot[q][:, h * DK_B:(h + 1) * DK_B]
            qg[q, h] = qh * jnp.exp(gh)
            kg[q, h] = kh * jnp.exp(-gh)
            kd[q, h] = kh * jnp.exp(gt - gh)
            vh[q, h] = v_refs[q][:, h * DV_B:(h + 1) * DV_B]
    chains = [(q, h) for q in seqs for h in heads]
    attn = {c: jnp.where(mask, _dot_nt(qg[c], kg[c]), 0.0) for c in chains}
    o_intra = {c: _dot(attn[c], vh[c]) for c in chains}
    upd = {(j, c): _dot_tn(kd[c][sl(j)], vh[c][sl(j)]) for j in range(n_sub) for c in chains}
    gl = lambda j, c: glcol[c[0]][c[1] * DK_B:(c[1] + 1) * DK_B, j:j + 1]

    o_inter = {c: [] for c in chains}
    if carry:
        cur = {c: s_ref[c[0], c[1]] for c in chains}
        for j in range(n_sub):
            for c in chains:
                o_inter[c].append(_dot(qg[c][sl(j)], cur[c]))
                cur[c] = cur[c] * gl(j, c) + upd[j, c]
        for c in chains:
            s_ref[c[0], c[1]] = cur[c]

        @pl.when(t == n_steps - 1)
        def _():
            for c in chains:
                sout_ref[c[0], c[1]] = cur[c]
    else:
        for j in range(n_sub):
            for c in chains:
                s = s0_ref[j, c[1]]
                o_inter[c].append(_dot(qg[c][sl(j)], s))
                sout_ref[j, c[1]] = s * gl(j, c) + upd[j, c]

    for (q, h) in chains:
        o = o_intra[q, h] + jnp.concatenate(o_inter[q, h], axis=0)
        r = o * lax.rsqrt(jnp.mean(o * o, axis=-1, keepdims=True) + RMS_EPS) * nw
        gated = r * _silu(g_refs[q][:, h * DV_B:(h + 1) * DV_B])
        if carry:
            o_ref[q, :, h * DV_B:(h + 1) * DV_B] = gated
        else:
            o_ref[:, h * DV_B:(h + 1) * DV_B] = gated


def _gla(p, row0, batch, seq, wgk, bgk, norm_w, s0):
    cs = math.gcd(seq, GLA_CHUNK)
    carry = seq > cs
    if carry:
        tb = math.gcd(seq, GLA_TB)
        n_seq = math.gcd(batch, GLA_SEQS_CARRY)
        grid = (batch // n_seq, seq // tb)
        n_state = n_seq
    else:
        tb = math.gcd(batch * seq, GLA_TB)
        n_seq = 1
        n_state = tb // cs
        grid = (batch // n_state, 1)
    n_steps = grid[1]
    blk0 = row0 // tb
    rowblk = lambda q: (lambda b, t: blk0 + (b * n_seq + q) * n_steps + t)
    pcol = lambda width, off: [pl.BlockSpec((tb, width), lambda b, t, rb=rowblk(q): (rb(b, t), off // width))
                               for q in range(n_seq)]
    kern = functools.partial(_gla_kernel, tb=tb, cs=cs, n_seq=n_seq, carry=carry, n_steps=n_steps)
    if carry:
        o_shape, o_spec = (batch, seq, W_B), pl.BlockSpec((n_seq, tb, W_B), lambda b, t: (b, t, 0))
    else:
        o_shape, o_spec = (batch * seq, W_B), pl.BlockSpec((tb, W_B), lambda b, t: (b, 0))
    o, s_new = pl.pallas_call(
        kern,
        out_shape=(jax.ShapeDtypeStruct(o_shape, F32),
                   jax.ShapeDtypeStruct((batch, N_HEADS, DK_B, DV_B), F32)),
        grid=grid,
        in_specs=[
            *pcol(2 * N_HEADS * DK_B, OFF_QK_B), *pcol(W_B, OFF_V_B), *pcol(W_B, OFF_G_B), *pcol(LANES, OFF_SMALL),
            pl.BlockSpec((LANES, N_HEADS * DK_B), lambda b, t: (0, 0)),
            pl.BlockSpec((1, N_HEADS * DK_B), lambda b, t: (0, 0)),
            pl.BlockSpec((1, DV_B), lambda b, t: (0, 0)),
            pl.BlockSpec((n_state, N_HEADS, DK_B, DV_B), lambda b, t: (b, 0, 0, 0)),
        ],
        out_specs=(
            o_spec,
            pl.BlockSpec((n_state, N_HEADS, DK_B, DV_B), lambda b, t: (b, 0, 0, 0)),
        ),
        scratch_shapes=[pltpu.VMEM((n_seq, N_HEADS, DK_B, DV_B), F32)],
        compiler_params=pltpu.CompilerParams(
            dimension_semantics=("arbitrary", "arbitrary"), vmem_limit_bytes=VMEM_LIMIT),
        name=f"gla_c{cs}",
    )(*([p] * (4 * n_seq)), wgk, bgk.reshape(1, N_HEADS * DK_B), norm_w.reshape(1, DV_B), s0)
    return o.reshape(batch * seq, W_B), s_new


def _topk_softmax(logits):
    tm = logits.shape[0]
    lane = lax.broadcasted_iota(jnp.int32, (tm, LANES), 1)
    lane_f = lane.astype(F32)
    neg = jnp.float32(-jnp.inf)
    cur = jnp.where(lane < N_EXPERTS, logits, neg)
    vals, idxs = [], []
    for _ in range(TOP_K):
        m = jnp.max(cur, axis=-1, keepdims=True)
        idx = jnp.min(jnp.where(cur == m, lane_f, float(LANES)), axis=-1, keepdims=True)
        vals.append(m)
        idxs.append(idx)
        cur = jnp.where(lane_f == idx, neg, cur)
    exps = [jnp.exp(v - vals[0]) for v in vals]
    den = exps[0] + exps[1] + exps[2] + exps[3]
    te = jnp.zeros((tm, LANES), F32)
    tg = jnp.zeros((tm, LANES), F32)
    for k in range(TOP_K):
        te = jnp.where(lane == k, idxs[k], te)
        tg = jnp.where(lane == k, exps[k] / den, tg)
    return te.astype(jnp.int32), tg


def _merge_block(i, xp_ref, xs_ref, oap_ref, oas_ref, obp_ref, obs_ref, ga_ref, gb_ref, wua_ref, wub_ref, wo_ref,
                 n2_ref, wr_ref, br_ref, x1_ref, h2_ref, te_ref, tg_ref, *, n_first):
    tm = x1_ref.shape[0]
    hm = tm // MERGE_SPLIT
    groups = [slice(a * hm, (a + 1) * hm) for a in range(MERGE_SPLIT)]
    oa = _pick(i, n_first, oap_ref, oas_ref).astype(BF16)
    ob = _pick(i, n_first, obp_ref, obs_ref).astype(BF16)
    x = _pick(i, n_first, xp_ref, xs_ref)
    ua = [jnp.dot(oa[g], wua_ref[...], preferred_element_type=F32) for g in groups]
    ub = [jnp.dot(ob[g], wub_ref[...], preferred_element_type=F32) for g in groups]
    merged = [jax.nn.sigmoid(ga_ref[g, :]) * ua[a] + jax.nn.sigmoid(gb_ref[g, :]) * ub[a]
              for a, g in enumerate(groups)]
    x1 = [x[g] + jnp.dot(merged[a].astype(BF16), wo_ref[...], preferred_element_type=F32)
          for a, g in enumerate(groups)]
    h2 = [v * lax.rsqrt(jnp.mean(v * v, axis=-1, keepdims=True) + RMS_EPS) * n2_ref[...] for v in x1]
    h_hi = [v.astype(BF16) for v in h2]
    h_lo = [(v - hi.astype(F32)).astype(BF16) for v, hi in zip(h2, h_hi)]
    l_hi = [jnp.dot(hi, wr_ref[...], preferred_element_type=F32) for hi in h_hi]
    l_lo = [jnp.dot(lo, wr_ref[:, :LANES], preferred_element_type=F32) for lo in h_lo]
    for a, g in enumerate(groups):
        x1_ref[g, :] = x1[a]
        _store_packed_rows(h2_ref, (), a * hm, _pack_bf16_pairs(h2[a]))
        logits = l_hi[a][:, :LANES] + l_hi[a][:, LANES:] + l_lo[a] + br_ref[...]
        te, tg = _topk_softmax(logits)
        te_ref[g, :] = te
        tg_ref[g, :] = tg


def _merge_kernel(xp_hbm, xs_hbm, oap_hbm, oas_hbm, obp_hbm, obs_hbm, p_hbm, wua_ref, wub_ref, wo_ref, n2_ref, wr_ref,
                  br_ref, x1_hbm, h2_hbm, te_hbm, tg_hbm, step_ref, *, n_first, n_steps, tm):
    row = lambda i: (i, 0)
    first = lambda i: (jnp.minimum(i, n_first - 1), 0)
    second = lambda i: (jnp.maximum(i - n_first, 0), 0)
    deep = pl.Buffered(MERGE_IN_BUFFERS)

    def split(block):
        return (pl.BlockSpec(block, first, pipeline_mode=deep), pl.BlockSpec(block, second))

    def body(xp_ref, xs_ref, oap_ref, oas_ref, obp_ref, obs_ref, ga_ref, gb_ref, x1_ref, h2_ref, te_ref, tg_ref):
        i = step_ref[0]
        step_ref[0] = i + 1
        _merge_block(i, xp_ref, xs_ref, oap_ref, oas_ref, obp_ref, obs_ref, ga_ref, gb_ref, wua_ref, wub_ref,
                     wo_ref, n2_ref, wr_ref, br_ref, x1_ref, h2_ref, te_ref, tg_ref, n_first=n_first)

    step_ref[0] = 0
    pltpu.emit_pipeline(
        body,
        grid=(n_steps,),
        in_specs=[
            *split((tm, D_MODEL)),
            *split((tm, W_A)),
            *split((tm, W_B)),
            pl.BlockSpec((tm, D_MODEL), lambda i: (i, OFF_GATE_A // D_MODEL), pipeline_mode=deep),
            pl.BlockSpec((tm, D_MODEL), lambda i: (i, OFF_GATE_B // D_MODEL), pipeline_mode=deep),
        ],
        out_specs=[pl.BlockSpec((tm, D_MODEL), row), pl.BlockSpec((tm * PACK_ROWS, LANES), row),
                   pl.BlockSpec((tm, LANES), row), pl.BlockSpec((tm, LANES), row)],
    )(xp_hbm, xs_hbm, oap_hbm, oas_hbm, obp_hbm, obs_hbm, p_hbm, p_hbm, x1_hbm, h2_hbm, te_hbm, tg_hbm)


def _merge(x_p, x_s, oa_p, oa_s, ob_p, ob_s, p, wua, wub, wo, norm2_w, wr, br):
    n = x_p.shape[0] + x_s.shape[0]
    tm = MERGE_TM
    n_first = x_p.shape[0] // tm
    hbm = pl.BlockSpec(memory_space=pl.ANY)
    vmem = pl.BlockSpec(memory_space=pltpu.VMEM)
    return pl.pallas_call(
        functools.partial(_merge_kernel, n_first=n_first, n_steps=n // tm, tm=tm),
        out_shape=(jax.ShapeDtypeStruct((n, D_MODEL), F32), jax.ShapeDtypeStruct((n * PACK_ROWS, LANES), jnp.uint32),
                   jax.ShapeDtypeStruct((n, LANES), jnp.int32), jax.ShapeDtypeStruct((n, LANES), F32)),
        in_specs=[hbm] * 7 + [vmem] * 6,
        out_specs=(hbm, hbm, hbm, hbm),
        scratch_shapes=[pltpu.SMEM((1,), jnp.int32)],
        compiler_params=pltpu.CompilerParams(vmem_limit_bytes=VMEM_LIMIT),
        name="merge_router",
    )(x_p, x_s, oa_p, oa_s, ob_p, ob_s, p, wua, wub, wo, norm2_w.reshape(1, D_MODEL), wr, br)


def _moe_kernel(blk_e_ref, nxt_e_ref, n_sub_ref, n_used_ref, src_ref, src_nxt_ref, dst_ref, h2_ref, wgu_hbm, bgu_ref,
                wdn_hbm, bdn_ref, y_hbm, xstage, ybuf, wgu_stage, wdn_stage, wgu_bf, wdn_bf, ssem, wsem):
    i = pl.program_id(0)
    n_used = n_used_ref[0]
    bm = MOE_BM
    pr = PACK_ROWS
    e_cur = blk_e_ref[i]

    def weight_copies(e):
        return (pltpu.make_async_copy(wgu_hbm.at[e], wgu_stage, wsem.at[0]),
                pltpu.make_async_copy(wdn_hbm.at[e], wdn_stage, wsem.at[1]))

    @pl.when((i == 0) & (n_used > 0))
    def _():
        for cp in weight_copies(e_cur):
            cp.start()

    @pl.when((i < n_used) & ((i == 0) | (e_cur != blk_e_ref[jnp.maximum(i - 1, 0)])))
    def _():
        for cp in weight_copies(e_cur):
            cp.wait()
        wgu_bf[...] = wgu_stage[...].astype(BF16)
        wdn_bf[...] = wdn_stage[...].astype(BF16)

        @pl.when(nxt_e_ref[i] != e_cur)
        def _():
            for cp in weight_copies(nxt_e_ref[i]):
                cp.start(priority=1)

    def scatter_copy(row, s, r):
        return pltpu.make_async_copy(ybuf.at[s, pl.ds(r * pr, pr), :],
                                     y_hbm.at[pl.ds(pl.multiple_of(row * pr, pr), pr), :], ssem.at[s])

    @pl.when(i == 0)
    def _():
        ybuf[...] = jnp.zeros_like(ybuf)
        n_real = y_hbm.shape[0] - MOE_SUB * bm * pr
        for s in range(MOE_SUB):
            sink = pltpu.make_async_copy(ybuf.at[s], y_hbm.at[pl.ds(n_real + s * bm * pr, bm * pr), :], ssem.at[s])
            sink.start()
            sink.wait()

    def gather(idx_ref, sub, slot):
        for r in range(bm):
            tok = pl.multiple_of(idx_ref[0, 0, sub * bm + r] * pr, pr)
            xstage[slot, pl.ds(r * pr, pr), :] = h2_ref[pl.ds(tok, pr), :]

    def wait_prev_scatter(s):
        @pl.when((i >= 1) & (n_sub_ref[jnp.maximum(i - 1, 0)] > s))
        def _():
            for r in range(bm):
                scatter_copy(0, s, r).wait()

    def expert_mlp(s):
        x = _load_packed_rows(xstage, (s,), bm, BF16)
        gu = jnp.dot(x, wgu_bf[...], preferred_element_type=F32) + bgu_ref[...]
        gate = jnp.minimum(gu[:, :D_FF], SWIGLU_LIMIT)
        up = jnp.clip(gu[:, D_FF:], -SWIGLU_LIMIT, SWIGLU_LIMIT)
        hmid = (up + 1.0) * (gate * jax.nn.sigmoid(SWIGLU_ALPHA * gate))
        y = jnp.dot(hmid.astype(BF16), wdn_bf[...], preferred_element_type=F32) + bdn_ref[...]
        _store_packed_rows(ybuf, (s,), 0, _pack_bf16_pairs(y))
        for r in range(bm):
            scatter_copy(dst_ref[0, 0, s * bm + r], s, r).start()

    assert MOE_SUB == 2

    @pl.when((i == 0) & (n_used > 0))
    def _():
        gather(src_ref, 0, 0)

    @pl.when(i < n_used)
    def _():
        wait_prev_scatter(0)
        gather(src_ref, 1, 1)
        expert_mlp(0)
        wait_prev_scatter(1)

        @pl.when(n_sub_ref[i] > 1)
        def _():
            gather(src_nxt_ref, 0, 0)
            expert_mlp(1)

        @pl.when(n_sub_ref[i] <= 1)
        def _():
            gather(src_nxt_ref, 0, 0)

    @pl.when(i == n_used - 1)
    def _():
        for s in range(MOE_SUB):
            @pl.when(n_sub_ref[i] > s)
            def _():
                for r in range(bm):
                    scatter_copy(0, s, r).wait()


def _moe(blk_e, nxt_e, n_sub, n_used, src_tok, dst_row, h2, wgu, bgu, wdn, bdn, y_rows):
    n_steps = src_tok.shape[0]
    step_rows = MOE_SUB * MOE_BM
    cur = lambda i, be, ne, ns, nu: (i, 0, 0)
    wsel = lambda i, be, ne, ns, nu: (be[i], 0, 0)
    smem_blk = lambda im: pl.BlockSpec((1, 1, step_rows), im, memory_space=pltpu.SMEM)
    grid_spec = pltpu.PrefetchScalarGridSpec(
        num_scalar_prefetch=4,
        grid=(n_steps,),
        in_specs=[
            smem_blk(cur), smem_blk(lambda i, be, ne, ns, nu: (jnp.minimum(i + 1, n_steps - 1), 0, 0)), smem_blk(cur),
            pl.BlockSpec(h2.shape, lambda i, be, ne, ns, nu: (0, 0), pipeline_mode=pl.Buffered(1)),
            pl.BlockSpec(memory_space=pl.ANY),
            pl.BlockSpec((None, 1, 2 * D_FF), wsel),
            pl.BlockSpec(memory_space=pl.ANY),
            pl.BlockSpec((None, 1, D_MODEL), wsel),
        ],
        out_specs=pl.BlockSpec(memory_space=pl.ANY),
        scratch_shapes=[pltpu.VMEM((MOE_SUB, MOE_BM * PACK_ROWS, LANES), jnp.uint32),
                        pltpu.VMEM((MOE_SUB, MOE_BM * PACK_ROWS, LANES), jnp.uint32),
                        pltpu.VMEM((D_MODEL, 2 * D_FF), F32), pltpu.VMEM((D_FF, D_MODEL), F32),
                        pltpu.VMEM((D_MODEL, 2 * D_FF), BF16), pltpu.VMEM((D_FF, D_MODEL), BF16),
                        pltpu.SemaphoreType.DMA((MOE_SUB,)), pltpu.SemaphoreType.DMA((2,))],
    )
    return pl.pallas_call(
        _moe_kernel,
        out_shape=jax.ShapeDtypeStruct((y_rows * PACK_ROWS, LANES), jnp.uint32),
        grid_spec=grid_spec,
        compiler_params=pltpu.CompilerParams(
            dimension_semantics=("arbitrary",), vmem_limit_bytes=MOE_VMEM_LIMIT),
        name="moe_experts",
    )(blk_e, nxt_e, n_sub, n_used, src_tok, src_tok, dst_row, h2, wgu, bgu, wdn, bdn)


def _route(top_e, n_tok):
    bm = MOE_SUB * MOE_BM
    n_assign = n_tok * TOP_K
    n_blocks = n_assign // bm + N_EXPERTS
    n_dummy = N_EXPERTS * bm
    flat_e = top_e.reshape(-1)
    counts = jnp.sum((flat_e[:, None] == jnp.arange(N_EXPERTS, dtype=jnp.int32)[None, :]).astype(jnp.int32), axis=0)
    padded = (counts + bm - 1) // bm * bm
    pad_end = jnp.cumsum(padded)
    d_e = jnp.arange(n_dummy, dtype=jnp.int32) // bm
    d_active = (jnp.arange(n_dummy, dtype=jnp.int32) % bm) < (padded - counts)[d_e]
    keys = jnp.concatenate([flat_e * 2, jnp.where(d_active, d_e * 2 + 1, 2 * N_EXPERTS)])
    vbits = n_assign.bit_length()
    marker = (1 << vbits) - 1
    vals = jnp.concatenate([jnp.arange(n_assign, dtype=jnp.int32), jnp.full((n_dummy,), marker, jnp.int32)])
    assign = jnp.sort(keys * (1 << vbits) + vals)[:n_blocks * bm] & marker
    rows = jnp.arange(n_blocks * bm, dtype=jnp.int32)
    valid = assign != marker
    src_tok = jnp.where(valid, assign // TOP_K, 0)
    dst_row = jnp.where(valid, (assign % TOP_K) * n_tok + assign // TOP_K, n_assign + rows % bm)
    blk_start = jnp.arange(n_blocks, dtype=jnp.int32) * bm
    blk_e = jnp.minimum(jnp.sum((pad_end[None, :] <= blk_start[:, None]).astype(jnp.int32), axis=1),
                        N_EXPERTS - 1).astype(jnp.int32)
    n_used = (pad_end[-1] // bm).astype(jnp.int32).reshape(1)
    later = (blk_e[None, :] > blk_e[:, None]) & (blk_start[None, :] < pad_end[-1])
    nxt_e = jnp.min(jnp.where(later, blk_e[None, :], N_EXPERTS), axis=1)
    nxt_e = jnp.where(nxt_e == N_EXPERTS, blk_e, nxt_e).astype(jnp.int32)
    n_sub = jnp.sum(valid.reshape(n_blocks, MOE_SUB, MOE_BM)[:, :, 0].astype(jnp.int32), axis=1)
    return (blk_e, nxt_e, n_sub, n_used, src_tok.reshape(n_blocks, 1, bm).astype(jnp.int32),
            dst_row.reshape(n_blocks, 1, bm).astype(jnp.int32))


def _combine_kernel(x1_ref, y0_ref, y1_ref, y2_ref, y3_ref, tg_ref, nf_ref, op_ref, os_ref, *, n_first):
    tg = tg_ref[...]
    tm = x1_ref.shape[0]
    rows = lambda y_ref: _load_packed_rows(y_ref, (), tm, F32)
    acc = rows(y0_ref) * tg[:, 0:1]
    for k, y_ref in enumerate((y1_ref, y2_ref, y3_ref), start=1):
        acc = acc + rows(y_ref) * tg[:, k:k + 1]
    x = x1_ref[...] + acc
    out = x * lax.rsqrt(jnp.mean(x * x, axis=-1, keepdims=True) + RMS_EPS) * nf_ref[...]
    i = pl.program_id(0)

    @pl.when(i < n_first)
    def _():
        op_ref[...] = out

    @pl.when(i >= n_first)
    def _():
        os_ref[...] = out


def _combine(x1, y, tg, norm_f_w, n_p):
    n = x1.shape[0]
    tm = COMBINE_TM
    n_first = n_p // tm
    row = lambda i: (i, 0)
    y_spec = lambda k: pl.BlockSpec((tm * PACK_ROWS, LANES), lambda i: (k * (n // tm) + i, 0))
    return pl.pallas_call(
        functools.partial(_combine_kernel, n_first=n_first),
        out_shape=(jax.ShapeDtypeStruct((n_p, D_MODEL), F32), jax.ShapeDtypeStruct((n - n_p, D_MODEL), F32)),
        grid=(n // tm,),
        in_specs=[pl.BlockSpec((tm, D_MODEL), row), y_spec(0), y_spec(1), y_spec(2), y_spec(3),
                  pl.BlockSpec((tm, LANES), row), pl.BlockSpec((1, D_MODEL), lambda i: (0, 0))],
        out_specs=_split_specs((tm, D_MODEL), n_first),
        compiler_params=pltpu.CompilerParams(
            dimension_semantics=("arbitrary",), vmem_limit_bytes=VMEM_LIMIT),
        name="combine_norm",
    )(x1, y, y, y, y, tg, norm_f_w.reshape(1, D_MODEL))


def _pack_w_in(w_in):
    s = np.cumsum((0,) + IN_SIZES)
    w_in = w_in.astype(BF16)
    qkv, z, al, be, qb, kb, vb, gb, gk, ga, gbt = [w_in[:, s[i]:s[i + 1]] for i in range(len(IN_SIZES))]
    small = jnp.concatenate(
        [al, be, gk, jnp.zeros((D_MODEL, LANES - 2 * N_HEADS - GK_RANK), w_in.dtype)], axis=1)
    return jnp.concatenate([qkv, z, ga, gbt, qb, kb, vb, gb, small], axis=1)


def kernel(x_prompt, x_sample, state_gdn, state_gdn_conv, state_gla, norm1_w, w_in, conv_w, a_log,
           dt_bias, gdn_norm_w, gla_gk_up, gla_gk_b, gla_norm_w, w_up_a, w_up_b, w_o, norm2_w,
           w_router, b_router, w_gate_up, b_gate_up, w_down, b_down, norm_f_w):
    bp, tp, _ = x_prompt.shape
    bs, ts, _ = x_sample.shape
    n_p, n_s = bp * tp, bs * ts
    n = n_p + n_s
    l = 0

    x_p = x_prompt.reshape(n_p, D_MODEL)
    x_s = x_sample.reshape(n_s, D_MODEL)
    p = _inproj(x_p, x_s, norm1_w[l], _pack_w_in(w_in[l]))

    hp = jnp.zeros((8, LANES), F32).at[0, :N_HEADS].set(a_log[l]).at[1, :N_HEADS].set(dt_bias[l])
    wgk = jnp.zeros((LANES, N_HEADS * DK_B), F32).at[2 * N_HEADS:2 * N_HEADS + GK_RANK].set(gla_gk_up[l]).astype(BF16)
    zeros = lambda shape: jnp.zeros(shape, F32)

    oa_p, gdn_p, conv_p = _gdn(p, 0, bp, tp, conv_w[l], hp, gdn_norm_w[l],
                               zeros((bp, CONV_W - 1, CONV_DIM)), zeros((bp, N_HEADS, DK_A, DV_A)))
    oa_s, gdn_s, conv_s = _gdn(p, n_p, bs, ts, conv_w[l], hp, gdn_norm_w[l], state_gdn_conv[l], state_gdn[l])
    ob_p, gla_p = _gla(p, 0, bp, tp, wgk, gla_gk_b[l], gla_norm_w[l], zeros((bp, N_HEADS, DK_B, DV_B)))
    ob_s, gla_s = _gla(p, n_p, bs, ts, wgk, gla_gk_b[l], gla_norm_w[l], state_gla[l])

    wr = jnp.zeros((D_MODEL, LANES), F32).at[:, :N_EXPERTS].set(w_router[l])
    wr_hi = wr.astype(BF16)
    wr = jnp.concatenate([wr_hi, (wr - wr_hi.astype(F32)).astype(BF16)], axis=1)
    br = jnp.zeros((1, LANES), F32).at[0, :N_EXPERTS].set(b_router[l])
    x1, h2, te, tg = _merge(x_p, x_s, oa_p, oa_s, ob_p, ob_s, p, w_up_a[l].astype(BF16),
                            w_up_b[l].astype(BF16), w_o[l].astype(BF16), norm2_w[l], wr, br)

    blk_e, nxt_e, n_sub, n_used, src_tok, dst_row = _route(te[:, :TOP_K], n)
    y = _moe(blk_e, nxt_e, n_sub, n_used, src_tok, dst_row, h2, w_gate_up[l],
             b_gate_up[l].reshape(N_EXPERTS, 1, 2 * D_FF), w_down[l],
             b_down[l].reshape(N_EXPERTS, 1, D_MODEL), n * TOP_K + MOE_SUB * MOE_BM)
    out_p, out_s = _combine(x1, y, tg, norm_f_w, n_p)
    y_prompt = out_p.reshape(bp, tp, D_MODEL)
    y_sample = out_s.reshape(bs, ts, D_MODEL)
    return (y_prompt, y_sample, gdn_p[None], conv_p[None], gla_p[None], gdn_s[None], conv_s[None],
            gla_s[None])
```
